```python
import math
import jax, jax.numpy as jnp
from jax import lax
import numpy as np

D_MODEL = 1024
BATCH = 32
SEQ = 256
DEPTH = 2
DEC_BATCH = 4
DEC_SEQ = 4096
PAST_LEN = 256

GRID_W = 64
D_MIX = D_MODEL
BRANCH_W = D_MIX // 4
Q_BLOCK = 128
ROPE_THETA = 10000.0
EPS = 1e-6

MLA_V = 64
MLA_HEADS = BRANCH_W // MLA_V
MLA_NOPE = 64
MLA_ROPE = 32
MLA_QK = MLA_NOPE + MLA_ROPE
MLA_Q_RANK = 192
MLA_KV_RANK = 128
MLA_W = MLA_HEADS * MLA_V

SSD_P = 64
SSD_HEADS = BRANCH_W // SSD_P
SSD_W = SSD_HEADS * SSD_P
SSD_GROUPS = 2
SSD_N = 64
SSD_CONV = 3
SSD_CHUNK = 128
SSD_CONV_CH = SSD_W + 2 * SSD_GROUPS * SSD_N

CF_W = BRANCH_W
CF_K = 31

GQA_HD = 64
GQA_HEADS = BRANCH_W // GQA_HD
GQA_KV_HEADS = 2
GQA_W = GQA_HEADS * GQA_HD

SPLITS = (MLA_Q_RANK, MLA_KV_RANK, MLA_ROPE, MLA_W,
          SSD_CONV_CH, 2 * SSD_HEADS, SSD_W,
          CF_W, CF_W, CF_W,
          GQA_W, GQA_KV_HEADS * GQA_HD, GQA_KV_HEADS * GQA_HD, GQA_W)
IN_W = sum(SPLITS)

kernel_name = 'hybrid_diffusion_mla_ssd_conformer_gqa'


def split_proj(u):
    outs = []
    o = 0
    for s in SPLITS:
        outs.append(u[..., o:o + s])
        o += s
    return outs


def rms_norm(x, g):
    xf = x.astype(jnp.float32)
    y = xf * lax.rsqrt(jnp.mean(xf * xf, axis=-1, keepdims=True) + EPS)
    return (y * g.astype(jnp.float32)).astype(x.dtype)


def layer_norm(x, g, b):
    xf = x.astype(jnp.float32)
    mu = jnp.mean(xf, axis=-1, keepdims=True)
    var = jnp.mean(jnp.square(xf - mu), axis=-1, keepdims=True)
    y = (xf - mu) * lax.rsqrt(var + EPS)
    return (y * g.astype(jnp.float32) + b.astype(jnp.float32)).astype(x.dtype)


def dwconv(x, w, b):
    k = w.shape[0]
    y = lax.conv_general_dilated(x, w[:, None, :], window_strides=(1,),
                                 padding=[(k // 2, k // 2)],
                                 dimension_numbers=('NWC', 'WIO', 'NWC'),
                                 feature_group_count=x.shape[-1])
    return y + b


def axial_rope_tables(n_tokens, dim):
    n_rows = n_tokens // GRID_W
    row = jnp.repeat(jnp.arange(n_rows), GRID_W).astype(jnp.float32)
    col = jnp.tile(jnp.arange(GRID_W), n_rows).astype(jnp.float32)
    quarter = dim // 4
    inv = ROPE_THETA ** (-jnp.arange(quarter, dtype=jnp.float32) / quarter)
    ar = row[:, None] * inv
    ac = col[:, None] * inv
    ang = jnp.concatenate([ar, ar, ac, ac], axis=-1)
    return jnp.cos(ang), jnp.sin(ang)


def apply_axial_rope(x, cos, sin):
    q = x.shape[-1] // 4
    x1, x2, x3, x4 = x[..., :q], x[..., q:2 * q], x[..., 2 * q:3 * q], x[..., 3 * q:]
    xrot = jnp.concatenate([-x2, x1, -x4, x3], axis=-1)
    return x * cos[:, None, :].astype(x.dtype) + xrot * sin[:, None, :].astype(x.dtype)


def block_attention(q, k, v, scale):
    b, sq, h, dk = q.shape
    hk = k.shape[2]
    rep = h // hk
    dv = v.shape[-1]
    nb = sq // Q_BLOCK
    qb = jnp.moveaxis(q.reshape(b, nb, Q_BLOCK, hk, rep, dk), 1, 0)

    def one_block(qblk):
        s = jnp.einsum('bqgrd,bkgd->bgrqk', qblk, k).astype(jnp.float32) * scale
        p = jax.nn.softmax(s, axis=-1).astype(v.dtype)
        return jnp.einsum('bgrqk,bkgd->bqgrd', p, v)

    o = lax.map(one_block, qb)
    return jnp.moveaxis(o, 0, 1).reshape(b, sq, h, dv)


def mla_queries(q_lat, q_norm_g, w_uq, q_head_g, rope):
    b, L, _ = q_lat.shape
    q = (rms_norm(q_lat, q_norm_g) @ w_uq).reshape(b, L, MLA_HEADS, MLA_QK)
    q = rms_norm(q, q_head_g)
    if rope is not None:
        q = jnp.concatenate([q[..., :MLA_NOPE], apply_axial_rope(q[..., MLA_NOPE:], *rope)], axis=-1)
    return q


def mla_keys(ckv, krope, w_ukv, k_head_g, rope):
    b, L, _ = ckv.shape
    kv = (ckv @ w_ukv).reshape(b, L, MLA_HEADS, MLA_NOPE + MLA_V)
    k_nope, v = kv[..., :MLA_NOPE], kv[..., MLA_NOPE:]
    k = jnp.concatenate([k_nope, jnp.broadcast_to(krope[:, :, None, :], (b, L, MLA_HEADS, MLA_ROPE))], axis=-1)
    k = rms_norm(k, k_head_g)
    if rope is not None:
        k = jnp.concatenate([k[..., :MLA_NOPE], apply_axial_rope(k[..., MLA_NOPE:], *rope)], axis=-1)
    return k, v


def ssd_scan(x, dt, a, bm, cm, s0):
    b, L, H, P = x.shape
    N = bm.shape[-1]
    Q = SSD_CHUNK
    nc = L // Q
    xc = x.reshape(b, nc, Q, H, P)
    dtc = dt.reshape(b, nc, Q, H)
    bc = bm.reshape(b, nc, Q, H, N)
    cc = cm.reshape(b, nc, Q, H, N)
    acum = jnp.cumsum(dtc * a, axis=2)
    seg = acum[:, :, :, None, :] - acum[:, :, None, :, :]
    causal = jnp.tril(jnp.ones((Q, Q), dtype=bool))[None, None, :, :, None]
    decay = jnp.exp(jnp.where(causal, seg, -jnp.inf))
    xdt = xc * dtc[..., None]
    scores = jnp.einsum('bcihn,bcjhn->bcijh', cc, bc) * decay
    y_diag = jnp.einsum('bcijh,bcjhp->bcihp', scores, xdt)
    tail = jnp.exp(acum[:, :, -1:, :] - acum)
    chunk_states = jnp.einsum('bcjhn,bcjh,bcjhp->bchpn', bc, tail, xdt)
    chunk_decay = jnp.exp(acum[:, :, -1, :])

    def step(s, inp):
        dcy, st = inp
        return (dcy[:, :, None, None] * s + st).astype(s.dtype), s

    s_final, s_prev = lax.scan(step, s0.astype(chunk_states.dtype),
                               (jnp.moveaxis(chunk_decay, 1, 0), jnp.moveaxis(chunk_states, 1, 0)))
    y_off = jnp.einsum('bcihn,cbhpn->bcihp', cc * jnp.exp(acum)[..., None], s_prev)
    return (y_diag + y_off).reshape(b, L, H, P), s_final


def ssd_mixer(xbc, dt_raw, z, conv_w, conv_b, dt_bias, a_log, d_skip, norm_g, s_init):
    b, L, _ = xbc.shape
    xbc = jax.nn.silu(dwconv(xbc, conv_w, conv_b))
    xs = xbc[..., :SSD_W].reshape(b, L, SSD_HEADS, SSD_P)
    rep = SSD_HEADS // SSD_GROUPS
    bm = jnp.repeat(xbc[..., SSD_W:SSD_W + SSD_GROUPS * SSD_N].reshape(b, L, SSD_GROUPS, SSD_N), rep, axis=2)
    cm = jnp.repeat(xbc[..., SSD_W + SSD_GROUPS * SSD_N:].reshape(b, L, SSD_GROUPS, SSD_N), rep, axis=2)
    ys = []
    states = []
    for d in range(2):
        dt = jax.nn.softplus(dt_raw[..., d * SSD_HEADS:(d + 1) * SSD_HEADS] + dt_bias[d])
        a = -jnp.exp(a_log[d])
        if d == 0:
            y, s = ssd_scan(xs, dt, a, bm, cm, s_init[:, d])
        else:
            y, s = ssd_scan(xs[:, ::-1], dt[:, ::-1], a, bm[:, ::-1], cm[:, ::-1], s_init[:, d])
            y = y[:, ::-1]
        ys.append(y)
        states.append(s)
    y = ys[0] + ys[1] + d_skip[:, None] * xs
    y = rms_norm(y.reshape(b, L, SSD_W) * jax.nn.silu(z), norm_g)
    return y, jnp.stack(states, axis=1)


def conformer_mixer(val, glu_gate, gate, conv_w, conv_b, ln_g, ln_b):
    g = val * jax.nn.sigmoid(glu_gate)
    y = layer_norm(dwconv(g, conv_w, conv_b), ln_g, ln_b)
    return jax.nn.silu(y) * jax.nn.silu(gate)


def trunk_layer(x, mod, l, prm, ctx, ropes):
    b, L, _ = x.shape
    latent = ctx is not None
    rope_a, rope_d = ropes
    shift, scale, gate = jnp.split(mod, 3, axis=-1)
    h = rms_norm(x, prm['norm_g'][l]) * (1.0 + scale) + shift
    u = h @ prm['w_in'][l]
    (a_q, a_kv, a_r, a_g, b_xbc, b_dt, b_z, c_v, c_gl, c_g, d_q, d_k, d_v, d_g) = split_proj(u)

    ckv = rms_norm(a_kv, prm['mla_kv_norm_g'][l])
    q_a = mla_queries(a_q, prm['mla_q_norm_g'][l], prm['mla_w_uq'][l], prm['mla_q_head_g'][l], rope_a)
    k_a, v_a = mla_keys(ckv, a_r, prm['mla_w_ukv'][l], prm['mla_k_head_g'][l], rope_a)
    if latent:
        k_c, v_c = mla_keys(ctx[0], ctx[1], prm['mla_w_ukv'][l], prm['mla_k_head_g'][l], None)
        k_a = jnp.concatenate([k_c, k_a], axis=1)
        v_a = jnp.concatenate([v_c, v_a], axis=1)
    o_a = block_attention(q_a, k_a, v_a, MLA_QK ** -0.5).reshape(b, L, MLA_W) * jax.nn.silu(a_g)

    s_init = ctx[4] if latent else jnp.zeros((b, 2, SSD_HEADS, SSD_P, SSD_N), x.dtype)
    o_b, s_fin = ssd_mixer(b_xbc, b_dt, b_z, prm['ssd_conv_w'][l], prm['ssd_conv_b'][l],
                           prm['ssd_dt_bias'][l], prm['ssd_a_log'][l], prm['ssd_d'][l],
                           prm['ssd_norm_g'][l], s_init)

    o_c = conformer_mixer(c_v, c_gl, c_g, prm['cf_conv_w'][l], prm['cf_conv_b'][l],
                          prm['cf_ln_g'][l], prm['cf_ln_b'][l])

    q_d = rms_norm(d_q.reshape(b, L, GQA_HEADS, GQA_HD), prm['gqa_q_g'][l])
    k_d = rms_norm(d_k.reshape(b, L, GQA_KV_HEADS, GQA_HD), prm['gqa_k_g'][l])
    v_d = d_v.reshape(b, L, GQA_KV_HEADS, GQA_HD)
    if latent:
        q_att = apply_axial_rope(q_d, *rope_d)
        k_att = jnp.concatenate([ctx[2], apply_axial_rope(k_d, *rope_d)], axis=1)
        v_att = jnp.concatenate([ctx[3], v_d], axis=1)
    else:
        q_att, k_att, v_att = q_d, k_d, v_d
    o_d = block_attention(q_att, k_att, v_att, GQA_HD ** -0.5).reshape(b, L, GQA_W) * jax.nn.silu(d_g)

    out = jnp.concatenate([o_a, o_b, o_c, o_d], axis=-1) @ prm['w_out'][l]
    x = x + gate * out
    new_ctx = None if latent else (ckv, a_r, k_d, v_d, s_fin)
    return x, new_ctx


def setup_inputs(seed: int = 0) -> dict:
    key = jax.random.key(seed)
    ks = jax.random.split(key, 32)
    f32 = jnp.float32

    def nrm(k, shape, s=1.0):
        return s * jax.random.normal(k, shape, f32)

    def gain(k, shape):
        return 1.0 + 0.02 * jax.random.normal(k, shape, f32)

    dt0 = jnp.exp(jax.random.uniform(ks[22], (DEPTH, 2, SSD_HEADS), f32, math.log(1e-3), math.log(1e-1)))
    dt_bias = dt0 + jnp.log(-jnp.expm1(-dt0))
    a_log = jnp.log(jax.random.uniform(ks[23], (DEPTH, 2, SSD_HEADS), f32, 1.0, 16.0))
    return {
        'x_prompt': nrm(ks[0], (BATCH, SEQ, D_MODEL)),
        'x_sample': nrm(ks[1], (DEC_BATCH, DEC_SEQ, D_MODEL)),
        'cache_mla_ckv': nrm(ks[2], (DEC_BATCH, DEPTH, PAST_LEN, MLA_KV_RANK)),
        'cache_mla_krope': nrm(ks[3], (DEC_BATCH, DEPTH, PAST_LEN, MLA_ROPE)),
        'cache_gqa_k': nrm(ks[4], (DEC_BATCH, DEPTH, PAST_LEN, GQA_KV_HEADS, GQA_HD)),
        'cache_gqa_v': nrm(ks[5], (DEC_BATCH, DEPTH, PAST_LEN, GQA_KV_HEADS, GQA_HD)),
        'state_ssd': nrm(ks[6], (DEC_BATCH, DEPTH, 2, SSD_HEADS, SSD_P, SSD_N), 0.1),
        'c': nrm(ks[7], (DEC_BATCH, D_MODEL)),
        'c_ctx': nrm(ks[8], (D_MODEL,)),
        'w_mod': nrm(ks[9], (DEPTH, D_MODEL, 3 * D_MODEL), D_MODEL ** -0.5),
        'b_mod': nrm(ks[10], (DEPTH, 3 * D_MODEL), 0.02),
        'norm_g': gain(ks[11], (DEPTH, D_MODEL)),
        'w_in': nrm(ks[12], (DEPTH, D_MODEL, IN_W), D_MODEL ** -0.5),
        'w_out': nrm(ks[13], (DEPTH, D_MIX, D_MODEL), D_MIX ** -0.5),
        'mla_q_norm_g': gain(ks[14], (DEPTH, MLA_Q_RANK)),
        'mla_w_uq': nrm(ks[15], (DEPTH, MLA_Q_RANK, MLA_HEADS * MLA_QK), MLA_Q_RANK ** -0.5),
        'mla_kv_norm_g': gain(ks[16], (DEPTH, MLA_KV_RANK)),
        'mla_w_ukv': nrm(ks[17], (DEPTH, MLA_KV_RANK, MLA_HEADS * (MLA_NOPE + MLA_V)), MLA_KV_RANK ** -0.5),
        'mla_q_head_g': gain(ks[18], (DEPTH, MLA_QK)),
        'mla_k_head_g': gain(ks[19], (DEPTH, MLA_QK)),
        'ssd_conv_w': nrm(ks[20], (DEPTH, SSD_CONV, SSD_CONV_CH), SSD_CONV ** -0.5),
        'ssd_conv_b': nrm(ks[21], (DEPTH, SSD_CONV_CH), 0.02),
        'ssd_dt_bias': dt_bias,
        'ssd_a_log': a_log,
        'ssd_d': gain(ks[24], (DEPTH, SSD_HEADS)),
        'ssd_norm_g': gain(ks[25], (DEPTH, SSD_W)),
        'cf_conv_w': nrm(ks[26], (DEPTH, CF_K, CF_W), CF_K ** -0.5),
        'cf_conv_b': nrm(ks[27], (DEPTH, CF_W), 0.02),
        'cf_ln_g': gain(ks[28], (DEPTH, CF_W)),
        'cf_ln_b': nrm(ks[29], (DEPTH, CF_W), 0.02),
        'gqa_q_g': gain(ks[30], (DEPTH, GQA_HD)),
        'gqa_k_g': gain(ks[31], (DEPTH, GQA_HD)),
    }


def reference(x_prompt, x_sample, cache_mla_ckv, cache_mla_krope, cache_gqa_k, cache_gqa_v, state_ssd,
              c, c_ctx, w_mod, b_mod, norm_g, w_in, w_out,
              mla_q_norm_g, mla_w_uq, mla_kv_norm_g, mla_w_ukv, mla_q_head_g, mla_k_head_g,
              ssd_conv_w, ssd_conv_b, ssd_dt_bias, ssd_a_log, ssd_d, ssd_norm_g,
              cf_conv_w, cf_conv_b, cf_ln_g, cf_ln_b, gqa_q_g, gqa_k_g):
    prm = dict(norm_g=norm_g, w_in=w_in, w_out=w_out,
               mla_q_norm_g=mla_q_norm_g, mla_w_uq=mla_w_uq, mla_kv_norm_g=mla_kv_norm_g,
               mla_w_ukv=mla_w_ukv, mla_q_head_g=mla_q_head_g, mla_k_head_g=mla_k_head_g,
               ssd_conv_w=ssd_conv_w, ssd_conv_b=ssd_conv_b, ssd_dt_bias=ssd_dt_bias,
               ssd_a_log=ssd_a_log, ssd_d=ssd_d, ssd_norm_g=ssd_norm_g,
               cf_conv_w=cf_conv_w, cf_conv_b=cf_conv_b, cf_ln_g=cf_ln_g, cf_ln_b=cf_ln_b,
               gqa_q_g=gqa_q_g, gqa_k_g=gqa_k_g)

    y = x_prompt
    ctx_out = []
    for l in range(DEPTH):
        mod = (jax.nn.silu(c_ctx) @ w_mod[l] + b_mod[l])[None, None, :]
        y, new_ctx = trunk_layer(y, mod, l, prm, None, (None, None))
        ctx_out.append(new_ctx)
    new_mla_ckv = jnp.stack([t[0] for t in ctx_out], axis=1)
    new_mla_krope = jnp.stack([t[1] for t in ctx_out], axis=1)
    new_gqa_k = jnp.stack([t[2] for t in ctx_out], axis=1)
    new_gqa_v = jnp.stack([t[3] for t in ctx_out], axis=1)
    new_state_ssd = jnp.stack([t[4] for t in ctx_out], axis=1)

    n_lat = x_sample.shape[1]
    ropes = (axial_rope_tables(n_lat, MLA_ROPE), axial_rope_tables(n_lat, GQA_HD))
    z = x_sample
    for l in range(DEPTH):
        mod = (jax.nn.silu(c) @ w_mod[l] + b_mod[l])[:, None, :]
        ctx = (cache_mla_ckv[:, l], cache_mla_krope[:, l], cache_gqa_k[:, l], cache_gqa_v[:, l], state_ssd[:, l])
        z, _ = trunk_layer(z, mod, l, prm, ctx, ropes)

    return (y, z, new_mla_ckv, new_mla_krope, new_gqa_k, new_gqa_v, new_state_ssd)
```

```python
import functools
import math

import jax
import jax.numpy as jnp
from jax import lax
from jax.experimental import pallas as pl
from jax.experimental.pallas import tpu as pltpu

F32 = jnp.float32
BF16 = jnp.bfloat16

LANES = 128
VMEM_CAP = 56 * 1024 * 1024

EPS = 1e-6
ROPE_THETA = 10000.0
GRID_W = 64

D_MODEL = 1024
BRANCH_W = 256
MLA_HEADS, MLA_NOPE, MLA_ROPE, MLA_V = 4, 64, 32, 64
MLA_QK = MLA_NOPE + MLA_ROPE
MLA_Q_RANK, MLA_KV_RANK = 192, 128
SSD_HEADS, SSD_P, SSD_N, SSD_GROUPS, SSD_CONV, SSD_CHUNK = 4, 64, 64, 2, 3, 128
SSD_W = SSD_HEADS * SSD_P
SSD_CONV_CH = SSD_W + 2 * SSD_GROUPS * SSD_N
CF_W, CF_K = 256, 31
GQA_HEADS, GQA_KV_HEADS, GQA_HD = 4, 2, 64
SPLITS = (MLA_Q_RANK, MLA_KV_RANK, MLA_ROPE, MLA_HEADS * MLA_V,
          SSD_CONV_CH, 2 * SSD_HEADS, SSD_W,
          CF_W, CF_W, CF_W,
          GQA_HEADS * GQA_HD, GQA_KV_HEADS * GQA_HD, GQA_KV_HEADS * GQA_HD, GQA_HEADS * GQA_HD)

O_QL, O_KV, O_R, O_AG = 0, 256, 384, 512
O_XBC, O_DT, O_Z = 768, 1280, 1408
O_CV, O_CGL, O_CG = 1664, 1920, 2176
O_DQ, O_DK, O_DV, O_DG = 2432, 2944, 3200, 3456
IN_WP = 3712

CONV_HALO_SSD = 8
CONV_HALO_CF = 16

NT_DIMS = (((1,), (1,)), ((), ()))


def _silu(x):
    return x / (1.0 + jnp.exp(-x))


def _sigmoid(x):
    return 1.0 / (1.0 + jnp.exp(-x))


def _softplus(x):
    return jnp.maximum(x, 0.0) + jnp.log1p(jnp.exp(-jnp.abs(x)))


def _rms(x, n):
    return x * lax.rsqrt(jnp.sum(x * x, axis=-1, keepdims=True) * (1.0 / n) + EPS)


def _dot(a, b):
    return jnp.dot(a, b, preferred_element_type=F32)


def _dot_nt(a, b):
    return lax.dot_general(a, b, NT_DIMS, preferred_element_type=F32)


def _params(sem, vmem_bytes):
    limit = int(min(max(vmem_bytes, 32 * 1024 * 1024), VMEM_CAP))
    return pltpu.CompilerParams(dimension_semantics=sem, vmem_limit_bytes=limit)


def _pad_heads(w, nh, d, dp):
    s = w.shape[:-1]
    w = w.reshape(s + (nh, d))
    w = jnp.pad(w, [(0, 0)] * len(s) + [(0, 0), (0, dp - d)])
    return w.reshape(s + (nh * dp,))


def _pad_last(w, lo, hi):
    return jnp.pad(w, [(0, 0)] * (w.ndim - 1) + [(lo, hi)])


def _mod_kernel(c_ref, w_ref, b_ref, o_ref):
    c = c_ref[...]
    o_ref[0] = _dot(_silu(c).astype(BF16), w_ref[0].astype(BF16)) + b_ref[0]


def _modulation(cvec, w_mod, b_mod):
    depth, d, d3 = w_mod.shape
    tn = d
    return pl.pallas_call(
        _mod_kernel,
        out_shape=jax.ShapeDtypeStruct((depth, 8, d3), F32),
        grid=(depth, d3 // tn),
        in_specs=[pl.BlockSpec((8, d), lambda l, j: (0, 0)),
                  pl.BlockSpec((1, d, tn), lambda l, j: (l, 0, j)),
                  pl.BlockSpec((1, 1, tn), lambda l, j: (l, 0, j))],
        out_specs=pl.BlockSpec((1, 8, tn), lambda l, j: (l, 0, j)),
        compiler_params=_params(("arbitrary", "arbitrary"), 4 * d * tn * 4),
        name="modulation",
    )(cvec, w_mod, b_mod.reshape(depth, 1, d3))


def _rope(heads, tab_ref, shift):
    cos, sin_a, sin_b = tab_ref[0], tab_ref[1], tab_ref[2]
    outs = []
    for xh in heads:
        up = pltpu.roll(xh, LANES - shift, 1)
        dn = pltpu.roll(xh, shift, 1)
        outs.append(xh * cos + up * sin_a + dn * sin_b)
    return outs


def _head_norm(x, gain, n):
    return [_rms(x[:, h * LANES:(h + 1) * LANES], n) * gain for h in range(x.shape[-1] // LANES)]


def _inproj_kernel(latent, *refs):
    if latent:
        (x_ref, mod_ref, ng_ref, win_ref, wuq_ref, wukv_ref, gains_ref, ra_ref, rd_ref,
         qa_ref, ka_ref, va_ref, ag_ref, qd_ref, kd_ref, vd_ref, dg_ref,
         xbc_ref, dt_ref, z_ref, glu_ref, cg_ref) = refs
    else:
        (x_ref, mod_ref, ng_ref, win_ref, wuq_ref, wukv_ref, gains_ref,
         qa_ref, ka_ref, va_ref, ag_ref, qd_ref, kd_ref, vd_ref, dg_ref,
         xbc_ref, dt_ref, z_ref, glu_ref, cg_ref,
         ckv_ref, kr_ref, gk_ref, gv_ref) = refs

    x = x_ref[0]
    shift = mod_ref[0, 0:1, :]
    scale = mod_ref[0, 1:2, :]
    h = _rms(x, D_MODEL) * ng_ref[...] * (1.0 + scale) + shift
    u = _dot(h.astype(BF16), win_ref[...])

    g_ql = gains_ref[0:1, :]
    g_qh = gains_ref[1:2, 0:LANES]
    g_kv = gains_ref[2:3, 0:LANES]
    g_kh = gains_ref[3:4, 0:LANES]
    g_dq = gains_ref[4:5, 0:LANES]
    g_dk = gains_ref[5:6, 0:LANES]

    ql = _rms(u[:, O_QL:O_QL + 256], MLA_Q_RANK) * g_ql
    q = _dot(ql.astype(BF16), wuq_ref[...])
    qh = _head_norm(q, g_qh, MLA_QK)
    ckv = _rms(u[:, O_KV:O_KV + LANES], MLA_KV_RANK) * g_kv
    kv = _dot(ckv.astype(BF16), wukv_ref[...])
    kr = u[:, O_R:O_R + LANES]
    kh = [_rms(kv[:, i * LANES:(i + 1) * LANES] + kr, MLA_QK) * g_kh for i in range(MLA_HEADS)]
    if latent:
        qh = _rope(qh, ra_ref, MLA_ROPE // 4)
        kh = _rope(kh, ra_ref, MLA_ROPE // 4)
    for i in range(MLA_HEADS):
        qa_ref[0, :, i * LANES:(i + 1) * LANES] = qh[i].astype(BF16)
        ka_ref[0, :, i * LANES:(i + 1) * LANES] = kh[i].astype(BF16)
    va_ref[0] = kv[:, MLA_HEADS * LANES:].astype(BF16)
    ag_ref[0] = u[:, O_AG:O_AG + BRANCH_W]

    qd = _head_norm(u[:, O_DQ:O_DQ + GQA_HEADS * LANES], g_dq, GQA_HD)
    kd = _head_norm(u[:, O_DK:O_DK + GQA_KV_HEADS * LANES], g_dk, GQA_HD)
    vd = u[:, O_DV:O_DV + GQA_KV_HEADS * LANES]
    if not latent:
        ckv_ref[0] = ckv
        kr_ref[0] = kr[:, MLA_NOPE:MLA_NOPE + MLA_ROPE]
        for i in range(GQA_KV_HEADS):
            gk_ref[0, :, i * GQA_HD:(i + 1) * GQA_HD] = kd[i][:, 0:GQA_HD]
            gv_ref[0, :, i * GQA_HD:(i + 1) * GQA_HD] = vd[:, i * LANES:i * LANES + GQA_HD]
    else:
        qd = _rope(qd, rd_ref, GQA_HD // 4)
        kd = _rope(kd, rd_ref, GQA_HD // 4)
    for i in range(GQA_HEADS):
        qd_ref[0, :, i * LANES:(i + 1) * LANES] = qd[i].astype(BF16)
    for i in range(GQA_KV_HEADS):
        kd_ref[0, :, i * LANES:(i + 1) * LANES] = kd[i].astype(BF16)
    vd_ref[0] = vd.astype(BF16)
    dg_ref[0] = u[:, O_DG:O_DG + BRANCH_W]

    xbc_ref[0] = u[:, O_XBC:O_XBC + SSD_CONV_CH]
    dt_ref[0] = u[:, O_DT:O_DT + LANES]
    z_ref[0] = u[:, O_Z:O_Z + SSD_W]
    glu_ref[0] = u[:, O_CV:O_CV + CF_W] * _sigmoid(u[:, O_CGL:O_CGL + CF_W])
    cg_ref[0] = u[:, O_CG:O_CG + CF_W]


def _inproj(x, mod, ng, win, wuq, wukv, gains, ropes, tm):
    b, s, d = x.shape
    latent = ropes is not None
    mod_b = mod.shape[0]
    tok = lambda w: pl.BlockSpec((1, tm, w), lambda bi, i: (bi, i, 0))
    full = lambda a: pl.BlockSpec(a.shape, lambda bi, i: (0,) * a.ndim)
    in_specs = [tok(d),
                pl.BlockSpec((1, 3, d), (lambda bi, i: (bi, 0, 0)) if mod_b > 1 else (lambda bi, i: (0, 0, 0))),
                full(ng), full(win), full(wuq), full(wukv), full(gains)]
    args = [x, mod, ng, win, wuq, wukv, gains]
    if latent:
        in_specs += [pl.BlockSpec((3, tm, LANES), lambda bi, i: (0, i, 0))] * 2
        args += list(ropes)
    widths = [(4 * LANES, BF16), (4 * LANES, BF16), (4 * LANES, BF16), (BRANCH_W, F32),
              (4 * LANES, BF16), (2 * LANES, BF16), (2 * LANES, BF16), (BRANCH_W, F32),
              (SSD_CONV_CH, F32), (LANES, F32), (SSD_W, F32), (CF_W, F32), (CF_W, F32)]
    if not latent:
        widths += [(MLA_KV_RANK, F32), (MLA_ROPE, F32), (LANES, F32), (LANES, F32)]
    out_shape = [jax.ShapeDtypeStruct((b, s, w), dt) for w, dt in widths]
    out_specs = [tok(w) for w, _ in widths]
    out_bytes = sum(w * jnp.dtype(dt).itemsize for w, dt in widths) * tm
    vmem = 2 * (tm * d * 4 + win.size * 2 + wuq.size * 2 + wukv.size * 2 + out_bytes) + 3 * tm * IN_WP * 4
    return pl.pallas_call(
        functools.partial(_inproj_kernel, latent),
        out_shape=out_shape,
        grid=(b, s // tm),
        in_specs=in_specs,
        out_specs=out_specs,
        compiler_params=_params(("parallel", "parallel"), vmem),
        name="inproj_lat" if latent else "inproj_ctx",
    )(*args)


def _ctxkv_kernel(ckv_ref, kr_ref, wukv_ref, gkh_ref, kc_ref, vc_ref):
    kv = _dot(ckv_ref[0].astype(BF16), wukv_ref[...])
    kr = kr_ref[0]
    g = gkh_ref[...]
    for i in range(MLA_HEADS):
        k = _rms(kv[:, i * LANES:(i + 1) * LANES] + kr, MLA_QK) * g
        kc_ref[0, :, i * LANES:(i + 1) * LANES] = k.astype(BF16)
    vc_ref[0] = kv[:, MLA_HEADS * LANES:].astype(BF16)


def _ctx_kv(ckv, kr_pad, wukv, g_kh):
    b, p, r = ckv.shape
    w = MLA_HEADS * LANES
    return pl.pallas_call(
        _ctxkv_kernel,
        out_shape=[jax.ShapeDtypeStruct((b, p, w), BF16)] * 2,
        grid=(b,),
        in_specs=[pl.BlockSpec((1, p, r), lambda bi: (bi, 0, 0)),
                  pl.BlockSpec((1, p, LANES), lambda bi: (bi, 0, 0)),
                  pl.BlockSpec(wukv.shape, lambda bi: (0, 0)),
                  pl.BlockSpec(g_kh.shape, lambda bi: (0, 0))],
        out_specs=[pl.BlockSpec((1, p, w), lambda bi: (bi, 0, 0))] * 2,
        compiler_params=_params(("parallel",), 8 * p * w * 4),
        name="ctx_kv",
    )(ckv, kr_pad, wukv, g_kh)


def _attn_kernel(n_heads, rep, has_ctx, *refs):
    if has_ctx:
        q_ref, k_ref, v_ref, kc_ref, vc_ref, g_ref, o_ref = refs
    else:
        q_ref, k_ref, v_ref, g_ref, o_ref = refs
    hd = o_ref.shape[-1] // n_heads
    for h in range(n_heads):
        g = h // rep
        hs = slice(h * LANES, (h + 1) * LANES)
        gs = slice(g * LANES, (g + 1) * LANES)
        qh = q_ref[0, :, hs]
        s = _dot_nt(qh, k_ref[0, :, gs])
        m = jnp.max(s, axis=-1, keepdims=True)
        if has_ctx:
            sc = _dot_nt(qh, kc_ref[0, :, gs])
            m = jnp.maximum(m, jnp.max(sc, axis=-1, keepdims=True))
        p = jnp.exp(s - m)
        l = jnp.sum(p, axis=-1, keepdims=True)
        o = _dot(p.astype(BF16), v_ref[0, :, gs])
        if has_ctx:
            pc = jnp.exp(sc - m)
            l = l + jnp.sum(pc, axis=-1, keepdims=True)
            o = o + _dot(pc.astype(BF16), vc_ref[0, :, gs])
        o = o / l
        os_ = slice(h * hd, (h + 1) * hd)
        o_ref[0, :, os_] = o[:, 0:hd] * _silu(g_ref[0, :, os_])


def _attention(q, k, v, ctx, gate, n_heads, rep, tq, name):
    b, s, qw = q.shape
    kw = k.shape[-1]
    ow = gate.shape[-1]
    has_ctx = ctx is not None
    in_specs = [pl.BlockSpec((1, tq, qw), lambda bi, i: (bi, i, 0)),
                pl.BlockSpec((1, s, kw), lambda bi, i: (bi, 0, 0)),
                pl.BlockSpec((1, s, kw), lambda bi, i: (bi, 0, 0))]
    args = [q, k, v]
    p = 0
    if has_ctx:
        p = ctx[0].shape[1]
        in_specs += [pl.BlockSpec((1, p, kw), lambda bi, i: (bi, 0, 0))] * 2
        args += list(ctx)
    in_specs.append(pl.BlockSpec((1, tq, ow), lambda bi, i: (bi, i, 0)))
    args.append(gate)
    vmem = 2 * (tq * qw * 2 + 2 * (s + p) * kw * 2 + 2 * tq * ow * 4) + 4 * tq * (s + p) * 4
    return pl.pallas_call(
        functools.partial(_attn_kernel, n_heads, rep, has_ctx),
        out_shape=jax.ShapeDtypeStruct((b, s, ow), F32),
        grid=(b, s // tq),
        in_specs=in_specs,
        out_specs=pl.BlockSpec((1, tq, ow), lambda bi, i: (bi, i, 0)),
        compiler_params=_params(("parallel", "arbitrary"), vmem),
        name=name,
    )(*args)


def _conv_kernel(tl, xc_ref, xp_ref, xn_ref, gc_ref, gp_ref, gn_ref, cg_ref,
                 w3_ref, b3_ref, w31_ref, b31_ref, lng_ref, lnb_ref,
                 xo_ref, co_ref, ext3, ext31):
    i = pl.program_id(1)
    keep_prev = (i > 0).astype(F32)
    keep_next = (i < pl.num_programs(1) - 1).astype(F32)

    h3 = CONV_HALO_SSD
    ext3[0:h3, :] = xp_ref[0] * keep_prev
    ext3[h3:h3 + tl, :] = xc_ref[0]
    ext3[h3 + tl:2 * h3 + tl, :] = xn_ref[0] * keep_next
    acc = jnp.zeros((tl, SSD_CONV_CH), F32) + b3_ref[...]
    for k in range(SSD_CONV):
        acc = acc + w3_ref[k:k + 1, :] * ext3[h3 - SSD_CONV // 2 + k:h3 - SSD_CONV // 2 + k + tl, :]
    xo_ref[0] = _silu(acc)

    h31 = CONV_HALO_CF
    ext31[0:h31, :] = gp_ref[0] * keep_prev
    ext31[h31:h31 + tl, :] = gc_ref[0]
    ext31[h31 + tl:2 * h31 + tl, :] = gn_ref[0] * keep_next
    acc = jnp.zeros((tl, CF_W), F32) + b31_ref[...]
    for k in range(CF_K):
        acc = acc + w31_ref[k:k + 1, :] * ext31[h31 - CF_K // 2 + k:h31 - CF_K // 2 + k + tl, :]
    mu = jnp.mean(acc, axis=-1, keepdims=True)
    cen = acc - mu
    var = jnp.mean(cen * cen, axis=-1, keepdims=True)
    y = cen * lax.rsqrt(var + EPS) * lng_ref[...] + lnb_ref[...]
    co_ref[0] = _silu(y) * _silu(cg_ref[0])


def _convs(xbc, glu, cg, w3, b3, w31, b31, lng, lnb, tl):
    b, s, _ = xbc.shape
    nt = s // tl

    def cur(w):
        return pl.BlockSpec((1, tl, w), lambda bi, i: (bi, i, 0))

    def prev(w, hrows):
        r = tl // hrows
        return pl.BlockSpec((1, hrows, w), lambda bi, i: (bi, jnp.maximum(i * r - 1, 0), 0))

    def nxt(w, hrows):
        r = tl // hrows
        return pl.BlockSpec((1, hrows, w), lambda bi, i: (bi, jnp.minimum((i + 1) * r, nt * r - 1), 0))

    full = lambda a: pl.BlockSpec(a.shape, lambda bi, i: (0,) * a.ndim)
    vmem = 4 * tl * (SSD_CONV_CH * 2 + CF_W * 3) * 4 + 6 * tl * SSD_CONV_CH * 4
    return pl.pallas_call(
        functools.partial(_conv_kernel, tl),
        out_shape=[jax.ShapeDtypeStruct((b, s, SSD_CONV_CH), F32),
                   jax.ShapeDtypeStruct((b, s, CF_W), F32)],
        grid=(b, nt),
        in_specs=[cur(SSD_CONV_CH), prev(SSD_CONV_CH, CONV_HALO_SSD), nxt(SSD_CONV_CH, CONV_HALO_SSD),
                  cur(CF_W), prev(CF_W, CONV_HALO_CF), nxt(CF_W, CONV_HALO_CF), cur(CF_W),
                  full(w3), full(b3), full(w31), full(b31), full(lng), full(lnb)],
        out_specs=[cur(SSD_CONV_CH), cur(CF_W)],
        scratch_shapes=[pltpu.VMEM((tl + 2 * CONV_HALO_SSD, SSD_CONV_CH), F32),
                        pltpu.VMEM((tl + 2 * CONV_HALO_CF, CF_W), F32)],
        compiler_params=_params(("parallel", "parallel"), vmem),
        name="convs",
    )(xbc, xbc, xbc, glu, glu, glu, cg, w3, b3, w31, b31, lng, lnb)


def _cumsum_rows(tri, x):
    hi = x.astype(BF16)
    r1 = x - hi.astype(F32)
    mid = r1.astype(BF16)
    lo = (r1 - mid.astype(F32)).astype(BF16)
    return _dot(tri, hi) + _dot(tri, mid) + _dot(tri, lo)


def _ssd_chunk(c, d, xc_ref, dt_ref, bias, a_row, dskip, y_ref, st_ref):
    q = SSD_CHUNK
    r0 = pl.multiple_of(c * q, q)
    rows = pl.ds(r0, q)
    xs = xc_ref[0, rows, 0:SSD_W]
    bm = xc_ref[0, rows, SSD_W:SSD_W + LANES]
    cm = xc_ref[0, rows, SSD_W + LANES:SSD_W + 2 * LANES]
    dt = _softplus(dt_ref[0, rows, :] + bias)
    ri = lax.broadcasted_iota(jnp.int32, (q, q), 0)
    ci = lax.broadcasted_iota(jnp.int32, (q, q), 1)
    mask = (ri >= ci) if d == 0 else (ri <= ci)
    acum = _cumsum_rows(mask.astype(F32).astype(BF16), dt * a_row)
    acum_t = acum.T
    tot = acum[q - 1:q, :] if d == 0 else acum[0:1, :]
    etot = jnp.exp(tot)
    eac = jnp.exp(acum)
    tail = jnp.exp(tot - acum)
    lo_half = lax.broadcasted_iota(jnp.int32, (q, LANES), 1) < SSD_P
    top_half = lax.broadcasted_iota(jnp.int32, (LANES, SSD_N), 0) < SSD_P
    for g in range(SSD_GROUPS):
        k0 = d * SSD_HEADS + 2 * g
        k1 = k0 + 1

        def sel(mat):
            return jnp.where(lo_half, mat[:, k0:k0 + 1], mat[:, k1:k1 + 1])

        xg = xs[:, g * LANES:(g + 1) * LANES]
        xdt = xg * sel(dt)
        bg = bm[:, g * SSD_N:(g + 1) * SSD_N].astype(BF16)
        cg = cm[:, g * SSD_N:(g + 1) * SSD_N].astype(BF16)
        cb = _dot_nt(cg, bg)
        yg = None
        for hh, k in enumerate((k0, k1)):
            seg = acum[:, k:k + 1] - acum_t[k:k + 1, :]
            dec = jnp.where(mask, jnp.exp(seg), 0.0)
            xm = jnp.where(lo_half if hh == 0 else jnp.logical_not(lo_half), xdt, 0.0)
            part = _dot((cb * dec).astype(BF16), xm.astype(BF16))
            yg = part if yg is None else yg + part
        s_prev = st_ref[0, d, g]
        yg = yg + _dot_nt(cg, s_prev.astype(BF16)) * sel(eac)
        cs = _dot((xdt * sel(tail)).T.astype(BF16), bg)
        cdec = jnp.where(top_half, etot[:, k0:k0 + 1], etot[:, k1:k1 + 1])
        st_ref[0, d, g] = cdec * s_prev + cs
        if d == 0:
            yg = yg + dskip[:, g * LANES:(g + 1) * LANES] * xg
        cols = slice(g * LANES, (g + 1) * LANES)
        y_ref[0, rows, cols] = y_ref[0, rows, cols] + yg


def _ssd_kernel(nc, xc_ref, dt_ref, s0_ref, bias_ref, alog_ref, d_ref, y_ref, st_ref):
    lane = lax.broadcasted_iota(jnp.int32, (1, LANES), 1)
    a_row = jnp.where(lane < 2 * SSD_HEADS, -jnp.exp(alog_ref[...]), 0.0)
    bias = bias_ref[...]
    dskip = d_ref[...]
    st_ref[...] = s0_ref[...]
    y_ref[...] = jnp.zeros(y_ref.shape, F32)

    def body(c, carry):
        _ssd_chunk(c, 0, xc_ref, dt_ref, bias, a_row, dskip, y_ref, st_ref)
        _ssd_chunk(nc - 1 - c, 1, xc_ref, dt_ref, bias, a_row, dskip, y_ref, st_ref)
        return carry

    lax.fori_loop(0, nc, body, 0)


def _ssd(xc, dt, s0, bias, alog, dskip):
    b, s, _ = xc.shape
    nc = s // SSD_CHUNK
    st_shape = (b, 2, SSD_GROUPS, 2 * SSD_P, SSD_N)
    st_spec = pl.BlockSpec((1,) + st_shape[1:], lambda bi: (bi, 0, 0, 0, 0))
    row = lambda a: pl.BlockSpec(a.shape, lambda bi: (0, 0))
    vmem = 2 * s * (SSD_CONV_CH + LANES + SSD_W) * 4 + 8 * 1024 * 1024
    return pl.pallas_call(
        functools.partial(_ssd_kernel, nc),
        out_shape=[jax.ShapeDtypeStruct((b, s, SSD_W), F32), jax.ShapeDtypeStruct(st_shape, F32)],
        grid=(b,),
        in_specs=[pl.BlockSpec((1, s, SSD_CONV_CH), lambda bi: (bi, 0, 0)),
                  pl.BlockSpec((1, s, LANES), lambda bi: (bi, 0, 0)),
                  st_spec, row(bias), row(alog), row(dskip)],
        out_specs=[pl.BlockSpec((1, s, SSD_W), lambda bi: (bi, 0, 0)), st_spec],
        compiler_params=_params(("parallel",), vmem),
        name="ssd",
    )(xc, dt, s0, bias, alog, dskip)


def _outproj_kernel(oa_ref, ys_ref, z_ref, oc_ref, od_ref, x_ref, mod_ref, ng_ref, w_ref, o_ref):
    ob = _rms(ys_ref[0] * _silu(z_ref[0]), SSD_W) * ng_ref[...]
    acc = _dot(oa_ref[0].astype(BF16), w_ref[0:BRANCH_W, :])
    acc = acc + _dot(ob.astype(BF16), w_ref[BRANCH_W:2 * BRANCH_W, :])
    acc = acc + _dot(oc_ref[0].astype(BF16), w_ref[2 * BRANCH_W:3 * BRANCH_W, :])
    acc = acc + _dot(od_ref[0].astype(BF16), w_ref[3 * BRANCH_W:4 * BRANCH_W, :])
    o_ref[0] = x_ref[0] + mod_ref[0, 2:3, :] * acc


def _outproj(oa, ys, z, oc, od, x, mod, ng, w, tm):
    b, s, d = x.shape
    mod_b = mod.shape[0]
    br = lambda: pl.BlockSpec((1, tm, BRANCH_W), lambda bi, i: (bi, i, 0))
    xs = pl.BlockSpec((1, tm, d), lambda bi, i: (bi, i, 0))
    vmem = 2 * (5 * tm * BRANCH_W * 4 + 2 * tm * d * 4 + w.size * 2) + 2 * tm * d * 4
    return pl.pallas_call(
        _outproj_kernel,
        out_shape=jax.ShapeDtypeStruct((b, s, d), F32),
        grid=(b, s // tm),
        in_specs=[br(), br(), br(), br(), br(), xs,
                  pl.BlockSpec((1, 3, d), (lambda bi, i: (bi, 0, 0)) if mod_b > 1 else (lambda bi, i: (0, 0, 0))),
                  pl.BlockSpec(ng.shape, lambda bi, i: (0, 0)),
                  pl.BlockSpec(w.shape, lambda bi, i: (0, 0))],
        out_specs=xs,
        compiler_params=_params(("parallel", "parallel"), vmem),
        name="outproj",
    )(oa, ys, z, oc, od, x, mod, ng, w)


def _rope_table(n_tokens, dim, lane0):
    n_rows = n_tokens // GRID_W
    row = jnp.repeat(jnp.arange(n_rows), GRID_W).astype(F32)
    col = jnp.tile(jnp.arange(GRID_W), n_rows).astype(F32)
    quarter = dim // 4
    inv = ROPE_THETA ** (-jnp.arange(quarter, dtype=F32) / quarter)
    ar = row[:, None] * inv
    ac = col[:, None] * inv
    zero = jnp.zeros_like(ar)
    cos = jnp.concatenate([jnp.cos(ar), jnp.cos(ar), jnp.cos(ac), jnp.cos(ac)], axis=-1)
    sin_a = jnp.concatenate([-jnp.sin(ar), zero, -jnp.sin(ac), zero], axis=-1)
    sin_b = jnp.concatenate([zero, jnp.sin(ar), zero, jnp.sin(ac)], axis=-1)
    hi = LANES - lane0 - dim
    cos = jnp.pad(cos, ((0, 0), (lane0, 0)), constant_values=1.0)
    cos = jnp.pad(cos, ((0, 0), (0, hi)), constant_values=1.0)
    return jnp.stack([cos, _pad_last(sin_a, lane0, hi), _pad_last(sin_b, lane0, hi)])


def _layer_weights(l, w_in, w_out, mla_q_norm_g, mla_w_uq, mla_kv_norm_g, mla_w_ukv, mla_q_head_g,
                   mla_k_head_g, gqa_q_g, gqa_k_g):
    pieces = []
    o = 0
    for sz in SPLITS:
        pieces.append(w_in[l][:, o:o + sz])
        o += sz
    (a_q, a_kv, a_r, a_g, b_xbc, b_dt, b_z, c_v, c_gl, c_g, d_q, d_k, d_v, d_g) = pieces
    win = jnp.concatenate([
        _pad_last(a_q, 0, 256 - MLA_Q_RANK), a_kv, _pad_last(a_r, MLA_NOPE, LANES - MLA_QK), a_g,
        b_xbc, _pad_last(b_dt, 0, LANES - 2 * SSD_HEADS), b_z, c_v, c_gl, c_g,
        _pad_heads(d_q, GQA_HEADS, GQA_HD, LANES), _pad_heads(d_k, GQA_KV_HEADS, GQA_HD, LANES),
        _pad_heads(d_v, GQA_KV_HEADS, GQA_HD, LANES), d_g], axis=-1).astype(BF16)
    wuq = jnp.pad(_pad_heads(mla_w_uq[l], MLA_HEADS, MLA_QK, LANES), ((0, 256 - MLA_Q_RANK), (0, 0))).astype(BF16)
    kv = mla_w_ukv[l].reshape(MLA_KV_RANK, MLA_HEADS, MLA_NOPE + MLA_V)
    wk = jnp.pad(kv[:, :, :MLA_NOPE], ((0, 0), (0, 0), (0, LANES - MLA_NOPE))).reshape(MLA_KV_RANK, -1)
    wv = jnp.pad(kv[:, :, MLA_NOPE:], ((0, 0), (0, 0), (0, LANES - MLA_V))).reshape(MLA_KV_RANK, -1)
    wukv = jnp.concatenate([wk, wv], axis=-1).astype(BF16)
    row = lambda v: _pad_last(v, 0, 256 - v.shape[-1])
    gains = jnp.stack([row(mla_q_norm_g[l]),
                       row(mla_q_head_g[l] * (MLA_QK ** -0.5)),
                       row(mla_kv_norm_g[l]),
                       row(mla_k_head_g[l]),
                       row(gqa_q_g[l] * (GQA_HD ** -0.5)),
                       row(gqa_k_g[l]),
                       jnp.zeros((256,), F32), jnp.zeros((256,), F32)])
    return win, wuq, wukv, gains, w_out[l].astype(BF16)


def _mixer_layer(x, mod, l, wts, prm, ctx, ropes, tm, tq):
    win, wuq, wukv, gains, wout = wts
    latent = ctx is not None
    outs = _inproj(x, mod, prm['norm_g'][l][None, :], win, wuq, wukv, gains, ropes, tm)
    qa, ka, va, ag, qd, kd, vd, dg, xbc, dt, z, glu, cg = outs[:13]
    b = x.shape[0]
    if latent:
        kr_pad = _pad_last(ctx[1], MLA_NOPE, LANES - MLA_QK)
        ctx_a = _ctx_kv(ctx[0], kr_pad, wukv, gains[3:4, 0:LANES])
        p = ctx[2].shape[1]
        ctx_d = (_pad_last(ctx[2], 0, LANES - GQA_HD).reshape(b, p, -1).astype(BF16),
                 _pad_last(ctx[3], 0, LANES - GQA_HD).reshape(b, p, -1).astype(BF16))
        s0 = ctx[4].reshape(b, 2, SSD_GROUPS, 2 * SSD_P, SSD_N)
    else:
        ctx_a = ctx_d = None
        s0 = jnp.zeros((b, 2, SSD_GROUPS, 2 * SSD_P, SSD_N), F32)
    oa = _attention(qa, ka, va, ctx_a, ag, MLA_HEADS, 1, tq, "attn_mla")
    od = _attention(qd, kd, vd, ctx_d, dg, GQA_HEADS, GQA_HEADS // GQA_KV_HEADS, tq, "attn_gqa")
    xconv, oc = _convs(xbc, glu, cg, prm['ssd_conv_w'][l], prm['ssd_conv_b'][l][None, :],
                       prm['cf_conv_w'][l], prm['cf_conv_b'][l][None, :],
                       prm['cf_ln_g'][l][None, :], prm['cf_ln_b'][l][None, :], min(256, x.shape[1]))
    bias = _pad_last(prm['ssd_dt_bias'][l].reshape(1, -1), 0, LANES - 2 * SSD_HEADS)
    alog = _pad_last(prm['ssd_a_log'][l].reshape(1, -1), 0, LANES - 2 * SSD_HEADS)
    dskip = jnp.repeat(prm['ssd_d'][l], SSD_P)[None, :]
    ys, s_fin = _ssd(xconv, dt, s0, bias, alog, dskip)
    y = _outproj(oa, ys, z, oc, od, x, mod, prm['ssd_norm_g'][l][None, :], wout, tm)
    new_ctx = None
    if not latent:
        ckv, kr, gk, gv = outs[13:]
        new_ctx = (ckv, kr, gk.reshape(b, -1, GQA_KV_HEADS, GQA_HD), gv.reshape(b, -1, GQA_KV_HEADS, GQA_HD),
                   s_fin.reshape(b, 2, SSD_HEADS, SSD_P, SSD_N))
    return y, new_ctx


def kernel(x_prompt, x_sample, cache_mla_ckv, cache_mla_krope, cache_gqa_k, cache_gqa_v, state_ssd, c, c_ctx, w_mod, b_mod, norm_g, w_in, w_out, mla_q_norm_g, mla_w_uq, mla_kv_norm_g, mla_w_ukv, mla_q_head_g, mla_k_head_g, ssd_conv_w, ssd_conv_b, ssd_dt_bias, ssd_a_log, ssd_d, ssd_norm_g, cf_conv_w, cf_conv_b, cf_ln_g, cf_ln_b, gqa_q_g, gqa_k_g):
    depth = w_in.shape[0]
    dec_b = x_sample.shape[0]
    assert dec_b < 8 and x_prompt.shape[-1] == D_MODEL and w_in.shape[-1] == sum(SPLITS)
    prm = dict(norm_g=norm_g, ssd_conv_w=ssd_conv_w, ssd_conv_b=ssd_conv_b, ssd_dt_bias=ssd_dt_bias,
               ssd_a_log=ssd_a_log, ssd_d=ssd_d, ssd_norm_g=ssd_norm_g, cf_conv_w=cf_conv_w,
               cf_conv_b=cf_conv_b, cf_ln_g=cf_ln_g, cf_ln_b=cf_ln_b)

    cvec = jnp.concatenate([c, c_ctx[None, :], jnp.zeros((8 - dec_b - 1, D_MODEL), F32)], axis=0)
    mod = _modulation(cvec, w_mod, b_mod).reshape(depth, 8, 3, D_MODEL)
    wts = [_layer_weights(l, w_in, w_out, mla_q_norm_g, mla_w_uq, mla_kv_norm_g, mla_w_ukv,
                          mla_q_head_g, mla_k_head_g, gqa_q_g, gqa_k_g) for l in range(depth)]

    y = x_prompt
    ctx_out = []
    seq = x_prompt.shape[1]
    for l in range(depth):
        y, new_ctx = _mixer_layer(y, mod[l, dec_b:dec_b + 1], l, wts[l], prm, None, None,
                                  min(256, seq), min(256, seq))
        ctx_out.append(new_ctx)
    new = [jnp.stack([t[i] for t in ctx_out], axis=1) for i in range(5)]

    n_lat = x_sample.shape[1]
    ropes = (_rope_table(n_lat, MLA_ROPE, MLA_NOPE), _rope_table(n_lat, GQA_HD, 0))
    z = x_sample
    for l in range(depth):
        ctx = (cache_mla_ckv[:, l], cache_mla_krope[:, l], cache_gqa_k[:, l], cache_gqa_v[:, l], state_ssd[:, l])
        z, _ = _mixer_layer(z, mod[l, 0:dec_b], l, wts[l], prm, ctx, ropes, 256, 256)

    return (y, z, new[0], new[1], new[2], new[3], new[4])
```

```python
import functools
import math

import jax
import jax.numpy as jnp
from jax import lax
from jax.experimental import pallas as pl
from jax.experimental.pallas import tpu as pltpu

F32 = jnp.float32
BF16 = jnp.bfloat16

LANES = 128
VMEM_CAP = 56 * 1024 * 1024

EPS = 1e-6
ROPE_THETA = 10000.0
GRID_W = 64
LOG2E = math.log2(math.e)

D_MODEL = 1024
BRANCH_W = 256
MLA_HEADS, MLA_NOPE, MLA_ROPE, MLA_V = 4, 64, 32, 64
MLA_QK = MLA_NOPE + MLA_ROPE
MLA_Q_RANK, MLA_KV_RANK = 192, 128
SSD_HEADS, SSD_P, SSD_N, SSD_GROUPS, SSD_CONV, SSD_CHUNK = 4, 64, 64, 2, 3, 128
SSD_W = SSD_HEADS * SSD_P
SSD_CONV_CH = SSD_W + 2 * SSD_GROUPS * SSD_N
CF_W, CF_K = 256, 31
GQA_HEADS, GQA_KV_HEADS, GQA_HD = 4, 2, 64
SPLITS = (MLA_Q_RANK, MLA_KV_RANK, MLA_ROPE, MLA_HEADS * MLA_V,
          SSD_CONV_CH, 2 * SSD_HEADS, SSD_W,
          CF_W, CF_W, CF_W,
          GQA_HEADS * GQA_HD, GQA_KV_HEADS * GQA_HD, GQA_KV_HEADS * GQA_HD, GQA_HEADS * GQA_HD)

O_QL, O_KV, O_R, O_AG = 0, 256, 384, 512
O_XBC, O_DT, O_Z = 768, 1280, 1408
O_CV, O_CGL, O_CG = 1664, 1920, 2176
O_DQ, O_DK, O_DV, O_DG = 2432, 2944, 3200, 3456
IN_WP = 3712

CONV_HALO_SSD = 8
CONV_HALO_CF = 16

ATT_QK_CHUNK = 512
ATT_PV_CHUNK = 256

NT_DIMS = (((1,), (1,)), ((), ()))


def _silu(x):
    return x / (1.0 + jnp.exp(-x))


def _sigmoid(x):
    return 1.0 / (1.0 + jnp.exp(-x))


def _softplus(x):
    return jnp.maximum(x, 0.0) + jnp.log1p(jnp.exp(-jnp.abs(x)))


def _rms(x, n):
    return x * lax.rsqrt(jnp.sum(x * x, axis=-1, keepdims=True) * (1.0 / n) + EPS)


def _dot(a, b):
    return jnp.dot(a, b, preferred_element_type=F32)


def _dot_nt(a, b):
    return lax.dot_general(a, b, NT_DIMS, preferred_element_type=F32)


def _params(sem, vmem_bytes):
    limit = int(min(max(vmem_bytes, 32 * 1024 * 1024), VMEM_CAP))
    return pltpu.CompilerParams(dimension_semantics=sem, vmem_limit_bytes=limit)


def _pad_heads(w, nh, d, dp):
    s = w.shape[:-1]
    w = w.reshape(s + (nh, d))
    w = jnp.pad(w, [(0, 0)] * len(s) + [(0, 0), (0, dp - d)])
    return w.reshape(s + (nh * dp,))


def _pad_last(w, lo, hi):
    return jnp.pad(w, [(0, 0)] * (w.ndim - 1) + [(lo, hi)])


def _layer_spec(a, l, grid_rank):
    zeros = (0,) * (a.ndim - 1)
    if grid_rank == 1:
        return pl.BlockSpec((1,) + a.shape[1:], lambda bi: (l,) + zeros)
    return pl.BlockSpec((1,) + a.shape[1:], lambda bi, i: (l,) + zeros)


def _with_ones_lane(v, hd):
    lane = lax.broadcasted_iota(jnp.int32, v.shape, v.ndim - 1)
    return jnp.where(lane % LANES == hd, 1.0, v)


def _mod_kernel(c_ref, w_ref, b_ref, o_ref):
    c = c_ref[...]
    o_ref[0] = _dot(_silu(c).astype(BF16), w_ref[0].astype(BF16)) + b_ref[0]


def _modulation(cvec, w_mod, b_mod):
    depth, d, d3 = w_mod.shape
    tn = d
    return pl.pallas_call(
        _mod_kernel,
        out_shape=jax.ShapeDtypeStruct((depth, 8, d3), F32),
        grid=(depth, d3 // tn),
        in_specs=[pl.BlockSpec((8, d), lambda l, j: (0, 0)),
                  pl.BlockSpec((1, d, tn), lambda l, j: (l, 0, j)),
                  pl.BlockSpec((1, 1, tn), lambda l, j: (l, 0, j))],
        out_specs=pl.BlockSpec((1, 8, tn), lambda l, j: (l, 0, j)),
        compiler_params=_params(("arbitrary", "arbitrary"), 4 * d * tn * 4),
        name="modulation",
    )(cvec, w_mod, b_mod.reshape(depth, 1, d3))


def _rope(heads, tab_ref, shift):
    cos, sin_a, sin_b = tab_ref[0], tab_ref[1], tab_ref[2]
    outs = []
    for xh in heads:
        up = pltpu.roll(xh, LANES - shift, 1)
        dn = pltpu.roll(xh, shift, 1)
        outs.append(xh * cos + up * sin_a + dn * sin_b)
    return outs


def _head_norm(x, gain, n):
    return [_rms(x[:, h * LANES:(h + 1) * LANES], n) * gain for h in range(x.shape[-1] // LANES)]


def _inproj_kernel(latent, *refs):
    if latent:
        (x_ref, mod_ref, ng_ref, win_ref, wuq_ref, wukv_ref, gains_ref, ra_ref, rd_ref,
         qa_ref, ka_ref, va_ref, ag_ref, qd_ref, kd_ref, vd_ref, dg_ref,
         xbc_ref, dt_ref, z_ref, glu_ref, cg_ref) = refs
    else:
        (x_ref, mod_ref, ng_ref, win_ref, wuq_ref, wukv_ref, gains_ref,
         qa_ref, ka_ref, va_ref, ag_ref, qd_ref, kd_ref, vd_ref, dg_ref,
         xbc_ref, dt_ref, z_ref, glu_ref, cg_ref,
         ckv_ref, kr_ref, gk_ref, gv_ref) = refs

    x = x_ref[0]
    shift = mod_ref[0, 0, 0:1, :]
    scale = mod_ref[0, 0, 1:2, :]
    h = _rms(x, D_MODEL) * ng_ref[0] * (1.0 + scale) + shift
    u = _dot(h.astype(BF16), win_ref[0])

    g_ql = gains_ref[0, 0:1, :]
    g_qh = gains_ref[0, 1:2, 0:LANES]
    g_kv = gains_ref[0, 2:3, 0:LANES]
    g_kh = gains_ref[0, 3:4, 0:LANES]
    g_dq = gains_ref[0, 4:5, 0:LANES]
    g_dk = gains_ref[0, 5:6, 0:LANES]

    ql = _rms(u[:, O_QL:O_QL + 256], MLA_Q_RANK) * g_ql
    q = _dot(ql.astype(BF16), wuq_ref[0])
    qh = _head_norm(q, g_qh, MLA_QK)
    ckv = _rms(u[:, O_KV:O_KV + LANES], MLA_KV_RANK) * g_kv
    kv = _dot(ckv.astype(BF16), wukv_ref[0])
    kr = u[:, O_R:O_R + LANES]
    kh = [_rms(kv[:, i * LANES:(i + 1) * LANES] + kr, MLA_QK) * g_kh for i in range(MLA_HEADS)]
    if latent:
        qh = _rope(qh, ra_ref, MLA_ROPE // 4)
        kh = _rope(kh, ra_ref, MLA_ROPE // 4)
    for i in range(MLA_HEADS):
        qa_ref[0, :, i * LANES:(i + 1) * LANES] = qh[i].astype(BF16)
        ka_ref[0, :, i * LANES:(i + 1) * LANES] = kh[i].astype(BF16)
    va_ref[0] = _with_ones_lane(kv[:, MLA_HEADS * LANES:], MLA_V).astype(BF16)
    ag_ref[0] = u[:, O_AG:O_AG + BRANCH_W]

    qd = _head_norm(u[:, O_DQ:O_DQ + GQA_HEADS * LANES], g_dq, GQA_HD)
    kd = _head_norm(u[:, O_DK:O_DK + GQA_KV_HEADS * LANES], g_dk, GQA_HD)
    vd = u[:, O_DV:O_DV + GQA_KV_HEADS * LANES]
    if not latent:
        ckv_ref[0] = ckv
        kr_ref[0] = kr[:, MLA_NOPE:MLA_NOPE + MLA_ROPE]
        for i in range(GQA_KV_HEADS):
            gk_ref[0, :, i * GQA_HD:(i + 1) * GQA_HD] = kd[i][:, 0:GQA_HD]
            gv_ref[0, :, i * GQA_HD:(i + 1) * GQA_HD] = vd[:, i * LANES:i * LANES + GQA_HD]
    else:
        qd = _rope(qd, rd_ref, GQA_HD // 4)
        kd = _rope(kd, rd_ref, GQA_HD // 4)
    for i in range(GQA_HEADS):
        qd_ref[0, :, i * LANES:(i + 1) * LANES] = qd[i].astype(BF16)
    for i in range(GQA_KV_HEADS):
        kd_ref[0, :, i * LANES:(i + 1) * LANES] = kd[i].astype(BF16)
    vd_ref[0] = _with_ones_lane(vd, GQA_HD).astype(BF16)
    dg_ref[0] = u[:, O_DG:O_DG + BRANCH_W]

    xbc_ref[0] = u[:, O_XBC:O_XBC + SSD_CONV_CH]
    dt_ref[0] = u[:, O_DT:O_DT + LANES]
    z_ref[0] = u[:, O_Z:O_Z + SSD_W]
    glu_ref[0] = u[:, O_CV:O_CV + CF_W] * _sigmoid(u[:, O_CGL:O_CGL + CF_W])
    cg_ref[0] = u[:, O_CG:O_CG + CF_W]


def _mod_spec(mod, l, row):
    d = mod.shape[-1]
    if row is None:
        return pl.BlockSpec((1, 1, 3, d), lambda bi, i: (l, bi, 0, 0))
    return pl.BlockSpec((1, 1, 3, d), lambda bi, i: (l, row, 0, 0))


def _inproj(x, mod, mod_row, l, w, ropes, tm):
    b, s, d = x.shape
    latent = ropes is not None
    tok = lambda wd: pl.BlockSpec((1, tm, wd), lambda bi, i: (bi, i, 0))
    weights = [w['ng'], w['win'], w['wuq'], w['wukv'], w['gains']]
    in_specs = [tok(d), _mod_spec(mod, l, mod_row)] + [_layer_spec(a, l, 2) for a in weights]
    args = [x, mod] + weights
    if latent:
        in_specs += [pl.BlockSpec((3, tm, LANES), lambda bi, i: (0, i, 0))] * 2
        args += list(ropes)
    widths = [(4 * LANES, BF16), (4 * LANES, BF16), (4 * LANES, BF16), (BRANCH_W, F32),
              (4 * LANES, BF16), (2 * LANES, BF16), (2 * LANES, BF16), (BRANCH_W, F32),
              (SSD_CONV_CH, F32), (LANES, F32), (SSD_W, F32), (CF_W, F32), (CF_W, F32)]
    if not latent:
        widths += [(MLA_KV_RANK, F32), (MLA_ROPE, F32), (LANES, F32), (LANES, F32)]
    out_shape = [jax.ShapeDtypeStruct((b, s, wd), dt) for wd, dt in widths]
    out_specs = [tok(wd) for wd, _ in widths]
    out_bytes = sum(wd * jnp.dtype(dt).itemsize for wd, dt in widths) * tm
    w_bytes = sum(a[0].size * a.dtype.itemsize for a in weights)
    vmem = 2 * (tm * d * 4 + w_bytes + out_bytes) + 3 * tm * IN_WP * 4
    return pl.pallas_call(
        functools.partial(_inproj_kernel, latent),
        out_shape=out_shape,
        grid=(b, s // tm),
        in_specs=in_specs,
        out_specs=out_specs,
        compiler_params=_params(("parallel", "parallel"), vmem),
        name="inproj_lat" if latent else "inproj_ctx",
    )(*args)


def _ctxkv_kernel(ckv_ref, kr_ref, gk_ref, gv_ref, wukv_ref, gains_ref, kc_ref, vc_ref, kdc_ref, vdc_ref):
    kv = _dot(ckv_ref[0, 0].astype(BF16), wukv_ref[0])
    kr = kr_ref[0, 0]
    g = gains_ref[0, 3:4, 0:LANES]
    for i in range(MLA_HEADS):
        k = _rms(kv[:, i * LANES:(i + 1) * LANES] + kr, MLA_QK) * g
        kc_ref[0, :, i * LANES:(i + 1) * LANES] = k.astype(BF16)
    vc_ref[0] = _with_ones_lane(kv[:, MLA_HEADS * LANES:], MLA_V).astype(BF16)
    gk = gk_ref[0, 0]
    gv = gv_ref[0, 0]
    p = gk.shape[0]
    pad = jnp.zeros((p, LANES - GQA_HD), F32)
    for i in range(GQA_KV_HEADS):
        lo = i * LANES
        kdc_ref[0, :, lo:lo + GQA_HD] = gk[:, i * GQA_HD:(i + 1) * GQA_HD].astype(BF16)
        kdc_ref[0, :, lo + GQA_HD:lo + LANES] = pad.astype(BF16)
        vdc_ref[0, :, lo:lo + GQA_HD] = gv[:, i * GQA_HD:(i + 1) * GQA_HD].astype(BF16)
        vdc_ref[0, :, lo + GQA_HD:lo + LANES] = _with_ones_lane(pad, 0).astype(BF16)


def _ctx_kv(ckv, kr_pad, gk, gv, l, w):
    b, _, p, _ = ckv.shape
    cache = lambda a: pl.BlockSpec((1, 1, p, a.shape[-1]), lambda bi: (bi, l, 0, 0))
    out = lambda wd: pl.BlockSpec((1, p, wd), lambda bi: (bi, 0, 0))
    wa, wd = MLA_HEADS * LANES, GQA_KV_HEADS * LANES
    return pl.pallas_call(
        _ctxkv_kernel,
        out_shape=[jax.ShapeDtypeStruct((b, p, wa), BF16), jax.ShapeDtypeStruct((b, p, wa), BF16),
                   jax.ShapeDtypeStruct((b, p, wd), BF16), jax.ShapeDtypeStruct((b, p, wd), BF16)],
        grid=(b,),
        in_specs=[cache(ckv), cache(kr_pad), cache(gk), cache(gv),
                  _layer_spec(w['wukv'], l, 1), _layer_spec(w['gains'], l, 1)],
        out_specs=[out(wa), out(wa), out(wd), out(wd)],
        compiler_params=_params(("parallel",), 16 * p * wa * 4),
        name="ctx_kv",
    )(ckv, kr_pad, gk, gv, w['wukv'], w['gains'])


def _lane_tile_max(t):
    r = t[:, 0:LANES]
    for j in range(1, t.shape[-1] // LANES):
        r = jnp.maximum(r, t[:, j * LANES:(j + 1) * LANES])
    return r


def _attn_kernel(n_heads, rep, has_ctx, *refs):
    if has_ctx:
        q_ref, k_ref, v_ref, kc_ref, vc_ref, g_ref, o_ref, s0_ref, s1_ref = refs
        p_len = kc_ref.shape[1]
    else:
        q_ref, k_ref, v_ref, g_ref, o_ref, s0_ref, s1_ref = refs
        p_len = 0
    s_len = k_ref.shape[1]
    hd = o_ref.shape[-1] // n_heads
    kb = min(ATT_QK_CHUNK, s_len)
    kb2 = ATT_PV_CHUNK
    for h in range(n_heads):
        s_ref = s0_ref if h % 2 == 0 else s1_ref
        g = h // rep
        hs = slice(h * LANES, (h + 1) * LANES)
        gs = slice(g * LANES, (g + 1) * LANES)
        qh = q_ref[0, :, hs]
        mx = None
        for c in range(s_len // kb):
            sc = _dot_nt(qh, k_ref[0, c * kb:(c + 1) * kb, gs])
            s_ref[:, c * kb:(c + 1) * kb] = sc
            t = _lane_tile_max(sc)
            mx = t if mx is None else jnp.maximum(mx, t)
        if has_ctx:
            sc = _dot_nt(qh, kc_ref[0, :, gs])
            s_ref[:, s_len:s_len + p_len] = sc
            mx = jnp.maximum(mx, _lane_tile_max(sc))
        m = jnp.max(mx, axis=-1, keepdims=True)
        acc = None
        for c in range((s_len + p_len) // kb2):
            r0 = c * kb2
            p = jnp.exp2(s_ref[:, r0:r0 + kb2] - m).astype(BF16)
            vv = v_ref[0, r0:r0 + kb2, gs] if r0 < s_len else vc_ref[0, r0 - s_len:r0 - s_len + kb2, gs]
            part = _dot(p, vv)
            acc = part if acc is None else acc + part
        o = acc[:, 0:hd] / acc[:, hd:hd + 1]
        os_ = slice(h * hd, (h + 1) * hd)
        o_ref[0, :, os_] = o * _silu(g_ref[0, :, os_])


def _attention(q, k, v, ctx, gate, n_heads, rep, tq, name):
    b, s, qw = q.shape
    kw = k.shape[-1]
    ow = gate.shape[-1]
    has_ctx = ctx is not None
    in_specs = [pl.BlockSpec((1, tq, qw), lambda bi, i: (bi, i, 0)),
                pl.BlockSpec((1, s, kw), lambda bi, i: (bi, 0, 0)),
                pl.BlockSpec((1, s, kw), lambda bi, i: (bi, 0, 0))]
    args = [q, k, v]
    p = 0
    if has_ctx:
        p = ctx[0].shape[1]
        in_specs += [pl.BlockSpec((1, p, kw), lambda bi, i: (bi, 0, 0))] * 2
        args += list(ctx)
    in_specs.append(pl.BlockSpec((1, tq, ow), lambda bi, i: (bi, i, 0)))
    args.append(gate)
    assert s % min(ATT_QK_CHUNK, s) == 0 and s % ATT_PV_CHUNK == 0 and p % ATT_PV_CHUNK == 0
    vmem = 2 * (tq * qw * 2 + 2 * (s + p) * kw * 2 + 2 * tq * ow * 4) + 2 * tq * (s + p) * 4 + 4 * tq * 1024 * 4
    return pl.pallas_call(
        functools.partial(_attn_kernel, n_heads, rep, has_ctx),
        out_shape=jax.ShapeDtypeStruct((b, s, ow), F32),
        grid=(b, s // tq),
        in_specs=in_specs,
        out_specs=pl.BlockSpec((1, tq, ow), lambda bi, i: (bi, i, 0)),
        scratch_shapes=[pltpu.VMEM((tq, s + p), F32), pltpu.VMEM((tq, s + p), F32)],
        compiler_params=_params(("parallel", "arbitrary"), vmem),
        name=name,
    )(*args)


def _conv_kernel(tl, xc_ref, xp_ref, xn_ref, gc_ref, gp_ref, gn_ref, cg_ref,
                 w3_ref, b3_ref, w31_ref, b31_ref, lng_ref, lnb_ref,
                 xo_ref, co_ref, ext3, ext31):
    i = pl.program_id(1)
    keep_prev = (i > 0).astype(F32)
    keep_next = (i < pl.num_programs(1) - 1).astype(F32)

    h3 = CONV_HALO_SSD
    ext3[0:h3, :] = xp_ref[0] * keep_prev
    ext3[h3:h3 + tl, :] = xc_ref[0]
    ext3[h3 + tl:2 * h3 + tl, :] = xn_ref[0] * keep_next
    acc = jnp.zeros((tl, SSD_CONV_CH), F32) + b3_ref[0]
    for k in range(SSD_CONV):
        acc = acc + w3_ref[0, k:k + 1, :] * ext3[h3 - SSD_CONV // 2 + k:h3 - SSD_CONV // 2 + k + tl, :]
    xo_ref[0] = _silu(acc)

    h31 = CONV_HALO_CF
    ext31[0:h31, :] = gp_ref[0] * keep_prev
    ext31[h31:h31 + tl, :] = gc_ref[0]
    ext31[h31 + tl:2 * h31 + tl, :] = gn_ref[0] * keep_next
    acc = jnp.zeros((tl, CF_W), F32) + b31_ref[0]
    for k in range(CF_K):
        acc = acc + w31_ref[0, k:k + 1, :] * ext31[h31 - CF_K // 2 + k:h31 - CF_K // 2 + k + tl, :]
    mu = jnp.mean(acc, axis=-1, keepdims=True)
    cen = acc - mu
    var = jnp.mean(cen * cen, axis=-1, keepdims=True)
    y = cen * lax.rsqrt(var + EPS) * lng_ref[0] + lnb_ref[0]
    co_ref[0] = _silu(y) * _silu(cg_ref[0])


def _convs(xbc, glu, cg, l, w, tl):
    b, s, _ = xbc.shape
    nt = s // tl

    def cur(wd):
        return pl.BlockSpec((1, tl, wd), lambda bi, i: (bi, i, 0))

    def prev(wd, hrows):
        r = tl // hrows
        return pl.BlockSpec((1, hrows, wd), lambda bi, i: (bi, jnp.maximum(i * r - 1, 0), 0))

    def nxt(wd, hrows):
        r = tl // hrows
        return pl.BlockSpec((1, hrows, wd), lambda bi, i: (bi, jnp.minimum((i + 1) * r, nt * r - 1), 0))

    weights = [w['w3'], w['b3'], w['w31'], w['b31'], w['lng'], w['lnb']]
    vmem = 4 * tl * (SSD_CONV_CH * 2 + CF_W * 3) * 4 + 6 * tl * SSD_CONV_CH * 4
    return pl.pallas_call(
        functools.partial(_conv_kernel, tl),
        out_shape=[jax.ShapeDtypeStruct((b, s, SSD_CONV_CH), F32),
                   jax.ShapeDtypeStruct((b, s, CF_W), F32)],
        grid=(b, nt),
        in_specs=[cur(SSD_CONV_CH), prev(SSD_CONV_CH, CONV_HALO_SSD), nxt(SSD_CONV_CH, CONV_HALO_SSD),
                  cur(CF_W), prev(CF_W, CONV_HALO_CF), nxt(CF_W, CONV_HALO_CF), cur(CF_W)]
                 + [_layer_spec(a, l, 2) for a in weights],
        out_specs=[cur(SSD_CONV_CH), cur(CF_W)],
        scratch_shapes=[pltpu.VMEM((tl + 2 * CONV_HALO_SSD, SSD_CONV_CH), F32),
                        pltpu.VMEM((tl + 2 * CONV_HALO_CF, CF_W), F32)],
        compiler_params=_params(("parallel", "parallel"), vmem),
        name="convs",
    )(xbc, xbc, xbc, glu, glu, glu, cg, *weights)


def _cumsum_rows(tri, x):
    hi = x.astype(BF16)
    r1 = x - hi.astype(F32)
    mid = r1.astype(BF16)
    lo = (r1 - mid.astype(F32)).astype(BF16)
    return _dot(tri, hi) + _dot(tri, mid) + _dot(tri, lo)


def _ssd_chunk(c, d, xc_ref, dt_ref, bias, a_row, dskip, y_ref, st_ref):
    q = SSD_CHUNK
    r0 = pl.multiple_of(c * q, q)
    rows = pl.ds(r0, q)
    xs = xc_ref[0, rows, 0:SSD_W]
    bm = xc_ref[0, rows, SSD_W:SSD_W + LANES]
    cm = xc_ref[0, rows, SSD_W + LANES:SSD_W + 2 * LANES]
    dt = _softplus(dt_ref[0, rows, :] + bias)
    ri = lax.broadcasted_iota(jnp.int32, (q, q), 0)
    ci = lax.broadcasted_iota(jnp.int32, (q, q), 1)
    mask = (ri >= ci) if d == 0 else (ri <= ci)
    acum = _cumsum_rows(mask.astype(F32).astype(BF16), dt * a_row)
    acum_t = acum.T
    tot = acum[q - 1:q, :] if d == 0 else acum[0:1, :]
    etot = jnp.exp(tot)
    eac = jnp.exp(acum)
    tail = jnp.exp(tot - acum)
    lo_half = lax.broadcasted_iota(jnp.int32, (q, LANES), 1) < SSD_P
    top_half = lax.broadcasted_iota(jnp.int32, (LANES, SSD_N), 0) < SSD_P
    for g in range(SSD_GROUPS):
        k0 = d * SSD_HEADS + 2 * g
        k1 = k0 + 1

        def sel(mat):
            return jnp.where(lo_half, mat[:, k0:k0 + 1], mat[:, k1:k1 + 1])

        xg = xs[:, g * LANES:(g + 1) * LANES]
        xdt = xg * sel(dt)
        bg = bm[:, g * SSD_N:(g + 1) * SSD_N].astype(BF16)
        cg = cm[:, g * SSD_N:(g + 1) * SSD_N].astype(BF16)
        cb = _dot_nt(cg, bg)
        yg = None
        for hh, k in enumerate((k0, k1)):
            seg = acum[:, k:k + 1] - acum_t[k:k + 1, :]
            dec = jnp.where(mask, jnp.exp(seg), 0.0)
            xm = jnp.where(lo_half if hh == 0 else jnp.logical_not(lo_half), xdt, 0.0)
            part = _dot((cb * dec).astype(BF16), xm.astype(BF16))
            yg = part if yg is None else yg + part
        s_prev = st_ref[0, d, g]
        yg = yg + _dot_nt(cg, s_prev.astype(BF16)) * sel(eac)
        cs = _dot((xdt * sel(tail)).T.astype(BF16), bg)
        cdec = jnp.where(top_half, etot[:, k0:k0 + 1], etot[:, k1:k1 + 1])
        st_ref[0, d, g] = cdec * s_prev + cs
        if d == 0:
            yg = yg + dskip[:, g * LANES:(g + 1) * LANES] * xg
        cols = slice(g * LANES, (g + 1) * LANES)
        y_ref[0, rows, cols] = y_ref[0, rows, cols] + yg


def _ssd_kernel(nc, has_init, *refs):
    if has_init:
        xc_ref, dt_ref, s0_ref, bias_ref, alog_ref, d_ref, y_ref, st_ref = refs
        st_ref[...] = s0_ref[0]
    else:
        xc_ref, dt_ref, bias_ref, alog_ref, d_ref, y_ref, st_ref = refs
        st_ref[...] = jnp.zeros(st_ref.shape, F32)
    lane = lax.broadcasted_iota(jnp.int32, (1, LANES), 1)
    a_row = jnp.where(lane < 2 * SSD_HEADS, -jnp.exp(alog_ref[0]), 0.0)
    bias = bias_ref[0]
    dskip = d_ref[0]
    y_ref[...] = jnp.zeros(y_ref.shape, F32)

    def body(c, carry):
        _ssd_chunk(c, 0, xc_ref, dt_ref, bias, a_row, dskip, y_ref, st_ref)
        _ssd_chunk(nc - 1 - c, 1, xc_ref, dt_ref, bias, a_row, dskip, y_ref, st_ref)
        return carry

    lax.fori_loop(0, nc, body, 0)


def _ssd(xc, dt, s0, l, w):
    b, s, _ = xc.shape
    nc = s // SSD_CHUNK
    st_shape = (b, 2, SSD_GROUPS, 2 * SSD_P, SSD_N)
    st_spec = pl.BlockSpec((1,) + st_shape[1:], lambda bi: (bi, 0, 0, 0, 0))
    in_specs = [pl.BlockSpec((1, s, SSD_CONV_CH), lambda bi: (bi, 0, 0)),
                pl.BlockSpec((1, s, LANES), lambda bi: (bi, 0, 0))]
    args = [xc, dt]
    if s0 is not None:
        in_specs.append(pl.BlockSpec((1, 1) + st_shape[1:], lambda bi: (bi, l, 0, 0, 0, 0)))
        args.append(s0)
    weights = [w['dt_bias'], w['a_log'], w['dskip']]
    vmem = 2 * s * (SSD_CONV_CH + LANES + SSD_W) * 4 + 8 * 1024 * 1024
    return pl.pallas_call(
        functools.partial(_ssd_kernel, nc, s0 is not None),
        out_shape=[jax.ShapeDtypeStruct((b, s, SSD_W), F32), jax.ShapeDtypeStruct(st_shape, F32)],
        grid=(b,),
        in_specs=in_specs + [_layer_spec(a, l, 1) for a in weights],
        out_specs=[pl.BlockSpec((1, s, SSD_W), lambda bi: (bi, 0, 0)), st_spec],
        compiler_params=_params(("parallel",), vmem),
        name="ssd",
    )(*args, *weights)


def _outproj_kernel(oa_ref, ys_ref, z_ref, oc_ref, od_ref, x_ref, mod_ref, ng_ref, w_ref, o_ref):
    ob = _rms(ys_ref[0] * _silu(z_ref[0]), SSD_W) * ng_ref[0]
    acc = _dot(oa_ref[0].astype(BF16), w_ref[0, 0:BRANCH_W, :])
    acc = acc + _dot(ob.astype(BF16), w_ref[0, BRANCH_W:2 * BRANCH_W, :])
    acc = acc + _dot(oc_ref[0].astype(BF16), w_ref[0, 2 * BRANCH_W:3 * BRANCH_W, :])
    acc = acc + _dot(od_ref[0].astype(BF16), w_ref[0, 3 * BRANCH_W:4 * BRANCH_W, :])
    o_ref[0] = x_ref[0] + mod_ref[0, 0, 2:3, :] * acc


def _outproj(oa, ys, z, oc, od, x, mod, mod_row, l, w, tm):
    b, s, d = x.shape
    br = lambda: pl.BlockSpec((1, tm, BRANCH_W), lambda bi, i: (bi, i, 0))
    xs = pl.BlockSpec((1, tm, d), lambda bi, i: (bi, i, 0))
    vmem = 2 * (5 * tm * BRANCH_W * 4 + 2 * tm * d * 4 + d * d * 2) + 2 * tm * d * 4
    return pl.pallas_call(
        _outproj_kernel,
        out_shape=jax.ShapeDtypeStruct((b, s, d), F32),
        grid=(b, s // tm),
        in_specs=[br(), br(), br(), br(), br(), xs, _mod_spec(mod, l, mod_row),
                  _layer_spec(w['ssd_ng'], l, 2), _layer_spec(w['wout'], l, 2)],
        out_specs=xs,
        compiler_params=_params(("parallel", "parallel"), vmem),
        name="outproj",
    )(oa, ys, z, oc, od, x, mod, w['ssd_ng'], w['wout'])


def _rope_table(n_tokens, dim, lane0):
    n_rows = n_tokens // GRID_W
    row = jnp.repeat(jnp.arange(n_rows), GRID_W).astype(F32)
    col = jnp.tile(jnp.arange(GRID_W), n_rows).astype(F32)
    quarter = dim // 4
    inv = ROPE_THETA ** (-jnp.arange(quarter, dtype=F32) / quarter)
    ar = row[:, None] * inv
    ac = col[:, None] * inv
    zero = jnp.zeros_like(ar)
    cos = jnp.concatenate([jnp.cos(ar), jnp.cos(ar), jnp.cos(ac), jnp.cos(ac)], axis=-1)
    sin_a = jnp.concatenate([-jnp.sin(ar), zero, -jnp.sin(ac), zero], axis=-1)
    sin_b = jnp.concatenate([zero, jnp.sin(ar), zero, jnp.sin(ac)], axis=-1)
    hi = LANES - lane0 - dim
    cos = jnp.pad(cos, ((0, 0), (lane0, hi)), constant_values=1.0)
    return jnp.stack([cos, _pad_last(sin_a, lane0, hi), _pad_last(sin_b, lane0, hi)])


def _prepare_weights(w_in, w_out, norm_g, mla_q_norm_g, mla_w_uq, mla_kv_norm_g, mla_w_ukv, mla_q_head_g,
                     mla_k_head_g, gqa_q_g, gqa_k_g, ssd_conv_w, ssd_conv_b, ssd_dt_bias, ssd_a_log, ssd_d,
                     ssd_norm_g, cf_conv_w, cf_conv_b, cf_ln_g, cf_ln_b):
    depth = w_in.shape[0]
    pieces = []
    o = 0
    for sz in SPLITS:
        pieces.append(w_in[:, :, o:o + sz])
        o += sz
    (a_q, a_kv, a_r, a_g, b_xbc, b_dt, b_z, c_v, c_gl, c_g, d_q, d_k, d_v, d_g) = pieces
    win = jnp.concatenate([
        _pad_last(a_q, 0, 256 - MLA_Q_RANK), a_kv, _pad_last(a_r, MLA_NOPE, LANES - MLA_QK), a_g,
        b_xbc, _pad_last(b_dt, 0, LANES - 2 * SSD_HEADS), b_z, c_v, c_gl, c_g,
        _pad_heads(d_q, GQA_HEADS, GQA_HD, LANES), _pad_heads(d_k, GQA_KV_HEADS, GQA_HD, LANES),
        _pad_heads(d_v, GQA_KV_HEADS, GQA_HD, LANES), d_g], axis=-1).astype(BF16)
    wuq = jnp.pad(_pad_heads(mla_w_uq, MLA_HEADS, MLA_QK, LANES),
                  ((0, 0), (0, 256 - MLA_Q_RANK), (0, 0))).astype(BF16)
    kv = mla_w_ukv.reshape(depth, MLA_KV_RANK, MLA_HEADS, MLA_NOPE + MLA_V)
    wk = _pad_last(kv[..., :MLA_NOPE], 0, LANES - MLA_NOPE).reshape(depth, MLA_KV_RANK, -1)
    wv = _pad_last(kv[..., MLA_NOPE:], 0, LANES - MLA_V).reshape(depth, MLA_KV_RANK, -1)
    wukv = jnp.concatenate([wk, wv], axis=-1).astype(BF16)
    row = lambda v: _pad_last(v, 0, 256 - v.shape[-1])
    zero = jnp.zeros((depth, 256), F32)
    gains = jnp.stack([row(mla_q_norm_g),
                       row(mla_q_head_g * (MLA_QK ** -0.5 * LOG2E)),
                       row(mla_kv_norm_g),
                       row(mla_k_head_g),
                       row(gqa_q_g * (GQA_HD ** -0.5 * LOG2E)),
                       row(gqa_k_g), zero, zero], axis=1)
    vec = lambda v: v[:, None, :]
    lanes8 = lambda v: _pad_last(v.reshape(depth, 1, -1), 0, LANES - 2 * SSD_HEADS)
    return dict(win=win, wuq=wuq, wukv=wukv, gains=gains, wout=w_out.astype(BF16), ng=vec(norm_g),
                w3=ssd_conv_w, b3=vec(ssd_conv_b), w31=cf_conv_w, b31=vec(cf_conv_b),
                lng=vec(cf_ln_g), lnb=vec(cf_ln_b), dt_bias=lanes8(ssd_dt_bias), a_log=lanes8(ssd_a_log),
                dskip=vec(jnp.repeat(ssd_d, SSD_P, axis=-1)), ssd_ng=vec(ssd_norm_g))


def _mixer_layer(x, mod, mod_row, l, w, caches, ropes, tm, tq):
    latent = caches is not None
    outs = _inproj(x, mod, mod_row, l, w, ropes, tm)
    qa, ka, va, ag, qd, kd, vd, dg, xbc, dt, z, glu, cg = outs[:13]
    b = x.shape[0]
    ctx_a = ctx_d = s0 = None
    if latent:
        kc, vc, kdc, vdc = _ctx_kv(caches['ckv'], caches['kr_pad'], caches['gk'], caches['gv'], l, w)
        ctx_a, ctx_d, s0 = (kc, vc), (kdc, vdc), caches['state']
    oa = _attention(qa, ka, va, ctx_a, ag, MLA_HEADS, 1, tq, "attn_mla")
    od = _attention(qd, kd, vd, ctx_d, dg, GQA_HEADS, GQA_HEADS // GQA_KV_HEADS, tq, "attn_gqa")
    xconv, oc = _convs(xbc, glu, cg, l, w, min(256, x.shape[1]))
    ys, s_fin = _ssd(xconv, dt, s0, l, w)
    y = _outproj(oa, ys, z, oc, od, x, mod, mod_row, l, w, tm)
    new_ctx = None
    if not latent:
        ckv, kr, gk, gv = outs[13:]
        new_ctx = (ckv, kr, gk.reshape(b, -1, GQA_KV_HEADS, GQA_HD), gv.reshape(b, -1, GQA_KV_HEADS, GQA_HD),
                   s_fin.reshape(b, 2, SSD_HEADS, SSD_P, SSD_N))
    return y, new_ctx


def kernel(x_prompt, x_sample, cache_mla_ckv, cache_mla_krope, cache_gqa_k, cache_gqa_v, state_ssd, c, c_ctx, w_mod, b_mod, norm_g, w_in, w_out, mla_q_norm_g, mla_w_uq, mla_kv_norm_g, mla_w_ukv, mla_q_head_g, mla_k_head_g, ssd_conv_w, ssd_conv_b, ssd_dt_bias, ssd_a_log, ssd_d, ssd_norm_g, cf_conv_w, cf_conv_b, cf_ln_g, cf_ln_b, gqa_q_g, gqa_k_g):
    depth = w_in.shape[0]
    dec_b, n_lat = x_sample.shape[0], x_sample.shape[1]
    seq = x_prompt.shape[1]
    past = cache_mla_ckv.shape[2]
    assert dec_b < 8 and x_prompt.shape[-1] == D_MODEL and w_in.shape[-1] == sum(SPLITS)

    cvec = jnp.concatenate([c, c_ctx[None, :], jnp.zeros((8 - dec_b - 1, D_MODEL), F32)], axis=0)
    mod = _modulation(cvec, w_mod, b_mod).reshape(depth, 8, 3, D_MODEL)
    w = _prepare_weights(w_in, w_out, norm_g, mla_q_norm_g, mla_w_uq, mla_kv_norm_g, mla_w_ukv, mla_q_head_g,
                         mla_k_head_g, gqa_q_g, gqa_k_g, ssd_conv_w, ssd_conv_b, ssd_dt_bias, ssd_a_log, ssd_d,
                         ssd_norm_g, cf_conv_w, cf_conv_b, cf_ln_g, cf_ln_b)

    y = x_prompt
    ctx_out = []
    for l in range(depth):
        y, new_ctx = _mixer_layer(y, mod, dec_b, l, w, None, None, min(256, seq), min(256, seq))
        ctx_out.append(new_ctx)
    new = [jnp.stack([t[i] for t in ctx_out], axis=1) for i in range(5)]

    caches = dict(ckv=cache_mla_ckv,
                  kr_pad=_pad_last(cache_mla_krope, MLA_NOPE, LANES - MLA_QK),
                  gk=cache_gqa_k.reshape(dec_b, depth, past, GQA_KV_HEADS * GQA_HD),
                  gv=cache_gqa_v.reshape(dec_b, depth, past, GQA_KV_HEADS * GQA_HD),
                  state=state_ssd.reshape(dec_b, depth, 2, SSD_GROUPS, 2 * SSD_P, SSD_N))
    ropes = (_rope_table(n_lat, MLA_ROPE, MLA_NOPE), _rope_table(n_lat, GQA_HD, 0))
    z = x_sample
    for l in range(depth):
        z, _ = _mixer_layer(z, mod, None, l, w, caches, ropes, 256, 512)

    return (y, z, new[0], new[1], new[2], new[3], new[4])
```

```python
import functools
import math

import jax
import jax.numpy as jnp
from jax import lax
from jax.experimental import pallas as pl
from jax.experimental.pallas import tpu as pltpu

F32 = jnp.float32
BF16 = jnp.bfloat16

LANES = 128
VMEM_CAP = 56 * 1024 * 1024

EPS = 1e-6
ROPE_THETA = 10000.0
GRID_W = 64
LOG2E = math.log2(math.e)

D_MODEL = 1024
BRANCH_W = 256
MLA_HEADS, MLA_NOPE, MLA_ROPE, MLA_V = 4, 64, 32, 64
MLA_QK = MLA_NOPE + MLA_ROPE
MLA_Q_RANK, MLA_KV_RANK = 192, 128
SSD_HEADS, SSD_P, SSD_N, SSD_GROUPS, SSD_CONV, SSD_CHUNK = 4, 64, 64, 2, 3, 128
SSD_W = SSD_HEADS * SSD_P
SSD_CONV_CH = SSD_W + 2 * SSD_GROUPS * SSD_N
CF_W, CF_K = 256, 31
GQA_HEADS, GQA_KV_HEADS, GQA_HD = 4, 2, 64
SPLITS = (MLA_Q_RANK, MLA_KV_RANK, MLA_ROPE, MLA_HEADS * MLA_V,
          SSD_CONV_CH, 2 * SSD_HEADS, SSD_W,
          CF_W, CF_W, CF_W,
          GQA_HEADS * GQA_HD, GQA_KV_HEADS * GQA_HD, GQA_KV_HEADS * GQA_HD, GQA_HEADS * GQA_HD)

O_QL, O_KV, O_R, O_AG = 0, 256, 384, 512
O_XBC, O_DT, O_Z = 768, 1280, 1408
O_CV, O_CGL, O_CG = 1664, 1920, 2176
O_DQ, O_DK, O_DV, O_DG = 2432, 2944, 3200, 3456
IN_WP = 3712

CONV_HALO_SSD = 8
CONV_HALO_CF = 16

SSD_UNROLL = 2
INPROJ_SUB = 256
ATT_QK_CHUNK = 512
ATT_PV_CHUNK = 256

NT_DIMS = (((1,), (1,)), ((), ()))


def _silu(x):
    return x / (1.0 + jnp.exp(-x))


def _sigmoid(x):
    return 1.0 / (1.0 + jnp.exp(-x))


def _softplus(x):
    return jnp.maximum(x, 0.0) + jnp.log1p(jnp.exp(-jnp.abs(x)))


def _rms(x, n):
    return x * lax.rsqrt(jnp.sum(x * x, axis=-1, keepdims=True) * (1.0 / n) + EPS)


def _dot(a, b):
    return jnp.dot(a, b, preferred_element_type=F32)


def _dot_nt(a, b):
    return lax.dot_general(a, b, NT_DIMS, preferred_element_type=F32)


def _params(sem, vmem_bytes):
    limit = int(min(max(vmem_bytes, 32 * 1024 * 1024), VMEM_CAP))
    return pltpu.CompilerParams(dimension_semantics=sem, vmem_limit_bytes=limit)


def _pad_heads(w, nh, d, dp):
    s = w.shape[:-1]
    w = w.reshape(s + (nh, d))
    w = jnp.pad(w, [(0, 0)] * len(s) + [(0, 0), (0, dp - d)])
    return w.reshape(s + (nh * dp,))


def _pad_last(w, lo, hi):
    return jnp.pad(w, [(0, 0)] * (w.ndim - 1) + [(lo, hi)])


def _layer_spec(a, l, grid_rank):
    zeros = (0,) * (a.ndim - 1)
    if grid_rank == 1:
        return pl.BlockSpec((1,) + a.shape[1:], lambda bi: (l,) + zeros)
    return pl.BlockSpec((1,) + a.shape[1:], lambda bi, i: (l,) + zeros)


def _with_ones_lane(v, hd):
    lane = lax.broadcasted_iota(jnp.int32, v.shape, v.ndim - 1)
    return jnp.where(lane % LANES == hd, 1.0, v)


def _mod_kernel(c_ref, w_ref, b_ref, o_ref):
    c = c_ref[...]
    o_ref[0] = _dot(_silu(c).astype(BF16), w_ref[0].astype(BF16)) + b_ref[0]


def _modulation(cvec, w_mod, b_mod):
    depth, d, d3 = w_mod.shape
    tn = d
    return pl.pallas_call(
        _mod_kernel,
        out_shape=jax.ShapeDtypeStruct((depth, 8, d3), F32),
        grid=(depth, d3 // tn),
        in_specs=[pl.BlockSpec((8, d), lambda l, j: (0, 0)),
                  pl.BlockSpec((1, d, tn), lambda l, j: (l, 0, j)),
                  pl.BlockSpec((1, 1, tn), lambda l, j: (l, 0, j))],
        out_specs=pl.BlockSpec((1, 8, tn), lambda l, j: (l, 0, j)),
        compiler_params=_params(("arbitrary", "arbitrary"), 4 * d * tn * 4),
        name="modulation",
    )(cvec, w_mod, b_mod.reshape(depth, 1, d3))


def _rope(x, tab_ref, rows, shift):
    cos, sin_a, sin_b = tab_ref[0, rows, :], tab_ref[1, rows, :], tab_ref[2, rows, :]
    outs = []
    for h in range(x.shape[-1] // LANES):
        xh = x[:, h * LANES:(h + 1) * LANES]
        up = pltpu.roll(xh, LANES - shift, 1)
        dn = pltpu.roll(xh, shift, 1)
        outs.append(xh * cos + up * sin_a + dn * sin_b)
    return jnp.concatenate(outs, axis=-1)


def _head_norm(x, gain, n, ones_bd):
    xx = (x * x).astype(BF16)
    w2 = 2 * LANES
    ss = jnp.concatenate([_dot(xx[:, j:j + w2], ones_bd) for j in range(0, x.shape[-1], w2)], axis=-1)
    return x * lax.rsqrt(ss * (1.0 / n) + EPS) * gain


def _inproj_rows(latent, rows, refs):
    if latent:
        (x_ref, mod_ref, ng_ref, win_ref, wuq_ref, wukv_ref, gains_ref, ra_ref, rd_ref,
         qa_ref, ka_ref, va_ref, ag_ref, qd_ref, kd_ref, vd_ref, dg_ref,
         xbc_ref, dt_ref, z_ref, glu_ref, cg_ref) = refs
    else:
        (x_ref, mod_ref, ng_ref, win_ref, wuq_ref, wukv_ref, gains_ref,
         qa_ref, ka_ref, va_ref, ag_ref, qd_ref, kd_ref, vd_ref, dg_ref,
         xbc_ref, dt_ref, z_ref, glu_ref, cg_ref,
         ckv_ref, kr_ref, gk_ref, gv_ref) = refs

    x = x_ref[0, rows, :]
    shift = mod_ref[0, 0, 0:1, :]
    scale = mod_ref[0, 0, 1:2, :]
    h = _rms(x, D_MODEL) * ng_ref[0] * (1.0 + scale) + shift
    u = _dot(h.astype(BF16), win_ref[0])

    wa, wd = MLA_HEADS * LANES, GQA_KV_HEADS * LANES
    g_ql = gains_ref[0, 0:1, 0:256]
    g_qh = gains_ref[0, 1:2, :]
    g_kv = gains_ref[0, 2:3, 0:LANES]
    g_kh = gains_ref[0, 3:4, :]
    g_dq = gains_ref[0, 4:5, :]
    g_dk = gains_ref[0, 5:6, 0:wd]
    ri = lax.broadcasted_iota(jnp.int32, (2 * LANES, 2 * LANES), 0) // LANES
    ci = lax.broadcasted_iota(jnp.int32, (2 * LANES, 2 * LANES), 1) // LANES
    ones_bd = (ri == ci).astype(F32).astype(BF16)

    ql = _rms(u[:, O_QL:O_QL + 256], MLA_Q_RANK) * g_ql
    q = _head_norm(_dot(ql.astype(BF16), wuq_ref[0]), g_qh, MLA_QK, ones_bd)
    ckv = _rms(u[:, O_KV:O_KV + LANES], MLA_KV_RANK) * g_kv
    kv = _dot(ckv.astype(BF16), wukv_ref[0])
    kr = u[:, O_R:O_R + LANES]
    k = _head_norm(kv[:, 0:wa] + jnp.concatenate([kr] * MLA_HEADS, axis=-1), g_kh, MLA_QK, ones_bd)
    if latent:
        q = _rope(q, ra_ref, rows, MLA_ROPE // 4)
        k = _rope(k, ra_ref, rows, MLA_ROPE // 4)
    qa_ref[0, rows, :] = q.astype(BF16)
    ka_ref[0, rows, :] = k.astype(BF16)
    va_ref[0, rows, :] = _with_ones_lane(kv[:, wa:], MLA_V).astype(BF16)
    ag_ref[0, rows, :] = u[:, O_AG:O_AG + BRANCH_W]

    qd = _head_norm(u[:, O_DQ:O_DQ + GQA_HEADS * LANES], g_dq, GQA_HD, ones_bd)
    kd = _head_norm(u[:, O_DK:O_DK + wd], g_dk, GQA_HD, ones_bd)
    vd = u[:, O_DV:O_DV + wd]
    if not latent:
        ckv_ref[0, rows, :] = ckv
        kr_ref[0, rows, :] = kr[:, MLA_NOPE:MLA_NOPE + MLA_ROPE]
        for i in range(GQA_KV_HEADS):
            gk_ref[0, rows, i * GQA_HD:(i + 1) * GQA_HD] = kd[:, i * LANES:i * LANES + GQA_HD]
            gv_ref[0, rows, i * GQA_HD:(i + 1) * GQA_HD] = vd[:, i * LANES:i * LANES + GQA_HD]
    else:
        qd = _rope(qd, rd_ref, rows, GQA_HD // 4)
        kd = _rope(kd, rd_ref, rows, GQA_HD // 4)
    qd_ref[0, rows, :] = qd.astype(BF16)
    kd_ref[0, rows, :] = kd.astype(BF16)
    vd_ref[0, rows, :] = _with_ones_lane(vd, GQA_HD).astype(BF16)
    dg_ref[0, rows, :] = u[:, O_DG:O_DG + BRANCH_W]

    xbc_ref[0, rows, :] = u[:, O_XBC:O_XBC + SSD_CONV_CH]
    dt_ref[0, rows, :] = u[:, O_DT:O_DT + LANES]
    z_ref[0, rows, :] = u[:, O_Z:O_Z + SSD_W]
    glu_ref[0, rows, :] = u[:, O_CV:O_CV + CF_W] * _sigmoid(u[:, O_CGL:O_CGL + CF_W])
    cg_ref[0, rows, :] = u[:, O_CG:O_CG + CF_W]


def _inproj_kernel(latent, sub, *refs):
    tm = refs[0].shape[1]
    for r0 in range(0, tm, sub):
        _inproj_rows(latent, slice(r0, r0 + sub), refs)


def _mod_spec(mod, l, row):
    d = mod.shape[-1]
    if row is None:
        return pl.BlockSpec((1, 1, 3, d), lambda bi, i: (l, bi, 0, 0))
    return pl.BlockSpec((1, 1, 3, d), lambda bi, i: (l, row, 0, 0))


def _inproj(x, mod, mod_row, l, w, ropes, tm):
    b, s, d = x.shape
    latent = ropes is not None
    tok = lambda wd: pl.BlockSpec((1, tm, wd), lambda bi, i: (bi, i, 0))
    weights = [w['ng'], w['win'], w['wuq'], w['wukv'], w['gains']]
    in_specs = [tok(d), _mod_spec(mod, l, mod_row)] + [_layer_spec(a, l, 2) for a in weights]
    args = [x, mod] + weights
    if latent:
        in_specs += [pl.BlockSpec((3, tm, LANES), lambda bi, i: (0, i, 0))] * 2
        args += list(ropes)
    widths = [(4 * LANES, BF16), (4 * LANES, BF16), (4 * LANES, BF16), (BRANCH_W, F32),
              (4 * LANES, BF16), (2 * LANES, BF16), (2 * LANES, BF16), (BRANCH_W, F32),
              (SSD_CONV_CH, F32), (LANES, F32), (SSD_W, F32), (CF_W, F32), (CF_W, F32)]
    if not latent:
        widths += [(MLA_KV_RANK, F32), (MLA_ROPE, F32), (LANES, F32), (LANES, F32)]
    out_shape = [jax.ShapeDtypeStruct((b, s, wd), dt) for wd, dt in widths]
    out_specs = [tok(wd) for wd, _ in widths]
    out_bytes = sum(wd * jnp.dtype(dt).itemsize for wd, dt in widths) * tm
    w_bytes = sum(a[0].size * a.dtype.itemsize for a in weights)
    sub = min(INPROJ_SUB, tm)
    vmem = 2 * (tm * d * 4 + w_bytes + out_bytes) + 3 * sub * IN_WP * 4
    return pl.pallas_call(
        functools.partial(_inproj_kernel, latent, sub),
        out_shape=out_shape,
        grid=(b, s // tm),
        in_specs=in_specs,
        out_specs=out_specs,
        compiler_params=_params(("parallel", "parallel"), vmem),
        name="inproj_lat" if latent else "inproj_ctx",
    )(*args)


def _ctxkv_kernel(ckv_ref, kr_ref, gk_ref, gv_ref, wukv_ref, gains_ref, kc_ref, vc_ref, kdc_ref, vdc_ref):
    kv = _dot(ckv_ref[0, 0].astype(BF16), wukv_ref[0])
    kr = kr_ref[0, 0]
    g = gains_ref[0, 3:4, 0:LANES]
    for i in range(MLA_HEADS):
        k = _rms(kv[:, i * LANES:(i + 1) * LANES] + kr, MLA_QK) * g
        kc_ref[0, :, i * LANES:(i + 1) * LANES] = k.astype(BF16)
    vc_ref[0] = _with_ones_lane(kv[:, MLA_HEADS * LANES:], MLA_V).astype(BF16)
    gk = gk_ref[0, 0]
    gv = gv_ref[0, 0]
    p = gk.shape[0]
    pad = jnp.zeros((p, LANES - GQA_HD), F32)
    for i in range(GQA_KV_HEADS):
        lo = i * LANES
        kdc_ref[0, :, lo:lo + GQA_HD] = gk[:, i * GQA_HD:(i + 1) * GQA_HD].astype(BF16)
        kdc_ref[0, :, lo + GQA_HD:lo + LANES] = pad.astype(BF16)
        vdc_ref[0, :, lo:lo + GQA_HD] = gv[:, i * GQA_HD:(i + 1) * GQA_HD].astype(BF16)
        vdc_ref[0, :, lo + GQA_HD:lo + LANES] = _with_ones_lane(pad, 0).astype(BF16)


def _ctx_kv(ckv, kr_pad, gk, gv, l, w):
    b, _, p, _ = ckv.shape
    cache = lambda a: pl.BlockSpec((1, 1, p, a.shape[-1]), lambda bi: (bi, l, 0, 0))
    out = lambda wd: pl.BlockSpec((1, p, wd), lambda bi: (bi, 0, 0))
    wa, wd = MLA_HEADS * LANES, GQA_KV_HEADS * LANES
    return pl.pallas_call(
        _ctxkv_kernel,
        out_shape=[jax.ShapeDtypeStruct((b, p, wa), BF16), jax.ShapeDtypeStruct((b, p, wa), BF16),
                   jax.ShapeDtypeStruct((b, p, wd), BF16), jax.ShapeDtypeStruct((b, p, wd), BF16)],
        grid=(b,),
        in_specs=[cache(ckv), cache(kr_pad), cache(gk), cache(gv),
                  _layer_spec(w['wukv'], l, 1), _layer_spec(w['gains'], l, 1)],
        out_specs=[out(wa), out(wa), out(wd), out(wd)],
        compiler_params=_params(("parallel",), 16 * p * wa * 4),
        name="ctx_kv",
    )(ckv, kr_pad, gk, gv, w['wukv'], w['gains'])


def _lane_tile_max(t):
    r = t[:, 0:LANES]
    for j in range(1, t.shape[-1] // LANES):
        r = jnp.maximum(r, t[:, j * LANES:(j + 1) * LANES])
    return r


def _attn_kernel(n_heads, rep, has_ctx, *refs):
    if has_ctx:
        q_ref, k_ref, v_ref, kc_ref, vc_ref, g_ref, o_ref, s0_ref, s1_ref = refs
        p_len = kc_ref.shape[1]
    else:
        q_ref, k_ref, v_ref, g_ref, o_ref, s0_ref, s1_ref = refs
        p_len = 0
    s_len = k_ref.shape[1]
    hd = o_ref.shape[-1] // n_heads
    kb = min(ATT_QK_CHUNK, s_len)
    kb2 = ATT_PV_CHUNK
    for h in range(n_heads):
        s_ref = s0_ref if h % 2 == 0 else s1_ref
        g = h // rep
        hs = slice(h * LANES, (h + 1) * LANES)
        gs = slice(g * LANES, (g + 1) * LANES)
        qh = q_ref[0, :, hs]
        mx = None
        for c in range(s_len // kb):
            sc = _dot_nt(qh, k_ref[0, c * kb:(c + 1) * kb, gs])
            s_ref[:, c * kb:(c + 1) * kb] = sc
            t = _lane_tile_max(sc)
            mx = t if mx is None else jnp.maximum(mx, t)
        if has_ctx:
            sc = _dot_nt(qh, kc_ref[0, :, gs])
            s_ref[:, s_len:s_len + p_len] = sc
            mx = jnp.maximum(mx, _lane_tile_max(sc))
        m = jnp.max(mx, axis=-1, keepdims=True)
        acc = None
        for c in range((s_len + p_len) // kb2):
            r0 = c * kb2
            p = jnp.exp2(s_ref[:, r0:r0 + kb2] - m).astype(BF16)
            vv = v_ref[0, r0:r0 + kb2, gs] if r0 < s_len else vc_ref[0, r0 - s_len:r0 - s_len + kb2, gs]
            part = _dot(p, vv)
            acc = part if acc is None else acc + part
        o = acc[:, 0:hd] / acc[:, hd:hd + 1]
        os_ = slice(h * hd, (h + 1) * hd)
        o_ref[0, :, os_] = o * _silu(g_ref[0, :, os_])


def _attention(q, k, v, ctx, gate, n_heads, rep, tq, name):
    b, s, qw = q.shape
    kw = k.shape[-1]
    ow = gate.shape[-1]
    has_ctx = ctx is not None
    in_specs = [pl.BlockSpec((1, tq, qw), lambda bi, i: (bi, i, 0)),
                pl.BlockSpec((1, s, kw), lambda bi, i: (bi, 0, 0)),
                pl.BlockSpec((1, s, kw), lambda bi, i: (bi, 0, 0))]
    args = [q, k, v]
    p = 0
    if has_ctx:
        p = ctx[0].shape[1]
        in_specs += [pl.BlockSpec((1, p, kw), lambda bi, i: (bi, 0, 0))] * 2
        args += list(ctx)
    in_specs.append(pl.BlockSpec((1, tq, ow), lambda bi, i: (bi, i, 0)))
    args.append(gate)
    assert s % min(ATT_QK_CHUNK, s) == 0 and s % ATT_PV_CHUNK == 0 and p % ATT_PV_CHUNK == 0
    vmem = 2 * (tq * qw * 2 + 2 * (s + p) * kw * 2 + 2 * tq * ow * 4) + 2 * tq * (s + p) * 4 + 4 * tq * 1024 * 4
    return pl.pallas_call(
        functools.partial(_attn_kernel, n_heads, rep, has_ctx),
        out_shape=jax.ShapeDtypeStruct((b, s, ow), F32),
        grid=(b, s // tq),
        in_specs=in_specs,
        out_specs=pl.BlockSpec((1, tq, ow), lambda bi, i: (bi, i, 0)),
        scratch_shapes=[pltpu.VMEM((tq, s + p), F32), pltpu.VMEM((tq, s + p), F32)],
        compiler_params=_params(("parallel", "arbitrary"), vmem),
        name=name,
    )(*args)


def _conv_kernel(tl, xc_ref, xp_ref, xn_ref, gc_ref, gp_ref, gn_ref, cg_ref,
                 w3_ref, b3_ref, w31_ref, b31_ref, lng_ref, lnb_ref,
                 xo_ref, co_ref, ext3, ext31):
    i = pl.program_id(1)
    keep_prev = (i > 0).astype(F32)
    keep_next = (i < pl.num_programs(1) - 1).astype(F32)

    h3 = CONV_HALO_SSD
    ext3[0:h3, :] = xp_ref[0] * keep_prev
    ext3[h3:h3 + tl, :] = xc_ref[0]
    ext3[h3 + tl:2 * h3 + tl, :] = xn_ref[0] * keep_next
    acc = jnp.zeros((tl, SSD_CONV_CH), F32) + b3_ref[0]
    for k in range(SSD_CONV):
        acc = acc + w3_ref[0, k:k + 1, :] * ext3[h3 - SSD_CONV // 2 + k:h3 - SSD_CONV // 2 + k + tl, :]
    xo_ref[0] = _silu(acc)

    h31 = CONV_HALO_CF
    ext31[0:h31, :] = gp_ref[0] * keep_prev
    ext31[h31:h31 + tl, :] = gc_ref[0]
    ext31[h31 + tl:2 * h31 + tl, :] = gn_ref[0] * keep_next
    acc = jnp.zeros((tl, CF_W), F32) + b31_ref[0]
    o_lo = h31 - CF_K // 2
    ext = ext31[...]
    n_ext = ext.shape[0]
    for r in range(8):
        taps = [k for k in range(CF_K) if (o_lo + k) % 8 == r]
        if not taps:
            continue
        shifted = ext if r == 0 else pltpu.roll(ext, n_ext - r, 0)
        for k in taps:
            a = (o_lo + k) // 8 * 8
            acc = acc + w31_ref[0, k:k + 1, :] * shifted[a:a + tl, :]
    mu = jnp.mean(acc, axis=-1, keepdims=True)
    cen = acc - mu
    var = jnp.mean(cen * cen, axis=-1, keepdims=True)
    y = cen * lax.rsqrt(var + EPS) * lng_ref[0] + lnb_ref[0]
    co_ref[0] = _silu(y) * _silu(cg_ref[0])


def _convs(xbc, glu, cg, l, w, tl):
    b, s, _ = xbc.shape
    nt = s // tl

    def cur(wd):
        return pl.BlockSpec((1, tl, wd), lambda bi, i: (bi, i, 0))

    def prev(wd, hrows):
        r = tl // hrows
        return pl.BlockSpec((1, hrows, wd), lambda bi, i: (bi, jnp.maximum(i * r - 1, 0), 0))

    def nxt(wd, hrows):
        r = tl // hrows
        return pl.BlockSpec((1, hrows, wd), lambda bi, i: (bi, jnp.minimum((i + 1) * r, nt * r - 1), 0))

    weights = [w['w3'], w['b3'], w['w31'], w['b31'], w['lng'], w['lnb']]
    vmem = 4 * tl * (SSD_CONV_CH * 2 + CF_W * 3) * 4 + 6 * tl * SSD_CONV_CH * 4
    return pl.pallas_call(
        functools.partial(_conv_kernel, tl),
        out_shape=[jax.ShapeDtypeStruct((b, s, SSD_CONV_CH), F32),
                   jax.ShapeDtypeStruct((b, s, CF_W), F32)],
        grid=(b, nt),
        in_specs=[cur(SSD_CONV_CH), prev(SSD_CONV_CH, CONV_HALO_SSD), nxt(SSD_CONV_CH, CONV_HALO_SSD),
                  cur(CF_W), prev(CF_W, CONV_HALO_CF), nxt(CF_W, CONV_HALO_CF), cur(CF_W)]
                 + [_layer_spec(a, l, 2) for a in weights],
        out_specs=[cur(SSD_CONV_CH), cur(CF_W)],
        scratch_shapes=[pltpu.VMEM((tl + 2 * CONV_HALO_SSD, SSD_CONV_CH), F32),
                        pltpu.VMEM((tl + 2 * CONV_HALO_CF, CF_W), F32)],
        compiler_params=_params(("parallel", "parallel"), vmem),
        name="convs",
    )(xbc, xbc, xbc, glu, glu, glu, cg, *weights)


def _cumsum_rows(tri, x):
    hi = x.astype(BF16)
    r1 = x - hi.astype(F32)
    mid = r1.astype(BF16)
    lo = (r1 - mid.astype(F32)).astype(BF16)
    return _dot(tri, hi) + _dot(tri, mid) + _dot(tri, lo)


def _ssd_chunk(c, d, xc_ref, dt_ref, bias, a_row, dskip, y_ref, st_ref):
    q = SSD_CHUNK
    r0 = pl.multiple_of(c * q, q)
    rows = pl.ds(r0, q)
    xs = xc_ref[0, rows, 0:SSD_W]
    bm = xc_ref[0, rows, SSD_W:SSD_W + LANES]
    cm = xc_ref[0, rows, SSD_W + LANES:SSD_W + 2 * LANES]
    dt = _softplus(dt_ref[0, rows, :] + bias)
    ri = lax.broadcasted_iota(jnp.int32, (q, q), 0)
    ci = lax.broadcasted_iota(jnp.int32, (q, q), 1)
    mask = (ri >= ci) if d == 0 else (ri <= ci)
    acum = _cumsum_rows(mask.astype(F32).astype(BF16), dt * a_row)
    acum_t = acum.T
    tot = acum[q - 1:q, :] if d == 0 else acum[0:1, :]
    etot = jnp.exp(tot)
    eac = jnp.exp(acum)
    tail = jnp.exp(tot - acum)
    lo_half = lax.broadcasted_iota(jnp.int32, (q, LANES), 1) < SSD_P
    top_half = lax.broadcasted_iota(jnp.int32, (LANES, SSD_N), 0) < SSD_P
    for g in range(SSD_GROUPS):
        k0 = d * SSD_HEADS + 2 * g
        k1 = k0 + 1

        def sel(mat):
            return jnp.where(lo_half, mat[:, k0:k0 + 1], mat[:, k1:k1 + 1])

        xg = xs[:, g * LANES:(g + 1) * LANES]
        xdt = xg * sel(dt)
        bg = bm[:, g * SSD_N:(g + 1) * SSD_N].astype(BF16)
        cg = cm[:, g * SSD_N:(g + 1) * SSD_N].astype(BF16)
        cb = _dot_nt(cg, bg)
        yg = None
        for hh, k in enumerate((k0, k1)):
            seg = acum[:, k:k + 1] - acum_t[k:k + 1, :]
            dec = jnp.where(mask, jnp.exp(seg), 0.0)
            xm = jnp.where(lo_half if hh == 0 else jnp.logical_not(lo_half), xdt, 0.0)
            part = _dot((cb * dec).astype(BF16), xm.astype(BF16))
            yg = part if yg is None else yg + part
        s_prev = st_ref[0, d, g]
        yg = yg + _dot_nt(cg, s_prev.astype(BF16)) * sel(eac)
        cs = _dot((xdt * sel(tail)).T.astype(BF16), bg)
        cdec = jnp.where(top_half, etot[:, k0:k0 + 1], etot[:, k1:k1 + 1])
        st_ref[0, d, g] = cdec * s_prev + cs
        if d == 0:
            yg = yg + dskip[:, g * LANES:(g + 1) * LANES] * xg
        cols = slice(g * LANES, (g + 1) * LANES)
        y_ref[0, rows, cols] = y_ref[0, rows, cols] + yg


def _ssd_kernel(nc, has_init, *refs):
    if has_init:
        xc_ref, dt_ref, s0_ref, bias_ref, alog_ref, d_ref, y_ref, st_ref = refs
        st_ref[...] = s0_ref[0]
    else:
        xc_ref, dt_ref, bias_ref, alog_ref, d_ref, y_ref, st_ref = refs
        st_ref[...] = jnp.zeros(st_ref.shape, F32)
    lane = lax.broadcasted_iota(jnp.int32, (1, LANES), 1)
    a_row = jnp.where(lane < 2 * SSD_HEADS, -jnp.exp(alog_ref[0]), 0.0)
    bias = bias_ref[0]
    dskip = d_ref[0]
    y_ref[...] = jnp.zeros(y_ref.shape, F32)

    def body(i, carry):
        for j in range(SSD_UNROLL):
            c = i * SSD_UNROLL + j
            _ssd_chunk(c, 0, xc_ref, dt_ref, bias, a_row, dskip, y_ref, st_ref)
            _ssd_chunk(nc - 1 - c, 1, xc_ref, dt_ref, bias, a_row, dskip, y_ref, st_ref)
        return carry

    lax.fori_loop(0, nc // SSD_UNROLL, body, 0)


def _ssd(xc, dt, s0, l, w):
    b, s, _ = xc.shape
    nc = s // SSD_CHUNK
    assert nc % SSD_UNROLL == 0
    st_shape = (b, 2, SSD_GROUPS, 2 * SSD_P, SSD_N)
    st_spec = pl.BlockSpec((1,) + st_shape[1:], lambda bi: (bi, 0, 0, 0, 0))
    in_specs = [pl.BlockSpec((1, s, SSD_CONV_CH), lambda bi: (bi, 0, 0)),
                pl.BlockSpec((1, s, LANES), lambda bi: (bi, 0, 0))]
    args = [xc, dt]
    if s0 is not None:
        in_specs.append(pl.BlockSpec((1, 1) + st_shape[1:], lambda bi: (bi, l, 0, 0, 0, 0)))
        args.append(s0)
    weights = [w['dt_bias'], w['a_log'], w['dskip']]
    vmem = 2 * s * (SSD_CONV_CH + LANES + SSD_W) * 4 + 8 * 1024 * 1024
    return pl.pallas_call(
        functools.partial(_ssd_kernel, nc, s0 is not None),
        out_shape=[jax.ShapeDtypeStruct((b, s, SSD_W), F32), jax.ShapeDtypeStruct(st_shape, F32)],
        grid=(b,),
        in_specs=in_specs + [_layer_spec(a, l, 1) for a in weights],
        out_specs=[pl.BlockSpec((1, s, SSD_W), lambda bi: (bi, 0, 0)), st_spec],
        compiler_params=_params(("parallel",), vmem),
        name="ssd",
    )(*args, *weights)


def _outproj_kernel(oa_ref, ys_ref, z_ref, oc_ref, od_ref, x_ref, mod_ref, ng_ref, w_ref, o_ref):
    ob = _rms(ys_ref[0] * _silu(z_ref[0]), SSD_W) * ng_ref[0]
    acc = _dot(oa_ref[0].astype(BF16), w_ref[0, 0:BRANCH_W, :])
    acc = acc + _dot(ob.astype(BF16), w_ref[0, BRANCH_W:2 * BRANCH_W, :])
    acc = acc + _dot(oc_ref[0].astype(BF16), w_ref[0, 2 * BRANCH_W:3 * BRANCH_W, :])
    acc = acc + _dot(od_ref[0].astype(BF16), w_ref[0, 3 * BRANCH_W:4 * BRANCH_W, :])
    o_ref[0] = x_ref[0] + mod_ref[0, 0, 2:3, :] * acc


def _outproj(oa, ys, z, oc, od, x, mod, mod_row, l, w, tm):
    b, s, d = x.shape
    br = lambda: pl.BlockSpec((1, tm, BRANCH_W), lambda bi, i: (bi, i, 0))
    xs = pl.BlockSpec((1, tm, d), lambda bi, i: (bi, i, 0))
    vmem = 2 * (5 * tm * BRANCH_W * 4 + 2 * tm * d * 4 + d * d * 2) + 2 * tm * d * 4
    return pl.pallas_call(
        _outproj_kernel,
        out_shape=jax.ShapeDtypeStruct((b, s, d), F32),
        grid=(b, s // tm),
        in_specs=[br(), br(), br(), br(), br(), xs, _mod_spec(mod, l, mod_row),
                  _layer_spec(w['ssd_ng'], l, 2), _layer_spec(w['wout'], l, 2)],
        out_specs=xs,
        compiler_params=_params(("parallel", "parallel"), vmem),
        name="outproj",
    )(oa, ys, z, oc, od, x, mod, w['ssd_ng'], w['wout'])


def _rope_table(n_tokens, dim, lane0):
    n_rows = n_tokens // GRID_W
    row = jnp.repeat(jnp.arange(n_rows), GRID_W).astype(F32)
    col = jnp.tile(jnp.arange(GRID_W), n_rows).astype(F32)
    quarter = dim // 4
    inv = ROPE_THETA ** (-jnp.arange(quarter, dtype=F32) / quarter)
    ar = row[:, None] * inv
    ac = col[:, None] * inv
    zero = jnp.zeros_like(ar)
    cos = jnp.concatenate([jnp.cos(ar), jnp.cos(ar), jnp.cos(ac), jnp.cos(ac)], axis=-1)
    sin_a = jnp.concatenate([-jnp.sin(ar), zero, -jnp.sin(ac), zero], axis=-1)
    sin_b = jnp.concatenate([zero, jnp.sin(ar), zero, jnp.sin(ac)], axis=-1)
    hi = LANES - lane0 - dim
    cos = jnp.pad(cos, ((0, 0), (lane0, hi)), constant_values=1.0)
    return jnp.stack([cos, _pad_last(sin_a, lane0, hi), _pad_last(sin_b, lane0, hi)])


def _prepare_weights(w_in, w_out, norm_g, mla_q_norm_g, mla_w_uq, mla_kv_norm_g, mla_w_ukv, mla_q_head_g,
                     mla_k_head_g, gqa_q_g, gqa_k_g, ssd_conv_w, ssd_conv_b, ssd_dt_bias, ssd_a_log, ssd_d,
                     ssd_norm_g, cf_conv_w, cf_conv_b, cf_ln_g, cf_ln_b):
    depth = w_in.shape[0]
    pieces = []
    o = 0
    for sz in SPLITS:
        pieces.append(w_in[:, :, o:o + sz])
        o += sz
    (a_q, a_kv, a_r, a_g, b_xbc, b_dt, b_z, c_v, c_gl, c_g, d_q, d_k, d_v, d_g) = pieces
    win = jnp.concatenate([
        _pad_last(a_q, 0, 256 - MLA_Q_RANK), a_kv, _pad_last(a_r, MLA_NOPE, LANES - MLA_QK), a_g,
        b_xbc, _pad_last(b_dt, 0, LANES - 2 * SSD_HEADS), b_z, c_v, c_gl, c_g,
        _pad_heads(d_q, GQA_HEADS, GQA_HD, LANES), _pad_heads(d_k, GQA_KV_HEADS, GQA_HD, LANES),
        _pad_heads(d_v, GQA_KV_HEADS, GQA_HD, LANES), d_g], axis=-1).astype(BF16)
    wuq = jnp.pad(_pad_heads(mla_w_uq, MLA_HEADS, MLA_QK, LANES),
                  ((0, 0), (0, 256 - MLA_Q_RANK), (0, 0))).astype(BF16)
    kv = mla_w_ukv.reshape(depth, MLA_KV_RANK, MLA_HEADS, MLA_NOPE + MLA_V)
    wk = _pad_last(kv[..., :MLA_NOPE], 0, LANES - MLA_NOPE).reshape(depth, MLA_KV_RANK, -1)
    wv = _pad_last(kv[..., MLA_NOPE:], 0, LANES - MLA_V).reshape(depth, MLA_KV_RANK, -1)
    wukv = jnp.concatenate([wk, wv], axis=-1).astype(BF16)
    gw = MLA_HEADS * LANES
    row = lambda v: _pad_last(v, 0, gw - v.shape[-1])
    slots = lambda v, n: row(jnp.tile(_pad_last(v, 0, LANES - v.shape[-1]), (1, n)))
    zero = jnp.zeros((depth, gw), F32)
    gains = jnp.stack([row(mla_q_norm_g),
                       slots(mla_q_head_g * (MLA_QK ** -0.5 * LOG2E), MLA_HEADS),
                       row(mla_kv_norm_g),
                       slots(mla_k_head_g, MLA_HEADS),
                       slots(gqa_q_g * (GQA_HD ** -0.5 * LOG2E), GQA_HEADS),
                       slots(gqa_k_g, GQA_KV_HEADS), zero, zero], axis=1)
    vec = lambda v: v[:, None, :]
    lanes8 = lambda v: _pad_last(v.reshape(depth, 1, -1), 0, LANES - 2 * SSD_HEADS)
    return dict(win=win, wuq=wuq, wukv=wukv, gains=gains, wout=w_out.astype(BF16), ng=vec(norm_g),
                w3=ssd_conv_w, b3=vec(ssd_conv_b), w31=cf_conv_w, b31=vec(cf_conv_b),
                lng=vec(cf_ln_g), lnb=vec(cf_ln_b), dt_bias=lanes8(ssd_dt_bias), a_log=lanes8(ssd_a_log),
                dskip=vec(jnp.repeat(ssd_d, SSD_P, axis=-1)), ssd_ng=vec(ssd_norm_g))


def _mixer_layer(x, mod, mod_row, l, w, caches, ropes, tm, tq):
    latent = caches is not None
    b, s, d = x.shape
    flat = (lambda a: a) if latent else (lambda a: a.reshape(1, b * s, a.shape[-1]))
    unflat = (lambda a: a) if latent else (lambda a: a.reshape(b, s, a.shape[-1]))
    outs = [unflat(o) for o in _inproj(flat(x), mod, mod_row, l, w, ropes, tm)]
    qa, ka, va, ag, qd, kd, vd, dg, xbc, dt, z, glu, cg = outs[:13]
    ctx_a = ctx_d = s0 = None
    if latent:
        kc, vc, kdc, vdc = _ctx_kv(caches['ckv'], caches['kr_pad'], caches['gk'], caches['gv'], l, w)
        ctx_a, ctx_d, s0 = (kc, vc), (kdc, vdc), caches['state']
    oa = _attention(qa, ka, va, ctx_a, ag, MLA_HEADS, 1, tq, "attn_mla")
    od = _attention(qd, kd, vd, ctx_d, dg, GQA_HEADS, GQA_HEADS // GQA_KV_HEADS, tq, "attn_gqa")
    xconv, oc = _convs(xbc, glu, cg, l, w, min(256, x.shape[1]))
    ys, s_fin = _ssd(xconv, dt, s0, l, w)
    y = unflat(_outproj(flat(oa), flat(ys), flat(z), flat(oc), flat(od), flat(x), mod, mod_row, l, w, tm))
    new_ctx = None
    if not latent:
        ckv, kr, gk, gv = outs[13:]
        new_ctx = (ckv, kr, gk.reshape(b, -1, GQA_KV_HEADS, GQA_HD), gv.reshape(b, -1, GQA_KV_HEADS, GQA_HD),
                   s_fin.reshape(b, 2, SSD_HEADS, SSD_P, SSD_N))
    return y, new_ctx


def kernel(x_prompt, x_sample, cache_mla_ckv, cache_mla_krope, cache_gqa_k, cache_gqa_v, state_ssd, c, c_ctx, w_mod, b_mod, norm_g, w_in, w_out, mla_q_norm_g, mla_w_uq, mla_kv_norm_g, mla_w_ukv, mla_q_head_g, mla_k_head_g, ssd_conv_w, ssd_conv_b, ssd_dt_bias, ssd_a_log, ssd_d, ssd_norm_g, cf_conv_w, cf_conv_b, cf_ln_g, cf_ln_b, gqa_q_g, gqa_k_g):
    depth = w_in.shape[0]
    dec_b, n_lat = x_sample.shape[0], x_sample.shape[1]
    seq = x_prompt.shape[1]
    past = cache_mla_ckv.shape[2]
    assert dec_b < 8 and x_prompt.shape[-1] == D_MODEL and w_in.shape[-1] == sum(SPLITS)

    cvec = jnp.concatenate([c, c_ctx[None, :], jnp.zeros((8 - dec_b - 1, D_MODEL), F32)], axis=0)
    mod = _modulation(cvec, w_mod, b_mod).reshape(depth, 8, 3, D_MODEL)
    w = _prepare_weights(w_in, w_out, norm_g, mla_q_norm_g, mla_w_uq, mla_kv_norm_g, mla_w_ukv, mla_q_head_g,
                         mla_k_head_g, gqa_q_g, gqa_k_g, ssd_conv_w, ssd_conv_b, ssd_dt_bias, ssd_a_log, ssd_d,
                         ssd_norm_g, cf_conv_w, cf_conv_b, cf_ln_g, cf_ln_b)

    y = x_prompt
    ctx_out = []
    for l in range(depth):
        y, new_ctx = _mixer_layer(y, mod, dec_b, l, w, None, None, 512, min(256, seq))
        ctx_out.append(new_ctx)
    new = [jnp.stack([t[i] for t in ctx_out], axis=1) for i in range(5)]

    caches = dict(ckv=cache_mla_ckv,
                  kr_pad=_pad_last(cache_mla_krope, MLA_NOPE, LANES - MLA_QK),
                  gk=cache_gqa_k.reshape(dec_b, depth, past, GQA_KV_HEADS * GQA_HD),
                  gv=cache_gqa_v.reshape(dec_b, depth, past, GQA_KV_HEADS * GQA_HD),
                  state=state_ssd.reshape(dec_b, depth, 2, SSD_GROUPS, 2 * SSD_P, SSD_N))
    ropes = (_rope_table(n_lat, MLA_ROPE, MLA_NOPE), _rope_table(n_lat, GQA_HD, 0))
    z = x_sample
    for l in range(depth):
        z, _ = _mixer_layer(z, mod, None, l, w, caches, ropes, 512, 512)

    return (y, z, new[0], new[1], new[2], new[3], new[4])
```

```python
import functools
import math

import jax
import jax.numpy as jnp
from jax import lax
from jax.experimental import pallas as pl
from jax.experimental.pallas import tpu as pltpu

F32 = jnp.float32
BF16 = jnp.bfloat16

LANES = 128
VMEM_CAP = 56 * 1024 * 1024

EPS = 1e-6
ROPE_THETA = 10000.0
GRID_W = 64
LOG2E = math.log2(math.e)

D_MODEL = 1024
BRANCH_W = 256
MLA_HEADS, MLA_NOPE, MLA_ROPE, MLA_V = 4, 64, 32, 64
MLA_QK = MLA_NOPE + MLA_ROPE
MLA_Q_RANK, MLA_KV_RANK = 192, 128
SSD_HEADS, SSD_P, SSD_N, SSD_GROUPS, SSD_CONV, SSD_CHUNK = 4, 64, 64, 2, 3, 128
SSD_W = SSD_HEADS * SSD_P
SSD_CONV_CH = SSD_W + 2 * SSD_GROUPS * SSD_N
CF_W, CF_K = 256, 31
GQA_HEADS, GQA_KV_HEADS, GQA_HD = 4, 2, 64
SPLITS = (MLA_Q_RANK, MLA_KV_RANK, MLA_ROPE, MLA_HEADS * MLA_V,
          SSD_CONV_CH, 2 * SSD_HEADS, SSD_W,
          CF_W, CF_W, CF_W,
          GQA_HEADS * GQA_HD, GQA_KV_HEADS * GQA_HD, GQA_KV_HEADS * GQA_HD, GQA_HEADS * GQA_HD)

O_QL, O_KV, O_R, O_AG = 0, 256, 384, 512
O_XBC, O_DT, O_Z = 768, 1280, 1408
O_CV, O_CGL, O_CG = 1664, 1920, 2176
O_DQ, O_DK, O_DV, O_DG = 2432, 2944, 3200, 3456
IN_WP = 3712

CONV_HALO_SSD = 8
CONV_HALO_CF = 16

SSD_UNROLL = 4
INPROJ_SUB = 256
ATT_QK_CHUNK = 512
ATT_PV_CHUNK = 256

NT_DIMS = (((1,), (1,)), ((), ()))


def _silu(x):
    return x / (1.0 + jnp.exp(-x))


def _sigmoid(x):
    return 1.0 / (1.0 + jnp.exp(-x))


def _softplus(x):
    return jnp.maximum(x, 0.0) + jnp.log1p(jnp.exp(-jnp.abs(x)))


def _rms(x, n):
    return x * lax.rsqrt(jnp.sum(x * x, axis=-1, keepdims=True) * (1.0 / n) + EPS)


def _dot(a, b):
    return jnp.dot(a, b, preferred_element_type=F32)


def _dot_nt(a, b):
    return lax.dot_general(a, b, NT_DIMS, preferred_element_type=F32)


def _params(sem, vmem_bytes):
    limit = int(min(max(vmem_bytes, 32 * 1024 * 1024), VMEM_CAP))
    return pltpu.CompilerParams(dimension_semantics=sem, vmem_limit_bytes=limit)


def _pad_heads(w, nh, d, dp):
    s = w.shape[:-1]
    w = w.reshape(s + (nh, d))
    w = jnp.pad(w, [(0, 0)] * len(s) + [(0, 0), (0, dp - d)])
    return w.reshape(s + (nh * dp,))


def _pad_last(w, lo, hi):
    return jnp.pad(w, [(0, 0)] * (w.ndim - 1) + [(lo, hi)])


def _layer_spec(a, l, grid_rank):
    zeros = (0,) * (a.ndim - 1)
    if grid_rank == 1:
        return pl.BlockSpec((1,) + a.shape[1:], lambda bi: (l,) + zeros)
    return pl.BlockSpec((1,) + a.shape[1:], lambda bi, i: (l,) + zeros)


def _with_ones_lane(v, hd):
    lane = lax.broadcasted_iota(jnp.int32, v.shape, v.ndim - 1)
    return jnp.where(lane % LANES == hd, 1.0, v)


def _mod_kernel(c_ref, w_ref, b_ref, o_ref):
    c = c_ref[...]
    o_ref[0] = _dot(_silu(c).astype(BF16), w_ref[0].astype(BF16)) + b_ref[0]


def _modulation(cvec, w_mod, b_mod):
    depth, d, d3 = w_mod.shape
    tn = d
    return pl.pallas_call(
        _mod_kernel,
        out_shape=jax.ShapeDtypeStruct((depth, 8, d3), F32),
        grid=(depth, d3 // tn),
        in_specs=[pl.BlockSpec((8, d), lambda l, j: (0, 0)),
                  pl.BlockSpec((1, d, tn), lambda l, j: (l, 0, j)),
                  pl.BlockSpec((1, 1, tn), lambda l, j: (l, 0, j))],
        out_specs=pl.BlockSpec((1, 8, tn), lambda l, j: (l, 0, j)),
        compiler_params=_params(("arbitrary", "arbitrary"), 4 * d * tn * 4),
        name="modulation",
    )(cvec, w_mod, b_mod.reshape(depth, 1, d3))


def _rope(x, tab_ref, rows, shift):
    cos, sin_a, sin_b = tab_ref[0, rows, :], tab_ref[1, rows, :], tab_ref[2, rows, :]
    outs = []
    for h in range(x.shape[-1] // LANES):
        xh = x[:, h * LANES:(h + 1) * LANES]
        up = pltpu.roll(xh, LANES - shift, 1)
        dn = pltpu.roll(xh, shift, 1)
        outs.append(xh * cos + up * sin_a + dn * sin_b)
    return jnp.concatenate(outs, axis=-1)


def _head_norm(x, gain, n, ones_bd):
    xx = (x * x).astype(BF16)
    w2 = 2 * LANES
    ss = jnp.concatenate([_dot(xx[:, j:j + w2], ones_bd) for j in range(0, x.shape[-1], w2)], axis=-1)
    return x * lax.rsqrt(ss * (1.0 / n) + EPS) * gain


def _inproj_refs(latent, refs):
    names = ['x', 'mod', 'ng', 'win', 'wuq', 'wukv', 'gains']
    if latent:
        names += ['ra', 'rd']
    names += ['qa', 'ka', 'va', 'ag', 'qd', 'kd', 'vd', 'dg', 'xbc', 'dt', 'z', 'glu', 'cg']
    if not latent:
        names += ['ckv', 'kr', 'gk', 'gv']
    return dict(zip(names, refs))


def _inproj_main(r, rows):
    x = r['x'][0, rows, :]
    shift = r['mod'][0, 0, 0:1, :]
    scale = r['mod'][0, 0, 1:2, :]
    h = _rms(x, D_MODEL) * r['ng'][0] * (1.0 + scale) + shift
    return _dot(h.astype(BF16), r['win'][0])


def _inproj_latents(r, u):
    g_ql = r['gains'][0, 0:1, 0:256]
    g_kv = r['gains'][0, 2:3, 0:LANES]
    ql = _rms(u[:, O_QL:O_QL + 256], MLA_Q_RANK) * g_ql
    ckv = _rms(u[:, O_KV:O_KV + LANES], MLA_KV_RANK) * g_kv
    return _dot(ql.astype(BF16), r['wuq'][0]), ckv, _dot(ckv.astype(BF16), r['wukv'][0])


def _inproj_tail(latent, r, rows, u, q, ckv, kv):
    wa, wd = MLA_HEADS * LANES, GQA_KV_HEADS * LANES
    g_qh = r['gains'][0, 1:2, :]
    g_kh = r['gains'][0, 3:4, :]
    g_dq = r['gains'][0, 4:5, :]
    g_dk = r['gains'][0, 5:6, 0:wd]
    ri = lax.broadcasted_iota(jnp.int32, (2 * LANES, 2 * LANES), 0) // LANES
    ci = lax.broadcasted_iota(jnp.int32, (2 * LANES, 2 * LANES), 1) // LANES
    ones_bd = (ri == ci).astype(F32).astype(BF16)

    q = _head_norm(q, g_qh, MLA_QK, ones_bd)
    kr = u[:, O_R:O_R + LANES]
    k = _head_norm(kv[:, 0:wa] + jnp.concatenate([kr] * MLA_HEADS, axis=-1), g_kh, MLA_QK, ones_bd)
    qd = _head_norm(u[:, O_DQ:O_DQ + GQA_HEADS * LANES], g_dq, GQA_HD, ones_bd)
    kd = _head_norm(u[:, O_DK:O_DK + wd], g_dk, GQA_HD, ones_bd)
    vd = u[:, O_DV:O_DV + wd]
    if latent:
        q = _rope(q, r['ra'], rows, MLA_ROPE // 4)
        k = _rope(k, r['ra'], rows, MLA_ROPE // 4)
        qd = _rope(qd, r['rd'], rows, GQA_HD // 4)
        kd_att = _rope(kd, r['rd'], rows, GQA_HD // 4)
    else:
        kd_att = kd
        r['ckv'][0, rows, :] = ckv
        r['kr'][0, rows, :] = kr[:, MLA_NOPE:MLA_NOPE + MLA_ROPE]
        for i in range(GQA_KV_HEADS):
            r['gk'][0, rows, i * GQA_HD:(i + 1) * GQA_HD] = kd[:, i * LANES:i * LANES + GQA_HD]
            r['gv'][0, rows, i * GQA_HD:(i + 1) * GQA_HD] = vd[:, i * LANES:i * LANES + GQA_HD]
    r['qa'][0, rows, :] = q.astype(BF16)
    r['ka'][0, rows, :] = k.astype(BF16)
    r['va'][0, rows, :] = _with_ones_lane(kv[:, wa:], MLA_V).astype(BF16)
    r['ag'][0, rows, :] = u[:, O_AG:O_AG + BRANCH_W]
    r['qd'][0, rows, :] = qd.astype(BF16)
    r['kd'][0, rows, :] = kd_att.astype(BF16)
    r['vd'][0, rows, :] = _with_ones_lane(vd, GQA_HD).astype(BF16)
    r['dg'][0, rows, :] = u[:, O_DG:O_DG + BRANCH_W]

    r['xbc'][0, rows, :] = u[:, O_XBC:O_XBC + SSD_CONV_CH]
    r['dt'][0, rows, :] = u[:, O_DT:O_DT + LANES]
    r['z'][0, rows, :] = u[:, O_Z:O_Z + SSD_W]
    r['glu'][0, rows, :] = u[:, O_CV:O_CV + CF_W] * _sigmoid(u[:, O_CGL:O_CGL + CF_W])
    r['cg'][0, rows, :] = u[:, O_CG:O_CG + CF_W]


def _inproj_kernel(latent, sub, *refs):
    r = _inproj_refs(latent, refs)
    tm = r['x'].shape[1]
    for r0 in range(0, tm, sub):
        rows = slice(r0, r0 + sub)
        u = _inproj_main(r, rows)
        q, ckv, kv = _inproj_latents(r, u)
        _inproj_tail(latent, r, rows, u, q, ckv, kv)


def _mod_spec(mod, l, row):
    d = mod.shape[-1]
    if row is None:
        return pl.BlockSpec((1, 1, 3, d), lambda bi, i: (l, bi, 0, 0))
    return pl.BlockSpec((1, 1, 3, d), lambda bi, i: (l, row, 0, 0))


def _inproj(x, mod, mod_row, l, w, ropes, tm):
    b, s, d = x.shape
    latent = ropes is not None
    tok = lambda wd: pl.BlockSpec((1, tm, wd), lambda bi, i: (bi, i, 0))
    weights = [w['ng'], w['win'], w['wuq'], w['wukv'], w['gains']]
    in_specs = [tok(d), _mod_spec(mod, l, mod_row)] + [_layer_spec(a, l, 2) for a in weights]
    args = [x, mod] + weights
    if latent:
        in_specs += [pl.BlockSpec((3, tm, LANES), lambda bi, i: (0, i, 0))] * 2
        args += list(ropes)
    widths = [(4 * LANES, BF16), (4 * LANES, BF16), (4 * LANES, BF16), (BRANCH_W, F32),
              (4 * LANES, BF16), (2 * LANES, BF16), (2 * LANES, BF16), (BRANCH_W, F32),
              (SSD_CONV_CH, F32), (LANES, F32), (SSD_W, F32), (CF_W, F32), (CF_W, F32)]
    if not latent:
        widths += [(MLA_KV_RANK, F32), (MLA_ROPE, F32), (LANES, F32), (LANES, F32)]
    out_shape = [jax.ShapeDtypeStruct((b, s, wd), dt) for wd, dt in widths]
    out_specs = [tok(wd) for wd, _ in widths]
    out_bytes = sum(wd * jnp.dtype(dt).itemsize for wd, dt in widths) * tm
    w_bytes = sum(a[0].size * a.dtype.itemsize for a in weights)
    sub = min(INPROJ_SUB, tm)
    vmem = 2 * (tm * d * 4 + w_bytes + out_bytes) + 3 * sub * IN_WP * 4
    return pl.pallas_call(
        functools.partial(_inproj_kernel, latent, sub),
        out_shape=out_shape,
        grid=(b, s // tm),
        in_specs=in_specs,
        out_specs=out_specs,
        compiler_params=_params(("parallel", "parallel"), vmem),
        name="inproj_lat" if latent else "inproj_ctx",
    )(*args)


def _ctxkv_kernel(ckv_ref, kr_ref, gk_ref, gv_ref, wukv_ref, gains_ref, kc_ref, vc_ref, kdc_ref, vdc_ref):
    kv = _dot(ckv_ref[0, 0].astype(BF16), wukv_ref[0])
    kr = kr_ref[0, 0]
    g = gains_ref[0, 3:4, 0:LANES]
    for i in range(MLA_HEADS):
        k = _rms(kv[:, i * LANES:(i + 1) * LANES] + kr, MLA_QK) * g
        kc_ref[0, :, i * LANES:(i + 1) * LANES] = k.astype(BF16)
    vc_ref[0] = _with_ones_lane(kv[:, MLA_HEADS * LANES:], MLA_V).astype(BF16)
    gk = gk_ref[0, 0]
    gv = gv_ref[0, 0]
    p = gk.shape[0]
    pad = jnp.zeros((p, LANES - GQA_HD), F32)
    for i in range(GQA_KV_HEADS):
        lo = i * LANES
        kdc_ref[0, :, lo:lo + GQA_HD] = gk[:, i * GQA_HD:(i + 1) * GQA_HD].astype(BF16)
        kdc_ref[0, :, lo + GQA_HD:lo + LANES] = pad.astype(BF16)
        vdc_ref[0, :, lo:lo + GQA_HD] = gv[:, i * GQA_HD:(i + 1) * GQA_HD].astype(BF16)
        vdc_ref[0, :, lo + GQA_HD:lo + LANES] = _with_ones_lane(pad, 0).astype(BF16)


def _ctx_kv(ckv, kr_pad, gk, gv, l, w):
    b, _, p, _ = ckv.shape
    cache = lambda a: pl.BlockSpec((1, 1, p, a.shape[-1]), lambda bi: (bi, l, 0, 0))
    out = lambda wd: pl.BlockSpec((1, p, wd), lambda bi: (bi, 0, 0))
    wa, wd = MLA_HEADS * LANES, GQA_KV_HEADS * LANES
    return pl.pallas_call(
        _ctxkv_kernel,
        out_shape=[jax.ShapeDtypeStruct((b, p, wa), BF16), jax.ShapeDtypeStruct((b, p, wa), BF16),
                   jax.ShapeDtypeStruct((b, p, wd), BF16), jax.ShapeDtypeStruct((b, p, wd), BF16)],
        grid=(b,),
        in_specs=[cache(ckv), cache(kr_pad), cache(gk), cache(gv),
                  _layer_spec(w['wukv'], l, 1), _layer_spec(w['gains'], l, 1)],
        out_specs=[out(wa), out(wa), out(wd), out(wd)],
        compiler_params=_params(("parallel",), 16 * p * wa * 4),
        name="ctx_kv",
    )(ckv, kr_pad, gk, gv, w['wukv'], w['gains'])


def _lane_tile_max(t):
    r = t[:, 0:LANES]
    for j in range(1, t.shape[-1] // LANES):
        r = jnp.maximum(r, t[:, j * LANES:(j + 1) * LANES])
    return r


def _attn_kernel(n_heads, rep, has_ctx, *refs):
    if has_ctx:
        q_ref, k_ref, v_ref, kc_ref, vc_ref, g_ref, o_ref, s0_ref, s1_ref = refs
        p_len = kc_ref.shape[1]
    else:
        q_ref, k_ref, v_ref, g_ref, o_ref, s0_ref, s1_ref = refs
        p_len = 0
    s_len = k_ref.shape[1]
    hd = o_ref.shape[-1] // n_heads
    kb = min(ATT_QK_CHUNK, s_len)
    kb2 = ATT_PV_CHUNK
    for h in range(n_heads):
        s_ref = s0_ref if h % 2 == 0 else s1_ref
        g = h // rep
        hs = slice(h * LANES, (h + 1) * LANES)
        gs = slice(g * LANES, (g + 1) * LANES)
        qh = q_ref[0, :, hs]
        mx = None
        for c in range(s_len // kb):
            sc = _dot_nt(qh, k_ref[0, c * kb:(c + 1) * kb, gs])
            s_ref[:, c * kb:(c + 1) * kb] = sc
            t = _lane_tile_max(sc)
            mx = t if mx is None else jnp.maximum(mx, t)
        if has_ctx:
            sc = _dot_nt(qh, kc_ref[0, :, gs])
            s_ref[:, s_len:s_len + p_len] = sc
            mx = jnp.maximum(mx, _lane_tile_max(sc))
        m = jnp.max(mx, axis=-1, keepdims=True)
        acc = None
        for c in range((s_len + p_len) // kb2):
            r0 = c * kb2
            p = jnp.exp2(s_ref[:, r0:r0 + kb2] - m).astype(BF16)
            vv = v_ref[0, r0:r0 + kb2, gs] if r0 < s_len else vc_ref[0, r0 - s_len:r0 - s_len + kb2, gs]
            part = _dot(p, vv)
            acc = part if acc is None else acc + part
        o = acc[:, 0:hd] / acc[:, hd:hd + 1]
        os_ = slice(h * hd, (h + 1) * hd)
        o_ref[0, :, os_] = o * _silu(g_ref[0, :, os_])


def _attention(q, k, v, ctx, gate, n_heads, rep, tq, name):
    b, s, qw = q.shape
    kw = k.shape[-1]
    ow = gate.shape[-1]
    has_ctx = ctx is not None
    in_specs = [pl.BlockSpec((1, tq, qw), lambda bi, i: (bi, i, 0)),
                pl.BlockSpec((1, s, kw), lambda bi, i: (bi, 0, 0)),
                pl.BlockSpec((1, s, kw), lambda bi, i: (bi, 0, 0))]
    args = [q, k, v]
    p = 0
    if has_ctx:
        p = ctx[0].shape[1]
        in_specs += [pl.BlockSpec((1, p, kw), lambda bi, i: (bi, 0, 0))] * 2
        args += list(ctx)
    in_specs.append(pl.BlockSpec((1, tq, ow), lambda bi, i: (bi, i, 0)))
    args.append(gate)
    assert s % min(ATT_QK_CHUNK, s) == 0 and s % ATT_PV_CHUNK == 0 and p % ATT_PV_CHUNK == 0
    vmem = 2 * (tq * qw * 2 + 2 * (s + p) * kw * 2 + 2 * tq * ow * 4) + 2 * tq * (s + p) * 4 + 4 * tq * 1024 * 4
    return pl.pallas_call(
        functools.partial(_attn_kernel, n_heads, rep, has_ctx),
        out_shape=jax.ShapeDtypeStruct((b, s, ow), F32),
        grid=(b, s // tq),
        in_specs=in_specs,
        out_specs=pl.BlockSpec((1, tq, ow), lambda bi, i: (bi, i, 0)),
        scratch_shapes=[pltpu.VMEM((tq, s + p), F32), pltpu.VMEM((tq, s + p), F32)],
        compiler_params=_params(("parallel", "arbitrary"), vmem),
        name=name,
    )(*args)


def _conv_kernel(tl, xc_ref, xp_ref, xn_ref, gc_ref, gp_ref, gn_ref, cg_ref,
                 w3_ref, b3_ref, w31_ref, b31_ref, lng_ref, lnb_ref,
                 xo_ref, co_ref, ext3, ext31):
    i = pl.program_id(1)
    keep_prev = (i > 0).astype(F32)
    keep_next = (i < pl.num_programs(1) - 1).astype(F32)

    h3 = CONV_HALO_SSD
    ext3[0:h3, :] = xp_ref[0] * keep_prev
    ext3[h3:h3 + tl, :] = xc_ref[0]
    ext3[h3 + tl:2 * h3 + tl, :] = xn_ref[0] * keep_next
    acc = jnp.zeros((tl, SSD_CONV_CH), F32) + b3_ref[0]
    for k in range(SSD_CONV):
        acc = acc + w3_ref[0, k:k + 1, :] * ext3[h3 - SSD_CONV // 2 + k:h3 - SSD_CONV // 2 + k + tl, :]
    xo_ref[0] = _silu(acc)

    h31 = CONV_HALO_CF
    ext31[0:h31, :] = gp_ref[0] * keep_prev
    ext31[h31:h31 + tl, :] = gc_ref[0]
    ext31[h31 + tl:2 * h31 + tl, :] = gn_ref[0] * keep_next
    acc = jnp.zeros((tl, CF_W), F32) + b31_ref[0]
    o_lo = h31 - CF_K // 2
    ext = ext31[...]
    n_ext = ext.shape[0]
    for r in range(8):
        taps = [k for k in range(CF_K) if (o_lo + k) % 8 == r]
        if not taps:
            continue
        shifted = ext if r == 0 else pltpu.roll(ext, n_ext - r, 0)
        for k in taps:
            a = (o_lo + k) // 8 * 8
            acc = acc + w31_ref[0, k:k + 1, :] * shifted[a:a + tl, :]
    mu = jnp.mean(acc, axis=-1, keepdims=True)
    cen = acc - mu
    var = jnp.mean(cen * cen, axis=-1, keepdims=True)
    y = cen * lax.rsqrt(var + EPS) * lng_ref[0] + lnb_ref[0]
    co_ref[0] = _silu(y) * _silu(cg_ref[0])


def _convs(xbc, glu, cg, l, w, tl):
    b, s, _ = xbc.shape
    nt = s // tl

    def cur(wd):
        return pl.BlockSpec((1, tl, wd), lambda bi, i: (bi, i, 0))

    def prev(wd, hrows):
        r = tl // hrows
        return pl.BlockSpec((1, hrows, wd), lambda bi, i: (bi, jnp.maximum(i * r - 1, 0), 0))

    def nxt(wd, hrows):
        r = tl // hrows
        return pl.BlockSpec((1, hrows, wd), lambda bi, i: (bi, jnp.minimum((i + 1) * r, nt * r - 1), 0))

    weights = [w['w3'], w['b3'], w['w31'], w['b31'], w['lng'], w['lnb']]
    vmem = 4 * tl * (SSD_CONV_CH * 2 + CF_W * 3) * 4 + 6 * tl * SSD_CONV_CH * 4
    return pl.pallas_call(
        functools.partial(_conv_kernel, tl),
        out_shape=[jax.ShapeDtypeStruct((b, s, SSD_CONV_CH), F32),
                   jax.ShapeDtypeStruct((b, s, CF_W), F32)],
        grid=(b, nt),
        in_specs=[cur(SSD_CONV_CH), prev(SSD_CONV_CH, CONV_HALO_SSD), nxt(SSD_CONV_CH, CONV_HALO_SSD),
                  cur(CF_W), prev(CF_W, CONV_HALO_CF), nxt(CF_W, CONV_HALO_CF), cur(CF_W)]
                 + [_layer_spec(a, l, 2) for a in weights],
        out_specs=[cur(SSD_CONV_CH), cur(CF_W)],
        scratch_shapes=[pltpu.VMEM((tl + 2 * CONV_HALO_SSD, SSD_CONV_CH), F32),
                        pltpu.VMEM((tl + 2 * CONV_HALO_CF, CF_W), F32)],
        compiler_params=_params(("parallel", "parallel"), vmem),
        name="convs",
    )(xbc, xbc, xbc, glu, glu, glu, cg, *weights)


def _cumsum_rows(tri, x):
    hi = x.astype(BF16)
    r1 = x - hi.astype(F32)
    mid = r1.astype(BF16)
    lo = (r1 - mid.astype(F32)).astype(BF16)
    return _dot(tri, hi) + _dot(tri, mid) + _dot(tri, lo)


SSD_NK = 2 * SSD_HEADS


def _ssd_stage_local(c, d, xc_ref, dt_ref, bias, a_row):
    q = SSD_CHUNK
    rows = pl.ds(pl.multiple_of(c * q, q), q)
    t = dict(rows=rows, d=d)
    t['xs'] = xc_ref[0, rows, 0:SSD_W]
    t['bm'] = xc_ref[0, rows, SSD_W:SSD_W + LANES]
    t['cm'] = xc_ref[0, rows, SSD_W + LANES:SSD_W + 2 * LANES].astype(BF16)
    t['dt'] = _softplus(dt_ref[0, rows, :] + bias)
    ri = lax.broadcasted_iota(jnp.int32, (q, q), 0)
    ci = lax.broadcasted_iota(jnp.int32, (q, q), 1)
    t['mask'] = (ri >= ci) if d == 0 else (ri <= ci)
    t['acum'] = _cumsum_rows(t['mask'].astype(F32).astype(BF16), t['dt'] * a_row)
    return t


def _ssd_stage_cb(t):
    q = SSD_CHUNK
    lane = lax.broadcasted_iota(jnp.int32, (q, LANES), 1)
    t['tr'] = jnp.where(lane < SSD_NK, t['dt'], t['acum']).T
    bt = t['bm'].T
    t['bt'] = bt
    grp_row = lax.broadcasted_iota(jnp.int32, (LANES, q), 0) < SSD_N
    t['cb'] = [_dot(t['cm'], jnp.where(grp_row if g == 0 else jnp.logical_not(grp_row), bt, 0.0).astype(BF16))
               for g in range(SSD_GROUPS)]
    t['last'] = q - 1 if t['d'] == 0 else 0
    t['etot'] = jnp.exp(t['acum'][t['last']:t['last'] + 1, :])
    t['eac'] = jnp.exp(t['acum'])


def _ssd_stage_diag(t, dskip):
    q = SSD_CHUNK
    d = t['d']
    lo_half = lax.broadcasted_iota(jnp.int32, (q, LANES), 1) < SSD_P
    tr, acum, last = t['tr'], t['acum'], t['last']
    t['yd'], t['cs'] = [], []
    for g in range(SSD_GROUPS):
        k0 = d * SSD_HEADS + 2 * g
        xg = t['xs'][:, g * LANES:(g + 1) * LANES]
        xgb = xg.astype(BF16)
        btg = t['bt'][g * SSD_N:(g + 1) * SSD_N, :]
        yg = None
        cs = []
        for hh in range(2):
            k = k0 + hh
            dt_row = tr[k:k + 1, :]
            ac_row = tr[SSD_NK + k:SSD_NK + k + 1, :]
            dec = jnp.where(t['mask'], jnp.exp(acum[:, k:k + 1] - ac_row), 0.0)
            xm = jnp.where(lo_half if hh == 0 else jnp.logical_not(lo_half), xg, 0.0).astype(BF16)
            part = _dot((t['cb'][g] * dec * dt_row).astype(BF16), xm)
            yg = part if yg is None else yg + part
            w_row = dt_row * jnp.exp(tr[SSD_NK + k:SSD_NK + k + 1, last:last + 1] - ac_row)
            cs.append(_dot((btg * w_row).astype(BF16), xgb))
        if d == 0:
            yg = yg + dskip[:, g * LANES:(g + 1) * LANES] * xg
        t['yd'].append(yg)
        t['cs'].append(cs)


def _ssd_stage_state(t, states):
    q = SSD_CHUNK
    d = t['d']
    lo_half = lax.broadcasted_iota(jnp.int32, (q, LANES), 1) < SSD_P
    lane_n = lax.broadcasted_iota(jnp.int32, (SSD_N, LANES), 1) < SSD_P
    zero = jnp.zeros((SSD_N, LANES), F32)
    ys, new_states = [], []
    for g in range(SSD_GROUPS):
        k0 = d * SSD_HEADS + 2 * g
        s_prev = states[g]
        sz = jnp.concatenate([s_prev, zero] if g == 0 else [zero, s_prev], axis=0).astype(BF16)
        eacs = jnp.where(lo_half, t['eac'][:, k0:k0 + 1], t['eac'][:, k0 + 1:k0 + 2])
        ys.append(t['yd'][g] + _dot(t['cm'], sz) * eacs)
        cdec = jnp.where(lane_n, t['etot'][:, k0:k0 + 1], t['etot'][:, k0 + 1:k0 + 2])
        new_states.append(cdec * s_prev + jnp.where(lane_n, t['cs'][g][0], t['cs'][g][1]))
    return ys, new_states


def _ssd_kernel(nc, unroll, has_init, *refs):
    if has_init:
        xc_ref, dt_ref, s0_ref, bias_ref, alog_ref, d_ref, yf_ref, yb_ref, st_ref = refs
        st_ref[...] = s0_ref[0]
    else:
        xc_ref, dt_ref, bias_ref, alog_ref, d_ref, yf_ref, yb_ref, st_ref = refs
        st_ref[...] = jnp.zeros(st_ref.shape, F32)
    lane = lax.broadcasted_iota(jnp.int32, (1, LANES), 1)
    a_row = jnp.where(lane < 2 * SSD_NK, -jnp.exp(alog_ref[0]), 0.0)
    bias = bias_ref[0]
    dskip = d_ref[0]

    def body(i, carry):
        states = [[st_ref[0, d, g] for g in range(SSD_GROUPS)] for d in range(2)]
        work = []
        for j in range(unroll):
            c = i * unroll + j
            work.append((_ssd_stage_local(c, 0, xc_ref, dt_ref, bias, a_row), yf_ref))
            work.append((_ssd_stage_local(nc - 1 - c, 1, xc_ref, dt_ref, bias, a_row), yb_ref))
        for t, _ in work:
            _ssd_stage_cb(t)
        for t, _ in work:
            _ssd_stage_diag(t, dskip)
        writes = []
        for t, y_ref in work:
            ys, states[t['d']] = _ssd_stage_state(t, states[t['d']])
            writes.append((y_ref, t['rows'], ys))
        for y_ref, rows, ys in writes:
            for g in range(SSD_GROUPS):
                y_ref[0, rows, g * LANES:(g + 1) * LANES] = ys[g]
        for d in range(2):
            for g in range(SSD_GROUPS):
                st_ref[0, d, g] = states[d][g]
        return carry

    lax.fori_loop(0, nc // unroll, body, 0)


def _ssd(xc, dt, s0, l, w):
    b, s, _ = xc.shape
    nc = s // SSD_CHUNK
    unroll = min(SSD_UNROLL, nc)
    assert nc % unroll == 0
    st_shape = (b, 2, SSD_GROUPS, SSD_N, 2 * SSD_P)
    st_spec = pl.BlockSpec((1,) + st_shape[1:], lambda bi: (bi, 0, 0, 0, 0))
    y_spec = pl.BlockSpec((1, s, SSD_W), lambda bi: (bi, 0, 0))
    in_specs = [pl.BlockSpec((1, s, SSD_CONV_CH), lambda bi: (bi, 0, 0)),
                pl.BlockSpec((1, s, LANES), lambda bi: (bi, 0, 0))]
    args = [xc, dt]
    if s0 is not None:
        in_specs.append(pl.BlockSpec((1, 1) + st_shape[1:], lambda bi: (bi, l, 0, 0, 0, 0)))
        args.append(s0)
    weights = [w['dt_bias'], w['a_log'], w['dskip']]
    vmem = 2 * s * (SSD_CONV_CH + LANES + 2 * SSD_W) * 4 + 8 * 1024 * 1024
    return pl.pallas_call(
        functools.partial(_ssd_kernel, nc, unroll, s0 is not None),
        out_shape=[jax.ShapeDtypeStruct((b, s, SSD_W), F32), jax.ShapeDtypeStruct((b, s, SSD_W), F32),
                   jax.ShapeDtypeStruct(st_shape, F32)],
        grid=(b,),
        in_specs=in_specs + [_layer_spec(a, l, 1) for a in weights],
        out_specs=[y_spec, y_spec, st_spec],
        compiler_params=_params(("parallel",), vmem),
        name="ssd",
    )(*args, *weights)


def _outproj_kernel(oa_ref, yf_ref, yb_ref, z_ref, oc_ref, od_ref, x_ref, mod_ref, ng_ref, w_ref, o_ref):
    ob = _rms((yf_ref[0] + yb_ref[0]) * _silu(z_ref[0]), SSD_W) * ng_ref[0]
    acc = _dot(oa_ref[0].astype(BF16), w_ref[0, 0:BRANCH_W, :])
    acc = acc + _dot(ob.astype(BF16), w_ref[0, BRANCH_W:2 * BRANCH_W, :])
    acc = acc + _dot(oc_ref[0].astype(BF16), w_ref[0, 2 * BRANCH_W:3 * BRANCH_W, :])
    acc = acc + _dot(od_ref[0].astype(BF16), w_ref[0, 3 * BRANCH_W:4 * BRANCH_W, :])
    o_ref[0] = x_ref[0] + mod_ref[0, 0, 2:3, :] * acc


def _outproj(oa, yf, yb, z, oc, od, x, mod, mod_row, l, w, tm):
    b, s, d = x.shape
    br = lambda: pl.BlockSpec((1, tm, BRANCH_W), lambda bi, i: (bi, i, 0))
    xs = pl.BlockSpec((1, tm, d), lambda bi, i: (bi, i, 0))
    vmem = 2 * (6 * tm * BRANCH_W * 4 + 2 * tm * d * 4 + d * d * 2) + 2 * tm * d * 4
    return pl.pallas_call(
        _outproj_kernel,
        out_shape=jax.ShapeDtypeStruct((b, s, d), F32),
        grid=(b, s // tm),
        in_specs=[br(), br(), br(), br(), br(), br(), xs, _mod_spec(mod, l, mod_row),
                  _layer_spec(w['ssd_ng'], l, 2), _layer_spec(w['wout'], l, 2)],
        out_specs=xs,
        compiler_params=_params(("parallel", "parallel"), vmem),
        name="outproj",
    )(oa, yf, yb, z, oc, od, x, mod, w['ssd_ng'], w['wout'])


def _rope_table(n_tokens, dim, lane0):
    n_rows = n_tokens // GRID_W
    row = jnp.repeat(jnp.arange(n_rows), GRID_W).astype(F32)
    col = jnp.tile(jnp.arange(GRID_W), n_rows).astype(F32)
    quarter = dim // 4
    inv = ROPE_THETA ** (-jnp.arange(quarter, dtype=F32) / quarter)
    ar = row[:, None] * inv
    ac = col[:, None] * inv
    zero = jnp.zeros_like(ar)
    cos = jnp.concatenate([jnp.cos(ar), jnp.cos(ar), jnp.cos(ac), jnp.cos(ac)], axis=-1)
    sin_a = jnp.concatenate([-jnp.sin(ar), zero, -jnp.sin(ac), zero], axis=-1)
    sin_b = jnp.concatenate([zero, jnp.sin(ar), zero, jnp.sin(ac)], axis=-1)
    hi = LANES - lane0 - dim
    cos = jnp.pad(cos, ((0, 0), (lane0, hi)), constant_values=1.0)
    return jnp.stack([cos, _pad_last(sin_a, lane0, hi), _pad_last(sin_b, lane0, hi)])


def _prepare_weights(w_in, w_out, norm_g, mla_q_norm_g, mla_w_uq, mla_kv_norm_g, mla_w_ukv, mla_q_head_g,
                     mla_k_head_g, gqa_q_g, gqa_k_g, ssd_conv_w, ssd_conv_b, ssd_dt_bias, ssd_a_log, ssd_d,
                     ssd_norm_g, cf_conv_w, cf_conv_b, cf_ln_g, cf_ln_b):
    depth = w_in.shape[0]
    pieces = []
    o = 0
    for sz in SPLITS:
        pieces.append(w_in[:, :, o:o + sz])
        o += sz
    (a_q, a_kv, a_r, a_g, b_xbc, b_dt, b_z, c_v, c_gl, c_g, d_q, d_k, d_v, d_g) = pieces
    win = jnp.concatenate([
        _pad_last(a_q, 0, 256 - MLA_Q_RANK), a_kv, _pad_last(a_r, MLA_NOPE, LANES - MLA_QK), a_g,
        b_xbc, _pad_last(jnp.concatenate([b_dt, b_dt], axis=-1), 0, LANES - 2 * SSD_NK), b_z, c_v, c_gl, c_g,
        _pad_heads(d_q, GQA_HEADS, GQA_HD, LANES), _pad_heads(d_k, GQA_KV_HEADS, GQA_HD, LANES),
        _pad_heads(d_v, GQA_KV_HEADS, GQA_HD, LANES), d_g], axis=-1).astype(BF16)
    wuq = jnp.pad(_pad_heads(mla_w_uq, MLA_HEADS, MLA_QK, LANES),
                  ((0, 0), (0, 256 - MLA_Q_RANK), (0, 0))).astype(BF16)
    kv = mla_w_ukv.reshape(depth, MLA_KV_RANK, MLA_HEADS, MLA_NOPE + MLA_V)
    wk = _pad_last(kv[..., :MLA_NOPE], 0, LANES - MLA_NOPE).reshape(depth, MLA_KV_RANK, -1)
    wv = _pad_last(kv[..., MLA_NOPE:], 0, LANES - MLA_V).reshape(depth, MLA_KV_RANK, -1)
    wukv = jnp.concatenate([wk, wv], axis=-1).astype(BF16)
    gw = MLA_HEADS * LANES
    row = lambda v: _pad_last(v, 0, gw - v.shape[-1])
    slots = lambda v, n: row(jnp.tile(_pad_last(v, 0, LANES - v.shape[-1]), (1, n)))
    zero = jnp.zeros((depth, gw), F32)
    gains = jnp.stack([row(mla_q_norm_g),
                       slots(mla_q_head_g * (MLA_QK ** -0.5 * LOG2E), MLA_HEADS),
                       row(mla_kv_norm_g),
                       slots(mla_k_head_g, MLA_HEADS),
                       slots(gqa_q_g * (GQA_HD ** -0.5 * LOG2E), GQA_HEADS),
                       slots(gqa_k_g, GQA_KV_HEADS), zero, zero], axis=1)
    vec = lambda v: v[:, None, :]
    lanes8 = lambda v: _pad_last(jnp.tile(v.reshape(depth, 1, -1), (1, 1, 2)), 0, LANES - 2 * SSD_NK)
    return dict(win=win, wuq=wuq, wukv=wukv, gains=gains, wout=w_out.astype(BF16), ng=vec(norm_g),
                w3=ssd_conv_w, b3=vec(ssd_conv_b), w31=cf_conv_w, b31=vec(cf_conv_b),
                lng=vec(cf_ln_g), lnb=vec(cf_ln_b), dt_bias=lanes8(ssd_dt_bias), a_log=lanes8(ssd_a_log),
                dskip=vec(jnp.repeat(ssd_d, SSD_P, axis=-1)), ssd_ng=vec(ssd_norm_g))


def _mixer_layer(x, mod, mod_row, l, w, caches, ropes, tm, tq):
    latent = caches is not None
    b, s, d = x.shape
    flat = (lambda a: a) if latent else (lambda a: a.reshape(1, b * s, a.shape[-1]))
    unflat = (lambda a: a) if latent else (lambda a: a.reshape(b, s, a.shape[-1]))
    outs = [unflat(o) for o in _inproj(flat(x), mod, mod_row, l, w, ropes, tm)]
    qa, ka, va, ag, qd, kd, vd, dg, xbc, dt, z, glu, cg = outs[:13]
    ctx_a = ctx_d = s0 = None
    if latent:
        kc, vc, kdc, vdc = _ctx_kv(caches['ckv'], caches['kr_pad'], caches['gk'], caches['gv'], l, w)
        ctx_a, ctx_d, s0 = (kc, vc), (kdc, vdc), caches['state']
    oa = _attention(qa, ka, va, ctx_a, ag, MLA_HEADS, 1, tq, "attn_mla")
    od = _attention(qd, kd, vd, ctx_d, dg, GQA_HEADS, GQA_HEADS // GQA_KV_HEADS, tq, "attn_gqa")
    xconv, oc = _convs(xbc, glu, cg, l, w, min(256, x.shape[1]))
    yf, yb, s_fin = _ssd(xconv, dt, s0, l, w)
    y = unflat(_outproj(flat(oa), flat(yf), flat(yb), flat(z), flat(oc), flat(od), flat(x), mod, mod_row, l, w, tm))
    new_ctx = None
    if not latent:
        ckv, kr, gk, gv = outs[13:]
        new_ctx = (ckv, kr, gk.reshape(b, -1, GQA_KV_HEADS, GQA_HD), gv.reshape(b, -1, GQA_KV_HEADS, GQA_HD),
                   s_fin.reshape(b, 2, SSD_GROUPS, SSD_N, 2, SSD_P).transpose(0, 1, 2, 4, 5, 3)
                   .reshape(b, 2, SSD_HEADS, SSD_P, SSD_N))
    return y, new_ctx


def kernel(x_prompt, x_sample, cache_mla_ckv, cache_mla_krope, cache_gqa_k, cache_gqa_v, state_ssd, c, c_ctx, w_mod, b_mod, norm_g, w_in, w_out, mla_q_norm_g, mla_w_uq, mla_kv_norm_g, mla_w_ukv, mla_q_head_g, mla_k_head_g, ssd_conv_w, ssd_conv_b, ssd_dt_bias, ssd_a_log, ssd_d, ssd_norm_g, cf_conv_w, cf_conv_b, cf_ln_g, cf_ln_b, gqa_q_g, gqa_k_g):
    depth = w_in.shape[0]
    dec_b, n_lat = x_sample.shape[0], x_sample.shape[1]
    seq = x_prompt.shape[1]
    past = cache_mla_ckv.shape[2]
    assert dec_b < 8 and x_prompt.shape[-1] == D_MODEL and w_in.shape[-1] == sum(SPLITS)

    cvec = jnp.concatenate([c, c_ctx[None, :], jnp.zeros((8 - dec_b - 1, D_MODEL), F32)], axis=0)
    mod = _modulation(cvec, w_mod, b_mod).reshape(depth, 8, 3, D_MODEL)
    w = _prepare_weights(w_in, w_out, norm_g, mla_q_norm_g, mla_w_uq, mla_kv_norm_g, mla_w_ukv, mla_q_head_g,
                         mla_k_head_g, gqa_q_g, gqa_k_g, ssd_conv_w, ssd_conv_b, ssd_dt_bias, ssd_a_log, ssd_d,
                         ssd_norm_g, cf_conv_w, cf_conv_b, cf_ln_g, cf_ln_b)

    y = x_prompt
    ctx_out = []
    for l in range(depth):
        y, new_ctx = _mixer_layer(y, mod, dec_b, l, w, None, None, 512, min(256, seq))
        ctx_out.append(new_ctx)
    new = [jnp.stack([t[i] for t in ctx_out], axis=1) for i in range(5)]

    caches = dict(ckv=cache_mla_ckv,
                  kr_pad=_pad_last(cache_mla_krope, MLA_NOPE, LANES - MLA_QK),
                  gk=cache_gqa_k.reshape(dec_b, depth, past, GQA_KV_HEADS * GQA_HD),
                  gv=cache_gqa_v.reshape(dec_b, depth, past, GQA_KV_HEADS * GQA_HD),
                  state=state_ssd.reshape(dec_b, depth, 2, SSD_GROUPS, 2, SSD_P, SSD_N)
                  .transpose(0, 1, 2, 3, 6, 4, 5).reshape(dec_b, depth, 2, SSD_GROUPS, SSD_N, 2 * SSD_P))
    ropes = (_rope_table(n_lat, MLA_ROPE, MLA_NOPE), _rope_table(n_lat, GQA_HD, 0))
    z = x_sample
    for l in range(depth):
        z, _ = _mixer_layer(z, mod, None, l, w, caches, ropes, 512, 512)

    return (y, z, new[0], new[1], new[2], new[3], new[4])
```

```python
import functools
import math

import jax
import jax.numpy as jnp
from jax import lax
from jax.experimental import pallas as pl
from jax.experimental.pallas import tpu as pltpu

F32 = jnp.float32
BF16 = jnp.bfloat16

LANES = 128
VMEM_CAP = 56 * 1024 * 1024

EPS = 1e-6
ROPE_THETA = 10000.0
GRID_W = 64
LOG2E = math.log2(math.e)

D_MODEL = 1024
BRANCH_W = 256
MLA_HEADS, MLA_NOPE, MLA_ROPE, MLA_V = 4, 64, 32, 64
MLA_QK = MLA_NOPE + MLA_ROPE
MLA_Q_RANK, MLA_KV_RANK = 192, 128
SSD_HEADS, SSD_P, SSD_N, SSD_GROUPS, SSD_CONV, SSD_CHUNK = 4, 64, 64, 2, 3, 128
SSD_W = SSD_HEADS * SSD_P
SSD_CONV_CH = SSD_W + 2 * SSD_GROUPS * SSD_N
CF_W, CF_K = 256, 31
GQA_HEADS, GQA_KV_HEADS, GQA_HD = 4, 2, 64
SPLITS = (MLA_Q_RANK, MLA_KV_RANK, MLA_ROPE, MLA_HEADS * MLA_V,
          SSD_CONV_CH, 2 * SSD_HEADS, SSD_W,
          CF_W, CF_W, CF_W,
          GQA_HEADS * GQA_HD, GQA_KV_HEADS * GQA_HD, GQA_KV_HEADS * GQA_HD, GQA_HEADS * GQA_HD)

O_QL, O_KV, O_R, O_AG = 0, 256, 384, 512
O_XBC, O_DT, O_Z = 768, 1280, 1408
O_CV, O_CGL, O_CG = 1664, 1920, 2176
O_DQ, O_DK, O_DV, O_DG = 2432, 2944, 3200, 3456
IN_WP = 3712

CONV_HALO_SSD = 8
CONV_HALO_CF = 16

SSD_UNROLL = 4
INPROJ_SUB = 256
ATT_QK_CHUNK = 512
ATT_PV_CHUNK = 256

NT_DIMS = (((1,), (1,)), ((), ()))


def _silu(x):
    return x / (1.0 + jnp.exp(-x))


def _sigmoid(x):
    return 1.0 / (1.0 + jnp.exp(-x))


def _softplus(x):
    return jnp.maximum(x, 0.0) + jnp.log1p(jnp.exp(-jnp.abs(x)))


def _rms(x, n):
    return x * lax.rsqrt(jnp.sum(x * x, axis=-1, keepdims=True) * (1.0 / n) + EPS)


def _dot(a, b):
    return jnp.dot(a, b, preferred_element_type=F32)


def _dot_nt(a, b):
    return lax.dot_general(a, b, NT_DIMS, preferred_element_type=F32)


def _params(sem, vmem_bytes):
    assert vmem_bytes <= VMEM_CAP
    return pltpu.CompilerParams(dimension_semantics=sem, vmem_limit_bytes=VMEM_CAP)


def _pad_heads(w, nh, d, dp):
    s = w.shape[:-1]
    w = w.reshape(s + (nh, d))
    w = jnp.pad(w, [(0, 0)] * len(s) + [(0, 0), (0, dp - d)])
    return w.reshape(s + (nh * dp,))


def _pad_last(w, lo, hi):
    return jnp.pad(w, [(0, 0)] * (w.ndim - 1) + [(lo, hi)])


def _layer_spec(a, l, grid_rank):
    zeros = (0,) * (a.ndim - 1)
    if grid_rank == 1:
        return pl.BlockSpec((1,) + a.shape[1:], lambda bi: (l,) + zeros)
    return pl.BlockSpec((1,) + a.shape[1:], lambda bi, i: (l,) + zeros)


def _with_ones_lane(v, hd):
    lane = lax.broadcasted_iota(jnp.int32, v.shape, v.ndim - 1)
    return jnp.where(lane % LANES == hd, 1.0, v)


def _mod_kernel(c_ref, w_ref, b_ref, o_ref):
    c = c_ref[...]
    o_ref[0] = _dot(_silu(c).astype(BF16), w_ref[0].astype(BF16)) + b_ref[0]


def _modulation(cvec, w_mod, b_mod):
    depth, d, d3 = w_mod.shape
    tn = d
    return pl.pallas_call(
        _mod_kernel,
        out_shape=jax.ShapeDtypeStruct((depth, 8, d3), F32),
        grid=(depth, d3 // tn),
        in_specs=[pl.BlockSpec((8, d), lambda l, j: (0, 0)),
                  pl.BlockSpec((1, d, tn), lambda l, j: (l, 0, j)),
                  pl.BlockSpec((1, 1, tn), lambda l, j: (l, 0, j))],
        out_specs=pl.BlockSpec((1, 8, tn), lambda l, j: (l, 0, j)),
        compiler_params=_params(("arbitrary", "arbitrary"), 4 * d * tn * 4),
        name="modulation",
    )(cvec, w_mod, b_mod.reshape(depth, 1, d3))


def _rope(x, tab_ref, rows, shift):
    cos, sin_a, sin_b = tab_ref[0, rows, :], tab_ref[1, rows, :], tab_ref[2, rows, :]
    outs = []
    for h in range(x.shape[-1] // LANES):
        xh = x[:, h * LANES:(h + 1) * LANES]
        up = pltpu.roll(xh, LANES - shift, 1)
        dn = pltpu.roll(xh, shift, 1)
        outs.append(xh * cos + up * sin_a + dn * sin_b)
    return jnp.concatenate(outs, axis=-1)


def _head_norm(x, gain, n, ones_bd):
    xx = (x * x).astype(BF16)
    w2 = 2 * LANES
    ss = jnp.concatenate([_dot(xx[:, j:j + w2], ones_bd) for j in range(0, x.shape[-1], w2)], axis=-1)
    return x * lax.rsqrt(ss * (1.0 / n) + EPS) * gain


def _inproj_refs(latent, refs):
    names = ['x', 'mod', 'ng', 'win', 'wuq', 'wukv', 'gains']
    if latent:
        names += ['ra', 'rd']
    names += ['qa', 'ka', 'va', 'ag', 'qd', 'kd', 'vd', 'dg', 'xbc', 'dt', 'z', 'glu', 'cg']
    if not latent:
        names += ['ckv', 'kr', 'gk', 'gv']
    return dict(zip(names, refs))


def _inproj_main(r, rows):
    x = r['x'][0, rows, :]
    shift = r['mod'][0, 0, 0:1, :]
    scale = r['mod'][0, 0, 1:2, :]
    h = _rms(x, D_MODEL) * r['ng'][0] * (1.0 + scale) + shift
    return _dot(h.astype(BF16), r['win'][0])


def _inproj_latents(r, u):
    g_ql = r['gains'][0, 0:1, 0:256]
    g_kv = r['gains'][0, 2:3, 0:LANES]
    ql = _rms(u[:, O_QL:O_QL + 256], MLA_Q_RANK) * g_ql
    ckv = _rms(u[:, O_KV:O_KV + LANES], MLA_KV_RANK) * g_kv
    return _dot(ql.astype(BF16), r['wuq'][0]), ckv, _dot(ckv.astype(BF16), r['wukv'][0])


def _inproj_tail(latent, r, rows, u, q, ckv, kv):
    wa, wd = MLA_HEADS * LANES, GQA_KV_HEADS * LANES
    g_qh = r['gains'][0, 1:2, :]
    g_kh = r['gains'][0, 3:4, :]
    g_dq = r['gains'][0, 4:5, :]
    g_dk = r['gains'][0, 5:6, 0:wd]
    ri = lax.broadcasted_iota(jnp.int32, (2 * LANES, 2 * LANES), 0) // LANES
    ci = lax.broadcasted_iota(jnp.int32, (2 * LANES, 2 * LANES), 1) // LANES
    ones_bd = (ri == ci).astype(F32).astype(BF16)

    q = _head_norm(q, g_qh, MLA_QK, ones_bd)
    kr = u[:, O_R:O_R + LANES]
    k = _head_norm(kv[:, 0:wa] + jnp.concatenate([kr] * MLA_HEADS, axis=-1), g_kh, MLA_QK, ones_bd)
    qd = _head_norm(u[:, O_DQ:O_DQ + GQA_HEADS * LANES], g_dq, GQA_HD, ones_bd)
    kd = _head_norm(u[:, O_DK:O_DK + wd], g_dk, GQA_HD, ones_bd)
    vd = u[:, O_DV:O_DV + wd]
    if latent:
        q = _rope(q, r['ra'], rows, MLA_ROPE // 4)
        k = _rope(k, r['ra'], rows, MLA_ROPE // 4)
        qd = _rope(qd, r['rd'], rows, GQA_HD // 4)
        kd_att = _rope(kd, r['rd'], rows, GQA_HD // 4)
    else:
        kd_att = kd
        r['ckv'][0, rows, :] = ckv
        r['kr'][0, rows, :] = kr[:, MLA_NOPE:MLA_NOPE + MLA_ROPE]
        for i in range(GQA_KV_HEADS):
            r['gk'][0, rows, i * GQA_HD:(i + 1) * GQA_HD] = kd[:, i * LANES:i * LANES + GQA_HD]
            r['gv'][0, rows, i * GQA_HD:(i + 1) * GQA_HD] = vd[:, i * LANES:i * LANES + GQA_HD]
    r['qa'][0, rows, :] = q.astype(BF16)
    r['ka'][0, rows, :] = k.astype(BF16)
    r['va'][0, rows, :] = _with_ones_lane(kv[:, wa:], MLA_V).astype(BF16)
    r['ag'][0, rows, :] = u[:, O_AG:O_AG + BRANCH_W]
    r['qd'][0, rows, :] = qd.astype(BF16)
    r['kd'][0, rows, :] = kd_att.astype(BF16)
    r['vd'][0, rows, :] = _with_ones_lane(vd, GQA_HD).astype(BF16)
    r['dg'][0, rows, :] = u[:, O_DG:O_DG + BRANCH_W]

    r['xbc'][0, rows, :] = u[:, O_XBC:O_XBC + SSD_CONV_CH]
    r['dt'][0, rows, :] = u[:, O_DT:O_DT + LANES]
    r['z'][0, rows, :] = u[:, O_Z:O_Z + SSD_W]
    r['glu'][0, rows, :] = u[:, O_CV:O_CV + CF_W] * _sigmoid(u[:, O_CGL:O_CGL + CF_W])
    r['cg'][0, rows, :] = u[:, O_CG:O_CG + CF_W]


def _inproj_kernel(latent, sub, *refs):
    r = _inproj_refs(latent, refs)
    tm = r['x'].shape[1]
    for r0 in range(0, tm, sub):
        rows = slice(r0, r0 + sub)
        u = _inproj_main(r, rows)
        q, ckv, kv = _inproj_latents(r, u)
        _inproj_tail(latent, r, rows, u, q, ckv, kv)


def _mod_spec(mod, l, row):
    d = mod.shape[-1]
    if row is None:
        return pl.BlockSpec((1, 1, 3, d), lambda bi, i: (l, bi, 0, 0))
    return pl.BlockSpec((1, 1, 3, d), lambda bi, i: (l, row, 0, 0))


def _inproj(x, mod, mod_row, l, w, ropes, tm):
    b, s, d = x.shape
    latent = ropes is not None
    tok = lambda wd: pl.BlockSpec((1, tm, wd), lambda bi, i: (bi, i, 0))
    weights = [w['ng'], w['win'], w['wuq'], w['wukv'], w['gains']]
    in_specs = [tok(d), _mod_spec(mod, l, mod_row)] + [_layer_spec(a, l, 2) for a in weights]
    args = [x, mod] + weights
    if latent:
        in_specs += [pl.BlockSpec((3, tm, LANES), lambda bi, i: (0, i, 0))] * 2
        args += list(ropes)
    widths = [(4 * LANES, BF16), (4 * LANES, BF16), (4 * LANES, BF16), (BRANCH_W, F32),
              (4 * LANES, BF16), (2 * LANES, BF16), (2 * LANES, BF16), (BRANCH_W, F32),
              (SSD_CONV_CH, F32), (LANES, F32), (SSD_W, F32), (CF_W, F32), (CF_W, F32)]
    if not latent:
        widths += [(MLA_KV_RANK, F32), (MLA_ROPE, F32), (LANES, F32), (LANES, F32)]
    out_shape = [jax.ShapeDtypeStruct((b, s, wd), dt) for wd, dt in widths]
    out_specs = [tok(wd) for wd, _ in widths]
    out_bytes = sum(wd * jnp.dtype(dt).itemsize for wd, dt in widths) * tm
    w_bytes = sum(a[0].size * a.dtype.itemsize for a in weights)
    sub = min(INPROJ_SUB, tm)
    vmem = 2 * (tm * d * 4 + w_bytes + out_bytes) + 3 * sub * IN_WP * 4
    return pl.pallas_call(
        functools.partial(_inproj_kernel, latent, sub),
        out_shape=out_shape,
        grid=(b, s // tm),
        in_specs=in_specs,
        out_specs=out_specs,
        compiler_params=_params(("parallel", "parallel"), vmem),
        name="inproj_lat" if latent else "inproj_ctx",
    )(*args)


def _ctxkv_kernel(ckv_ref, kr_ref, gk_ref, gv_ref, wukv_ref, gains_ref, kc_ref, vc_ref, kdc_ref, vdc_ref):
    kv = _dot(ckv_ref[0, 0].astype(BF16), wukv_ref[0])
    kr = kr_ref[0, 0]
    g = gains_ref[0, 3:4, 0:LANES]
    for i in range(MLA_HEADS):
        k = _rms(kv[:, i * LANES:(i + 1) * LANES] + kr, MLA_QK) * g
        kc_ref[0, :, i * LANES:(i + 1) * LANES] = k.astype(BF16)
    vc_ref[0] = _with_ones_lane(kv[:, MLA_HEADS * LANES:], MLA_V).astype(BF16)
    gk = gk_ref[0, 0]
    gv = gv_ref[0, 0]
    p = gk.shape[0]
    pad = jnp.zeros((p, LANES - GQA_HD), F32)
    for i in range(GQA_KV_HEADS):
        lo = i * LANES
        kdc_ref[0, :, lo:lo + GQA_HD] = gk[:, i * GQA_HD:(i + 1) * GQA_HD].astype(BF16)
        kdc_ref[0, :, lo + GQA_HD:lo + LANES] = pad.astype(BF16)
        vdc_ref[0, :, lo:lo + GQA_HD] = gv[:, i * GQA_HD:(i + 1) * GQA_HD].astype(BF16)
        vdc_ref[0, :, lo + GQA_HD:lo + LANES] = _with_ones_lane(pad, 0).astype(BF16)


def _ctx_kv(ckv, kr_pad, gk, gv, l, w):
    b, _, p, _ = ckv.shape
    cache = lambda a: pl.BlockSpec((1, 1, p, a.shape[-1]), lambda bi: (bi, l, 0, 0))
    out = lambda wd: pl.BlockSpec((1, p, wd), lambda bi: (bi, 0, 0))
    wa, wd = MLA_HEADS * LANES, GQA_KV_HEADS * LANES
    return pl.pallas_call(
        _ctxkv_kernel,
        out_shape=[jax.ShapeDtypeStruct((b, p, wa), BF16), jax.ShapeDtypeStruct((b, p, wa), BF16),
                   jax.ShapeDtypeStruct((b, p, wd), BF16), jax.ShapeDtypeStruct((b, p, wd), BF16)],
        grid=(b,),
        in_specs=[cache(ckv), cache(kr_pad), cache(gk), cache(gv),
                  _layer_spec(w['wukv'], l, 1), _layer_spec(w['gains'], l, 1)],
        out_specs=[out(wa), out(wa), out(wd), out(wd)],
        compiler_params=_params(("parallel",), 16 * p * wa * 4),
        name="ctx_kv",
    )(ckv, kr_pad, gk, gv, w['wukv'], w['gains'])


def _lane_tile_max(t):
    r = t[:, 0:LANES]
    for j in range(1, t.shape[-1] // LANES):
        r = jnp.maximum(r, t[:, j * LANES:(j + 1) * LANES])
    return r


def _attn_kernel(n_heads, rep, has_ctx, *refs):
    if has_ctx:
        q_ref, k_ref, v_ref, kc_ref, vc_ref, g_ref, o_ref, s0_ref, s1_ref = refs
        p_len = kc_ref.shape[1]
    else:
        q_ref, k_ref, v_ref, g_ref, o_ref, s0_ref, s1_ref = refs
        p_len = 0
    s_len = k_ref.shape[1]
    hd = o_ref.shape[-1] // n_heads
    kb = min(ATT_QK_CHUNK, s_len)
    kb2 = ATT_PV_CHUNK
    for h in range(n_heads):
        s_ref = s0_ref if h % 2 == 0 else s1_ref
        g = h // rep
        hs = slice(h * LANES, (h + 1) * LANES)
        gs = slice(g * LANES, (g + 1) * LANES)
        qh = q_ref[0, :, hs]
        mx = None
        for c in range(s_len // kb):
            sc = _dot_nt(qh, k_ref[0, c * kb:(c + 1) * kb, gs])
            s_ref[:, c * kb:(c + 1) * kb] = sc
            t = _lane_tile_max(sc)
            mx = t if mx is None else jnp.maximum(mx, t)
        if has_ctx:
            sc = _dot_nt(qh, kc_ref[0, :, gs])
            s_ref[:, s_len:s_len + p_len] = sc
            mx = jnp.maximum(mx, _lane_tile_max(sc))
        m = jnp.max(mx, axis=-1, keepdims=True)
        acc = None
        for c in range((s_len + p_len) // kb2):
            r0 = c * kb2
            p = jnp.exp2(s_ref[:, r0:r0 + kb2] - m).astype(BF16)
            vv = v_ref[0, r0:r0 + kb2, gs] if r0 < s_len else vc_ref[0, r0 - s_len:r0 - s_len + kb2, gs]
            part = _dot(p, vv)
            acc = part if acc is None else acc + part
        o = acc[:, 0:hd] / acc[:, hd:hd + 1]
        os_ = slice(h * hd, (h + 1) * hd)
        o_ref[0, :, os_] = o * _silu(g_ref[0, :, os_])


def _attention(q, k, v, ctx, gate, n_heads, rep, tq, name):
    b, s, qw = q.shape
    kw = k.shape[-1]
    ow = gate.shape[-1]
    has_ctx = ctx is not None
    in_specs = [pl.BlockSpec((1, tq, qw), lambda bi, i: (bi, i, 0)),
                pl.BlockSpec((1, s, kw), lambda bi, i: (bi, 0, 0)),
                pl.BlockSpec((1, s, kw), lambda bi, i: (bi, 0, 0))]
    args = [q, k, v]
    p = 0
    if has_ctx:
        p = ctx[0].shape[1]
        in_specs += [pl.BlockSpec((1, p, kw), lambda bi, i: (bi, 0, 0))] * 2
        args += list(ctx)
    in_specs.append(pl.BlockSpec((1, tq, ow), lambda bi, i: (bi, i, 0)))
    args.append(gate)
    assert s % min(ATT_QK_CHUNK, s) == 0 and s % ATT_PV_CHUNK == 0 and p % ATT_PV_CHUNK == 0
    vmem = 2 * (tq * qw * 2 + 2 * (s + p) * kw * 2 + 2 * tq * ow * 4) + 2 * tq * (s + p) * 4 + 4 * tq * 1024 * 4
    return pl.pallas_call(
        functools.partial(_attn_kernel, n_heads, rep, has_ctx),
        out_shape=jax.ShapeDtypeStruct((b, s, ow), F32),
        grid=(b, s // tq),
        in_specs=in_specs,
        out_specs=pl.BlockSpec((1, tq, ow), lambda bi, i: (bi, i, 0)),
        scratch_shapes=[pltpu.VMEM((tq, s + p), F32), pltpu.VMEM((tq, s + p), F32)],
        compiler_params=_params(("parallel", "arbitrary"), vmem),
        name=name,
    )(*args)


def _conv_kernel(tl, xc_ref, xp_ref, xn_ref, gc_ref, gp_ref, gn_ref, cg_ref,
                 w3_ref, b3_ref, w31_ref, b31_ref, lng_ref, lnb_ref,
                 xo_ref, co_ref, ext3, ext31):
    i = pl.program_id(1)
    keep_prev = (i > 0).astype(F32)
    keep_next = (i < pl.num_programs(1) - 1).astype(F32)

    h3 = CONV_HALO_SSD
    ext3[0:h3, :] = xp_ref[0] * keep_prev
    ext3[h3:h3 + tl, :] = xc_ref[0]
    ext3[h3 + tl:2 * h3 + tl, :] = xn_ref[0] * keep_next
    acc = jnp.zeros((tl, SSD_CONV_CH), F32) + b3_ref[0]
    for k in range(SSD_CONV):
        acc = acc + w3_ref[0, k:k + 1, :] * ext3[h3 - SSD_CONV // 2 + k:h3 - SSD_CONV // 2 + k + tl, :]
    xo_ref[0] = _silu(acc)

    h31 = CONV_HALO_CF
    ext31[0:h31, :] = gp_ref[0] * keep_prev
    ext31[h31:h31 + tl, :] = gc_ref[0]
    ext31[h31 + tl:2 * h31 + tl, :] = gn_ref[0] * keep_next
    acc = jnp.zeros((tl, CF_W), F32) + b31_ref[0]
    o_lo = h31 - CF_K // 2
    ext = ext31[...]
    n_ext = ext.shape[0]
    for r in range(8):
        taps = [k for k in range(CF_K) if (o_lo + k) % 8 == r]
        if not taps:
            continue
        shifted = ext if r == 0 else pltpu.roll(ext, n_ext - r, 0)
        for k in taps:
            a = (o_lo + k) // 8 * 8
            acc = acc + w31_ref[0, k:k + 1, :] * shifted[a:a + tl, :]
    mu = jnp.mean(acc, axis=-1, keepdims=True)
    cen = acc - mu
    var = jnp.mean(cen * cen, axis=-1, keepdims=True)
    y = cen * lax.rsqrt(var + EPS) * lng_ref[0] + lnb_ref[0]
    co_ref[0] = _silu(y) * _silu(cg_ref[0])


def _convs(xbc, glu, cg, l, w, tl):
    b, s, _ = xbc.shape
    nt = s // tl

    def cur(wd):
        return pl.BlockSpec((1, tl, wd), lambda bi, i: (bi, i, 0))

    def prev(wd, hrows):
        r = tl // hrows
        return pl.BlockSpec((1, hrows, wd), lambda bi, i: (bi, jnp.maximum(i * r - 1, 0), 0))

    def nxt(wd, hrows):
        r = tl // hrows
        return pl.BlockSpec((1, hrows, wd), lambda bi, i: (bi, jnp.minimum((i + 1) * r, nt * r - 1), 0))

    weights = [w['w3'], w['b3'], w['w31'], w['b31'], w['lng'], w['lnb']]
    vmem = 4 * tl * (SSD_CONV_CH * 2 + CF_W * 3) * 4 + 6 * tl * SSD_CONV_CH * 4
    return pl.pallas_call(
        functools.partial(_conv_kernel, tl),
        out_shape=[jax.ShapeDtypeStruct((b, s, SSD_CONV_CH), F32),
                   jax.ShapeDtypeStruct((b, s, CF_W), F32)],
        grid=(b, nt),
        in_specs=[cur(SSD_CONV_CH), prev(SSD_CONV_CH, CONV_HALO_SSD), nxt(SSD_CONV_CH, CONV_HALO_SSD),
                  cur(CF_W), prev(CF_W, CONV_HALO_CF), nxt(CF_W, CONV_HALO_CF), cur(CF_W)]
                 + [_layer_spec(a, l, 2) for a in weights],
        out_specs=[cur(SSD_CONV_CH), cur(CF_W)],
        scratch_shapes=[pltpu.VMEM((tl + 2 * CONV_HALO_SSD, SSD_CONV_CH), F32),
                        pltpu.VMEM((tl + 2 * CONV_HALO_CF, CF_W), F32)],
        compiler_params=_params(("parallel", "parallel"), vmem),
        name="convs",
    )(xbc, xbc, xbc, glu, glu, glu, cg, *weights)


def _split3(x):
    hi = x.astype(BF16)
    r1 = x - hi.astype(F32)
    mid = r1.astype(BF16)
    lo = (r1 - mid.astype(F32)).astype(BF16)
    return hi, mid, lo


def _cumsum_rows(tri, x):
    return sum(_dot(tri, part) for part in _split3(x))


def _transpose_exact(x):
    c = x.shape[1]
    eye = (lax.broadcasted_iota(jnp.int32, (c, c), 0) == lax.broadcasted_iota(jnp.int32, (c, c), 1))
    eye = eye.astype(F32).astype(BF16)
    return sum(_dot_nt(eye, part) for part in _split3(x))


SSD_NK = 2 * SSD_HEADS


def _ssd_stage_local(c, d, xc_ref, dt_ref, bias, a_row):
    q = SSD_CHUNK
    rows = pl.ds(pl.multiple_of(c * q, q), q)
    t = dict(rows=rows, d=d)
    t['xs'] = xc_ref[0, rows, 0:SSD_W]
    t['bm'] = xc_ref[0, rows, SSD_W:SSD_W + LANES]
    t['cm'] = xc_ref[0, rows, SSD_W + LANES:SSD_W + 2 * LANES].astype(BF16)
    t['dt'] = _softplus(dt_ref[0, rows, :] + bias)
    ri = lax.broadcasted_iota(jnp.int32, (q, q), 0)
    ci = lax.broadcasted_iota(jnp.int32, (q, q), 1)
    t['mask'] = (ri >= ci) if d == 0 else (ri <= ci)
    t['acum'] = _cumsum_rows(t['mask'].astype(F32).astype(BF16), t['dt'] * a_row)
    return t


def _ssd_stage_cb(t):
    q = SSD_CHUNK
    lane = lax.broadcasted_iota(jnp.int32, (q, LANES), 1)
    t['tr'] = jnp.where(lane < SSD_NK, t['dt'], t['acum']).T
    bt = t['bm'].T
    t['bt'] = bt
    grp_row = lax.broadcasted_iota(jnp.int32, (LANES, q), 0) < SSD_N
    t['cb'] = [_dot(t['cm'], jnp.where(grp_row if g == 0 else jnp.logical_not(grp_row), bt, 0.0).astype(BF16))
               for g in range(SSD_GROUPS)]
    t['last'] = q - 1 if t['d'] == 0 else 0
    t['etot'] = jnp.exp(t['acum'][t['last']:t['last'] + 1, :])
    t['eac'] = jnp.exp(t['acum'])


def _ssd_stage_diag(t, dskip):
    q = SSD_CHUNK
    d = t['d']
    lo_half = lax.broadcasted_iota(jnp.int32, (q, LANES), 1) < SSD_P
    tr, acum, last = t['tr'], t['acum'], t['last']
    t['yd'], t['cs'] = [], []
    for g in range(SSD_GROUPS):
        k0 = d * SSD_HEADS + 2 * g
        xg = t['xs'][:, g * LANES:(g + 1) * LANES]
        xgb = xg.astype(BF16)
        btg = t['bt'][g * SSD_N:(g + 1) * SSD_N, :]
        yg = None
        cs = []
        for hh in range(2):
            k = k0 + hh
            dt_row = tr[k:k + 1, :]
            ac_row = tr[SSD_NK + k:SSD_NK + k + 1, :]
            dec = jnp.where(t['mask'], jnp.exp(acum[:, k:k + 1] - ac_row), 0.0)
            xm = jnp.where(lo_half if hh == 0 else jnp.logical_not(lo_half), xg, 0.0).astype(BF16)
            part = _dot((t['cb'][g] * dec * dt_row).astype(BF16), xm)
            yg = part if yg is None else yg + part
            w_row = dt_row * jnp.exp(tr[SSD_NK + k:SSD_NK + k + 1, last:last + 1] - ac_row)
            cs.append(_dot((btg * w_row).astype(BF16), xgb))
        if d == 0:
            yg = yg + dskip[:, g * LANES:(g + 1) * LANES] * xg
        t['yd'].append(yg)
        t['cs'].append(cs)


def _ssd_stage_state(t, states):
    q = SSD_CHUNK
    d = t['d']
    lo_half = lax.broadcasted_iota(jnp.int32, (q, LANES), 1) < SSD_P
    lane_n = lax.broadcasted_iota(jnp.int32, (SSD_N, LANES), 1) < SSD_P
    zero = jnp.zeros((SSD_N, LANES), F32)
    ys, new_states = [], []
    for g in range(SSD_GROUPS):
        k0 = d * SSD_HEADS + 2 * g
        s_prev = states[g]
        sz = jnp.concatenate([s_prev, zero] if g == 0 else [zero, s_prev], axis=0).astype(BF16)
        eacs = jnp.where(lo_half, t['eac'][:, k0:k0 + 1], t['eac'][:, k0 + 1:k0 + 2])
        ys.append(t['yd'][g] + _dot(t['cm'], sz) * eacs)
        cdec = jnp.where(lane_n, t['etot'][:, k0:k0 + 1], t['etot'][:, k0 + 1:k0 + 2])
        new_states.append(cdec * s_prev + jnp.where(lane_n, t['cs'][g][0], t['cs'][g][1]))
    return ys, new_states


def _ssd_kernel(nc, unroll, has_init, *refs):
    if has_init:
        xc_ref, dt_ref, s0_ref, bias_ref, alog_ref, d_ref, yf_ref, yb_ref, so_ref, st_ref = refs
        for d in range(2):
            for g in range(SSD_GROUPS):
                st_ref[0, d, g] = _transpose_exact(s0_ref[0, 0, d, g])
    else:
        xc_ref, dt_ref, bias_ref, alog_ref, d_ref, yf_ref, yb_ref, so_ref, st_ref = refs
        st_ref[...] = jnp.zeros(st_ref.shape, F32)
    lane = lax.broadcasted_iota(jnp.int32, (1, LANES), 1)
    a_row = jnp.where(lane < 2 * SSD_NK, -jnp.exp(alog_ref[0]), 0.0)
    bias = bias_ref[0]
    dskip = d_ref[0]

    def body(i, carry):
        states = [[st_ref[0, d, g] for g in range(SSD_GROUPS)] for d in range(2)]
        work = []
        for j in range(unroll):
            c = i * unroll + j
            work.append((_ssd_stage_local(c, 0, xc_ref, dt_ref, bias, a_row), yf_ref))
            work.append((_ssd_stage_local(nc - 1 - c, 1, xc_ref, dt_ref, bias, a_row), yb_ref))
        for t, _ in work:
            _ssd_stage_cb(t)
        for t, _ in work:
            _ssd_stage_diag(t, dskip)
        writes = []
        for t, y_ref in work:
            ys, states[t['d']] = _ssd_stage_state(t, states[t['d']])
            writes.append((y_ref, t['rows'], ys))
        for y_ref, rows, ys in writes:
            for g in range(SSD_GROUPS):
                y_ref[0, rows, g * LANES:(g + 1) * LANES] = ys[g]
        for d in range(2):
            for g in range(SSD_GROUPS):
                st_ref[0, d, g] = states[d][g]
        return carry

    lax.fori_loop(0, nc // unroll, body, 0)
    for d in range(2):
        for g in range(SSD_GROUPS):
            so_ref[0, d, g] = _transpose_exact(st_ref[0, d, g])


def _ssd(xc, dt, s0, l, w):
    b, s, _ = xc.shape
    nc = s // SSD_CHUNK
    unroll = min(SSD_UNROLL, nc)
    assert nc % unroll == 0
    st_shape = (b, 2, SSD_GROUPS, 2 * SSD_P, SSD_N)
    st_spec = pl.BlockSpec((1,) + st_shape[1:], lambda bi: (bi, 0, 0, 0, 0))
    y_spec = pl.BlockSpec((1, s, SSD_W), lambda bi: (bi, 0, 0))
    in_specs = [pl.BlockSpec((1, s, SSD_CONV_CH), lambda bi: (bi, 0, 0)),
                pl.BlockSpec((1, s, LANES), lambda bi: (bi, 0, 0))]
    args = [xc, dt]
    if s0 is not None:
        in_specs.append(pl.BlockSpec((1, 1) + st_shape[1:], lambda bi: (bi, l, 0, 0, 0, 0)))
        args.append(s0)
    weights = [w['dt_bias'], w['a_log'], w['dskip']]
    vmem = 2 * s * (SSD_CONV_CH + LANES + 2 * SSD_W) * 4 + 8 * 1024 * 1024
    return pl.pallas_call(
        functools.partial(_ssd_kernel, nc, unroll, s0 is not None),
        out_shape=[jax.ShapeDtypeStruct((b, s, SSD_W), F32), jax.ShapeDtypeStruct((b, s, SSD_W), F32),
                   jax.ShapeDtypeStruct(st_shape, F32)],
        grid=(b,),
        in_specs=in_specs + [_layer_spec(a, l, 1) for a in weights],
        out_specs=[y_spec, y_spec, st_spec],
        scratch_shapes=[pltpu.VMEM((1, 2, SSD_GROUPS, SSD_N, 2 * SSD_P), F32)],
        compiler_params=_params(("parallel",), vmem),
        name="ssd",
    )(*args, *weights)


def _outproj_kernel(oa_ref, yf_ref, yb_ref, z_ref, oc_ref, od_ref, x_ref, mod_ref, ng_ref, w_ref, o_ref):
    ob = _rms((yf_ref[0] + yb_ref[0]) * _silu(z_ref[0]), SSD_W) * ng_ref[0]
    acc = _dot(oa_ref[0].astype(BF16), w_ref[0, 0:BRANCH_W, :])
    acc = acc + _dot(ob.astype(BF16), w_ref[0, BRANCH_W:2 * BRANCH_W, :])
    acc = acc + _dot(oc_ref[0].astype(BF16), w_ref[0, 2 * BRANCH_W:3 * BRANCH_W, :])
    acc = acc + _dot(od_ref[0].astype(BF16), w_ref[0, 3 * BRANCH_W:4 * BRANCH_W, :])
    o_ref[0] = x_ref[0] + mod_ref[0, 0, 2:3, :] * acc


def _outproj(oa, yf, yb, z, oc, od, x, mod, mod_row, l, w, tm):
    b, s, d = x.shape
    br = lambda: pl.BlockSpec((1, tm, BRANCH_W), lambda bi, i: (bi, i, 0))
    xs = pl.BlockSpec((1, tm, d), lambda bi, i: (bi, i, 0))
    vmem = 2 * (6 * tm * BRANCH_W * 4 + 2 * tm * d * 4 + d * d * 2) + 2 * tm * d * 4
    return pl.pallas_call(
        _outproj_kernel,
        out_shape=jax.ShapeDtypeStruct((b, s, d), F32),
        grid=(b, s // tm),
        in_specs=[br(), br(), br(), br(), br(), br(), xs, _mod_spec(mod, l, mod_row),
                  _layer_spec(w['ssd_ng'], l, 2), _layer_spec(w['wout'], l, 2)],
        out_specs=xs,
        compiler_params=_params(("parallel", "parallel"), vmem),
        name="outproj",
    )(oa, yf, yb, z, oc, od, x, mod, w['ssd_ng'], w['wout'])


def _rope_table(n_tokens, dim, lane0):
    n_rows = n_tokens // GRID_W
    row = jnp.repeat(jnp.arange(n_rows), GRID_W).astype(F32)
    col = jnp.tile(jnp.arange(GRID_W), n_rows).astype(F32)
    quarter = dim // 4
    inv = ROPE_THETA ** (-jnp.arange(quarter, dtype=F32) / quarter)
    ar = row[:, None] * inv
    ac = col[:, None] * inv
    zero = jnp.zeros_like(ar)
    cos = jnp.concatenate([jnp.cos(ar), jnp.cos(ar), jnp.cos(ac), jnp.cos(ac)], axis=-1)
    sin_a = jnp.concatenate([-jnp.sin(ar), zero, -jnp.sin(ac), zero], axis=-1)
    sin_b = jnp.concatenate([zero, jnp.sin(ar), zero, jnp.sin(ac)], axis=-1)
    hi = LANES - lane0 - dim
    cos = jnp.pad(cos, ((0, 0), (lane0, hi)), constant_values=1.0)
    return jnp.stack([cos, _pad_last(sin_a, lane0, hi), _pad_last(sin_b, lane0, hi)])


def _prepare_weights(w_in, w_out, norm_g, mla_q_norm_g, mla_w_uq, mla_kv_norm_g, mla_w_ukv, mla_q_head_g,
                     mla_k_head_g, gqa_q_g, gqa_k_g, ssd_conv_w, ssd_conv_b, ssd_dt_bias, ssd_a_log, ssd_d,
                     ssd_norm_g, cf_conv_w, cf_conv_b, cf_ln_g, cf_ln_b):
    depth = w_in.shape[0]
    pieces = []
    o = 0
    for sz in SPLITS:
        pieces.append(w_in[:, :, o:o + sz])
        o += sz
    (a_q, a_kv, a_r, a_g, b_xbc, b_dt, b_z, c_v, c_gl, c_g, d_q, d_k, d_v, d_g) = pieces
    win = jnp.concatenate([
        _pad_last(a_q, 0, 256 - MLA_Q_RANK), a_kv, _pad_last(a_r, MLA_NOPE, LANES - MLA_QK), a_g,
        b_xbc, _pad_last(jnp.concatenate([b_dt, b_dt], axis=-1), 0, LANES - 2 * SSD_NK), b_z, c_v, c_gl, c_g,
        _pad_heads(d_q, GQA_HEADS, GQA_HD, LANES), _pad_heads(d_k, GQA_KV_HEADS, GQA_HD, LANES),
        _pad_heads(d_v, GQA_KV_HEADS, GQA_HD, LANES), d_g], axis=-1).astype(BF16)
    wuq = jnp.pad(_pad_heads(mla_w_uq, MLA_HEADS, MLA_QK, LANES),
                  ((0, 0), (0, 256 - MLA_Q_RANK), (0, 0))).astype(BF16)
    kv = mla_w_ukv.reshape(depth, MLA_KV_RANK, MLA_HEADS, MLA_NOPE + MLA_V)
    wk = _pad_last(kv[..., :MLA_NOPE], 0, LANES - MLA_NOPE).reshape(depth, MLA_KV_RANK, -1)
    wv = _pad_last(kv[..., MLA_NOPE:], 0, LANES - MLA_V).reshape(depth, MLA_KV_RANK, -1)
    wukv = jnp.concatenate([wk, wv], axis=-1).astype(BF16)
    gw = MLA_HEADS * LANES
    row = lambda v: _pad_last(v, 0, gw - v.shape[-1])
    slots = lambda v, n: row(jnp.tile(_pad_last(v, 0, LANES - v.shape[-1]), (1, n)))
    zero = jnp.zeros((depth, gw), F32)
    gains = jnp.stack([row(mla_q_norm_g),
                       slots(mla_q_head_g * (MLA_QK ** -0.5 * LOG2E), MLA_HEADS),
                       row(mla_kv_norm_g),
                       slots(mla_k_head_g, MLA_HEADS),
                       slots(gqa_q_g * (GQA_HD ** -0.5 * LOG2E), GQA_HEADS),
                       slots(gqa_k_g, GQA_KV_HEADS), zero, zero], axis=1)
    vec = lambda v: v[:, None, :]
    lanes8 = lambda v: _pad_last(jnp.tile(v.reshape(depth, 1, -1), (1, 1, 2)), 0, LANES - 2 * SSD_NK)
    return dict(win=win, wuq=wuq, wukv=wukv, gains=gains, wout=w_out.astype(BF16), ng=vec(norm_g),
                w3=ssd_conv_w, b3=vec(ssd_conv_b), w31=cf_conv_w, b31=vec(cf_conv_b),
                lng=vec(cf_ln_g), lnb=vec(cf_ln_b), dt_bias=lanes8(ssd_dt_bias), a_log=lanes8(ssd_a_log),
                dskip=vec(jnp.repeat(ssd_d, SSD_P, axis=-1)), ssd_ng=vec(ssd_norm_g))


def _mixer_layer(x, mod, mod_row, l, w, caches, ropes, tm, tq):
    latent = caches is not None
    b, s, d = x.shape
    flat = (lambda a: a) if latent else (lambda a: a.reshape(1, b * s, a.shape[-1]))
    unflat = (lambda a: a) if latent else (lambda a: a.reshape(b, s, a.shape[-1]))
    outs = [unflat(o) for o in _inproj(flat(x), mod, mod_row, l, w, ropes, tm)]
    qa, ka, va, ag, qd, kd, vd, dg, xbc, dt, z, glu, cg = outs[:13]
    ctx_a = ctx_d = s0 = None
    if latent:
        kc, vc, kdc, vdc = _ctx_kv(caches['ckv'], caches['kr_pad'], caches['gk'], caches['gv'], l, w)
        ctx_a, ctx_d, s0 = (kc, vc), (kdc, vdc), caches['state']
    oa = _attention(qa, ka, va, ctx_a, ag, MLA_HEADS, 1, tq, "attn_mla")
    od = _attention(qd, kd, vd, ctx_d, dg, GQA_HEADS, GQA_HEADS // GQA_KV_HEADS, tq, "attn_gqa")
    xconv, oc = _convs(xbc, glu, cg, l, w, min(256, x.shape[1]))
    yf, yb, s_fin = _ssd(xconv, dt, s0, l, w)
    y = unflat(_outproj(flat(oa), flat(yf), flat(yb), flat(z), flat(oc), flat(od), flat(x), mod, mod_row, l, w, tm))
    new_ctx = None
    if not latent:
        ckv, kr, gk, gv = outs[13:]
        new_ctx = (ckv, kr, gk.reshape(b, -1, GQA_KV_HEADS, GQA_HD), gv.reshape(b, -1, GQA_KV_HEADS, GQA_HD),
                   s_fin.reshape(b, 2, SSD_HEADS, SSD_P, SSD_N))
    return y, new_ctx


def kernel(x_prompt, x_sample, cache_mla_ckv, cache_mla_krope, cache_gqa_k, cache_gqa_v, state_ssd, c, c_ctx, w_mod, b_mod, norm_g, w_in, w_out, mla_q_norm_g, mla_w_uq, mla_kv_norm_g, mla_w_ukv, mla_q_head_g, mla_k_head_g, ssd_conv_w, ssd_conv_b, ssd_dt_bias, ssd_a_log, ssd_d, ssd_norm_g, cf_conv_w, cf_conv_b, cf_ln_g, cf_ln_b, gqa_q_g, gqa_k_g):
    depth = w_in.shape[0]
    dec_b, n_lat = x_sample.shape[0], x_sample.shape[1]
    seq = x_prompt.shape[1]
    past = cache_mla_ckv.shape[2]
    assert dec_b < 8 and x_prompt.shape[-1] == D_MODEL and w_in.shape[-1] == sum(SPLITS)

    cvec = jnp.concatenate([c, c_ctx[None, :], jnp.zeros((8 - dec_b - 1, D_MODEL), F32)], axis=0)
    mod = _modulation(cvec, w_mod, b_mod).reshape(depth, 8, 3, D_MODEL)
    w = _prepare_weights(w_in, w_out, norm_g, mla_q_norm_g, mla_w_uq, mla_kv_norm_g, mla_w_ukv, mla_q_head_g,
                         mla_k_head_g, gqa_q_g, gqa_k_g, ssd_conv_w, ssd_conv_b, ssd_dt_bias, ssd_a_log, ssd_d,
                         ssd_norm_g, cf_conv_w, cf_conv_b, cf_ln_g, cf_ln_b)

    y = x_prompt
    ctx_out = []
    for l in range(depth):
        y, new_ctx = _mixer_layer(y, mod, dec_b, l, w, None, None, 512, min(256, seq))
        ctx_out.append(new_ctx)
    new = [jnp.stack([t[i] for t in ctx_out], axis=1) for i in range(5)]

    caches = dict(ckv=cache_mla_ckv,
                  kr_pad=_pad_last(cache_mla_krope, MLA_NOPE, LANES - MLA_QK),
                  gk=cache_gqa_k.reshape(dec_b, depth, past, GQA_KV_HEADS * GQA_HD),
                  gv=cache_gqa_v.reshape(dec_b, depth, past, GQA_KV_HEADS * GQA_HD),
                  state=state_ssd.reshape(dec_b, depth, 2, SSD_GROUPS, 2 * SSD_P, SSD_N))
    ropes = (_rope_table(n_lat, MLA_ROPE, MLA_NOPE), _rope_table(n_lat, GQA_HD, 0))
    z = x_sample
    for l in range(depth):
        z, _ = _mixer_layer(z, mod, None, l, w, caches, ropes, 512, 512)

    return (y, z, new[0], new[1], new[2], new[3], new[4])
```

```python
import functools
import math

import jax
import jax.numpy as jnp
from jax import lax
from jax.experimental import pallas as pl
from jax.experimental.pallas import tpu as pltpu

F32 = jnp.float32
BF16 = jnp.bfloat16

LANES = 128
VMEM_CAP = 56 * 1024 * 1024

EPS = 1e-6
ROPE_THETA = 10000.0
GRID_W = 64
LOG2E = math.log2(math.e)

D_MODEL = 1024
BRANCH_W = 256
MLA_HEADS, MLA_NOPE, MLA_ROPE, MLA_V = 4, 64, 32, 64
MLA_QK = MLA_NOPE + MLA_ROPE
MLA_Q_RANK, MLA_KV_RANK = 192, 128
SSD_HEADS, SSD_P, SSD_N, SSD_GROUPS, SSD_CONV, SSD_CHUNK = 4, 64, 64, 2, 3, 128
SSD_W = SSD_HEADS * SSD_P
SSD_CONV_CH = SSD_W + 2 * SSD_GROUPS * SSD_N
CF_W, CF_K = 256, 31
GQA_HEADS, GQA_KV_HEADS, GQA_HD = 4, 2, 64
SPLITS = (MLA_Q_RANK, MLA_KV_RANK, MLA_ROPE, MLA_HEADS * MLA_V,
          SSD_CONV_CH, 2 * SSD_HEADS, SSD_W,
          CF_W, CF_W, CF_W,
          GQA_HEADS * GQA_HD, GQA_KV_HEADS * GQA_HD, GQA_KV_HEADS * GQA_HD, GQA_HEADS * GQA_HD)

O_QL, O_KV, O_R, O_AG = 0, 256, 384, 512
O_XBC, O_DT, O_Z = 768, 1280, 1408
O_CV, O_CGL, O_CG = 1664, 1920, 2176
O_DQ, O_DK, O_DV, O_DG = 2432, 2944, 3200, 3456
IN_WP = 3712

CONV_HALO_SSD = 8
CONV_HALO_CF = 16

SSD_UNROLL = 4
INPROJ_SUB = 256
ATT_TQ = 1024
ATT_SUB = 512
ATT_QK_CHUNK = 512
ATT_PV_CHUNK = 256

NT_DIMS = (((1,), (1,)), ((), ()))


def _silu(x):
    return x / (1.0 + jnp.exp(-x))


def _sigmoid(x):
    return 1.0 / (1.0 + jnp.exp(-x))


def _softplus(x):
    return jnp.maximum(x, 0.0) + jnp.log1p(jnp.exp(-jnp.abs(x)))


def _rms(x, n):
    return x * lax.rsqrt(jnp.sum(x * x, axis=-1, keepdims=True) * (1.0 / n) + EPS)


def _dot(a, b):
    return jnp.dot(a, b, preferred_element_type=F32)


def _dot_nt(a, b):
    return lax.dot_general(a, b, NT_DIMS, preferred_element_type=F32)


def _params(sem, vmem_bytes):
    assert vmem_bytes <= VMEM_CAP
    return pltpu.CompilerParams(dimension_semantics=sem, vmem_limit_bytes=VMEM_CAP)


def _pad_heads(w, nh, d, dp):
    s = w.shape[:-1]
    w = w.reshape(s + (nh, d))
    w = jnp.pad(w, [(0, 0)] * len(s) + [(0, 0), (0, dp - d)])
    return w.reshape(s + (nh * dp,))


def _pad_last(w, lo, hi):
    return jnp.pad(w, [(0, 0)] * (w.ndim - 1) + [(lo, hi)])


def _layer_spec(a, l, grid_rank):
    zeros = (0,) * (a.ndim - 1)
    if grid_rank == 1:
        return pl.BlockSpec((1,) + a.shape[1:], lambda bi: (l,) + zeros)
    return pl.BlockSpec((1,) + a.shape[1:], lambda bi, i: (l,) + zeros)


def _with_ones_lane(v, hd):
    lane = lax.broadcasted_iota(jnp.int32, v.shape, v.ndim - 1)
    return jnp.where(lane % LANES == hd, 1.0, v)


def _mod_kernel(c_ref, w_ref, b_ref, o_ref):
    c = c_ref[...]
    o_ref[0] = _dot(_silu(c).astype(BF16), w_ref[0].astype(BF16)) + b_ref[0]


def _modulation(cvec, w_mod, b_mod):
    depth, d, d3 = w_mod.shape
    tn = d
    return pl.pallas_call(
        _mod_kernel,
        out_shape=jax.ShapeDtypeStruct((depth, 8, d3), F32),
        grid=(depth, d3 // tn),
        in_specs=[pl.BlockSpec((8, d), lambda l, j: (0, 0)),
                  pl.BlockSpec((1, d, tn), lambda l, j: (l, 0, j)),
                  pl.BlockSpec((1, 1, tn), lambda l, j: (l, 0, j))],
        out_specs=pl.BlockSpec((1, 8, tn), lambda l, j: (l, 0, j)),
        compiler_params=_params(("arbitrary", "arbitrary"), 4 * d * tn * 4),
        name="modulation",
    )(cvec, w_mod, b_mod.reshape(depth, 1, d3))


def _rope(x, tab_ref, rows, shift):
    cos, sin_a, sin_b = tab_ref[0, rows, :], tab_ref[1, rows, :], tab_ref[2, rows, :]
    outs = []
    for h in range(x.shape[-1] // LANES):
        xh = x[:, h * LANES:(h + 1) * LANES]
        up = pltpu.roll(xh, LANES - shift, 1)
        dn = pltpu.roll(xh, shift, 1)
        outs.append(xh * cos + up * sin_a + dn * sin_b)
    return jnp.concatenate(outs, axis=-1)


def _head_norm(x, gain, n, ones_bd):
    xx = (x * x).astype(BF16)
    w2 = 2 * LANES
    ss = jnp.concatenate([_dot(xx[:, j:j + w2], ones_bd) for j in range(0, x.shape[-1], w2)], axis=-1)
    return x * lax.rsqrt(ss * (1.0 / n) + EPS) * gain


def _inproj_refs(latent, refs):
    names = ['x', 'mod', 'ng', 'win', 'wuq', 'wukv', 'gains']
    if latent:
        names += ['ra', 'rd']
    names += ['qa', 'ka', 'va', 'ag', 'qd', 'kd', 'vd', 'dg', 'xbc', 'dt', 'z', 'glu', 'cg']
    if not latent:
        names += ['ckv', 'kr', 'gk', 'gv']
    return dict(zip(names, refs))


def _inproj_main(r, rows):
    x = r['x'][0, rows, :]
    shift = r['mod'][0, 0, 0:1, :]
    scale = r['mod'][0, 0, 1:2, :]
    h = _rms(x, D_MODEL) * r['ng'][0] * (1.0 + scale) + shift
    return _dot(h.astype(BF16), r['win'][0])


def _inproj_latents(r, u):
    g_ql = r['gains'][0, 0:1, 0:256]
    g_kv = r['gains'][0, 2:3, 0:LANES]
    ql = _rms(u[:, O_QL:O_QL + 256], MLA_Q_RANK) * g_ql
    ckv = _rms(u[:, O_KV:O_KV + LANES], MLA_KV_RANK) * g_kv
    return _dot(ql.astype(BF16), r['wuq'][0]), ckv, _dot(ckv.astype(BF16), r['wukv'][0])


def _inproj_tail(latent, r, rows, u, q, ckv, kv):
    wa, wd = MLA_HEADS * LANES, GQA_KV_HEADS * LANES
    g_qh = r['gains'][0, 1:2, :]
    g_kh = r['gains'][0, 3:4, :]
    g_dq = r['gains'][0, 4:5, :]
    g_dk = r['gains'][0, 5:6, 0:wd]
    ri = lax.broadcasted_iota(jnp.int32, (2 * LANES, 2 * LANES), 0) // LANES
    ci = lax.broadcasted_iota(jnp.int32, (2 * LANES, 2 * LANES), 1) // LANES
    ones_bd = (ri == ci).astype(F32).astype(BF16)

    q = _head_norm(q, g_qh, MLA_QK, ones_bd)
    kr = u[:, O_R:O_R + LANES]
    k = _head_norm(kv[:, 0:wa] + jnp.concatenate([kr] * MLA_HEADS, axis=-1), g_kh, MLA_QK, ones_bd)
    qd = _head_norm(u[:, O_DQ:O_DQ + GQA_HEADS * LANES], g_dq, GQA_HD, ones_bd)
    kd = _head_norm(u[:, O_DK:O_DK + wd], g_dk, GQA_HD, ones_bd)
    vd = u[:, O_DV:O_DV + wd]
    if latent:
        q = _rope(q, r['ra'], rows, MLA_ROPE // 4)
        k = _rope(k, r['ra'], rows, MLA_ROPE // 4)
        qd = _rope(qd, r['rd'], rows, GQA_HD // 4)
        kd_att = _rope(kd, r['rd'], rows, GQA_HD // 4)
    else:
        kd_att = kd
        r['ckv'][0, rows, :] = ckv
        r['kr'][0, rows, :] = kr[:, MLA_NOPE:MLA_NOPE + MLA_ROPE]
        for i in range(GQA_KV_HEADS):
            r['gk'][0, rows, i * GQA_HD:(i + 1) * GQA_HD] = kd[:, i * LANES:i * LANES + GQA_HD]
            r['gv'][0, rows, i * GQA_HD:(i + 1) * GQA_HD] = vd[:, i * LANES:i * LANES + GQA_HD]
    r['qa'][0, rows, :] = q.astype(BF16)
    r['ka'][0, rows, :] = k.astype(BF16)
    r['va'][0, rows, :] = _with_ones_lane(kv[:, wa:], MLA_V).astype(BF16)
    r['ag'][0, rows, :] = u[:, O_AG:O_AG + BRANCH_W]
    r['qd'][0, rows, :] = qd.astype(BF16)
    r['kd'][0, rows, :] = kd_att.astype(BF16)
    r['vd'][0, rows, :] = _with_ones_lane(vd, GQA_HD).astype(BF16)
    r['dg'][0, rows, :] = u[:, O_DG:O_DG + BRANCH_W]

    r['xbc'][0, rows, :] = u[:, O_XBC:O_XBC + SSD_CONV_CH]
    r['dt'][0, rows, :] = u[:, O_DT:O_DT + LANES]
    r['z'][0, rows, :] = u[:, O_Z:O_Z + SSD_W]
    r['glu'][0, rows, :] = u[:, O_CV:O_CV + CF_W] * _sigmoid(u[:, O_CGL:O_CGL + CF_W])
    r['cg'][0, rows, :] = u[:, O_CG:O_CG + CF_W]


def _inproj_kernel(latent, sub, *refs):
    r = _inproj_refs(latent, refs)
    tm = r['x'].shape[1]
    for r0 in range(0, tm, sub):
        rows = slice(r0, r0 + sub)
        u = _inproj_main(r, rows)
        q, ckv, kv = _inproj_latents(r, u)
        _inproj_tail(latent, r, rows, u, q, ckv, kv)


def _mod_spec(mod, l, row):
    d = mod.shape[-1]
    if row is None:
        return pl.BlockSpec((1, 1, 3, d), lambda bi, i: (l, bi, 0, 0))
    return pl.BlockSpec((1, 1, 3, d), lambda bi, i: (l, row, 0, 0))


def _inproj(x, mod, mod_row, l, w, ropes, tm):
    b, s, d = x.shape
    latent = ropes is not None
    tok = lambda wd: pl.BlockSpec((1, tm, wd), lambda bi, i: (bi, i, 0))
    weights = [w['ng'], w['win'], w['wuq'], w['wukv'], w['gains']]
    in_specs = [tok(d), _mod_spec(mod, l, mod_row)] + [_layer_spec(a, l, 2) for a in weights]
    args = [x, mod] + weights
    if latent:
        in_specs += [pl.BlockSpec((3, tm, LANES), lambda bi, i: (0, i, 0))] * 2
        args += list(ropes)
    widths = [(4 * LANES, BF16), (4 * LANES, BF16), (4 * LANES, BF16), (BRANCH_W, F32),
              (4 * LANES, BF16), (2 * LANES, BF16), (2 * LANES, BF16), (BRANCH_W, F32),
              (SSD_CONV_CH, F32), (LANES, F32), (SSD_W, F32), (CF_W, F32), (CF_W, F32)]
    if not latent:
        widths += [(MLA_KV_RANK, F32), (MLA_ROPE, F32), (LANES, F32), (LANES, F32)]
    out_shape = [jax.ShapeDtypeStruct((b, s, wd), dt) for wd, dt in widths]
    out_specs = [tok(wd) for wd, _ in widths]
    out_bytes = sum(wd * jnp.dtype(dt).itemsize for wd, dt in widths) * tm
    w_bytes = sum(a[0].size * a.dtype.itemsize for a in weights)
    sub = min(INPROJ_SUB, tm)
    vmem = 2 * (tm * d * 4 + w_bytes + out_bytes) + 3 * sub * IN_WP * 4
    return pl.pallas_call(
        functools.partial(_inproj_kernel, latent, sub),
        out_shape=out_shape,
        grid=(b, s // tm),
        in_specs=in_specs,
        out_specs=out_specs,
        compiler_params=_params(("parallel", "parallel"), vmem),
        name="inproj_lat" if latent else "inproj_ctx",
    )(*args)


def _ctxkv_kernel(ckv_ref, kr_ref, gk_ref, gv_ref, wukv_ref, gains_ref, kc_ref, vc_ref, kdc_ref, vdc_ref):
    kv = _dot(ckv_ref[0, 0].astype(BF16), wukv_ref[0])
    kr = kr_ref[0, 0]
    g = gains_ref[0, 3:4, 0:LANES]
    for i in range(MLA_HEADS):
        k = _rms(kv[:, i * LANES:(i + 1) * LANES] + kr, MLA_QK) * g
        kc_ref[0, :, i * LANES:(i + 1) * LANES] = k.astype(BF16)
    vc_ref[0] = _with_ones_lane(kv[:, MLA_HEADS * LANES:], MLA_V).astype(BF16)
    gk = gk_ref[0, 0]
    gv = gv_ref[0, 0]
    p = gk.shape[0]
    pad = jnp.zeros((p, LANES - GQA_HD), F32)
    for i in range(GQA_KV_HEADS):
        lo = i * LANES
        kdc_ref[0, :, lo:lo + GQA_HD] = gk[:, i * GQA_HD:(i + 1) * GQA_HD].astype(BF16)
        kdc_ref[0, :, lo + GQA_HD:lo + LANES] = pad.astype(BF16)
        vdc_ref[0, :, lo:lo + GQA_HD] = gv[:, i * GQA_HD:(i + 1) * GQA_HD].astype(BF16)
        vdc_ref[0, :, lo + GQA_HD:lo + LANES] = _with_ones_lane(pad, 0).astype(BF16)


def _ctx_kv(ckv, kr_pad, gk, gv, l, w):
    b, _, p, _ = ckv.shape
    cache = lambda a: pl.BlockSpec((1, 1, p, a.shape[-1]), lambda bi: (bi, l, 0, 0))
    out = lambda wd: pl.BlockSpec((1, p, wd), lambda bi: (bi, 0, 0))
    wa, wd = MLA_HEADS * LANES, GQA_KV_HEADS * LANES
    return pl.pallas_call(
        _ctxkv_kernel,
        out_shape=[jax.ShapeDtypeStruct((b, p, wa), BF16), jax.ShapeDtypeStruct((b, p, wa), BF16),
                   jax.ShapeDtypeStruct((b, p, wd), BF16), jax.ShapeDtypeStruct((b, p, wd), BF16)],
        grid=(b,),
        in_specs=[cache(ckv), cache(kr_pad), cache(gk), cache(gv),
                  _layer_spec(w['wukv'], l, 1), _layer_spec(w['gains'], l, 1)],
        out_specs=[out(wa), out(wa), out(wd), out(wd)],
        compiler_params=_params(("parallel",), 16 * p * wa * 4),
        name="ctx_kv",
    )(ckv, kr_pad, gk, gv, w['wukv'], w['gains'])


def _lane_tile_max(t):
    r = t[:, 0:LANES]
    for j in range(1, t.shape[-1] // LANES):
        r = jnp.maximum(r, t[:, j * LANES:(j + 1) * LANES])
    return r


def _attn_unit(n_heads, rep, has_ctx, refs, bi, rows, h, s_ref):
    if has_ctx:
        q_ref, k_ref, v_ref, kc_ref, vc_ref, g_ref, o_ref = refs
        p_len = kc_ref.shape[1]
    else:
        q_ref, k_ref, v_ref, g_ref, o_ref = refs
        p_len = 0
    s_len = k_ref.shape[1]
    hd = o_ref.shape[-1] // n_heads
    kb = min(ATT_QK_CHUNK, s_len)
    kb2 = ATT_PV_CHUNK
    g = h // rep
    hs = slice(h * LANES, (h + 1) * LANES)
    gs = slice(g * LANES, (g + 1) * LANES)
    qh = q_ref[bi, rows, hs]
    mx = None
    for c in range(s_len // kb):
        sc = _dot_nt(qh, k_ref[bi, c * kb:(c + 1) * kb, gs])
        s_ref[:, c * kb:(c + 1) * kb] = sc
        t = _lane_tile_max(sc)
        mx = t if mx is None else jnp.maximum(mx, t)
    if has_ctx:
        sc = _dot_nt(qh, kc_ref[bi, :, gs])
        s_ref[:, s_len:s_len + p_len] = sc
        mx = jnp.maximum(mx, _lane_tile_max(sc))
    m = jnp.max(mx, axis=-1, keepdims=True)
    acc = None
    for c in range((s_len + p_len) // kb2):
        r0 = c * kb2
        p = jnp.exp2(s_ref[:, r0:r0 + kb2] - m).astype(BF16)
        vv = v_ref[bi, r0:r0 + kb2, gs] if r0 < s_len else vc_ref[bi, r0 - s_len:r0 - s_len + kb2, gs]
        part = _dot(p, vv)
        acc = part if acc is None else acc + part
    o = acc[:, 0:hd] / acc[:, hd:hd + 1]
    os_ = slice(h * hd, (h + 1) * hd)
    o_ref[bi, rows, os_] = o * _silu(g_ref[bi, rows, os_])


def _attn_kernel(n_heads, rep, has_ctx, sub, *refs):
    s0_ref, s1_ref = refs[-2:]
    bb, tq = refs[0].shape[0], refs[0].shape[1]
    unit = 0
    for bi in range(bb):
        for r0 in range(0, tq, sub):
            for h in range(n_heads):
                _attn_unit(n_heads, rep, has_ctx, refs[:-2], bi, slice(r0, r0 + sub), h,
                           s0_ref if unit % 2 == 0 else s1_ref)
                unit += 1


def _attention(q, k, v, ctx, gate, n_heads, rep, bb, tq, sub, name):
    b, s, qw = q.shape
    kw = k.shape[-1]
    ow = gate.shape[-1]
    has_ctx = ctx is not None
    in_specs = [pl.BlockSpec((bb, tq, qw), lambda bi, i: (bi, i, 0)),
                pl.BlockSpec((bb, s, kw), lambda bi, i: (bi, 0, 0)),
                pl.BlockSpec((bb, s, kw), lambda bi, i: (bi, 0, 0))]
    args = [q, k, v]
    p = 0
    if has_ctx:
        p = ctx[0].shape[1]
        in_specs += [pl.BlockSpec((bb, p, kw), lambda bi, i: (bi, 0, 0))] * 2
        args += list(ctx)
    in_specs.append(pl.BlockSpec((bb, tq, ow), lambda bi, i: (bi, i, 0)))
    args.append(gate)
    assert s % min(ATT_QK_CHUNK, s) == 0 and s % ATT_PV_CHUNK == 0 and p % ATT_PV_CHUNK == 0
    assert b % bb == 0 and s % tq == 0 and tq % sub == 0
    vmem = (2 * bb * (tq * qw * 2 + 2 * (s + p) * kw * 2 + 2 * tq * ow * 4)
            + 2 * sub * (s + p) * 4 + 4 * sub * 1024 * 4)
    return pl.pallas_call(
        functools.partial(_attn_kernel, n_heads, rep, has_ctx, sub),
        out_shape=jax.ShapeDtypeStruct((b, s, ow), F32),
        grid=(b // bb, s // tq),
        in_specs=in_specs,
        out_specs=pl.BlockSpec((bb, tq, ow), lambda bi, i: (bi, i, 0)),
        scratch_shapes=[pltpu.VMEM((sub, s + p), F32), pltpu.VMEM((sub, s + p), F32)],
        compiler_params=_params(("parallel", "arbitrary"), vmem),
        name=name,
    )(*args)


def _conv_kernel(tl, xc_ref, xp_ref, xn_ref, gc_ref, gp_ref, gn_ref, cg_ref,
                 w3_ref, b3_ref, w31_ref, b31_ref, lng_ref, lnb_ref,
                 xo_ref, co_ref, ext3, ext31):
    i = pl.program_id(1)
    keep_prev = (i > 0).astype(F32)
    keep_next = (i < pl.num_programs(1) - 1).astype(F32)

    h3 = CONV_HALO_SSD
    ext3[0:h3, :] = xp_ref[0] * keep_prev
    ext3[h3:h3 + tl, :] = xc_ref[0]
    ext3[h3 + tl:2 * h3 + tl, :] = xn_ref[0] * keep_next
    acc = jnp.zeros((tl, SSD_CONV_CH), F32) + b3_ref[0]
    for k in range(SSD_CONV):
        acc = acc + w3_ref[0, k:k + 1, :] * ext3[h3 - SSD_CONV // 2 + k:h3 - SSD_CONV // 2 + k + tl, :]
    xo_ref[0] = _silu(acc)

    h31 = CONV_HALO_CF
    ext31[0:h31, :] = gp_ref[0] * keep_prev
    ext31[h31:h31 + tl, :] = gc_ref[0]
    ext31[h31 + tl:2 * h31 + tl, :] = gn_ref[0] * keep_next
    acc = jnp.zeros((tl, CF_W), F32) + b31_ref[0]
    o_lo = h31 - CF_K // 2
    ext = ext31[...]
    n_ext = ext.shape[0]
    for r in range(8):
        taps = [k for k in range(CF_K) if (o_lo + k) % 8 == r]
        if not taps:
            continue
        shifted = ext if r == 0 else pltpu.roll(ext, n_ext - r, 0)
        for k in taps:
            a = (o_lo + k) // 8 * 8
            acc = acc + w31_ref[0, k:k + 1, :] * shifted[a:a + tl, :]
    mu = jnp.mean(acc, axis=-1, keepdims=True)
    cen = acc - mu
    var = jnp.mean(cen * cen, axis=-1, keepdims=True)
    y = cen * lax.rsqrt(var + EPS) * lng_ref[0] + lnb_ref[0]
    co_ref[0] = _silu(y) * _silu(cg_ref[0])


def _convs(xbc, glu, cg, l, w, tl):
    b, s, _ = xbc.shape
    nt = s // tl

    def cur(wd):
        return pl.BlockSpec((1, tl, wd), lambda bi, i: (bi, i, 0))

    def prev(wd, hrows):
        r = tl // hrows
        return pl.BlockSpec((1, hrows, wd), lambda bi, i: (bi, jnp.maximum(i * r - 1, 0), 0))

    def nxt(wd, hrows):
        r = tl // hrows
        return pl.BlockSpec((1, hrows, wd), lambda bi, i: (bi, jnp.minimum((i + 1) * r, nt * r - 1), 0))

    weights = [w['w3'], w['b3'], w['w31'], w['b31'], w['lng'], w['lnb']]
    vmem = 4 * tl * (SSD_CONV_CH * 2 + CF_W * 3) * 4 + 6 * tl * SSD_CONV_CH * 4
    return pl.pallas_call(
        functools.partial(_conv_kernel, tl),
        out_shape=[jax.ShapeDtypeStruct((b, s, SSD_CONV_CH), F32),
                   jax.ShapeDtypeStruct((b, s, CF_W), F32)],
        grid=(b, nt),
        in_specs=[cur(SSD_CONV_CH), prev(SSD_CONV_CH, CONV_HALO_SSD), nxt(SSD_CONV_CH, CONV_HALO_SSD),
                  cur(CF_W), prev(CF_W, CONV_HALO_CF), nxt(CF_W, CONV_HALO_CF), cur(CF_W)]
                 + [_layer_spec(a, l, 2) for a in weights],
        out_specs=[cur(SSD_CONV_CH), cur(CF_W)],
        scratch_shapes=[pltpu.VMEM((tl + 2 * CONV_HALO_SSD, SSD_CONV_CH), F32),
                        pltpu.VMEM((tl + 2 * CONV_HALO_CF, CF_W), F32)],
        compiler_params=_params(("parallel", "parallel"), vmem),
        name="convs",
    )(xbc, xbc, xbc, glu, glu, glu, cg, *weights)


def _split3(x):
    hi = x.astype(BF16)
    r1 = x - hi.astype(F32)
    mid = r1.astype(BF16)
    lo = (r1 - mid.astype(F32)).astype(BF16)
    return hi, mid, lo


def _cumsum_rows(tri, x):
    return sum(_dot(tri, part) for part in _split3(x))


def _transpose_exact(x):
    c = x.shape[1]
    eye = (lax.broadcasted_iota(jnp.int32, (c, c), 0) == lax.broadcasted_iota(jnp.int32, (c, c), 1))
    eye = eye.astype(F32).astype(BF16)
    return sum(_dot_nt(eye, part) for part in _split3(x))


SSD_NK = 2 * SSD_HEADS


def _ssd_stage_local(c, d, xc_ref, dt_ref, bias, a_row):
    q = SSD_CHUNK
    rows = pl.ds(pl.multiple_of(c * q, q), q)
    t = dict(rows=rows, d=d)
    t['xs'] = xc_ref[0, rows, 0:SSD_W]
    t['bm'] = xc_ref[0, rows, SSD_W:SSD_W + LANES]
    t['cm'] = xc_ref[0, rows, SSD_W + LANES:SSD_W + 2 * LANES].astype(BF16)
    t['dt'] = _softplus(dt_ref[0, rows, :] + bias)
    ri = lax.broadcasted_iota(jnp.int32, (q, q), 0)
    ci = lax.broadcasted_iota(jnp.int32, (q, q), 1)
    t['mask'] = (ri >= ci) if d == 0 else (ri <= ci)
    t['acum'] = _cumsum_rows(t['mask'].astype(F32).astype(BF16), t['dt'] * a_row)
    return t


def _ssd_stage_cb(t):
    q = SSD_CHUNK
    lane = lax.broadcasted_iota(jnp.int32, (q, LANES), 1)
    t['tr'] = jnp.where(lane < SSD_NK, t['dt'], t['acum']).T
    bt = t['bm'].T
    t['bt'] = bt
    grp_row = lax.broadcasted_iota(jnp.int32, (LANES, q), 0) < SSD_N
    t['cb'] = [_dot(t['cm'], jnp.where(grp_row if g == 0 else jnp.logical_not(grp_row), bt, 0.0).astype(BF16))
               for g in range(SSD_GROUPS)]
    t['last'] = q - 1 if t['d'] == 0 else 0
    t['etot'] = jnp.exp(t['acum'][t['last']:t['last'] + 1, :])
    t['eac'] = jnp.exp(t['acum'])


def _ssd_stage_diag(t, dskip):
    q = SSD_CHUNK
    d = t['d']
    lo_half = lax.broadcasted_iota(jnp.int32, (q, LANES), 1) < SSD_P
    tr, acum, last = t['tr'], t['acum'], t['last']
    t['yd'], t['cs'] = [], []
    for g in range(SSD_GROUPS):
        k0 = d * SSD_HEADS + 2 * g
        xg = t['xs'][:, g * LANES:(g + 1) * LANES]
        xgb = xg.astype(BF16)
        btg = t['bt'][g * SSD_N:(g + 1) * SSD_N, :]
        yg = None
        cs = []
        for hh in range(2):
            k = k0 + hh
            dt_row = tr[k:k + 1, :]
            ac_row = tr[SSD_NK + k:SSD_NK + k + 1, :]
            dec = jnp.where(t['mask'], jnp.exp(acum[:, k:k + 1] - ac_row), 0.0)
            xm = jnp.where(lo_half if hh == 0 else jnp.logical_not(lo_half), xg, 0.0).astype(BF16)
            part = _dot((t['cb'][g] * dec * dt_row).astype(BF16), xm)
            yg = part if yg is None else yg + part
            w_row = dt_row * jnp.exp(tr[SSD_NK + k:SSD_NK + k + 1, last:last + 1] - ac_row)
            cs.append(_dot((btg * w_row).astype(BF16), xgb))
        if d == 0:
            yg = yg + dskip[:, g * LANES:(g + 1) * LANES] * xg
        t['yd'].append(yg)
        t['cs'].append(cs)


def _ssd_stage_state(t, states):
    q = SSD_CHUNK
    d = t['d']
    lo_half = lax.broadcasted_iota(jnp.int32, (q, LANES), 1) < SSD_P
    lane_n = lax.broadcasted_iota(jnp.int32, (SSD_N, LANES), 1) < SSD_P
    zero = jnp.zeros((SSD_N, LANES), F32)
    ys, new_states = [], []
    for g in range(SSD_GROUPS):
        k0 = d * SSD_HEADS + 2 * g
        s_prev = states[g]
        sz = jnp.concatenate([s_prev, zero] if g == 0 else [zero, s_prev], axis=0).astype(BF16)
        eacs = jnp.where(lo_half, t['eac'][:, k0:k0 + 1], t['eac'][:, k0 + 1:k0 + 2])
        ys.append(t['yd'][g] + _dot(t['cm'], sz) * eacs)
        cdec = jnp.where(lane_n, t['etot'][:, k0:k0 + 1], t['etot'][:, k0 + 1:k0 + 2])
        new_states.append(cdec * s_prev + jnp.where(lane_n, t['cs'][g][0], t['cs'][g][1]))
    return ys, new_states


def _ssd_kernel(nc, unroll, has_init, *refs):
    if has_init:
        xc_ref, dt_ref, s0_ref, bias_ref, alog_ref, d_ref, yf_ref, yb_ref, so_ref, st_ref = refs
        for d in range(2):
            for g in range(SSD_GROUPS):
                st_ref[0, d, g] = _transpose_exact(s0_ref[0, 0, d, g])
    else:
        xc_ref, dt_ref, bias_ref, alog_ref, d_ref, yf_ref, yb_ref, so_ref, st_ref = refs
        st_ref[...] = jnp.zeros(st_ref.shape, F32)
    lane = lax.broadcasted_iota(jnp.int32, (1, LANES), 1)
    a_row = jnp.where(lane < 2 * SSD_NK, -jnp.exp(alog_ref[0]), 0.0)
    bias = bias_ref[0]
    dskip = d_ref[0]

    def body(i, carry):
        states = [[st_ref[0, d, g] for g in range(SSD_GROUPS)] for d in range(2)]
        work = []
        for j in range(unroll):
            c = i * unroll + j
            work.append((_ssd_stage_local(c, 0, xc_ref, dt_ref, bias, a_row), yf_ref))
            work.append((_ssd_stage_local(nc - 1 - c, 1, xc_ref, dt_ref, bias, a_row), yb_ref))
        for t, _ in work:
            _ssd_stage_cb(t)
        for t, _ in work:
            _ssd_stage_diag(t, dskip)
        writes = []
        for t, y_ref in work:
            ys, states[t['d']] = _ssd_stage_state(t, states[t['d']])
            writes.append((y_ref, t['rows'], ys))
        for y_ref, rows, ys in writes:
            for g in range(SSD_GROUPS):
                y_ref[0, rows, g * LANES:(g + 1) * LANES] = ys[g]
        for d in range(2):
            for g in range(SSD_GROUPS):
                st_ref[0, d, g] = states[d][g]
        return carry

    lax.fori_loop(0, nc // unroll, body, 0)
    for d in range(2):
        for g in range(SSD_GROUPS):
            so_ref[0, d, g] = _transpose_exact(st_ref[0, d, g])


def _ssd(xc, dt, s0, l, w):
    b, s, _ = xc.shape
    nc = s // SSD_CHUNK
    unroll = min(SSD_UNROLL, nc)
    assert nc % unroll == 0
    st_shape = (b, 2, SSD_GROUPS, 2 * SSD_P, SSD_N)
    st_spec = pl.BlockSpec((1,) + st_shape[1:], lambda bi: (bi, 0, 0, 0, 0))
    y_spec = pl.BlockSpec((1, s, SSD_W), lambda bi: (bi, 0, 0))
    in_specs = [pl.BlockSpec((1, s, SSD_CONV_CH), lambda bi: (bi, 0, 0)),
                pl.BlockSpec((1, s, LANES), lambda bi: (bi, 0, 0))]
    args = [xc, dt]
    if s0 is not None:
        in_specs.append(pl.BlockSpec((1, 1) + st_shape[1:], lambda bi: (bi, l, 0, 0, 0, 0)))
        args.append(s0)
    weights = [w['dt_bias'], w['a_log'], w['dskip']]
    vmem = 2 * s * (SSD_CONV_CH + LANES + 2 * SSD_W) * 4 + 8 * 1024 * 1024
    return pl.pallas_call(
        functools.partial(_ssd_kernel, nc, unroll, s0 is not None),
        out_shape=[jax.ShapeDtypeStruct((b, s, SSD_W), F32), jax.ShapeDtypeStruct((b, s, SSD_W), F32),
                   jax.ShapeDtypeStruct(st_shape, F32)],
        grid=(b,),
        in_specs=in_specs + [_layer_spec(a, l, 1) for a in weights],
        out_specs=[y_spec, y_spec, st_spec],
        scratch_shapes=[pltpu.VMEM((1, 2, SSD_GROUPS, SSD_N, 2 * SSD_P), F32)],
        compiler_params=_params(("parallel",), vmem),
        name="ssd",
    )(*args, *weights)


def _outproj_kernel(oa_ref, yf_ref, yb_ref, z_ref, oc_ref, od_ref, x_ref, mod_ref, ng_ref, w_ref, o_ref):
    ob = _rms((yf_ref[0] + yb_ref[0]) * _silu(z_ref[0]), SSD_W) * ng_ref[0]
    acc = _dot(oa_ref[0].astype(BF16), w_ref[0, 0:BRANCH_W, :])
    acc = acc + _dot(ob.astype(BF16), w_ref[0, BRANCH_W:2 * BRANCH_W, :])
    acc = acc + _dot(oc_ref[0].astype(BF16), w_ref[0, 2 * BRANCH_W:3 * BRANCH_W, :])
    acc = acc + _dot(od_ref[0].astype(BF16), w_ref[0, 3 * BRANCH_W:4 * BRANCH_W, :])
    o_ref[0] = x_ref[0] + mod_ref[0, 0, 2:3, :] * acc


def _outproj(oa, yf, yb, z, oc, od, x, mod, mod_row, l, w, tm):
    b, s, d = x.shape
    br = lambda: pl.BlockSpec((1, tm, BRANCH_W), lambda bi, i: (bi, i, 0))
    xs = pl.BlockSpec((1, tm, d), lambda bi, i: (bi, i, 0))
    vmem = 2 * (6 * tm * BRANCH_W * 4 + 2 * tm * d * 4 + d * d * 2) + 2 * tm * d * 4
    return pl.pallas_call(
        _outproj_kernel,
        out_shape=jax.ShapeDtypeStruct((b, s, d), F32),
        grid=(b, s // tm),
        in_specs=[br(), br(), br(), br(), br(), br(), xs, _mod_spec(mod, l, mod_row),
                  _layer_spec(w['ssd_ng'], l, 2), _layer_spec(w['wout'], l, 2)],
        out_specs=xs,
        compiler_params=_params(("parallel", "parallel"), vmem),
        name="outproj",
    )(oa, yf, yb, z, oc, od, x, mod, w['ssd_ng'], w['wout'])


def _rope_table(n_tokens, dim, lane0):
    n_rows = n_tokens // GRID_W
    row = jnp.repeat(jnp.arange(n_rows), GRID_W).astype(F32)
    col = jnp.tile(jnp.arange(GRID_W), n_rows).astype(F32)
    quarter = dim // 4
    inv = ROPE_THETA ** (-jnp.arange(quarter, dtype=F32) / quarter)
    ar = row[:, None] * inv
    ac = col[:, None] * inv
    zero = jnp.zeros_like(ar)
    cos = jnp.concatenate([jnp.cos(ar), jnp.cos(ar), jnp.cos(ac), jnp.cos(ac)], axis=-1)
    sin_a = jnp.concatenate([-jnp.sin(ar), zero, -jnp.sin(ac), zero], axis=-1)
    sin_b = jnp.concatenate([zero, jnp.sin(ar), zero, jnp.sin(ac)], axis=-1)
    hi = LANES - lane0 - dim
    cos = jnp.pad(cos, ((0, 0), (lane0, hi)), constant_values=1.0)
    return jnp.stack([cos, _pad_last(sin_a, lane0, hi), _pad_last(sin_b, lane0, hi)])


def _prepare_weights(w_in, w_out, norm_g, mla_q_norm_g, mla_w_uq, mla_kv_norm_g, mla_w_ukv, mla_q_head_g,
                     mla_k_head_g, gqa_q_g, gqa_k_g, ssd_conv_w, ssd_conv_b, ssd_dt_bias, ssd_a_log, ssd_d,
                     ssd_norm_g, cf_conv_w, cf_conv_b, cf_ln_g, cf_ln_b):
    depth = w_in.shape[0]
    pieces = []
    o = 0
    for sz in SPLITS:
        pieces.append(w_in[:, :, o:o + sz])
        o += sz
    (a_q, a_kv, a_r, a_g, b_xbc, b_dt, b_z, c_v, c_gl, c_g, d_q, d_k, d_v, d_g) = pieces
    win = jnp.concatenate([
        _pad_last(a_q, 0, 256 - MLA_Q_RANK), a_kv, _pad_last(a_r, MLA_NOPE, LANES - MLA_QK), a_g,
        b_xbc, _pad_last(jnp.concatenate([b_dt, b_dt], axis=-1), 0, LANES - 2 * SSD_NK), b_z, c_v, c_gl, c_g,
        _pad_heads(d_q, GQA_HEADS, GQA_HD, LANES), _pad_heads(d_k, GQA_KV_HEADS, GQA_HD, LANES),
        _pad_heads(d_v, GQA_KV_HEADS, GQA_HD, LANES), d_g], axis=-1).astype(BF16)
    wuq = jnp.pad(_pad_heads(mla_w_uq, MLA_HEADS, MLA_QK, LANES),
                  ((0, 0), (0, 256 - MLA_Q_RANK), (0, 0))).astype(BF16)
    kv = mla_w_ukv.reshape(depth, MLA_KV_RANK, MLA_HEADS, MLA_NOPE + MLA_V)
    wk = _pad_last(kv[..., :MLA_NOPE], 0, LANES - MLA_NOPE).reshape(depth, MLA_KV_RANK, -1)
    wv = _pad_last(kv[..., MLA_NOPE:], 0, LANES - MLA_V).reshape(depth, MLA_KV_RANK, -1)
    wukv = jnp.concatenate([wk, wv], axis=-1).astype(BF16)
    gw = MLA_HEADS * LANES
    row = lambda v: _pad_last(v, 0, gw - v.shape[-1])
    slots = lambda v, n: row(jnp.tile(_pad_last(v, 0, LANES - v.shape[-1]), (1, n)))
    zero = jnp.zeros((depth, gw), F32)
    gains = jnp.stack([row(mla_q_norm_g),
                       slots(mla_q_head_g * (MLA_QK ** -0.5 * LOG2E), MLA_HEADS),
                       row(mla_kv_norm_g),
                       slots(mla_k_head_g, MLA_HEADS),
                       slots(gqa_q_g * (GQA_HD ** -0.5 * LOG2E), GQA_HEADS),
                       slots(gqa_k_g, GQA_KV_HEADS), zero, zero], axis=1)
    vec = lambda v: v[:, None, :]
    lanes8 = lambda v: _pad_last(jnp.tile(v.reshape(depth, 1, -1), (1, 1, 2)), 0, LANES - 2 * SSD_NK)
    return dict(win=win, wuq=wuq, wukv=wukv, gains=gains, wout=w_out.astype(BF16), ng=vec(norm_g),
                w3=ssd_conv_w, b3=vec(ssd_conv_b), w31=cf_conv_w, b31=vec(cf_conv_b),
                lng=vec(cf_ln_g), lnb=vec(cf_ln_b), dt_bias=lanes8(ssd_dt_bias), a_log=lanes8(ssd_a_log),
                dskip=vec(jnp.repeat(ssd_d, SSD_P, axis=-1)), ssd_ng=vec(ssd_norm_g))


def _mixer_layer(x, mod, mod_row, l, w, caches, ropes, tm):
    latent = caches is not None
    b, s, d = x.shape
    flat = (lambda a: a) if latent else (lambda a: a.reshape(1, b * s, a.shape[-1]))
    unflat = (lambda a: a) if latent else (lambda a: a.reshape(b, s, a.shape[-1]))
    outs = [unflat(o) for o in _inproj(flat(x), mod, mod_row, l, w, ropes, tm)]
    qa, ka, va, ag, qd, kd, vd, dg, xbc, dt, z, glu, cg = outs[:13]
    ctx_a = ctx_d = s0 = None
    if latent:
        kc, vc, kdc, vdc = _ctx_kv(caches['ckv'], caches['kr_pad'], caches['gk'], caches['gv'], l, w)
        ctx_a, ctx_d, s0 = (kc, vc), (kdc, vdc), caches['state']
    tq = min(ATT_TQ, s)
    bb = math.gcd(b, max(1, ATT_TQ // s))
    sub = min(ATT_SUB, tq)
    oa = _attention(qa, ka, va, ctx_a, ag, MLA_HEADS, 1, bb, tq, sub, "attn_mla")
    od = _attention(qd, kd, vd, ctx_d, dg, GQA_HEADS, GQA_HEADS // GQA_KV_HEADS, bb, tq, sub, "attn_gqa")
    xconv, oc = _convs(xbc, glu, cg, l, w, min(256, x.shape[1]))
    yf, yb, s_fin = _ssd(xconv, dt, s0, l, w)
    y = unflat(_outproj(flat(oa), flat(yf), flat(yb), flat(z), flat(oc), flat(od), flat(x), mod, mod_row, l, w, tm))
    new_ctx = None
    if not latent:
        ckv, kr, gk, gv = outs[13:]
        new_ctx = (ckv, kr, gk.reshape(b, -1, GQA_KV_HEADS, GQA_HD), gv.reshape(b, -1, GQA_KV_HEADS, GQA_HD),
                   s_fin.reshape(b, 2, SSD_HEADS, SSD_P, SSD_N))
    return y, new_ctx


def kernel(x_prompt, x_sample, cache_mla_ckv, cache_mla_krope, cache_gqa_k, cache_gqa_v, state_ssd, c, c_ctx, w_mod, b_mod, norm_g, w_in, w_out, mla_q_norm_g, mla_w_uq, mla_kv_norm_g, mla_w_ukv, mla_q_head_g, mla_k_head_g, ssd_conv_w, ssd_conv_b, ssd_dt_bias, ssd_a_log, ssd_d, ssd_norm_g, cf_conv_w, cf_conv_b, cf_ln_g, cf_ln_b, gqa_q_g, gqa_k_g):
    depth = w_in.shape[0]
    dec_b, n_lat = x_sample.shape[0], x_sample.shape[1]
    seq = x_prompt.shape[1]
    past = cache_mla_ckv.shape[2]
    assert dec_b < 8 and x_prompt.shape[-1] == D_MODEL and w_in.shape[-1] == sum(SPLITS)

    cvec = jnp.concatenate([c, c_ctx[None, :], jnp.zeros((8 - dec_b - 1, D_MODEL), F32)], axis=0)
    mod = _modulation(cvec, w_mod, b_mod).reshape(depth, 8, 3, D_MODEL)
    w = _prepare_weights(w_in, w_out, norm_g, mla_q_norm_g, mla_w_uq, mla_kv_norm_g, mla_w_ukv, mla_q_head_g,
                         mla_k_head_g, gqa_q_g, gqa_k_g, ssd_conv_w, ssd_conv_b, ssd_dt_bias, ssd_a_log, ssd_d,
                         ssd_norm_g, cf_conv_w, cf_conv_b, cf_ln_g, cf_ln_b)

    y = x_prompt
    ctx_out = []
    for l in range(depth):
        y, new_ctx = _mixer_layer(y, mod, dec_b, l, w, None, None, 512)
        ctx_out.append(new_ctx)
    new = [jnp.stack([t[i] for t in ctx_out], axis=1) for i in range(5)]

    caches = dict(ckv=cache_mla_ckv,
                  kr_pad=_pad_last(cache_mla_krope, MLA_NOPE, LANES - MLA_QK),
                  gk=cache_gqa_k.reshape(dec_b, depth, past, GQA_KV_HEADS * GQA_HD),
                  gv=cache_gqa_v.reshape(dec_b, depth, past, GQA_KV_HEADS * GQA_HD),
                  state=state_ssd.reshape(dec_b, depth, 2, SSD_GROUPS, 2 * SSD_P, SSD_N))
    ropes = (_rope_table(n_lat, MLA_ROPE, MLA_NOPE), _rope_table(n_lat, GQA_HD, 0))
    z = x_sample
    for l in range(depth):
        z, _ = _mixer_layer(z, mod, None, l, w, caches, ropes, 512)

    return (y, z, new[0], new[1], new[2], new[3], new[4])
```

```python
import functools
import math

import jax
import jax.numpy as jnp
from jax import lax
from jax.experimental import pallas as pl
from jax.experimental.pallas import tpu as pltpu

F32 = jnp.float32
BF16 = jnp.bfloat16

LANES = 128
VMEM_CAP = 56 * 1024 * 1024

EPS = 1e-6
ROPE_THETA = 10000.0
GRID_W = 64
LOG2E = math.log2(math.e)

D_MODEL = 1024
BRANCH_W = 256
MLA_HEADS, MLA_NOPE, MLA_ROPE, MLA_V = 4, 64, 32, 64
MLA_QK = MLA_NOPE + MLA_ROPE
MLA_Q_RANK, MLA_KV_RANK = 192, 128
SSD_HEADS, SSD_P, SSD_N, SSD_GROUPS, SSD_CONV, SSD_CHUNK = 4, 64, 64, 2, 3, 128
SSD_W = SSD_HEADS * SSD_P
SSD_CONV_CH = SSD_W + 2 * SSD_GROUPS * SSD_N
CF_W, CF_K = 256, 31
GQA_HEADS, GQA_KV_HEADS, GQA_HD = 4, 2, 64
SPLITS = (MLA_Q_RANK, MLA_KV_RANK, MLA_ROPE, MLA_HEADS * MLA_V,
          SSD_CONV_CH, 2 * SSD_HEADS, SSD_W,
          CF_W, CF_W, CF_W,
          GQA_HEADS * GQA_HD, GQA_KV_HEADS * GQA_HD, GQA_KV_HEADS * GQA_HD, GQA_HEADS * GQA_HD)

O_QL, O_KV, O_R, O_AG = 0, 256, 384, 512
O_XBC, O_DT, O_Z = 768, 1280, 1408
O_CV, O_CGL, O_CG = 1664, 1920, 2176
O_DQ, O_DK, O_DV, O_DG = 2432, 2944, 3200, 3456
IN_WP = 3712

CONV_HALO_SSD = 8
CONV_HALO_CF = 16

SSD_UNROLL = 4
INPROJ_SUB = 256
ATT_TQ = 2048
ATT_SUB = 512
ATT_QK_CHUNK = 512
ATT_PV_CHUNK = 256

NT_DIMS = (((1,), (1,)), ((), ()))


def _silu(x):
    return x / (1.0 + jnp.exp(-x))


def _sigmoid(x):
    return 1.0 / (1.0 + jnp.exp(-x))


def _softplus(x):
    return jnp.maximum(x, 0.0) + jnp.log1p(jnp.exp(-jnp.abs(x)))


def _rms(x, n):
    return x * lax.rsqrt(jnp.sum(x * x, axis=-1, keepdims=True) * (1.0 / n) + EPS)


def _dot(a, b):
    return jnp.dot(a, b, preferred_element_type=F32)


def _dot_nt(a, b):
    return lax.dot_general(a, b, NT_DIMS, preferred_element_type=F32)


def _params(sem, vmem_bytes):
    assert vmem_bytes <= VMEM_CAP
    return pltpu.CompilerParams(dimension_semantics=sem, vmem_limit_bytes=VMEM_CAP)


def _pad_heads(w, nh, d, dp):
    s = w.shape[:-1]
    w = w.reshape(s + (nh, d))
    w = jnp.pad(w, [(0, 0)] * len(s) + [(0, 0), (0, dp - d)])
    return w.reshape(s + (nh * dp,))


def _pad_last(w, lo, hi):
    return jnp.pad(w, [(0, 0)] * (w.ndim - 1) + [(lo, hi)])


def _layer_spec(a, l, grid_rank):
    zeros = (0,) * (a.ndim - 1)
    if grid_rank == 1:
        return pl.BlockSpec((1,) + a.shape[1:], lambda bi: (l,) + zeros)
    return pl.BlockSpec((1,) + a.shape[1:], lambda bi, i: (l,) + zeros)


def _with_ones_lane(v, hd):
    lane = lax.broadcasted_iota(jnp.int32, v.shape, v.ndim - 1)
    return jnp.where(lane % LANES == hd, 1.0, v)


def _mod_kernel(c_ref, w_ref, b_ref, o_ref):
    c = c_ref[...]
    o_ref[0] = _dot(_silu(c).astype(BF16), w_ref[0].astype(BF16)) + b_ref[0]


def _modulation(cvec, w_mod, b_mod):
    depth, d, d3 = w_mod.shape
    tn = d
    return pl.pallas_call(
        _mod_kernel,
        out_shape=jax.ShapeDtypeStruct((depth, 8, d3), F32),
        grid=(depth, d3 // tn),
        in_specs=[pl.BlockSpec((8, d), lambda l, j: (0, 0)),
                  pl.BlockSpec((1, d, tn), lambda l, j: (l, 0, j)),
                  pl.BlockSpec((1, 1, tn), lambda l, j: (l, 0, j))],
        out_specs=pl.BlockSpec((1, 8, tn), lambda l, j: (l, 0, j)),
        compiler_params=_params(("arbitrary", "arbitrary"), 4 * d * tn * 4),
        name="modulation",
    )(cvec, w_mod, b_mod.reshape(depth, 1, d3))


def _rope(x, tab_ref, rows, shift):
    cos, sin_a, sin_b = tab_ref[0, rows, :], tab_ref[1, rows, :], tab_ref[2, rows, :]
    outs = []
    for h in range(x.shape[-1] // LANES):
        xh = x[:, h * LANES:(h + 1) * LANES]
        up = pltpu.roll(xh, LANES - shift, 1)
        dn = pltpu.roll(xh, shift, 1)
        outs.append(xh * cos + up * sin_a + dn * sin_b)
    return jnp.concatenate(outs, axis=-1)


def _head_norm(x, gain, n, ones_bd):
    xx = (x * x).astype(BF16)
    w2 = 2 * LANES
    ss = jnp.concatenate([_dot(xx[:, j:j + w2], ones_bd) for j in range(0, x.shape[-1], w2)], axis=-1)
    return x * lax.rsqrt(ss * (1.0 / n) + EPS) * gain


def _inproj_refs(latent, refs):
    names = ['x', 'mod', 'ng', 'win', 'wuq', 'wukv', 'gains']
    if latent:
        names += ['ra', 'rd']
    names += ['qa', 'ka', 'va', 'ag', 'qd', 'kd', 'vd', 'dg', 'xbc', 'dt', 'z', 'glu', 'cg']
    if not latent:
        names += ['ckv', 'kr', 'gk', 'gv']
    return dict(zip(names, refs))


def _inproj_main(r, rows):
    x = r['x'][0, rows, :]
    shift = r['mod'][0, 0, 0:1, :]
    scale = r['mod'][0, 0, 1:2, :]
    h = _rms(x, D_MODEL) * r['ng'][0] * (1.0 + scale) + shift
    return _dot(h.astype(BF16), r['win'][0])


def _inproj_latents(r, u):
    g_ql = r['gains'][0, 0:1, 0:256]
    g_kv = r['gains'][0, 2:3, 0:LANES]
    ql = _rms(u[:, O_QL:O_QL + 256], MLA_Q_RANK) * g_ql
    ckv = _rms(u[:, O_KV:O_KV + LANES], MLA_KV_RANK) * g_kv
    return _dot(ql.astype(BF16), r['wuq'][0]), ckv, _dot(ckv.astype(BF16), r['wukv'][0])


def _inproj_tail(latent, r, rows, u, q, ckv, kv):
    wa, wd = MLA_HEADS * LANES, GQA_KV_HEADS * LANES
    g_qh = r['gains'][0, 1:2, :]
    g_kh = r['gains'][0, 3:4, :]
    g_dq = r['gains'][0, 4:5, :]
    g_dk = r['gains'][0, 5:6, 0:wd]
    ri = lax.broadcasted_iota(jnp.int32, (2 * LANES, 2 * LANES), 0) // LANES
    ci = lax.broadcasted_iota(jnp.int32, (2 * LANES, 2 * LANES), 1) // LANES
    ones_bd = (ri == ci).astype(F32).astype(BF16)

    q = _head_norm(q, g_qh, MLA_QK, ones_bd)
    kr = u[:, O_R:O_R + LANES]
    k = _head_norm(kv[:, 0:wa] + jnp.concatenate([kr] * MLA_HEADS, axis=-1), g_kh, MLA_QK, ones_bd)
    qd = _head_norm(u[:, O_DQ:O_DQ + GQA_HEADS * LANES], g_dq, GQA_HD, ones_bd)
    kd = _head_norm(u[:, O_DK:O_DK + wd], g_dk, GQA_HD, ones_bd)
    vd = u[:, O_DV:O_DV + wd]
    if latent:
        q = _rope(q, r['ra'], rows, MLA_ROPE // 4)
        k = _rope(k, r['ra'], rows, MLA_ROPE // 4)
        qd = _rope(qd, r['rd'], rows, GQA_HD // 4)
        kd_att = _rope(kd, r['rd'], rows, GQA_HD // 4)
    else:
        kd_att = kd
        r['ckv'][0, rows, :] = ckv
        r['kr'][0, rows, :] = kr[:, MLA_NOPE:MLA_NOPE + MLA_ROPE]
        for i in range(GQA_KV_HEADS):
            r['gk'][0, rows, i * GQA_HD:(i + 1) * GQA_HD] = kd[:, i * LANES:i * LANES + GQA_HD]
            r['gv'][0, rows, i * GQA_HD:(i + 1) * GQA_HD] = vd[:, i * LANES:i * LANES + GQA_HD]
    r['qa'][0, rows, :] = q.astype(BF16)
    r['ka'][0, rows, :] = k.astype(BF16)
    r['va'][0, rows, :] = _with_ones_lane(kv[:, wa:], MLA_V).astype(BF16)
    r['ag'][0, rows, :] = u[:, O_AG:O_AG + BRANCH_W]
    r['qd'][0, rows, :] = qd.astype(BF16)
    r['kd'][0, rows, :] = kd_att.astype(BF16)
    r['vd'][0, rows, :] = _with_ones_lane(vd, GQA_HD).astype(BF16)
    r['dg'][0, rows, :] = u[:, O_DG:O_DG + BRANCH_W]

    r['xbc'][0, rows, :] = u[:, O_XBC:O_XBC + SSD_CONV_CH]
    r['dt'][0, rows, :] = u[:, O_DT:O_DT + LANES]
    r['z'][0, rows, :] = u[:, O_Z:O_Z + SSD_W]
    r['glu'][0, rows, :] = u[:, O_CV:O_CV + CF_W] * _sigmoid(u[:, O_CGL:O_CGL + CF_W])
    r['cg'][0, rows, :] = u[:, O_CG:O_CG + CF_W]


def _inproj_kernel(latent, sub, *refs):
    r = _inproj_refs(latent, refs)
    tm = r['x'].shape[1]
    for r0 in range(0, tm, sub):
        rows = slice(r0, r0 + sub)
        u = _inproj_main(r, rows)
        q, ckv, kv = _inproj_latents(r, u)
        _inproj_tail(latent, r, rows, u, q, ckv, kv)


def _mod_spec(mod, l, row):
    d = mod.shape[-1]
    if row is None:
        return pl.BlockSpec((1, 1, 3, d), lambda bi, i: (l, bi, 0, 0))
    return pl.BlockSpec((1, 1, 3, d), lambda bi, i: (l, row, 0, 0))


def _inproj(x, mod, mod_row, l, w, ropes, tm):
    b, s, d = x.shape
    latent = ropes is not None
    tok = lambda wd: pl.BlockSpec((1, tm, wd), lambda bi, i: (bi, i, 0))
    weights = [w['ng'], w['win'], w['wuq'], w['wukv'], w['gains']]
    in_specs = [tok(d), _mod_spec(mod, l, mod_row)] + [_layer_spec(a, l, 2) for a in weights]
    args = [x, mod] + weights
    if latent:
        in_specs += [pl.BlockSpec((3, tm, LANES), lambda bi, i: (0, i, 0))] * 2
        args += list(ropes)
    widths = [(4 * LANES, BF16), (4 * LANES, BF16), (4 * LANES, BF16), (BRANCH_W, F32),
              (4 * LANES, BF16), (2 * LANES, BF16), (2 * LANES, BF16), (BRANCH_W, F32),
              (SSD_CONV_CH, F32), (LANES, F32), (SSD_W, F32), (CF_W, F32), (CF_W, F32)]
    if not latent:
        widths += [(MLA_KV_RANK, F32), (MLA_ROPE, F32), (LANES, F32), (LANES, F32)]
    out_shape = [jax.ShapeDtypeStruct((b, s, wd), dt) for wd, dt in widths]
    out_specs = [tok(wd) for wd, _ in widths]
    out_bytes = sum(wd * jnp.dtype(dt).itemsize for wd, dt in widths) * tm
    w_bytes = sum(a[0].size * a.dtype.itemsize for a in weights)
    sub = min(INPROJ_SUB, tm)
    vmem = 2 * (tm * d * 4 + w_bytes + out_bytes) + 3 * sub * IN_WP * 4
    return pl.pallas_call(
        functools.partial(_inproj_kernel, latent, sub),
        out_shape=out_shape,
        grid=(b, s // tm),
        in_specs=in_specs,
        out_specs=out_specs,
        compiler_params=_params(("parallel", "parallel"), vmem),
        name="inproj_lat" if latent else "inproj_ctx",
    )(*args)


def _ctxkv_kernel(ckv_ref, kr_ref, gk_ref, gv_ref, wukv_ref, gains_ref, kc_ref, vc_ref, kdc_ref, vdc_ref):
    kv = _dot(ckv_ref[0, 0].astype(BF16), wukv_ref[0])
    kr = kr_ref[0, 0]
    g = gains_ref[0, 3:4, 0:LANES]
    for i in range(MLA_HEADS):
        k = _rms(kv[:, i * LANES:(i + 1) * LANES] + kr, MLA_QK) * g
        kc_ref[0, :, i * LANES:(i + 1) * LANES] = k.astype(BF16)
    vc_ref[0] = _with_ones_lane(kv[:, MLA_HEADS * LANES:], MLA_V).astype(BF16)
    gk = gk_ref[0, 0]
    gv = gv_ref[0, 0]
    p = gk.shape[0]
    pad = jnp.zeros((p, LANES - GQA_HD), F32)
    for i in range(GQA_KV_HEADS):
        lo = i * LANES
        kdc_ref[0, :, lo:lo + GQA_HD] = gk[:, i * GQA_HD:(i + 1) * GQA_HD].astype(BF16)
        kdc_ref[0, :, lo + GQA_HD:lo + LANES] = pad.astype(BF16)
        vdc_ref[0, :, lo:lo + GQA_HD] = gv[:, i * GQA_HD:(i + 1) * GQA_HD].astype(BF16)
        vdc_ref[0, :, lo + GQA_HD:lo + LANES] = _with_ones_lane(pad, 0).astype(BF16)


def _ctx_kv(ckv, kr_pad, gk, gv, l, w):
    b, _, p, _ = ckv.shape
    cache = lambda a: pl.BlockSpec((1, 1, p, a.shape[-1]), lambda bi: (bi, l, 0, 0))
    out = lambda wd: pl.BlockSpec((1, p, wd), lambda bi: (bi, 0, 0))
    wa, wd = MLA_HEADS * LANES, GQA_KV_HEADS * LANES
    return pl.pallas_call(
        _ctxkv_kernel,
        out_shape=[jax.ShapeDtypeStruct((b, p, wa), BF16), jax.ShapeDtypeStruct((b, p, wa), BF16),
                   jax.ShapeDtypeStruct((b, p, wd), BF16), jax.ShapeDtypeStruct((b, p, wd), BF16)],
        grid=(b,),
        in_specs=[cache(ckv), cache(kr_pad), cache(gk), cache(gv),
                  _layer_spec(w['wukv'], l, 1), _layer_spec(w['gains'], l, 1)],
        out_specs=[out(wa), out(wa), out(wd), out(wd)],
        compiler_params=_params(("parallel",), 16 * p * wa * 4),
        name="ctx_kv",
    )(ckv, kr_pad, gk, gv, w['wukv'], w['gains'])


def _lane_tile_max(t):
    r = t[:, 0:LANES]
    for j in range(1, t.shape[-1] // LANES):
        r = jnp.maximum(r, t[:, j * LANES:(j + 1) * LANES])
    return r


def _attn_unit(n_heads, rep, has_ctx, refs, bi, rows, h, s_ref):
    if has_ctx:
        q_ref, k_ref, v_ref, kc_ref, vc_ref, g_ref, o_ref = refs
        p_len = kc_ref.shape[1]
    else:
        q_ref, k_ref, v_ref, g_ref, o_ref = refs
        p_len = 0
    s_len = k_ref.shape[1]
    hd = o_ref.shape[-1] // n_heads
    kb = min(ATT_QK_CHUNK, s_len)
    kb2 = ATT_PV_CHUNK
    g = h // rep
    hs = slice(h * LANES, (h + 1) * LANES)
    gs = slice(g * LANES, (g + 1) * LANES)
    qh = q_ref[bi, rows, hs]
    mx = None
    for c in range(s_len // kb):
        sc = _dot_nt(qh, k_ref[bi, c * kb:(c + 1) * kb, gs])
        s_ref[:, c * kb:(c + 1) * kb] = sc
        t = _lane_tile_max(sc)
        mx = t if mx is None else jnp.maximum(mx, t)
    if has_ctx:
        sc = _dot_nt(qh, kc_ref[bi, :, gs])
        s_ref[:, s_len:s_len + p_len] = sc
        mx = jnp.maximum(mx, _lane_tile_max(sc))
    m = jnp.max(mx, axis=-1, keepdims=True)
    acc = None
    for c in range((s_len + p_len) // kb2):
        r0 = c * kb2
        p = jnp.exp2(s_ref[:, r0:r0 + kb2] - m).astype(BF16)
        vv = v_ref[bi, r0:r0 + kb2, gs] if r0 < s_len else vc_ref[bi, r0 - s_len:r0 - s_len + kb2, gs]
        part = _dot(p, vv)
        acc = part if acc is None else acc + part
    o = acc[:, 0:hd] / acc[:, hd:hd + 1]
    os_ = slice(h * hd, (h + 1) * hd)
    o_ref[bi, rows, os_] = (o * _silu(g_ref[bi, rows, os_])).astype(o_ref.dtype)


def _attn_kernel(n_heads, rep, has_ctx, sub, *refs):
    s0_ref, s1_ref = refs[-2:]
    bb, tq = refs[0].shape[0], refs[0].shape[1]
    unit = 0
    for bi in range(bb):
        for r0 in range(0, tq, sub):
            for h in range(n_heads):
                _attn_unit(n_heads, rep, has_ctx, refs[:-2], bi, slice(r0, r0 + sub), h,
                           s0_ref if unit % 2 == 0 else s1_ref)
                unit += 1


def _attention(q, k, v, ctx, gate, n_heads, rep, bb, tq, sub, name):
    b, s, qw = q.shape
    kw = k.shape[-1]
    ow = gate.shape[-1]
    has_ctx = ctx is not None
    in_specs = [pl.BlockSpec((bb, tq, qw), lambda bi, i: (bi, i, 0)),
                pl.BlockSpec((bb, s, kw), lambda bi, i: (bi, 0, 0)),
                pl.BlockSpec((bb, s, kw), lambda bi, i: (bi, 0, 0))]
    args = [q, k, v]
    p = 0
    if has_ctx:
        p = ctx[0].shape[1]
        in_specs += [pl.BlockSpec((bb, p, kw), lambda bi, i: (bi, 0, 0))] * 2
        args += list(ctx)
    in_specs.append(pl.BlockSpec((bb, tq, ow), lambda bi, i: (bi, i, 0)))
    args.append(gate)
    assert s % min(ATT_QK_CHUNK, s) == 0 and s % ATT_PV_CHUNK == 0 and p % ATT_PV_CHUNK == 0
    assert b % bb == 0 and s % tq == 0 and tq % sub == 0
    vmem = (2 * bb * (tq * qw * 2 + 2 * (s + p) * kw * 2 + 2 * tq * ow * 4)
            + 2 * sub * (s + p) * 4 + 2 * sub * 1024 * 4)
    return pl.pallas_call(
        functools.partial(_attn_kernel, n_heads, rep, has_ctx, sub),
        out_shape=jax.ShapeDtypeStruct((b, s, ow), BF16),
        grid=(b // bb, s // tq),
        in_specs=in_specs,
        out_specs=pl.BlockSpec((bb, tq, ow), lambda bi, i: (bi, i, 0)),
        scratch_shapes=[pltpu.VMEM((sub, s + p), F32), pltpu.VMEM((sub, s + p), F32)],
        compiler_params=_params(("parallel", "arbitrary"), vmem),
        name=name,
    )(*args)


def _conv_kernel(tl, xc_ref, xp_ref, xn_ref, gc_ref, gp_ref, gn_ref, cg_ref,
                 w3_ref, b3_ref, w31_ref, b31_ref, lng_ref, lnb_ref,
                 xo_ref, co_ref, ext3, ext31):
    i = pl.program_id(1)
    keep_prev = (i > 0).astype(F32)
    keep_next = (i < pl.num_programs(1) - 1).astype(F32)

    h3 = CONV_HALO_SSD
    ext3[0:h3, :] = xp_ref[0] * keep_prev
    ext3[h3:h3 + tl, :] = xc_ref[0]
    ext3[h3 + tl:2 * h3 + tl, :] = xn_ref[0] * keep_next
    acc = jnp.zeros((tl, SSD_CONV_CH), F32) + b3_ref[0]
    for k in range(SSD_CONV):
        acc = acc + w3_ref[0, k:k + 1, :] * ext3[h3 - SSD_CONV // 2 + k:h3 - SSD_CONV // 2 + k + tl, :]
    xo_ref[0] = _silu(acc)

    h31 = CONV_HALO_CF
    ext31[0:h31, :] = gp_ref[0] * keep_prev
    ext31[h31:h31 + tl, :] = gc_ref[0]
    ext31[h31 + tl:2 * h31 + tl, :] = gn_ref[0] * keep_next
    acc = jnp.zeros((tl, CF_W), F32) + b31_ref[0]
    o_lo = h31 - CF_K // 2
    ext = ext31[...]
    n_ext = ext.shape[0]
    for r in range(8):
        taps = [k for k in range(CF_K) if (o_lo + k) % 8 == r]
        if not taps:
            continue
        shifted = ext if r == 0 else pltpu.roll(ext, n_ext - r, 0)
        for k in taps:
            a = (o_lo + k) // 8 * 8
            acc = acc + w31_ref[0, k:k + 1, :] * shifted[a:a + tl, :]
    mu = jnp.mean(acc, axis=-1, keepdims=True)
    cen = acc - mu
    var = jnp.mean(cen * cen, axis=-1, keepdims=True)
    y = cen * lax.rsqrt(var + EPS) * lng_ref[0] + lnb_ref[0]
    co_ref[0] = (_silu(y) * _silu(cg_ref[0])).astype(co_ref.dtype)


def _convs(xbc, glu, cg, l, w, tl):
    b, s, _ = xbc.shape
    nt = s // tl

    def cur(wd):
        return pl.BlockSpec((1, tl, wd), lambda bi, i: (bi, i, 0))

    def prev(wd, hrows):
        r = tl // hrows
        return pl.BlockSpec((1, hrows, wd), lambda bi, i: (bi, jnp.maximum(i * r - 1, 0), 0))

    def nxt(wd, hrows):
        r = tl // hrows
        return pl.BlockSpec((1, hrows, wd), lambda bi, i: (bi, jnp.minimum((i + 1) * r, nt * r - 1), 0))

    weights = [w['w3'], w['b3'], w['w31'], w['b31'], w['lng'], w['lnb']]
    vmem = 4 * tl * (SSD_CONV_CH * 2 + CF_W * 3) * 4 + 6 * tl * SSD_CONV_CH * 4
    return pl.pallas_call(
        functools.partial(_conv_kernel, tl),
        out_shape=[jax.ShapeDtypeStruct((b, s, SSD_CONV_CH), F32),
                   jax.ShapeDtypeStruct((b, s, CF_W), BF16)],
        grid=(b, nt),
        in_specs=[cur(SSD_CONV_CH), prev(SSD_CONV_CH, CONV_HALO_SSD), nxt(SSD_CONV_CH, CONV_HALO_SSD),
                  cur(CF_W), prev(CF_W, CONV_HALO_CF), nxt(CF_W, CONV_HALO_CF), cur(CF_W)]
                 + [_layer_spec(a, l, 2) for a in weights],
        out_specs=[cur(SSD_CONV_CH), cur(CF_W)],
        scratch_shapes=[pltpu.VMEM((tl + 2 * CONV_HALO_SSD, SSD_CONV_CH), F32),
                        pltpu.VMEM((tl + 2 * CONV_HALO_CF, CF_W), F32)],
        compiler_params=_params(("parallel", "parallel"), vmem),
        name="convs",
    )(xbc, xbc, xbc, glu, glu, glu, cg, *weights)


def _split3(x):
    hi = x.astype(BF16)
    r1 = x - hi.astype(F32)
    mid = r1.astype(BF16)
    lo = (r1 - mid.astype(F32)).astype(BF16)
    return hi, mid, lo


def _cumsum_rows(tri, x):
    return sum(_dot(tri, part) for part in _split3(x))


def _transpose_exact(x):
    c = x.shape[1]
    eye = (lax.broadcasted_iota(jnp.int32, (c, c), 0) == lax.broadcasted_iota(jnp.int32, (c, c), 1))
    eye = eye.astype(F32).astype(BF16)
    return sum(_dot_nt(eye, part) for part in _split3(x))


SSD_NK = 2 * SSD_HEADS


def _ssd_stage_local(c, d, xc_ref, dt_ref, bias, a_row):
    q = SSD_CHUNK
    rows = pl.ds(pl.multiple_of(c * q, q), q)
    t = dict(rows=rows, d=d)
    t['xs'] = xc_ref[0, rows, 0:SSD_W]
    t['bm'] = xc_ref[0, rows, SSD_W:SSD_W + LANES]
    t['cm'] = xc_ref[0, rows, SSD_W + LANES:SSD_W + 2 * LANES].astype(BF16)
    t['dt'] = _softplus(dt_ref[0, rows, :] + bias)
    ri = lax.broadcasted_iota(jnp.int32, (q, q), 0)
    ci = lax.broadcasted_iota(jnp.int32, (q, q), 1)
    t['mask'] = (ri >= ci) if d == 0 else (ri <= ci)
    t['acum'] = _cumsum_rows(t['mask'].astype(F32).astype(BF16), t['dt'] * a_row)
    return t


def _ssd_stage_cb(t):
    q = SSD_CHUNK
    lane = lax.broadcasted_iota(jnp.int32, (q, LANES), 1)
    t['tr'] = jnp.where(lane < SSD_NK, t['dt'], t['acum']).T
    bt = t['bm'].T
    t['bt'] = bt
    grp_row = lax.broadcasted_iota(jnp.int32, (LANES, q), 0) < SSD_N
    t['cb'] = [_dot(t['cm'], jnp.where(grp_row if g == 0 else jnp.logical_not(grp_row), bt, 0.0).astype(BF16))
               for g in range(SSD_GROUPS)]
    t['last'] = q - 1 if t['d'] == 0 else 0
    t['etot'] = jnp.exp(t['acum'][t['last']:t['last'] + 1, :])
    t['eac'] = jnp.exp(t['acum'])


def _ssd_stage_diag(t, dskip):
    q = SSD_CHUNK
    d = t['d']
    lo_half = lax.broadcasted_iota(jnp.int32, (q, LANES), 1) < SSD_P
    tr, acum, last = t['tr'], t['acum'], t['last']
    t['yd'], t['cs'] = [], []
    for g in range(SSD_GROUPS):
        k0 = d * SSD_HEADS + 2 * g
        xg = t['xs'][:, g * LANES:(g + 1) * LANES]
        xgb = xg.astype(BF16)
        btg = t['bt'][g * SSD_N:(g + 1) * SSD_N, :]
        yg = None
        cs = []
        for hh in range(2):
            k = k0 + hh
            dt_row = tr[k:k + 1, :]
            ac_row = tr[SSD_NK + k:SSD_NK + k + 1, :]
            dec = jnp.where(t['mask'], jnp.exp(acum[:, k:k + 1] - ac_row), 0.0)
            xm = jnp.where(lo_half if hh == 0 else jnp.logical_not(lo_half), xg, 0.0).astype(BF16)
            part = _dot((t['cb'][g] * dec * dt_row).astype(BF16), xm)
            yg = part if yg is None else yg + part
            w_row = dt_row * jnp.exp(tr[SSD_NK + k:SSD_NK + k + 1, last:last + 1] - ac_row)
            cs.append(_dot((btg * w_row).astype(BF16), xgb))
        if d == 0:
            yg = yg + dskip[:, g * LANES:(g + 1) * LANES] * xg
        t['yd'].append(yg)
        t['cs'].append(cs)


def _ssd_stage_state(t, states):
    q = SSD_CHUNK
    d = t['d']
    lo_half = lax.broadcasted_iota(jnp.int32, (q, LANES), 1) < SSD_P
    lane_n = lax.broadcasted_iota(jnp.int32, (SSD_N, LANES), 1) < SSD_P
    zero = jnp.zeros((SSD_N, LANES), F32)
    ys, new_states = [], []
    for g in range(SSD_GROUPS):
        k0 = d * SSD_HEADS + 2 * g
        s_prev = states[g]
        sz = jnp.concatenate([s_prev, zero] if g == 0 else [zero, s_prev], axis=0).astype(BF16)
        eacs = jnp.where(lo_half, t['eac'][:, k0:k0 + 1], t['eac'][:, k0 + 1:k0 + 2])
        ys.append(t['yd'][g] + _dot(t['cm'], sz) * eacs)
        cdec = jnp.where(lane_n, t['etot'][:, k0:k0 + 1], t['etot'][:, k0 + 1:k0 + 2])
        new_states.append(cdec * s_prev + jnp.where(lane_n, t['cs'][g][0], t['cs'][g][1]))
    return ys, new_states


def _ssd_kernel(nc, unroll, has_init, *refs):
    if has_init:
        xc_ref, dt_ref, s0_ref, bias_ref, alog_ref, d_ref, yf_ref, yb_ref, so_ref, st_ref = refs
        for d in range(2):
            for g in range(SSD_GROUPS):
                st_ref[0, d, g] = _transpose_exact(s0_ref[0, 0, d, g])
    else:
        xc_ref, dt_ref, bias_ref, alog_ref, d_ref, yf_ref, yb_ref, so_ref, st_ref = refs
        st_ref[...] = jnp.zeros(st_ref.shape, F32)
    lane = lax.broadcasted_iota(jnp.int32, (1, LANES), 1)
    a_row = jnp.where(lane < 2 * SSD_NK, -jnp.exp(alog_ref[0]), 0.0)
    bias = bias_ref[0]
    dskip = d_ref[0]

    def body(i, carry):
        states = [[st_ref[0, d, g] for g in range(SSD_GROUPS)] for d in range(2)]
        work = []
        for j in range(unroll):
            c = i * unroll + j
            work.append((_ssd_stage_local(c, 0, xc_ref, dt_ref, bias, a_row), yf_ref))
            work.append((_ssd_stage_local(nc - 1 - c, 1, xc_ref, dt_ref, bias, a_row), yb_ref))
        for t, _ in work:
            _ssd_stage_cb(t)
        for t, _ in work:
            _ssd_stage_diag(t, dskip)
        writes = []
        for t, y_ref in work:
            ys, states[t['d']] = _ssd_stage_state(t, states[t['d']])
            writes.append((y_ref, t['rows'], ys))
        for y_ref, rows, ys in writes:
            for g in range(SSD_GROUPS):
                y_ref[0, rows, g * LANES:(g + 1) * LANES] = ys[g]
        for d in range(2):
            for g in range(SSD_GROUPS):
                st_ref[0, d, g] = states[d][g]
        return carry

    lax.fori_loop(0, nc // unroll, body, 0)
    for d in range(2):
        for g in range(SSD_GROUPS):
            so_ref[0, d, g] = _transpose_exact(st_ref[0, d, g])


def _ssd(xc, dt, s0, l, w):
    b, s, _ = xc.shape
    nc = s // SSD_CHUNK
    unroll = min(SSD_UNROLL, nc)
    assert nc % unroll == 0
    st_shape = (b, 2, SSD_GROUPS, 2 * SSD_P, SSD_N)
    st_spec = pl.BlockSpec((1,) + st_shape[1:], lambda bi: (bi, 0, 0, 0, 0))
    y_spec = pl.BlockSpec((1, s, SSD_W), lambda bi: (bi, 0, 0))
    in_specs = [pl.BlockSpec((1, s, SSD_CONV_CH), lambda bi: (bi, 0, 0)),
                pl.BlockSpec((1, s, LANES), lambda bi: (bi, 0, 0))]
    args = [xc, dt]
    if s0 is not None:
        in_specs.append(pl.BlockSpec((1, 1) + st_shape[1:], lambda bi: (bi, l, 0, 0, 0, 0)))
        args.append(s0)
    weights = [w['dt_bias'], w['a_log'], w['dskip']]
    vmem = 2 * s * (SSD_CONV_CH + LANES + 2 * SSD_W) * 4 + 8 * 1024 * 1024
    return pl.pallas_call(
        functools.partial(_ssd_kernel, nc, unroll, s0 is not None),
        out_shape=[jax.ShapeDtypeStruct((b, s, SSD_W), F32), jax.ShapeDtypeStruct((b, s, SSD_W), F32),
                   jax.ShapeDtypeStruct(st_shape, F32)],
        grid=(b,),
        in_specs=in_specs + [_layer_spec(a, l, 1) for a in weights],
        out_specs=[y_spec, y_spec, st_spec],
        scratch_shapes=[pltpu.VMEM((1, 2, SSD_GROUPS, SSD_N, 2 * SSD_P), F32)],
        compiler_params=_params(("parallel",), vmem),
        name="ssd",
    )(*args, *weights)


def _outproj_kernel(oa_ref, yf_ref, yb_ref, z_ref, oc_ref, od_ref, x_ref, mod_ref, ng_ref, w_ref, o_ref):
    ob = _rms((yf_ref[0] + yb_ref[0]) * _silu(z_ref[0]), SSD_W) * ng_ref[0]
    acc = _dot(oa_ref[0], w_ref[0, 0:BRANCH_W, :])
    acc = acc + _dot(ob.astype(BF16), w_ref[0, BRANCH_W:2 * BRANCH_W, :])
    acc = acc + _dot(oc_ref[0], w_ref[0, 2 * BRANCH_W:3 * BRANCH_W, :])
    acc = acc + _dot(od_ref[0], w_ref[0, 3 * BRANCH_W:4 * BRANCH_W, :])
    o_ref[0] = x_ref[0] + mod_ref[0, 0, 2:3, :] * acc


def _outproj(oa, yf, yb, z, oc, od, x, mod, mod_row, l, w, tm):
    b, s, d = x.shape
    br = lambda: pl.BlockSpec((1, tm, BRANCH_W), lambda bi, i: (bi, i, 0))
    xs = pl.BlockSpec((1, tm, d), lambda bi, i: (bi, i, 0))
    vmem = 2 * (6 * tm * BRANCH_W * 4 + 2 * tm * d * 4 + d * d * 2) + 2 * tm * d * 4
    return pl.pallas_call(
        _outproj_kernel,
        out_shape=jax.ShapeDtypeStruct((b, s, d), F32),
        grid=(b, s // tm),
        in_specs=[br(), br(), br(), br(), br(), br(), xs, _mod_spec(mod, l, mod_row),
                  _layer_spec(w['ssd_ng'], l, 2), _layer_spec(w['wout'], l, 2)],
        out_specs=xs,
        compiler_params=_params(("parallel", "parallel"), vmem),
        name="outproj",
    )(oa, yf, yb, z, oc, od, x, mod, w['ssd_ng'], w['wout'])


def _rope_table(n_tokens, dim, lane0):
    n_rows = n_tokens // GRID_W
    row = jnp.repeat(jnp.arange(n_rows), GRID_W).astype(F32)
    col = jnp.tile(jnp.arange(GRID_W), n_rows).astype(F32)
    quarter = dim // 4
    inv = ROPE_THETA ** (-jnp.arange(quarter, dtype=F32) / quarter)
    ar = row[:, None] * inv
    ac = col[:, None] * inv
    zero = jnp.zeros_like(ar)
    cos = jnp.concatenate([jnp.cos(ar), jnp.cos(ar), jnp.cos(ac), jnp.cos(ac)], axis=-1)
    sin_a = jnp.concatenate([-jnp.sin(ar), zero, -jnp.sin(ac), zero], axis=-1)
    sin_b = jnp.concatenate([zero, jnp.sin(ar), zero, jnp.sin(ac)], axis=-1)
    hi = LANES - lane0 - dim
    cos = jnp.pad(cos, ((0, 0), (lane0, hi)), constant_values=1.0)
    return jnp.stack([cos, _pad_last(sin_a, lane0, hi), _pad_last(sin_b, lane0, hi)])


def _prepare_weights(w_in, w_out, norm_g, mla_q_norm_g, mla_w_uq, mla_kv_norm_g, mla_w_ukv, mla_q_head_g,
                     mla_k_head_g, gqa_q_g, gqa_k_g, ssd_conv_w, ssd_conv_b, ssd_dt_bias, ssd_a_log, ssd_d,
                     ssd_norm_g, cf_conv_w, cf_conv_b, cf_ln_g, cf_ln_b):
    depth = w_in.shape[0]
    pieces = []
    o = 0
    for sz in SPLITS:
        pieces.append(w_in[:, :, o:o + sz])
        o += sz
    (a_q, a_kv, a_r, a_g, b_xbc, b_dt, b_z, c_v, c_gl, c_g, d_q, d_k, d_v, d_g) = pieces
    win = jnp.concatenate([
        _pad_last(a_q, 0, 256 - MLA_Q_RANK), a_kv, _pad_last(a_r, MLA_NOPE, LANES - MLA_QK), a_g,
        b_xbc, _pad_last(jnp.concatenate([b_dt, b_dt], axis=-1), 0, LANES - 2 * SSD_NK), b_z, c_v, c_gl, c_g,
        _pad_heads(d_q, GQA_HEADS, GQA_HD, LANES), _pad_heads(d_k, GQA_KV_HEADS, GQA_HD, LANES),
        _pad_heads(d_v, GQA_KV_HEADS, GQA_HD, LANES), d_g], axis=-1).astype(BF16)
    wuq = jnp.pad(_pad_heads(mla_w_uq, MLA_HEADS, MLA_QK, LANES),
                  ((0, 0), (0, 256 - MLA_Q_RANK), (0, 0))).astype(BF16)
    kv = mla_w_ukv.reshape(depth, MLA_KV_RANK, MLA_HEADS, MLA_NOPE + MLA_V)
    wk = _pad_last(kv[..., :MLA_NOPE], 0, LANES - MLA_NOPE).reshape(depth, MLA_KV_RANK, -1)
    wv = _pad_last(kv[..., MLA_NOPE:], 0, LANES - MLA_V).reshape(depth, MLA_KV_RANK, -1)
    wukv = jnp.concatenate([wk, wv], axis=-1).astype(BF16)
    gw = MLA_HEADS * LANES
    row = lambda v: _pad_last(v, 0, gw - v.shape[-1])
    slots = lambda v, n: row(jnp.tile(_pad_last(v, 0, LANES - v.shape[-1]), (1, n)))
    zero = jnp.zeros((depth, gw), F32)
    gains = jnp.stack([row(mla_q_norm_g),
                       slots(mla_q_head_g * (MLA_QK ** -0.5 * LOG2E), MLA_HEADS),
                       row(mla_kv_norm_g),
                       slots(mla_k_head_g, MLA_HEADS),
                       slots(gqa_q_g * (GQA_HD ** -0.5 * LOG2E), GQA_HEADS),
                       slots(gqa_k_g, GQA_KV_HEADS), zero, zero], axis=1)
    vec = lambda v: v[:, None, :]
    lanes8 = lambda v: _pad_last(jnp.tile(v.reshape(depth, 1, -1), (1, 1, 2)), 0, LANES - 2 * SSD_NK)
    return dict(win=win, wuq=wuq, wukv=wukv, gains=gains, wout=w_out.astype(BF16), ng=vec(norm_g),
                w3=ssd_conv_w, b3=vec(ssd_conv_b), w31=cf_conv_w, b31=vec(cf_conv_b),
                lng=vec(cf_ln_g), lnb=vec(cf_ln_b), dt_bias=lanes8(ssd_dt_bias), a_log=lanes8(ssd_a_log),
                dskip=vec(jnp.repeat(ssd_d, SSD_P, axis=-1)), ssd_ng=vec(ssd_norm_g))


def _mixer_layer(x, mod, mod_row, l, w, caches, ropes, tm):
    latent = caches is not None
    b, s, d = x.shape
    flat = (lambda a: a) if latent else (lambda a: a.reshape(1, b * s, a.shape[-1]))
    unflat = (lambda a: a) if latent else (lambda a: a.reshape(b, s, a.shape[-1]))
    outs = [unflat(o) for o in _inproj(flat(x), mod, mod_row, l, w, ropes, tm)]
    qa, ka, va, ag, qd, kd, vd, dg, xbc, dt, z, glu, cg = outs[:13]
    ctx_a = ctx_d = s0 = None
    if latent:
        kc, vc, kdc, vdc = _ctx_kv(caches['ckv'], caches['kr_pad'], caches['gk'], caches['gv'], l, w)
        ctx_a, ctx_d, s0 = (kc, vc), (kdc, vdc), caches['state']
    tq = min(ATT_TQ, s)
    bb = 1
    sub = min(ATT_SUB, tq)
    oa = _attention(qa, ka, va, ctx_a, ag, MLA_HEADS, 1, bb, tq, sub, "attn_mla")
    od = _attention(qd, kd, vd, ctx_d, dg, GQA_HEADS, GQA_HEADS // GQA_KV_HEADS, bb, tq, sub, "attn_gqa")
    xconv, oc = _convs(xbc, glu, cg, l, w, min(256, x.shape[1]))
    yf, yb, s_fin = _ssd(xconv, dt, s0, l, w)
    y = unflat(_outproj(flat(oa), flat(yf), flat(yb), flat(z), flat(oc), flat(od), flat(x), mod, mod_row, l, w, tm))
    new_ctx = None
    if not latent:
        ckv, kr, gk, gv = outs[13:]
        new_ctx = (ckv, kr, gk.reshape(b, -1, GQA_KV_HEADS, GQA_HD), gv.reshape(b, -1, GQA_KV_HEADS, GQA_HD),
                   s_fin.reshape(b, 2, SSD_HEADS, SSD_P, SSD_N))
    return y, new_ctx


def kernel(x_prompt, x_sample, cache_mla_ckv, cache_mla_krope, cache_gqa_k, cache_gqa_v, state_ssd, c, c_ctx, w_mod, b_mod, norm_g, w_in, w_out, mla_q_norm_g, mla_w_uq, mla_kv_norm_g, mla_w_ukv, mla_q_head_g, mla_k_head_g, ssd_conv_w, ssd_conv_b, ssd_dt_bias, ssd_a_log, ssd_d, ssd_norm_g, cf_conv_w, cf_conv_b, cf_ln_g, cf_ln_b, gqa_q_g, gqa_k_g):
    depth = w_in.shape[0]
    dec_b, n_lat = x_sample.shape[0], x_sample.shape[1]
    seq = x_prompt.shape[1]
    past = cache_mla_ckv.shape[2]
    assert dec_b < 8 and x_prompt.shape[-1] == D_MODEL and w_in.shape[-1] == sum(SPLITS)

    cvec = jnp.concatenate([c, c_ctx[None, :], jnp.zeros((8 - dec_b - 1, D_MODEL), F32)], axis=0)
    mod = _modulation(cvec, w_mod, b_mod).reshape(depth, 8, 3, D_MODEL)
    w = _prepare_weights(w_in, w_out, norm_g, mla_q_norm_g, mla_w_uq, mla_kv_norm_g, mla_w_ukv, mla_q_head_g,
                         mla_k_head_g, gqa_q_g, gqa_k_g, ssd_conv_w, ssd_conv_b, ssd_dt_bias, ssd_a_log, ssd_d,
                         ssd_norm_g, cf_conv_w, cf_conv_b, cf_ln_g, cf_ln_b)

    y = x_prompt
    ctx_out = []
    for l in range(depth):
        y, new_ctx = _mixer_layer(y, mod, dec_b, l, w, None, None, 512)
        ctx_out.append(new_ctx)
    new = [jnp.stack([t[i] for t in ctx_out], axis=1) for i in range(5)]

    caches = dict(ckv=cache_mla_ckv,
                  kr_pad=_pad_last(cache_mla_krope, MLA_NOPE, LANES - MLA_QK),
                  gk=cache_gqa_k.reshape(dec_b, depth, past, GQA_KV_HEADS * GQA_HD),
                  gv=cache_gqa_v.reshape(dec_b, depth, past, GQA_KV_HEADS * GQA_HD),
                  state=state_ssd.reshape(dec_b, depth, 2, SSD_GROUPS, 2 * SSD_P, SSD_N))
    ropes = (_rope_table(n_lat, MLA_ROPE, MLA_NOPE), _rope_table(n_lat, GQA_HD, 0))
    z = x_sample
    for l in range(depth):
        z, _ = _mixer_layer(z, mod, None, l, w, caches, ropes, 512)

    return (y, z, new[0], new[1], new[2], new[3], new[4])
```

```python
import functools
import math

import jax
import jax.numpy as jnp
from jax import lax
from jax.experimental import pallas as pl
from jax.experimental.pallas import tpu as pltpu

F32 = jnp.float32
BF16 = jnp.bfloat16

LANES = 128
VMEM_CAP = 56 * 1024 * 1024

EPS = 1e-6
ROPE_THETA = 10000.0
GRID_W = 64
LOG2E = math.log2(math.e)

D_MODEL = 1024
BRANCH_W = 256
MLA_HEADS, MLA_NOPE, MLA_ROPE, MLA_V = 4, 64, 32, 64
MLA_QK = MLA_NOPE + MLA_ROPE
MLA_Q_RANK, MLA_KV_RANK = 192, 128
SSD_HEADS, SSD_P, SSD_N, SSD_GROUPS, SSD_CONV, SSD_CHUNK = 4, 64, 64, 2, 3, 128
SSD_W = SSD_HEADS * SSD_P
SSD_CONV_CH = SSD_W + 2 * SSD_GROUPS * SSD_N
CF_W, CF_K = 256, 31
GQA_HEADS, GQA_KV_HEADS, GQA_HD = 4, 2, 64
SPLITS = (MLA_Q_RANK, MLA_KV_RANK, MLA_ROPE, MLA_HEADS * MLA_V,
          SSD_CONV_CH, 2 * SSD_HEADS, SSD_W,
          CF_W, CF_W, CF_W,
          GQA_HEADS * GQA_HD, GQA_KV_HEADS * GQA_HD, GQA_KV_HEADS * GQA_HD, GQA_HEADS * GQA_HD)

O_QL, O_KV, O_R, O_AG = 0, 256, 384, 512
O_XBC, O_DT, O_Z = 768, 1280, 1408
O_CV, O_CGL, O_CG = 1664, 1920, 2176
O_DQ, O_DK, O_DV, O_DG = 2432, 2944, 3200, 3456
IN_WP = 3712

CONV_HALO_SSD = 8
CONV_HALO_CF = 16

SSD_UNROLL = 4
INPROJ_SUB = 256
ATT_TQ = 1024
ATT_VROWS = 80
ATT_SUB = 512
ATT_QK_CHUNK = 512
ATT_PV_CHUNK = 256

NT_DIMS = (((1,), (1,)), ((), ()))


def _silu(x):
    return x / (1.0 + jnp.exp(-x))


def _sigmoid(x):
    return 1.0 / (1.0 + jnp.exp(-x))


def _softplus(x):
    return jnp.maximum(x, 0.0) + jnp.log1p(jnp.exp(-jnp.abs(x)))


def _rms(x, n):
    return x * lax.rsqrt(jnp.sum(x * x, axis=-1, keepdims=True) * (1.0 / n) + EPS)


def _dot(a, b):
    return jnp.dot(a, b, preferred_element_type=F32)


def _dot_nt(a, b):
    return lax.dot_general(a, b, NT_DIMS, preferred_element_type=F32)


def _params(sem, vmem_bytes):
    assert vmem_bytes <= VMEM_CAP
    return pltpu.CompilerParams(dimension_semantics=sem, vmem_limit_bytes=VMEM_CAP)


def _pad_heads(w, nh, d, dp):
    s = w.shape[:-1]
    w = w.reshape(s + (nh, d))
    w = jnp.pad(w, [(0, 0)] * len(s) + [(0, 0), (0, dp - d)])
    return w.reshape(s + (nh * dp,))


def _pad_last(w, lo, hi):
    return jnp.pad(w, [(0, 0)] * (w.ndim - 1) + [(lo, hi)])


def _layer_spec(a, l, grid_rank):
    zeros = (0,) * (a.ndim - 1)
    if grid_rank == 1:
        return pl.BlockSpec((1,) + a.shape[1:], lambda bi: (l,) + zeros)
    return pl.BlockSpec((1,) + a.shape[1:], lambda bi, i: (l,) + zeros)


def _with_ones_lane(v, hd):
    lane = lax.broadcasted_iota(jnp.int32, v.shape, v.ndim - 1)
    return jnp.where(lane % LANES == hd, 1.0, v)


def _mod_kernel(c_ref, w_ref, b_ref, o_ref):
    c = c_ref[...]
    o_ref[0] = _dot(_silu(c).astype(BF16), w_ref[0].astype(BF16)) + b_ref[0]


def _modulation(cvec, w_mod, b_mod):
    depth, d, d3 = w_mod.shape
    tn = d
    return pl.pallas_call(
        _mod_kernel,
        out_shape=jax.ShapeDtypeStruct((depth, 8, d3), F32),
        grid=(depth, d3 // tn),
        in_specs=[pl.BlockSpec((8, d), lambda l, j: (0, 0)),
                  pl.BlockSpec((1, d, tn), lambda l, j: (l, 0, j)),
                  pl.BlockSpec((1, 1, tn), lambda l, j: (l, 0, j))],
        out_specs=pl.BlockSpec((1, 8, tn), lambda l, j: (l, 0, j)),
        compiler_params=_params(("arbitrary", "arbitrary"), 4 * d * tn * 4),
        name="modulation",
    )(cvec, w_mod, b_mod.reshape(depth, 1, d3))


def _rope(x, tab_ref, rows, shift):
    cos, sin_a, sin_b = tab_ref[0, rows, :], tab_ref[1, rows, :], tab_ref[2, rows, :]
    outs = []
    for h in range(x.shape[-1] // LANES):
        xh = x[:, h * LANES:(h + 1) * LANES]
        up = pltpu.roll(xh, LANES - shift, 1)
        dn = pltpu.roll(xh, shift, 1)
        outs.append(xh * cos + up * sin_a + dn * sin_b)
    return jnp.concatenate(outs, axis=-1)


def _head_norm(x, gain, n, ones_bd):
    xx = (x * x).astype(BF16)
    w2 = 2 * LANES
    ss = jnp.concatenate([_dot(xx[:, j:j + w2], ones_bd) for j in range(0, x.shape[-1], w2)], axis=-1)
    return x * lax.rsqrt(ss * (1.0 / n) + EPS) * gain


def _inproj_refs(latent, refs):
    names = ['x', 'mod', 'ng', 'win', 'wuq', 'wukv', 'gains']
    if latent:
        names += ['wvta', 'wvtd', 'ra', 'rd']
    names += ['qa', 'ka', 'va', 'ag', 'qd', 'kd', 'vd', 'dg', 'xbc', 'dt', 'z', 'glu', 'cg']
    if not latent:
        names += ['ckv', 'kr', 'gk', 'gv']
    return dict(zip(names, refs))


def _inproj_main(r, rows):
    x = r['x'][0, rows, :]
    shift = r['mod'][0, 0, 0:1, :]
    scale = r['mod'][0, 0, 1:2, :]
    h = (_rms(x, D_MODEL) * r['ng'][0] * (1.0 + scale) + shift).astype(BF16)
    return h, _dot(h, r['win'][0])


def _values_t(w_t, x):
    vt = _dot_nt(w_t, x)
    row = lax.broadcasted_iota(jnp.int32, vt.shape, 0)
    return jnp.where(row % ATT_VROWS == MLA_V, 1.0, vt)


def _inproj_latents(r, u):
    g_ql = r['gains'][0, 0:1, 0:256]
    g_kv = r['gains'][0, 2:3, 0:LANES]
    ql = _rms(u[:, O_QL:O_QL + 256], MLA_Q_RANK) * g_ql
    ckv = _rms(u[:, O_KV:O_KV + LANES], MLA_KV_RANK) * g_kv
    return _dot(ql.astype(BF16), r['wuq'][0]), ckv, _dot(ckv.astype(BF16), r['wukv'][0])


def _inproj_tail(latent, r, rows, h, u, q, ckv, kv):
    wa, wd = MLA_HEADS * LANES, GQA_KV_HEADS * LANES
    g_qh = r['gains'][0, 1:2, :]
    g_kh = r['gains'][0, 3:4, :]
    g_dq = r['gains'][0, 4:5, :]
    g_dk = r['gains'][0, 5:6, 0:wd]
    ri = lax.broadcasted_iota(jnp.int32, (2 * LANES, 2 * LANES), 0) // LANES
    ci = lax.broadcasted_iota(jnp.int32, (2 * LANES, 2 * LANES), 1) // LANES
    ones_bd = (ri == ci).astype(F32).astype(BF16)

    q = _head_norm(q, g_qh, MLA_QK, ones_bd)
    kr = u[:, O_R:O_R + LANES]
    k = _head_norm(kv[:, 0:wa] + jnp.concatenate([kr] * MLA_HEADS, axis=-1), g_kh, MLA_QK, ones_bd)
    qd = _head_norm(u[:, O_DQ:O_DQ + GQA_HEADS * LANES], g_dq, GQA_HD, ones_bd)
    kd = _head_norm(u[:, O_DK:O_DK + wd], g_dk, GQA_HD, ones_bd)
    vd = u[:, O_DV:O_DV + wd]
    if latent:
        q = _rope(q, r['ra'], rows, MLA_ROPE // 4)
        k = _rope(k, r['ra'], rows, MLA_ROPE // 4)
        qd = _rope(qd, r['rd'], rows, GQA_HD // 4)
        kd_att = _rope(kd, r['rd'], rows, GQA_HD // 4)
    else:
        kd_att = kd
        r['ckv'][0, rows, :] = ckv
        r['kr'][0, rows, :] = kr[:, MLA_NOPE:MLA_NOPE + MLA_ROPE]
        for i in range(GQA_KV_HEADS):
            r['gk'][0, rows, i * GQA_HD:(i + 1) * GQA_HD] = kd[:, i * LANES:i * LANES + GQA_HD]
            r['gv'][0, rows, i * GQA_HD:(i + 1) * GQA_HD] = vd[:, i * LANES:i * LANES + GQA_HD]
    r['qa'][0, rows, :] = q.astype(BF16)
    r['ka'][0, rows, :] = k.astype(BF16)
    if latent:
        r['va'][0, :, rows] = _values_t(r['wvta'][0], ckv.astype(BF16)).astype(BF16)
        r['vd'][0, :, rows] = _values_t(r['wvtd'][0], h).astype(BF16)
    else:
        r['va'][0, rows, :] = _with_ones_lane(kv[:, wa:], MLA_V).astype(BF16)
        r['vd'][0, rows, :] = _with_ones_lane(vd, GQA_HD).astype(BF16)
    r['ag'][0, rows, :] = u[:, O_AG:O_AG + BRANCH_W]
    r['qd'][0, rows, :] = qd.astype(BF16)
    r['kd'][0, rows, :] = kd_att.astype(BF16)
    r['dg'][0, rows, :] = u[:, O_DG:O_DG + BRANCH_W]

    r['xbc'][0, rows, :] = u[:, O_XBC:O_XBC + SSD_CONV_CH]
    r['dt'][0, rows, :] = u[:, O_DT:O_DT + LANES]
    r['z'][0, rows, :] = u[:, O_Z:O_Z + SSD_W]
    r['glu'][0, rows, :] = u[:, O_CV:O_CV + CF_W] * _sigmoid(u[:, O_CGL:O_CGL + CF_W])
    r['cg'][0, rows, :] = u[:, O_CG:O_CG + CF_W]


def _inproj_kernel(latent, sub, *refs):
    r = _inproj_refs(latent, refs)
    tm = r['x'].shape[1]
    for r0 in range(0, tm, sub):
        rows = slice(r0, r0 + sub)
        h, u = _inproj_main(r, rows)
        q, ckv, kv = _inproj_latents(r, u)
        _inproj_tail(latent, r, rows, h, u, q, ckv, kv)


def _mod_spec(mod, l, row):
    d = mod.shape[-1]
    if row is None:
        return pl.BlockSpec((1, 1, 3, d), lambda bi, i: (l, bi, 0, 0))
    return pl.BlockSpec((1, 1, 3, d), lambda bi, i: (l, row, 0, 0))


def _inproj(x, mod, mod_row, l, w, ropes, tm):
    b, s, d = x.shape
    latent = ropes is not None
    tok = lambda wd: pl.BlockSpec((1, tm, wd), lambda bi, i: (bi, i, 0))
    weights = [w['ng'], w['win'], w['wuq'], w['wukv'], w['gains']]
    if latent:
        weights += [w['wvta'], w['wvtd']]
    in_specs = [tok(d), _mod_spec(mod, l, mod_row)] + [_layer_spec(a, l, 2) for a in weights]
    args = [x, mod] + weights
    if latent:
        in_specs += [pl.BlockSpec((3, tm, LANES), lambda bi, i: (0, i, 0))] * 2
        args += list(ropes)
    widths = [(4 * LANES, BF16), (4 * LANES, BF16), (4 * LANES, BF16), (BRANCH_W, F32),
              (4 * LANES, BF16), (2 * LANES, BF16), (2 * LANES, BF16), (BRANCH_W, F32),
              (SSD_CONV_CH, F32), (LANES, F32), (SSD_W, F32), (CF_W, F32), (CF_W, F32)]
    if not latent:
        widths += [(MLA_KV_RANK, F32), (MLA_ROPE, F32), (LANES, F32), (LANES, F32)]
    out_shape = [jax.ShapeDtypeStruct((b, s, wd), dt) for wd, dt in widths]
    out_specs = [tok(wd) for wd, _ in widths]
    if latent:
        for pos, heads in ((2, MLA_HEADS), (6, GQA_KV_HEADS)):
            out_shape[pos] = jax.ShapeDtypeStruct((b, heads * ATT_VROWS, s), BF16)
            out_specs[pos] = pl.BlockSpec((1, heads * ATT_VROWS, tm), lambda bi, i: (bi, 0, i))
    out_bytes = sum(wd * jnp.dtype(dt).itemsize for wd, dt in widths) * tm
    w_bytes = sum(a[0].size * a.dtype.itemsize for a in weights)
    sub = min(INPROJ_SUB, tm)
    vmem = 2 * (tm * d * 4 + w_bytes + out_bytes) + 3 * sub * IN_WP * 4
    return pl.pallas_call(
        functools.partial(_inproj_kernel, latent, sub),
        out_shape=out_shape,
        grid=(b, s // tm),
        in_specs=in_specs,
        out_specs=out_specs,
        compiler_params=_params(("parallel", "parallel"), vmem),
        name="inproj_lat" if latent else "inproj_ctx",
    )(*args)


def _ctxkv_kernel(ckv_ref, kr_ref, gk_ref, gv_ref, wukv_ref, wvta_ref, gains_ref, kc_ref, vc_ref, kdc_ref, vdc_ref):
    ckv = ckv_ref[0, 0].astype(BF16)
    kv = _dot(ckv, wukv_ref[0])
    kr = kr_ref[0, 0]
    g = gains_ref[0, 3:4, 0:LANES]
    for i in range(MLA_HEADS):
        k = _rms(kv[:, i * LANES:(i + 1) * LANES] + kr, MLA_QK) * g
        kc_ref[0, :, i * LANES:(i + 1) * LANES] = k.astype(BF16)
    vc_ref[0] = _values_t(wvta_ref[0], ckv).astype(BF16)
    gk = gk_ref[0, 0]
    p = gk.shape[0]
    pad = jnp.zeros((p, LANES - GQA_HD), BF16)
    for i in range(GQA_KV_HEADS):
        lo = i * LANES
        kdc_ref[0, :, lo:lo + GQA_HD] = gk[:, i * GQA_HD:(i + 1) * GQA_HD].astype(BF16)
        kdc_ref[0, :, lo + GQA_HD:lo + LANES] = pad
    gvt = gv_ref[0, 0].T
    row = lax.broadcasted_iota(jnp.int32, (ATT_VROWS - GQA_HD, p), 0)
    tail = jnp.where(row == 0, 1.0, 0.0)
    pieces = []
    for i in range(GQA_KV_HEADS):
        pieces += [gvt[i * GQA_HD:(i + 1) * GQA_HD, :], tail]
    vdc_ref[0] = jnp.concatenate(pieces, axis=0).astype(BF16)


def _ctx_kv(ckv, kr_pad, gk, gv, l, w):
    b, _, p, _ = ckv.shape
    cache = lambda a: pl.BlockSpec((1, 1, p, a.shape[-1]), lambda bi: (bi, l, 0, 0))
    out = lambda wd: pl.BlockSpec((1, p, wd), lambda bi: (bi, 0, 0))
    out_t = lambda rows: pl.BlockSpec((1, rows, p), lambda bi: (bi, 0, 0))
    wa, wd = MLA_HEADS * LANES, GQA_KV_HEADS * LANES
    ra, rd = MLA_HEADS * ATT_VROWS, GQA_KV_HEADS * ATT_VROWS
    return pl.pallas_call(
        _ctxkv_kernel,
        out_shape=[jax.ShapeDtypeStruct((b, p, wa), BF16), jax.ShapeDtypeStruct((b, ra, p), BF16),
                   jax.ShapeDtypeStruct((b, p, wd), BF16), jax.ShapeDtypeStruct((b, rd, p), BF16)],
        grid=(b,),
        in_specs=[cache(ckv), cache(kr_pad), cache(gk), cache(gv),
                  _layer_spec(w['wukv'], l, 1), _layer_spec(w['wvta'], l, 1), _layer_spec(w['gains'], l, 1)],
        out_specs=[out(wa), out_t(ra), out(wd), out_t(rd)],
        compiler_params=_params(("parallel",), 16 * p * wa * 4),
        name="ctx_kv",
    )(ckv, kr_pad, gk, gv, w['wukv'], w['wvta'], w['gains'])


def _lane_tile_max(t):
    r = t[:, 0:LANES]
    for j in range(1, t.shape[-1] // LANES):
        r = jnp.maximum(r, t[:, j * LANES:(j + 1) * LANES])
    return r


def _attn_unit(n_heads, rep, has_ctx, refs, bi, rows, h, s_ref):
    if has_ctx:
        q_ref, k_ref, v_ref, kc_ref, vc_ref, g_ref, o_ref = refs
        p_len = kc_ref.shape[1]
    else:
        q_ref, k_ref, v_ref, g_ref, o_ref = refs
        p_len = 0
    s_len = k_ref.shape[1]
    hd = o_ref.shape[-1] // n_heads
    kb = min(ATT_QK_CHUNK, s_len)
    kb2 = ATT_PV_CHUNK
    g = h // rep
    hs = slice(h * LANES, (h + 1) * LANES)
    gs = slice(g * LANES, (g + 1) * LANES)
    qh = q_ref[bi, rows, hs]
    mx = None
    for c in range(s_len // kb):
        sc = _dot_nt(qh, k_ref[bi, c * kb:(c + 1) * kb, gs])
        s_ref[:, c * kb:(c + 1) * kb] = sc
        t = _lane_tile_max(sc)
        mx = t if mx is None else jnp.maximum(mx, t)
    if has_ctx:
        sc = _dot_nt(qh, kc_ref[bi, :, gs])
        s_ref[:, s_len:s_len + p_len] = sc
        mx = jnp.maximum(mx, _lane_tile_max(sc))
    m = jnp.max(mx, axis=-1, keepdims=True)
    acc = None
    for c in range((s_len + p_len) // kb2):
        r0 = c * kb2
        p = jnp.exp2(s_ref[:, r0:r0 + kb2] - m).astype(BF16)
        vv = v_ref[bi, r0:r0 + kb2, gs] if r0 < s_len else vc_ref[bi, r0 - s_len:r0 - s_len + kb2, gs]
        part = _dot(p, vv)
        acc = part if acc is None else acc + part
    o = acc[:, 0:hd] / acc[:, hd:hd + 1]
    os_ = slice(h * hd, (h + 1) * hd)
    o_ref[bi, rows, os_] = (o * _silu(g_ref[bi, rows, os_])).astype(o_ref.dtype)


def _attn_kernel(n_heads, rep, has_ctx, sub, *refs):
    s0_ref, s1_ref = refs[-2:]
    bb, tq = refs[0].shape[0], refs[0].shape[1]
    unit = 0
    for bi in range(bb):
        for r0 in range(0, tq, sub):
            for h in range(n_heads):
                _attn_unit(n_heads, rep, has_ctx, refs[:-2], bi, slice(r0, r0 + sub), h,
                           s0_ref if unit % 2 == 0 else s1_ref)
                unit += 1


def _attention(q, k, v, ctx, gate, n_heads, rep, bb, tq, sub, name):
    b, s, qw = q.shape
    kw = k.shape[-1]
    ow = gate.shape[-1]
    has_ctx = ctx is not None
    in_specs = [pl.BlockSpec((bb, tq, qw), lambda bi, i: (bi, i, 0)),
                pl.BlockSpec((bb, s, kw), lambda bi, i: (bi, 0, 0)),
                pl.BlockSpec((bb, s, kw), lambda bi, i: (bi, 0, 0))]
    args = [q, k, v]
    p = 0
    if has_ctx:
        p = ctx[0].shape[1]
        in_specs += [pl.BlockSpec((bb, p, kw), lambda bi, i: (bi, 0, 0))] * 2
        args += list(ctx)
    in_specs.append(pl.BlockSpec((bb, tq, ow), lambda bi, i: (bi, i, 0)))
    args.append(gate)
    assert s % min(ATT_QK_CHUNK, s) == 0 and s % ATT_PV_CHUNK == 0 and p % ATT_PV_CHUNK == 0
    assert b % bb == 0 and s % tq == 0 and tq % sub == 0
    vmem = (2 * bb * (tq * qw * 2 + 2 * (s + p) * kw * 2 + 2 * tq * ow * 4)
            + 2 * sub * (s + p) * 4 + 2 * sub * 1024 * 4)
    return pl.pallas_call(
        functools.partial(_attn_kernel, n_heads, rep, has_ctx, sub),
        out_shape=jax.ShapeDtypeStruct((b, s, ow), BF16),
        grid=(b // bb, s // tq),
        in_specs=in_specs,
        out_specs=pl.BlockSpec((bb, tq, ow), lambda bi, i: (bi, i, 0)),
        scratch_shapes=[pltpu.VMEM((sub, s + p), F32), pltpu.VMEM((sub, s + p), F32)],
        compiler_params=_params(("parallel", "arbitrary"), vmem),
        name=name,
    )(*args)


def _attn_t_scores(io, rep, sub, unit, s_ref):
    q_ref, k_ref, _, kc_ref = io[:4]
    r0, h = unit
    s_len, p_len = k_ref.shape[1], kc_ref.shape[1]
    kb = min(ATT_QK_CHUNK, s_len)
    gs = slice((h // rep) * LANES, (h // rep + 1) * LANES)
    qh = q_ref[0, r0:r0 + sub, h * LANES:(h + 1) * LANES]
    mx = None
    for c in range(s_len // kb):
        st = _dot_nt(k_ref[0, c * kb:(c + 1) * kb, gs], qh)
        s_ref[c * kb:(c + 1) * kb, :] = st
        t = jnp.max(st, axis=0, keepdims=True)
        mx = t if mx is None else jnp.maximum(mx, t)
    st = _dot_nt(kc_ref[0, :, gs], qh)
    s_ref[s_len:s_len + p_len, :] = st
    return jnp.maximum(mx, jnp.max(st, axis=0, keepdims=True))


def _attn_t_values(io, n_heads, rep, sub, unit, s_ref, m, ot_ref):
    _, k_ref, vt_ref, kc_ref, vct_ref, _, o_ref = io
    r0, h = unit
    s_len, p_len = k_ref.shape[1], kc_ref.shape[1]
    hd = o_ref.shape[-1] // n_heads
    vs = slice((h // rep) * ATT_VROWS, (h // rep + 1) * ATT_VROWS)
    kb2 = ATT_PV_CHUNK
    acc = None
    for c in range((s_len + p_len) // kb2):
        k0 = c * kb2
        pt = jnp.exp2(s_ref[k0:k0 + kb2, :] - m).astype(BF16)
        vv = vt_ref[0, vs, k0:k0 + kb2] if k0 < s_len else vct_ref[0, vs, k0 - s_len:k0 - s_len + kb2]
        part = _dot(vv, pt)
        acc = part if acc is None else acc + part
    ot_ref[h * hd:(h + 1) * hd, r0:r0 + sub] = acc[0:hd, :] / acc[hd:hd + 1, :]


def _attn_t_kernel(n_heads, rep, sub, *refs):
    io, (s0_ref, s1_ref, ot_ref) = refs[:-3], refs[-3:]
    tq = io[0].shape[1]
    units = [(r0, h) for r0 in range(0, tq, sub) for h in range(n_heads)]
    bufs = (s0_ref, s1_ref)
    m = _attn_t_scores(io, rep, sub, units[0], bufs[0])
    for u in range(len(units)):
        m_next = None
        if u + 1 < len(units):
            m_next = _attn_t_scores(io, rep, sub, units[u + 1], bufs[(u + 1) % 2])
        _attn_t_values(io, n_heads, rep, sub, units[u], bufs[u % 2], m, ot_ref)
        m = m_next
    g_ref, o_ref = io[5], io[6]
    o_ref[0] = (ot_ref[...].T * _silu(g_ref[0])).astype(o_ref.dtype)


def _attention_t(q, k, vt, ctx, gate, n_heads, rep, tq, sub, name):
    b, s, qw = q.shape
    kw = k.shape[-1]
    vr = vt.shape[1]
    ow = gate.shape[-1]
    kc, vct = ctx
    p = kc.shape[1]
    assert s % min(ATT_QK_CHUNK, s) == 0 and s % ATT_PV_CHUNK == 0 and p % ATT_PV_CHUNK == 0
    assert s % tq == 0 and tq % sub == 0
    tile = lambda wd: pl.BlockSpec((1, tq, wd), lambda bi, i: (bi, i, 0))
    whole = lambda a: pl.BlockSpec((1,) + a.shape[1:], lambda bi, i: (bi, 0, 0))
    vmem = (2 * (tq * qw * 2 + (s + p) * (kw + vr) * 2 + tq * ow * 6)
            + 2 * sub * (s + p) * 4 + ow * tq * 4 + 2 * sub * 1024 * 4)
    return pl.pallas_call(
        functools.partial(_attn_t_kernel, n_heads, rep, sub),
        out_shape=jax.ShapeDtypeStruct((b, s, ow), BF16),
        grid=(b, s // tq),
        in_specs=[tile(qw), whole(k), whole(vt), whole(kc), whole(vct), tile(ow)],
        out_specs=tile(ow),
        scratch_shapes=[pltpu.VMEM((s + p, sub), F32), pltpu.VMEM((s + p, sub), F32), pltpu.VMEM((ow, tq), F32)],
        compiler_params=_params(("parallel", "arbitrary"), vmem),
        name=name,
    )(q, k, vt, kc, vct, gate)


def _conv_kernel(tl, xc_ref, xp_ref, xn_ref, gc_ref, gp_ref, gn_ref, cg_ref,
                 w3_ref, b3_ref, w31_ref, b31_ref, lng_ref, lnb_ref,
                 xo_ref, co_ref, ext3, ext31):
    i = pl.program_id(1)
    keep_prev = (i > 0).astype(F32)
    keep_next = (i < pl.num_programs(1) - 1).astype(F32)

    h3 = CONV_HALO_SSD
    ext3[0:h3, :] = xp_ref[0] * keep_prev
    ext3[h3:h3 + tl, :] = xc_ref[0]
    ext3[h3 + tl:2 * h3 + tl, :] = xn_ref[0] * keep_next
    acc = jnp.zeros((tl, SSD_CONV_CH), F32) + b3_ref[0]
    for k in range(SSD_CONV):
        acc = acc + w3_ref[0, k:k + 1, :] * ext3[h3 - SSD_CONV // 2 + k:h3 - SSD_CONV // 2 + k + tl, :]
    xo_ref[0] = _silu(acc)

    h31 = CONV_HALO_CF
    ext31[0:h31, :] = gp_ref[0] * keep_prev
    ext31[h31:h31 + tl, :] = gc_ref[0]
    ext31[h31 + tl:2 * h31 + tl, :] = gn_ref[0] * keep_next
    acc = jnp.zeros((tl, CF_W), F32) + b31_ref[0]
    o_lo = h31 - CF_K // 2
    ext = ext31[...]
    n_ext = ext.shape[0]
    for r in range(8):
        taps = [k for k in range(CF_K) if (o_lo + k) % 8 == r]
        if not taps:
            continue
        shifted = ext if r == 0 else pltpu.roll(ext, n_ext - r, 0)
        for k in taps:
            a = (o_lo + k) // 8 * 8
            acc = acc + w31_ref[0, k:k + 1, :] * shifted[a:a + tl, :]
    mu = jnp.mean(acc, axis=-1, keepdims=True)
    cen = acc - mu
    var = jnp.mean(cen * cen, axis=-1, keepdims=True)
    y = cen * lax.rsqrt(var + EPS) * lng_ref[0] + lnb_ref[0]
    co_ref[0] = (_silu(y) * _silu(cg_ref[0])).astype(co_ref.dtype)


def _convs(xbc, glu, cg, l, w, tl):
    b, s, _ = xbc.shape
    nt = s // tl

    def cur(wd):
        return pl.BlockSpec((1, tl, wd), lambda bi, i: (bi, i, 0))

    def prev(wd, hrows):
        r = tl // hrows
        return pl.BlockSpec((1, hrows, wd), lambda bi, i: (bi, jnp.maximum(i * r - 1, 0), 0))

    def nxt(wd, hrows):
        r = tl // hrows
        return pl.BlockSpec((1, hrows, wd), lambda bi, i: (bi, jnp.minimum((i + 1) * r, nt * r - 1), 0))

    weights = [w['w3'], w['b3'], w['w31'], w['b31'], w['lng'], w['lnb']]
    vmem = 4 * tl * (SSD_CONV_CH * 2 + CF_W * 3) * 4 + 6 * tl * SSD_CONV_CH * 4
    return pl.pallas_call(
        functools.partial(_conv_kernel, tl),
        out_shape=[jax.ShapeDtypeStruct((b, s, SSD_CONV_CH), F32),
                   jax.ShapeDtypeStruct((b, s, CF_W), BF16)],
        grid=(b, nt),
        in_specs=[cur(SSD_CONV_CH), prev(SSD_CONV_CH, CONV_HALO_SSD), nxt(SSD_CONV_CH, CONV_HALO_SSD),
                  cur(CF_W), prev(CF_W, CONV_HALO_CF), nxt(CF_W, CONV_HALO_CF), cur(CF_W)]
                 + [_layer_spec(a, l, 2) for a in weights],
        out_specs=[cur(SSD_CONV_CH), cur(CF_W)],
        scratch_shapes=[pltpu.VMEM((tl + 2 * CONV_HALO_SSD, SSD_CONV_CH), F32),
                        pltpu.VMEM((tl + 2 * CONV_HALO_CF, CF_W), F32)],
        compiler_params=_params(("parallel", "parallel"), vmem),
        name="convs",
    )(xbc, xbc, xbc, glu, glu, glu, cg, *weights)


def _split3(x):
    hi = x.astype(BF16)
    r1 = x - hi.astype(F32)
    mid = r1.astype(BF16)
    lo = (r1 - mid.astype(F32)).astype(BF16)
    return hi, mid, lo


def _cumsum_rows(tri, x):
    return sum(_dot(tri, part) for part in _split3(x))


def _transpose_exact(x):
    c = x.shape[1]
    eye = (lax.broadcasted_iota(jnp.int32, (c, c), 0) == lax.broadcasted_iota(jnp.int32, (c, c), 1))
    eye = eye.astype(F32).astype(BF16)
    return sum(_dot_nt(eye, part) for part in _split3(x))


SSD_NK = 2 * SSD_HEADS


def _ssd_stage_local(c, d, xc_ref, dt_ref, bias, a_row):
    q = SSD_CHUNK
    rows = pl.ds(pl.multiple_of(c * q, q), q)
    t = dict(rows=rows, d=d)
    t['xs'] = xc_ref[0, rows, 0:SSD_W]
    t['bm'] = xc_ref[0, rows, SSD_W:SSD_W + LANES]
    t['cm'] = xc_ref[0, rows, SSD_W + LANES:SSD_W + 2 * LANES].astype(BF16)
    t['dt'] = _softplus(dt_ref[0, rows, :] + bias)
    ri = lax.broadcasted_iota(jnp.int32, (q, q), 0)
    ci = lax.broadcasted_iota(jnp.int32, (q, q), 1)
    t['mask'] = (ri >= ci) if d == 0 else (ri <= ci)
    t['acum'] = _cumsum_rows(t['mask'].astype(F32).astype(BF16), t['dt'] * a_row)
    return t


def _ssd_stage_cb(t):
    q = SSD_CHUNK
    lane = lax.broadcasted_iota(jnp.int32, (q, LANES), 1)
    t['tr'] = jnp.where(lane < SSD_NK, t['dt'], t['acum']).T
    bt = t['bm'].T
    t['bt'] = bt
    grp_row = lax.broadcasted_iota(jnp.int32, (LANES, q), 0) < SSD_N
    t['cb'] = [_dot(t['cm'], jnp.where(grp_row if g == 0 else jnp.logical_not(grp_row), bt, 0.0).astype(BF16))
               for g in range(SSD_GROUPS)]
    t['last'] = q - 1 if t['d'] == 0 else 0
    t['etot'] = jnp.exp(t['acum'][t['last']:t['last'] + 1, :])
    t['eac'] = jnp.exp(t['acum'])


def _ssd_stage_diag(t, dskip):
    q = SSD_CHUNK
    d = t['d']
    lo_half = lax.broadcasted_iota(jnp.int32, (q, LANES), 1) < SSD_P
    tr, acum, last = t['tr'], t['acum'], t['last']
    t['yd'], t['cs'] = [], []
    for g in range(SSD_GROUPS):
        k0 = d * SSD_HEADS + 2 * g
        xg = t['xs'][:, g * LANES:(g + 1) * LANES]
        xgb = xg.astype(BF16)
        btg = t['bt'][g * SSD_N:(g + 1) * SSD_N, :]
        yg = None
        cs = []
        for hh in range(2):
            k = k0 + hh
            dt_row = tr[k:k + 1, :]
            ac_row = tr[SSD_NK + k:SSD_NK + k + 1, :]
            dec = jnp.where(t['mask'], jnp.exp(acum[:, k:k + 1] - ac_row), 0.0)
            xm = jnp.where(lo_half if hh == 0 else jnp.logical_not(lo_half), xg, 0.0).astype(BF16)
            part = _dot((t['cb'][g] * dec * dt_row).astype(BF16), xm)
            yg = part if yg is None else yg + part
            w_row = dt_row * jnp.exp(tr[SSD_NK + k:SSD_NK + k + 1, last:last + 1] - ac_row)
            cs.append(_dot((btg * w_row).astype(BF16), xgb))
        if d == 0:
            yg = yg + dskip[:, g * LANES:(g + 1) * LANES] * xg
        t['yd'].append(yg)
        t['cs'].append(cs)


def _ssd_stage_state(t, states):
    q = SSD_CHUNK
    d = t['d']
    lo_half = lax.broadcasted_iota(jnp.int32, (q, LANES), 1) < SSD_P
    lane_n = lax.broadcasted_iota(jnp.int32, (SSD_N, LANES), 1) < SSD_P
    zero = jnp.zeros((SSD_N, LANES), F32)
    ys, new_states = [], []
    for g in range(SSD_GROUPS):
        k0 = d * SSD_HEADS + 2 * g
        s_prev = states[g]
        sz = jnp.concatenate([s_prev, zero] if g == 0 else [zero, s_prev], axis=0).astype(BF16)
        eacs = jnp.where(lo_half, t['eac'][:, k0:k0 + 1], t['eac'][:, k0 + 1:k0 + 2])
        ys.append(t['yd'][g] + _dot(t['cm'], sz) * eacs)
        cdec = jnp.where(lane_n, t['etot'][:, k0:k0 + 1], t['etot'][:, k0 + 1:k0 + 2])
        new_states.append(cdec * s_prev + jnp.where(lane_n, t['cs'][g][0], t['cs'][g][1]))
    return ys, new_states


def _ssd_kernel(nc, unroll, has_init, *refs):
    if has_init:
        xc_ref, dt_ref, s0_ref, bias_ref, alog_ref, d_ref, yf_ref, yb_ref, so_ref, st_ref = refs
        for d in range(2):
            for g in range(SSD_GROUPS):
                st_ref[0, d, g] = _transpose_exact(s0_ref[0, 0, d, g])
    else:
        xc_ref, dt_ref, bias_ref, alog_ref, d_ref, yf_ref, yb_ref, so_ref, st_ref = refs
        st_ref[...] = jnp.zeros(st_ref.shape, F32)
    lane = lax.broadcasted_iota(jnp.int32, (1, LANES), 1)
    a_row = jnp.where(lane < 2 * SSD_NK, -jnp.exp(alog_ref[0]), 0.0)
    bias = bias_ref[0]
    dskip = d_ref[0]

    def body(i, carry):
        states = [[st_ref[0, d, g] for g in range(SSD_GROUPS)] for d in range(2)]
        work = []
        for j in range(unroll):
            c = i * unroll + j
            work.append((_ssd_stage_local(c, 0, xc_ref, dt_ref, bias, a_row), yf_ref))
            work.append((_ssd_stage_local(nc - 1 - c, 1, xc_ref, dt_ref, bias, a_row), yb_ref))
        for t, _ in work:
            _ssd_stage_cb(t)
        for t, _ in work:
            _ssd_stage_diag(t, dskip)
        writes = []
        for t, y_ref in work:
            ys, states[t['d']] = _ssd_stage_state(t, states[t['d']])
            writes.append((y_ref, t['rows'], ys))
        for y_ref, rows, ys in writes:
            for g in range(SSD_GROUPS):
                y_ref[0, rows, g * LANES:(g + 1) * LANES] = ys[g]
        for d in range(2):
            for g in range(SSD_GROUPS):
                st_ref[0, d, g] = states[d][g]
        return carry

    lax.fori_loop(0, nc // unroll, body, 0)
    for d in range(2):
        for g in range(SSD_GROUPS):
            so_ref[0, d, g] = _transpose_exact(st_ref[0, d, g])


def _ssd(xc, dt, s0, l, w):
    b, s, _ = xc.shape
    nc = s // SSD_CHUNK
    unroll = min(SSD_UNROLL, nc)
    assert nc % unroll == 0
    st_shape = (b, 2, SSD_GROUPS, 2 * SSD_P, SSD_N)
    st_spec = pl.BlockSpec((1,) + st_shape[1:], lambda bi: (bi, 0, 0, 0, 0))
    y_spec = pl.BlockSpec((1, s, SSD_W), lambda bi: (bi, 0, 0))
    in_specs = [pl.BlockSpec((1, s, SSD_CONV_CH), lambda bi: (bi, 0, 0)),
                pl.BlockSpec((1, s, LANES), lambda bi: (bi, 0, 0))]
    args = [xc, dt]
    if s0 is not None:
        in_specs.append(pl.BlockSpec((1, 1) + st_shape[1:], lambda bi: (bi, l, 0, 0, 0, 0)))
        args.append(s0)
    weights = [w['dt_bias'], w['a_log'], w['dskip']]
    vmem = 2 * s * (SSD_CONV_CH + LANES + 2 * SSD_W) * 4 + 8 * 1024 * 1024
    return pl.pallas_call(
        functools.partial(_ssd_kernel, nc, unroll, s0 is not None),
        out_shape=[jax.ShapeDtypeStruct((b, s, SSD_W), F32), jax.ShapeDtypeStruct((b, s, SSD_W), F32),
                   jax.ShapeDtypeStruct(st_shape, F32)],
        grid=(b,),
        in_specs=in_specs + [_layer_spec(a, l, 1) for a in weights],
        out_specs=[y_spec, y_spec, st_spec],
        scratch_shapes=[pltpu.VMEM((1, 2, SSD_GROUPS, SSD_N, 2 * SSD_P), F32)],
        compiler_params=_params(("parallel",), vmem),
        name="ssd",
    )(*args, *weights)


def _outproj_kernel(oa_ref, yf_ref, yb_ref, z_ref, oc_ref, od_ref, x_ref, mod_ref, ng_ref, w_ref, o_ref):
    ob = _rms((yf_ref[0] + yb_ref[0]) * _silu(z_ref[0]), SSD_W) * ng_ref[0]
    acc = _dot(oa_ref[0], w_ref[0, 0:BRANCH_W, :])
    acc = acc + _dot(ob.astype(BF16), w_ref[0, BRANCH_W:2 * BRANCH_W, :])
    acc = acc + _dot(oc_ref[0], w_ref[0, 2 * BRANCH_W:3 * BRANCH_W, :])
    acc = acc + _dot(od_ref[0], w_ref[0, 3 * BRANCH_W:4 * BRANCH_W, :])
    o_ref[0] = x_ref[0] + mod_ref[0, 0, 2:3, :] * acc


def _outproj(oa, yf, yb, z, oc, od, x, mod, mod_row, l, w, tm):
    b, s, d = x.shape
    br = lambda: pl.BlockSpec((1, tm, BRANCH_W), lambda bi, i: (bi, i, 0))
    xs = pl.BlockSpec((1, tm, d), lambda bi, i: (bi, i, 0))
    vmem = 2 * (6 * tm * BRANCH_W * 4 + 2 * tm * d * 4 + d * d * 2) + 2 * tm * d * 4
    return pl.pallas_call(
        _outproj_kernel,
        out_shape=jax.ShapeDtypeStruct((b, s, d), F32),
        grid=(b, s // tm),
        in_specs=[br(), br(), br(), br(), br(), br(), xs, _mod_spec(mod, l, mod_row),
                  _layer_spec(w['ssd_ng'], l, 2), _layer_spec(w['wout'], l, 2)],
        out_specs=xs,
        compiler_params=_params(("parallel", "parallel"), vmem),
        name="outproj",
    )(oa, yf, yb, z, oc, od, x, mod, w['ssd_ng'], w['wout'])


def _rope_table(n_tokens, dim, lane0):
    n_rows = n_tokens // GRID_W
    row = jnp.repeat(jnp.arange(n_rows), GRID_W).astype(F32)
    col = jnp.tile(jnp.arange(GRID_W), n_rows).astype(F32)
    quarter = dim // 4
    inv = ROPE_THETA ** (-jnp.arange(quarter, dtype=F32) / quarter)
    ar = row[:, None] * inv
    ac = col[:, None] * inv
    zero = jnp.zeros_like(ar)
    cos = jnp.concatenate([jnp.cos(ar), jnp.cos(ar), jnp.cos(ac), jnp.cos(ac)], axis=-1)
    sin_a = jnp.concatenate([-jnp.sin(ar), zero, -jnp.sin(ac), zero], axis=-1)
    sin_b = jnp.concatenate([zero, jnp.sin(ar), zero, jnp.sin(ac)], axis=-1)
    hi = LANES - lane0 - dim
    cos = jnp.pad(cos, ((0, 0), (lane0, hi)), constant_values=1.0)
    return jnp.stack([cos, _pad_last(sin_a, lane0, hi), _pad_last(sin_b, lane0, hi)])


def _prepare_weights(w_in, w_out, norm_g, mla_q_norm_g, mla_w_uq, mla_kv_norm_g, mla_w_ukv, mla_q_head_g,
                     mla_k_head_g, gqa_q_g, gqa_k_g, ssd_conv_w, ssd_conv_b, ssd_dt_bias, ssd_a_log, ssd_d,
                     ssd_norm_g, cf_conv_w, cf_conv_b, cf_ln_g, cf_ln_b):
    depth = w_in.shape[0]
    pieces = []
    o = 0
    for sz in SPLITS:
        pieces.append(w_in[:, :, o:o + sz])
        o += sz
    (a_q, a_kv, a_r, a_g, b_xbc, b_dt, b_z, c_v, c_gl, c_g, d_q, d_k, d_v, d_g) = pieces
    win = jnp.concatenate([
        _pad_last(a_q, 0, 256 - MLA_Q_RANK), a_kv, _pad_last(a_r, MLA_NOPE, LANES - MLA_QK), a_g,
        b_xbc, _pad_last(jnp.concatenate([b_dt, b_dt], axis=-1), 0, LANES - 2 * SSD_NK), b_z, c_v, c_gl, c_g,
        _pad_heads(d_q, GQA_HEADS, GQA_HD, LANES), _pad_heads(d_k, GQA_KV_HEADS, GQA_HD, LANES),
        _pad_heads(d_v, GQA_KV_HEADS, GQA_HD, LANES), d_g], axis=-1).astype(BF16)
    wuq = jnp.pad(_pad_heads(mla_w_uq, MLA_HEADS, MLA_QK, LANES),
                  ((0, 0), (0, 256 - MLA_Q_RANK), (0, 0))).astype(BF16)
    kv = mla_w_ukv.reshape(depth, MLA_KV_RANK, MLA_HEADS, MLA_NOPE + MLA_V)
    wk = _pad_last(kv[..., :MLA_NOPE], 0, LANES - MLA_NOPE).reshape(depth, MLA_KV_RANK, -1)
    wv = _pad_last(kv[..., MLA_NOPE:], 0, LANES - MLA_V).reshape(depth, MLA_KV_RANK, -1)
    wukv = jnp.concatenate([wk, wv], axis=-1).astype(BF16)
    rows_t = lambda wt: jnp.pad(wt, ((0, 0), (0, 0), (0, ATT_VROWS - wt.shape[2]), (0, 0)))
    wvta = rows_t(kv[..., MLA_NOPE:].transpose(0, 2, 3, 1)).reshape(depth, MLA_HEADS * ATT_VROWS, MLA_KV_RANK)
    wvtd = rows_t(d_v.reshape(depth, -1, GQA_KV_HEADS, GQA_HD).transpose(0, 2, 3, 1))
    wvtd = wvtd.reshape(depth, GQA_KV_HEADS * ATT_VROWS, -1)
    gw = MLA_HEADS * LANES
    row = lambda v: _pad_last(v, 0, gw - v.shape[-1])
    slots = lambda v, n: row(jnp.tile(_pad_last(v, 0, LANES - v.shape[-1]), (1, n)))
    zero = jnp.zeros((depth, gw), F32)
    gains = jnp.stack([row(mla_q_norm_g),
                       slots(mla_q_head_g * (MLA_QK ** -0.5 * LOG2E), MLA_HEADS),
                       row(mla_kv_norm_g),
                       slots(mla_k_head_g, MLA_HEADS),
                       slots(gqa_q_g * (GQA_HD ** -0.5 * LOG2E), GQA_HEADS),
                       slots(gqa_k_g, GQA_KV_HEADS), zero, zero], axis=1)
    vec = lambda v: v[:, None, :]
    lanes8 = lambda v: _pad_last(jnp.tile(v.reshape(depth, 1, -1), (1, 1, 2)), 0, LANES - 2 * SSD_NK)
    return dict(win=win, wuq=wuq, wukv=wukv, wvta=wvta.astype(BF16), wvtd=wvtd.astype(BF16), gains=gains,
                wout=w_out.astype(BF16), ng=vec(norm_g),
                w3=ssd_conv_w, b3=vec(ssd_conv_b), w31=cf_conv_w, b31=vec(cf_conv_b),
                lng=vec(cf_ln_g), lnb=vec(cf_ln_b), dt_bias=lanes8(ssd_dt_bias), a_log=lanes8(ssd_a_log),
                dskip=vec(jnp.repeat(ssd_d, SSD_P, axis=-1)), ssd_ng=vec(ssd_norm_g))


def _mixer_layer(x, mod, mod_row, l, w, caches, ropes, tm):
    latent = caches is not None
    b, s, d = x.shape
    flat = (lambda a: a) if latent else (lambda a: a.reshape(1, b * s, a.shape[-1]))
    unflat = (lambda a: a) if latent else (lambda a: a.reshape(b, s, a.shape[-1]))
    outs = [unflat(o) for o in _inproj(flat(x), mod, mod_row, l, w, ropes, tm)]
    qa, ka, va, ag, qd, kd, vd, dg, xbc, dt, z, glu, cg = outs[:13]
    ctx_a = ctx_d = s0 = None
    if latent:
        kc, vc, kdc, vdc = _ctx_kv(caches['ckv'], caches['kr_pad'], caches['gk'], caches['gv'], l, w)
        ctx_a, ctx_d, s0 = (kc, vc), (kdc, vdc), caches['state']
    tq = min(ATT_TQ, s)
    sub = min(ATT_SUB, tq)
    rep = GQA_HEADS // GQA_KV_HEADS
    if latent:
        oa = _attention_t(qa, ka, va, ctx_a, ag, MLA_HEADS, 1, tq, sub, "attn_mla")
        od = _attention_t(qd, kd, vd, ctx_d, dg, GQA_HEADS, rep, tq, sub, "attn_gqa")
    else:
        oa = _attention(qa, ka, va, None, ag, MLA_HEADS, 1, 1, tq, sub, "attn_mla_ctx")
        od = _attention(qd, kd, vd, None, dg, GQA_HEADS, rep, 1, tq, sub, "attn_gqa_ctx")
    xconv, oc = _convs(xbc, glu, cg, l, w, min(256, x.shape[1]))
    yf, yb, s_fin = _ssd(xconv, dt, s0, l, w)
    y = unflat(_outproj(flat(oa), flat(yf), flat(yb), flat(z), flat(oc), flat(od), flat(x), mod, mod_row, l, w, tm))
    new_ctx = None
    if not latent:
        ckv, kr, gk, gv = outs[13:]
        new_ctx = (ckv, kr, gk.reshape(b, -1, GQA_KV_HEADS, GQA_HD), gv.reshape(b, -1, GQA_KV_HEADS, GQA_HD),
                   s_fin.reshape(b, 2, SSD_HEADS, SSD_P, SSD_N))
    return y, new_ctx


def kernel(x_prompt, x_sample, cache_mla_ckv, cache_mla_krope, cache_gqa_k, cache_gqa_v, state_ssd, c, c_ctx, w_mod, b_mod, norm_g, w_in, w_out, mla_q_norm_g, mla_w_uq, mla_kv_norm_g, mla_w_ukv, mla_q_head_g, mla_k_head_g, ssd_conv_w, ssd_conv_b, ssd_dt_bias, ssd_a_log, ssd_d, ssd_norm_g, cf_conv_w, cf_conv_b, cf_ln_g, cf_ln_b, gqa_q_g, gqa_k_g):
    depth = w_in.shape[0]
    dec_b, n_lat = x_sample.shape[0], x_sample.shape[1]
    seq = x_prompt.shape[1]
    past = cache_mla_ckv.shape[2]
    assert dec_b < 8 and x_prompt.shape[-1] == D_MODEL and w_in.shape[-1] == sum(SPLITS)

    cvec = jnp.concatenate([c, c_ctx[None, :], jnp.zeros((8 - dec_b - 1, D_MODEL), F32)], axis=0)
    mod = _modulation(cvec, w_mod, b_mod).reshape(depth, 8, 3, D_MODEL)
    w = _prepare_weights(w_in, w_out, norm_g, mla_q_norm_g, mla_w_uq, mla_kv_norm_g, mla_w_ukv, mla_q_head_g,
                         mla_k_head_g, gqa_q_g, gqa_k_g, ssd_conv_w, ssd_conv_b, ssd_dt_bias, ssd_a_log, ssd_d,
                         ssd_norm_g, cf_conv_w, cf_conv_b, cf_ln_g, cf_ln_b)

    y = x_prompt
    ctx_out = []
    for l in range(depth):
        y, new_ctx = _mixer_layer(y, mod, dec_b, l, w, None, None, 512)
        ctx_out.append(new_ctx)
    new = [jnp.stack([t[i] for t in ctx_out], axis=1) for i in range(5)]

    caches = dict(ckv=cache_mla_ckv,
                  kr_pad=_pad_last(cache_mla_krope, MLA_NOPE, LANES - MLA_QK),
                  gk=cache_gqa_k.reshape(dec_b, depth, past, GQA_KV_HEADS * GQA_HD),
                  gv=cache_gqa_v.reshape(dec_b, depth, past, GQA_KV_HEADS * GQA_HD),
                  state=state_ssd.reshape(dec_b, depth, 2, SSD_GROUPS, 2 * SSD_P, SSD_N))
    ropes = (_rope_table(n_lat, MLA_ROPE, MLA_NOPE), _rope_table(n_lat, GQA_HD, 0))
    z = x_sample
    for l in range(depth):
        z, _ = _mixer_layer(z, mod, None, l, w, caches, ropes, 512)

    return (y, z, new[0], new[1], new[2], new[3], new[4])
```

```python
import functools
import math

import jax
import jax.numpy as jnp
from jax import lax
from jax.experimental import pallas as pl
from jax.experimental.pallas import tpu as pltpu

F32 = jnp.float32
BF16 = jnp.bfloat16

LANES = 128
VMEM_CAP = 56 * 1024 * 1024

EPS = 1e-6
ROPE_THETA = 10000.0
GRID_W = 64
LOG2E = math.log2(math.e)

D_MODEL = 1024
BRANCH_W = 256
MLA_HEADS, MLA_NOPE, MLA_ROPE, MLA_V = 4, 64, 32, 64
MLA_QK = MLA_NOPE + MLA_ROPE
MLA_Q_RANK, MLA_KV_RANK = 192, 128
SSD_HEADS, SSD_P, SSD_N, SSD_GROUPS, SSD_CONV, SSD_CHUNK = 4, 64, 64, 2, 3, 128
SSD_W = SSD_HEADS * SSD_P
SSD_CONV_CH = SSD_W + 2 * SSD_GROUPS * SSD_N
CF_W, CF_K = 256, 31
GQA_HEADS, GQA_KV_HEADS, GQA_HD = 4, 2, 64
SPLITS = (MLA_Q_RANK, MLA_KV_RANK, MLA_ROPE, MLA_HEADS * MLA_V,
          SSD_CONV_CH, 2 * SSD_HEADS, SSD_W,
          CF_W, CF_W, CF_W,
          GQA_HEADS * GQA_HD, GQA_KV_HEADS * GQA_HD, GQA_KV_HEADS * GQA_HD, GQA_HEADS * GQA_HD)

O_QL, O_KV, O_R, O_AG = 0, 256, 384, 512
O_XBC, O_DT, O_Z = 768, 1280, 1408
O_CV, O_CGL, O_CG = 1664, 1920, 2176
O_DQ, O_DK, O_DV, O_DG = 2432, 2944, 3200, 3456
IN_WP = 3712

CONV_HALO_SSD = 8
CONV_HALO_CF = 16
CONV_TL = 512

SSD_UNROLL = 4
INPROJ_SUB = 256
ATT_TQ = 1024
ATT_VROWS = 80
ATT_SUB = 512
ATT_QK_CHUNK = 512
ATT_PV_CHUNK = 256

NT_DIMS = (((1,), (1,)), ((), ()))


def _silu(x):
    return x / (1.0 + jnp.exp(-x))


def _sigmoid(x):
    return 1.0 / (1.0 + jnp.exp(-x))


def _softplus(x):
    return jnp.maximum(x, 0.0) + jnp.log1p(jnp.exp(-jnp.abs(x)))


def _rms(x, n):
    return x * lax.rsqrt(jnp.sum(x * x, axis=-1, keepdims=True) * (1.0 / n) + EPS)


def _dot(a, b):
    return jnp.dot(a, b, preferred_element_type=F32)


def _dot_nt(a, b):
    return lax.dot_general(a, b, NT_DIMS, preferred_element_type=F32)


def _params(sem, vmem_bytes):
    assert vmem_bytes <= VMEM_CAP
    return pltpu.CompilerParams(dimension_semantics=sem, vmem_limit_bytes=VMEM_CAP)


def _pad_heads(w, nh, d, dp):
    s = w.shape[:-1]
    w = w.reshape(s + (nh, d))
    w = jnp.pad(w, [(0, 0)] * len(s) + [(0, 0), (0, dp - d)])
    return w.reshape(s + (nh * dp,))


def _pad_last(w, lo, hi):
    return jnp.pad(w, [(0, 0)] * (w.ndim - 1) + [(lo, hi)])


def _layer_spec(a, l, grid_rank):
    zeros = (0,) * (a.ndim - 1)
    if grid_rank == 1:
        return pl.BlockSpec((1,) + a.shape[1:], lambda bi: (l,) + zeros)
    return pl.BlockSpec((1,) + a.shape[1:], lambda bi, i: (l,) + zeros)


def _with_ones_lane(v, hd):
    lane = lax.broadcasted_iota(jnp.int32, v.shape, v.ndim - 1)
    return jnp.where(lane % LANES == hd, 1.0, v)


def _mod_kernel(c_ref, w_ref, b_ref, o_ref):
    c = c_ref[...]
    o_ref[0] = _dot(_silu(c).astype(BF16), w_ref[0].astype(BF16)) + b_ref[0]


def _modulation(cvec, w_mod, b_mod):
    depth, d, d3 = w_mod.shape
    tn = d
    return pl.pallas_call(
        _mod_kernel,
        out_shape=jax.ShapeDtypeStruct((depth, 8, d3), F32),
        grid=(depth, d3 // tn),
        in_specs=[pl.BlockSpec((8, d), lambda l, j: (0, 0)),
                  pl.BlockSpec((1, d, tn), lambda l, j: (l, 0, j)),
                  pl.BlockSpec((1, 1, tn), lambda l, j: (l, 0, j))],
        out_specs=pl.BlockSpec((1, 8, tn), lambda l, j: (l, 0, j)),
        compiler_params=_params(("arbitrary", "arbitrary"), 4 * d * tn * 4),
        name="modulation",
    )(cvec, w_mod, b_mod.reshape(depth, 1, d3))


def _rope(x, tab_ref, rows, shift):
    cos, sin_a, sin_b = tab_ref[0, rows, :], tab_ref[1, rows, :], tab_ref[2, rows, :]
    outs = []
    for h in range(x.shape[-1] // LANES):
        xh = x[:, h * LANES:(h + 1) * LANES]
        up = pltpu.roll(xh, LANES - shift, 1)
        dn = pltpu.roll(xh, shift, 1)
        outs.append(xh * cos + up * sin_a + dn * sin_b)
    return jnp.concatenate(outs, axis=-1)


def _head_norm(x, gain, n, ones_bd):
    xx = (x * x).astype(BF16)
    w2 = 2 * LANES
    ss = jnp.concatenate([_dot(xx[:, j:j + w2], ones_bd) for j in range(0, x.shape[-1], w2)], axis=-1)
    return x * lax.rsqrt(ss * (1.0 / n) + EPS) * gain


def _inproj_refs(latent, refs):
    names = ['x', 'mod', 'ng', 'win', 'wuq', 'wukv', 'gains']
    if latent:
        names += ['wvta', 'wvtd', 'ra', 'rd']
    names += ['qa', 'ka', 'va', 'ag', 'qd', 'kd', 'vd', 'dg', 'xbc', 'dt', 'z', 'glu', 'cg']
    if not latent:
        names += ['ckv', 'kr', 'gk', 'gv']
    return dict(zip(names, refs))


def _inproj_main(r, rows):
    x = r['x'][0, rows, :]
    shift = r['mod'][0, 0, 0:1, :]
    scale = r['mod'][0, 0, 1:2, :]
    h = (_rms(x, D_MODEL) * r['ng'][0] * (1.0 + scale) + shift).astype(BF16)
    return h, _dot(h, r['win'][0])


def _values_t(w_t, x):
    vt = _dot_nt(w_t, x)
    row = lax.broadcasted_iota(jnp.int32, vt.shape, 0)
    return jnp.where(row % ATT_VROWS == MLA_V, 1.0, vt)


def _inproj_latents(r, u):
    g_ql = r['gains'][0, 0:1, 0:256]
    g_kv = r['gains'][0, 2:3, 0:LANES]
    ql = _rms(u[:, O_QL:O_QL + 256], MLA_Q_RANK) * g_ql
    ckv = _rms(u[:, O_KV:O_KV + LANES], MLA_KV_RANK) * g_kv
    return _dot(ql.astype(BF16), r['wuq'][0]), ckv, _dot(ckv.astype(BF16), r['wukv'][0])


def _inproj_tail(latent, r, rows, h, u, q, ckv, kv):
    wa, wd = MLA_HEADS * LANES, GQA_KV_HEADS * LANES
    g_qh = r['gains'][0, 1:2, :]
    g_kh = r['gains'][0, 3:4, :]
    g_dq = r['gains'][0, 4:5, :]
    g_dk = r['gains'][0, 5:6, 0:wd]
    ri = lax.broadcasted_iota(jnp.int32, (2 * LANES, 2 * LANES), 0) // LANES
    ci = lax.broadcasted_iota(jnp.int32, (2 * LANES, 2 * LANES), 1) // LANES
    ones_bd = (ri == ci).astype(F32).astype(BF16)

    q = _head_norm(q, g_qh, MLA_QK, ones_bd)
    kr = u[:, O_R:O_R + LANES]
    k = _head_norm(kv[:, 0:wa] + jnp.concatenate([kr] * MLA_HEADS, axis=-1), g_kh, MLA_QK, ones_bd)
    qd = _head_norm(u[:, O_DQ:O_DQ + GQA_HEADS * LANES], g_dq, GQA_HD, ones_bd)
    kd = _head_norm(u[:, O_DK:O_DK + wd], g_dk, GQA_HD, ones_bd)
    vd = u[:, O_DV:O_DV + wd]
    if latent:
        q = _rope(q, r['ra'], rows, MLA_ROPE // 4)
        k = _rope(k, r['ra'], rows, MLA_ROPE // 4)
        qd = _rope(qd, r['rd'], rows, GQA_HD // 4)
        kd_att = _rope(kd, r['rd'], rows, GQA_HD // 4)
    else:
        kd_att = kd
        r['ckv'][0, rows, :] = ckv
        r['kr'][0, rows, :] = kr[:, MLA_NOPE:MLA_NOPE + MLA_ROPE]
        for i in range(GQA_KV_HEADS):
            r['gk'][0, rows, i * GQA_HD:(i + 1) * GQA_HD] = kd[:, i * LANES:i * LANES + GQA_HD]
            r['gv'][0, rows, i * GQA_HD:(i + 1) * GQA_HD] = vd[:, i * LANES:i * LANES + GQA_HD]
    r['qa'][0, rows, :] = q.astype(BF16)
    r['ka'][0, rows, :] = k.astype(BF16)
    if latent:
        r['va'][0, :, rows] = _values_t(r['wvta'][0], ckv.astype(BF16)).astype(BF16)
        r['vd'][0, :, rows] = _values_t(r['wvtd'][0], h).astype(BF16)
    else:
        r['va'][0, rows, :] = _with_ones_lane(kv[:, wa:], MLA_V).astype(BF16)
        r['vd'][0, rows, :] = _with_ones_lane(vd, GQA_HD).astype(BF16)
    r['ag'][0, rows, :] = u[:, O_AG:O_AG + BRANCH_W]
    r['qd'][0, rows, :] = qd.astype(BF16)
    r['kd'][0, rows, :] = kd_att.astype(BF16)
    r['dg'][0, rows, :] = u[:, O_DG:O_DG + BRANCH_W]

    r['xbc'][0, rows, :] = u[:, O_XBC:O_XBC + SSD_CONV_CH]
    r['dt'][0, rows, :] = u[:, O_DT:O_DT + LANES]
    r['z'][0, rows, :] = u[:, O_Z:O_Z + SSD_W]
    r['glu'][0, rows, :] = u[:, O_CV:O_CV + CF_W] * _sigmoid(u[:, O_CGL:O_CGL + CF_W])
    r['cg'][0, rows, :] = u[:, O_CG:O_CG + CF_W]


def _inproj_kernel(latent, sub, *refs):
    r = _inproj_refs(latent, refs)
    tm = r['x'].shape[1]
    for r0 in range(0, tm, sub):
        rows = slice(r0, r0 + sub)
        h, u = _inproj_main(r, rows)
        q, ckv, kv = _inproj_latents(r, u)
        _inproj_tail(latent, r, rows, h, u, q, ckv, kv)


def _mod_spec(mod, l, row):
    d = mod.shape[-1]
    if row is None:
        return pl.BlockSpec((1, 1, 3, d), lambda bi, i: (l, bi, 0, 0))
    return pl.BlockSpec((1, 1, 3, d), lambda bi, i: (l, row, 0, 0))


def _inproj(x, mod, mod_row, l, w, ropes, tm):
    b, s, d = x.shape
    latent = ropes is not None
    tok = lambda wd: pl.BlockSpec((1, tm, wd), lambda bi, i: (bi, i, 0))
    weights = [w['ng'], w['win'], w['wuq'], w['wukv'], w['gains']]
    if latent:
        weights += [w['wvta'], w['wvtd']]
    in_specs = [tok(d), _mod_spec(mod, l, mod_row)] + [_layer_spec(a, l, 2) for a in weights]
    args = [x, mod] + weights
    if latent:
        in_specs += [pl.BlockSpec((3, tm, LANES), lambda bi, i: (0, i, 0))] * 2
        args += list(ropes)
    widths = [(4 * LANES, BF16), (4 * LANES, BF16), (4 * LANES, BF16), (BRANCH_W, F32),
              (4 * LANES, BF16), (2 * LANES, BF16), (2 * LANES, BF16), (BRANCH_W, F32),
              (SSD_CONV_CH, F32), (LANES, F32), (SSD_W, F32), (CF_W, F32), (CF_W, F32)]
    if not latent:
        widths += [(MLA_KV_RANK, F32), (MLA_ROPE, F32), (LANES, F32), (LANES, F32)]
    out_shape = [jax.ShapeDtypeStruct((b, s, wd), dt) for wd, dt in widths]
    out_specs = [tok(wd) for wd, _ in widths]
    if latent:
        for pos, heads in ((2, MLA_HEADS), (6, GQA_KV_HEADS)):
            out_shape[pos] = jax.ShapeDtypeStruct((b, heads * ATT_VROWS, s), BF16)
            out_specs[pos] = pl.BlockSpec((1, heads * ATT_VROWS, tm), lambda bi, i: (bi, 0, i))
    out_bytes = sum(wd * jnp.dtype(dt).itemsize for wd, dt in widths) * tm
    w_bytes = sum(a[0].size * a.dtype.itemsize for a in weights)
    sub = min(INPROJ_SUB, tm)
    vmem = 2 * (tm * d * 4 + w_bytes + out_bytes) + 3 * sub * IN_WP * 4
    return pl.pallas_call(
        functools.partial(_inproj_kernel, latent, sub),
        out_shape=out_shape,
        grid=(b, s // tm),
        in_specs=in_specs,
        out_specs=out_specs,
        compiler_params=_params(("parallel", "parallel"), vmem),
        name="inproj_lat" if latent else "inproj_ctx",
    )(*args)


def _ctxkv_kernel(ckv_ref, kr_ref, gk_ref, gv_ref, wukv_ref, wvta_ref, gains_ref, kc_ref, vc_ref, kdc_ref, vdc_ref):
    ckv = ckv_ref[0, 0].astype(BF16)
    kv = _dot(ckv, wukv_ref[0])
    kr = kr_ref[0, 0]
    g = gains_ref[0, 3:4, 0:LANES]
    for i in range(MLA_HEADS):
        k = _rms(kv[:, i * LANES:(i + 1) * LANES] + kr, MLA_QK) * g
        kc_ref[0, :, i * LANES:(i + 1) * LANES] = k.astype(BF16)
    vc_ref[0] = _values_t(wvta_ref[0], ckv).astype(BF16)
    gk = gk_ref[0, 0]
    p = gk.shape[0]
    pad = jnp.zeros((p, LANES - GQA_HD), BF16)
    for i in range(GQA_KV_HEADS):
        lo = i * LANES
        kdc_ref[0, :, lo:lo + GQA_HD] = gk[:, i * GQA_HD:(i + 1) * GQA_HD].astype(BF16)
        kdc_ref[0, :, lo + GQA_HD:lo + LANES] = pad
    gvt = gv_ref[0, 0].T
    row = lax.broadcasted_iota(jnp.int32, (ATT_VROWS - GQA_HD, p), 0)
    tail = jnp.where(row == 0, 1.0, 0.0)
    pieces = []
    for i in range(GQA_KV_HEADS):
        pieces += [gvt[i * GQA_HD:(i + 1) * GQA_HD, :], tail]
    vdc_ref[0] = jnp.concatenate(pieces, axis=0).astype(BF16)


def _ctx_kv(ckv, kr_pad, gk, gv, l, w):
    b, _, p, _ = ckv.shape
    cache = lambda a: pl.BlockSpec((1, 1, p, a.shape[-1]), lambda bi: (bi, l, 0, 0))
    out = lambda wd: pl.BlockSpec((1, p, wd), lambda bi: (bi, 0, 0))
    out_t = lambda rows: pl.BlockSpec((1, rows, p), lambda bi: (bi, 0, 0))
    wa, wd = MLA_HEADS * LANES, GQA_KV_HEADS * LANES
    ra, rd = MLA_HEADS * ATT_VROWS, GQA_KV_HEADS * ATT_VROWS
    return pl.pallas_call(
        _ctxkv_kernel,
        out_shape=[jax.ShapeDtypeStruct((b, p, wa), BF16), jax.ShapeDtypeStruct((b, ra, p), BF16),
                   jax.ShapeDtypeStruct((b, p, wd), BF16), jax.ShapeDtypeStruct((b, rd, p), BF16)],
        grid=(b,),
        in_specs=[cache(ckv), cache(kr_pad), cache(gk), cache(gv),
                  _layer_spec(w['wukv'], l, 1), _layer_spec(w['wvta'], l, 1), _layer_spec(w['gains'], l, 1)],
        out_specs=[out(wa), out_t(ra), out(wd), out_t(rd)],
        compiler_params=_params(("parallel",), 16 * p * wa * 4),
        name="ctx_kv",
    )(ckv, kr_pad, gk, gv, w['wukv'], w['wvta'], w['gains'])


def _lane_tile_max(t):
    r = t[:, 0:LANES]
    for j in range(1, t.shape[-1] // LANES):
        r = jnp.maximum(r, t[:, j * LANES:(j + 1) * LANES])
    return r


def _attn_unit(n_heads, rep, has_ctx, refs, bi, rows, h, s_ref):
    if has_ctx:
        q_ref, k_ref, v_ref, kc_ref, vc_ref, g_ref, o_ref = refs
        p_len = kc_ref.shape[1]
    else:
        q_ref, k_ref, v_ref, g_ref, o_ref = refs
        p_len = 0
    s_len = k_ref.shape[1]
    hd = o_ref.shape[-1] // n_heads
    kb = min(ATT_QK_CHUNK, s_len)
    kb2 = ATT_PV_CHUNK
    g = h // rep
    hs = slice(h * LANES, (h + 1) * LANES)
    gs = slice(g * LANES, (g + 1) * LANES)
    qh = q_ref[bi, rows, hs]
    mx = None
    for c in range(s_len // kb):
        sc = _dot_nt(qh, k_ref[bi, c * kb:(c + 1) * kb, gs])
        s_ref[:, c * kb:(c + 1) * kb] = sc
        t = _lane_tile_max(sc)
        mx = t if mx is None else jnp.maximum(mx, t)
    if has_ctx:
        sc = _dot_nt(qh, kc_ref[bi, :, gs])
        s_ref[:, s_len:s_len + p_len] = sc
        mx = jnp.maximum(mx, _lane_tile_max(sc))
    m = jnp.max(mx, axis=-1, keepdims=True)
    acc = None
    for c in range((s_len + p_len) // kb2):
        r0 = c * kb2
        p = jnp.exp2(s_ref[:, r0:r0 + kb2] - m).astype(BF16)
        vv = v_ref[bi, r0:r0 + kb2, gs] if r0 < s_len else vc_ref[bi, r0 - s_len:r0 - s_len + kb2, gs]
        part = _dot(p, vv)
        acc = part if acc is None else acc + part
    o = acc[:, 0:hd] / acc[:, hd:hd + 1]
    os_ = slice(h * hd, (h + 1) * hd)
    o_ref[bi, rows, os_] = (o * _silu(g_ref[bi, rows, os_])).astype(o_ref.dtype)


def _attn_kernel(n_heads, rep, has_ctx, sub, *refs):
    s0_ref, s1_ref = refs[-2:]
    bb, tq = refs[0].shape[0], refs[0].shape[1]
    unit = 0
    for bi in range(bb):
        for r0 in range(0, tq, sub):
            for h in range(n_heads):
                _attn_unit(n_heads, rep, has_ctx, refs[:-2], bi, slice(r0, r0 + sub), h,
                           s0_ref if unit % 2 == 0 else s1_ref)
                unit += 1


def _attention(q, k, v, ctx, gate, n_heads, rep, bb, tq, sub, name):
    b, s, qw = q.shape
    kw = k.shape[-1]
    ow = gate.shape[-1]
    has_ctx = ctx is not None
    in_specs = [pl.BlockSpec((bb, tq, qw), lambda bi, i: (bi, i, 0)),
                pl.BlockSpec((bb, s, kw), lambda bi, i: (bi, 0, 0)),
                pl.BlockSpec((bb, s, kw), lambda bi, i: (bi, 0, 0))]
    args = [q, k, v]
    p = 0
    if has_ctx:
        p = ctx[0].shape[1]
        in_specs += [pl.BlockSpec((bb, p, kw), lambda bi, i: (bi, 0, 0))] * 2
        args += list(ctx)
    in_specs.append(pl.BlockSpec((bb, tq, ow), lambda bi, i: (bi, i, 0)))
    args.append(gate)
    assert s % min(ATT_QK_CHUNK, s) == 0 and s % ATT_PV_CHUNK == 0 and p % ATT_PV_CHUNK == 0
    assert b % bb == 0 and s % tq == 0 and tq % sub == 0
    vmem = (2 * bb * (tq * qw * 2 + 2 * (s + p) * kw * 2 + 2 * tq * ow * 4)
            + 2 * sub * (s + p) * 4 + 2 * sub * 1024 * 4)
    return pl.pallas_call(
        functools.partial(_attn_kernel, n_heads, rep, has_ctx, sub),
        out_shape=jax.ShapeDtypeStruct((b, s, ow), BF16),
        grid=(b // bb, s // tq),
        in_specs=in_specs,
        out_specs=pl.BlockSpec((bb, tq, ow), lambda bi, i: (bi, i, 0)),
        scratch_shapes=[pltpu.VMEM((sub, s + p), F32), pltpu.VMEM((sub, s + p), F32)],
        compiler_params=_params(("parallel", "arbitrary"), vmem),
        name=name,
    )(*args)


def _attn_t_scores(io, rep, sub, unit, s_ref):
    q_ref, k_ref, _, kc_ref = io[:4]
    r0, h = unit
    s_len, p_len = k_ref.shape[1], kc_ref.shape[1]
    kb = min(ATT_QK_CHUNK, s_len)
    gs = slice((h // rep) * LANES, (h // rep + 1) * LANES)
    qh = q_ref[0, r0:r0 + sub, h * LANES:(h + 1) * LANES]
    mx = None
    for c in range(s_len // kb):
        st = _dot_nt(k_ref[0, c * kb:(c + 1) * kb, gs], qh)
        s_ref[c * kb:(c + 1) * kb, :] = st
        t = jnp.max(st, axis=0, keepdims=True)
        mx = t if mx is None else jnp.maximum(mx, t)
    st = _dot_nt(kc_ref[0, :, gs], qh)
    s_ref[s_len:s_len + p_len, :] = st
    return jnp.maximum(mx, jnp.max(st, axis=0, keepdims=True))


def _attn_t_values(io, n_heads, rep, sub, unit, s_ref, m, ot_ref):
    _, k_ref, vt_ref, kc_ref, vct_ref, _, o_ref = io
    r0, h = unit
    s_len, p_len = k_ref.shape[1], kc_ref.shape[1]
    hd = o_ref.shape[-1] // n_heads
    vs = slice((h // rep) * ATT_VROWS, (h // rep + 1) * ATT_VROWS)
    kb2 = ATT_PV_CHUNK
    acc = None
    for c in range((s_len + p_len) // kb2):
        k0 = c * kb2
        pt = jnp.exp2(s_ref[k0:k0 + kb2, :] - m).astype(BF16)
        vv = vt_ref[0, vs, k0:k0 + kb2] if k0 < s_len else vct_ref[0, vs, k0 - s_len:k0 - s_len + kb2]
        part = _dot(vv, pt)
        acc = part if acc is None else acc + part
    ot_ref[h * hd:(h + 1) * hd, r0:r0 + sub] = acc[0:hd, :] / acc[hd:hd + 1, :]


def _attn_t_kernel(n_heads, rep, sub, *refs):
    io, (s0_ref, s1_ref, ot_ref) = refs[:-3], refs[-3:]
    tq = io[0].shape[1]
    units = [(r0, h) for r0 in range(0, tq, sub) for h in range(n_heads)]
    bufs = (s0_ref, s1_ref)
    m = _attn_t_scores(io, rep, sub, units[0], bufs[0])
    for u in range(len(units)):
        m_next = None
        if u + 1 < len(units):
            m_next = _attn_t_scores(io, rep, sub, units[u + 1], bufs[(u + 1) % 2])
        _attn_t_values(io, n_heads, rep, sub, units[u], bufs[u % 2], m, ot_ref)
        m = m_next
    g_ref, o_ref = io[5], io[6]
    o_ref[0] = (ot_ref[...].T * _silu(g_ref[0])).astype(o_ref.dtype)


def _attention_t(q, k, vt, ctx, gate, n_heads, rep, tq, sub, name):
    b, s, qw = q.shape
    kw = k.shape[-1]
    vr = vt.shape[1]
    ow = gate.shape[-1]
    kc, vct = ctx
    p = kc.shape[1]
    assert s % min(ATT_QK_CHUNK, s) == 0 and s % ATT_PV_CHUNK == 0 and p % ATT_PV_CHUNK == 0
    assert s % tq == 0 and tq % sub == 0
    tile = lambda wd: pl.BlockSpec((1, tq, wd), lambda bi, i: (bi, i, 0))
    whole = lambda a: pl.BlockSpec((1,) + a.shape[1:], lambda bi, i: (bi, 0, 0))
    vmem = (2 * (tq * qw * 2 + (s + p) * (kw + vr) * 2 + tq * ow * 6)
            + 2 * sub * (s + p) * 4 + ow * tq * 4 + 2 * sub * 1024 * 4)
    return pl.pallas_call(
        functools.partial(_attn_t_kernel, n_heads, rep, sub),
        out_shape=jax.ShapeDtypeStruct((b, s, ow), BF16),
        grid=(b, s // tq),
        in_specs=[tile(qw), whole(k), whole(vt), whole(kc), whole(vct), tile(ow)],
        out_specs=tile(ow),
        scratch_shapes=[pltpu.VMEM((s + p, sub), F32), pltpu.VMEM((s + p, sub), F32), pltpu.VMEM((ow, tq), F32)],
        compiler_params=_params(("parallel", "arbitrary"), vmem),
        name=name,
    )(q, k, vt, kc, vct, gate)


def _conv_kernel(tl, xc_ref, xp_ref, xn_ref, gc_ref, gp_ref, gn_ref, cg_ref,
                 w3_ref, b3_ref, w31_ref, b31_ref, lng_ref, lnb_ref,
                 xo_ref, co_ref, ext3, ext31):
    i = pl.program_id(1)
    keep_prev = (i > 0).astype(F32)
    keep_next = (i < pl.num_programs(1) - 1).astype(F32)

    h3 = CONV_HALO_SSD
    ext3[0:h3, :] = xp_ref[0] * keep_prev
    ext3[h3:h3 + tl, :] = xc_ref[0]
    ext3[h3 + tl:2 * h3 + tl, :] = xn_ref[0] * keep_next
    acc = jnp.zeros((tl, SSD_CONV_CH), F32) + b3_ref[0]
    for k in range(SSD_CONV):
        acc = acc + w3_ref[0, k:k + 1, :] * ext3[h3 - SSD_CONV // 2 + k:h3 - SSD_CONV // 2 + k + tl, :]
    xo_ref[0] = _silu(acc)

    h31 = CONV_HALO_CF
    ext31[0:h31, :] = gp_ref[0] * keep_prev
    ext31[h31:h31 + tl, :] = gc_ref[0]
    ext31[h31 + tl:2 * h31 + tl, :] = gn_ref[0] * keep_next
    acc = jnp.zeros((tl, CF_W), F32) + b31_ref[0]
    o_lo = h31 - CF_K // 2
    ext = ext31[...]
    n_ext = ext.shape[0]
    for r in range(8):
        taps = [k for k in range(CF_K) if (o_lo + k) % 8 == r]
        if not taps:
            continue
        shifted = ext if r == 0 else pltpu.roll(ext, n_ext - r, 0)
        for k in taps:
            a = (o_lo + k) // 8 * 8
            acc = acc + w31_ref[0, k:k + 1, :] * shifted[a:a + tl, :]
    mu = jnp.mean(acc, axis=-1, keepdims=True)
    cen = acc - mu
    var = jnp.mean(cen * cen, axis=-1, keepdims=True)
    y = cen * lax.rsqrt(var + EPS) * lng_ref[0] + lnb_ref[0]
    co_ref[0] = (_silu(y) * _silu(cg_ref[0])).astype(co_ref.dtype)


def _convs(xbc, glu, cg, l, w, tl):
    b, s, _ = xbc.shape
    nt = s // tl

    def cur(wd):
        return pl.BlockSpec((1, tl, wd), lambda bi, i: (bi, i, 0))

    def prev(wd, hrows):
        r = tl // hrows
        return pl.BlockSpec((1, hrows, wd), lambda bi, i: (bi, jnp.maximum(i * r - 1, 0), 0))

    def nxt(wd, hrows):
        r = tl // hrows
        return pl.BlockSpec((1, hrows, wd), lambda bi, i: (bi, jnp.minimum((i + 1) * r, nt * r - 1), 0))

    weights = [w['w3'], w['b3'], w['w31'], w['b31'], w['lng'], w['lnb']]
    vmem = 4 * tl * (SSD_CONV_CH * 2 + CF_W * 3) * 4 + 6 * tl * SSD_CONV_CH * 4
    return pl.pallas_call(
        functools.partial(_conv_kernel, tl),
        out_shape=[jax.ShapeDtypeStruct((b, s, SSD_CONV_CH), F32),
                   jax.ShapeDtypeStruct((b, s, CF_W), BF16)],
        grid=(b, nt),
        in_specs=[cur(SSD_CONV_CH), prev(SSD_CONV_CH, CONV_HALO_SSD), nxt(SSD_CONV_CH, CONV_HALO_SSD),
                  cur(CF_W), prev(CF_W, CONV_HALO_CF), nxt(CF_W, CONV_HALO_CF), cur(CF_W)]
                 + [_layer_spec(a, l, 2) for a in weights],
        out_specs=[cur(SSD_CONV_CH), cur(CF_W)],
        scratch_shapes=[pltpu.VMEM((tl + 2 * CONV_HALO_SSD, SSD_CONV_CH), F32),
                        pltpu.VMEM((tl + 2 * CONV_HALO_CF, CF_W), F32)],
        compiler_params=_params(("parallel", "parallel"), vmem),
        name="convs",
    )(xbc, xbc, xbc, glu, glu, glu, cg, *weights)


def _split3(x):
    hi = x.astype(BF16)
    r1 = x - hi.astype(F32)
    mid = r1.astype(BF16)
    lo = (r1 - mid.astype(F32)).astype(BF16)
    return hi, mid, lo


def _cumsum_rows(tri, x):
    return sum(_dot(tri, part) for part in _split3(x))


def _transpose_exact(x):
    c = x.shape[1]
    eye = (lax.broadcasted_iota(jnp.int32, (c, c), 0) == lax.broadcasted_iota(jnp.int32, (c, c), 1))
    eye = eye.astype(F32).astype(BF16)
    return sum(_dot_nt(eye, part) for part in _split3(x))


SSD_NK = 2 * SSD_HEADS


def _ssd_stage_local(c, d, xc_ref, dt_ref, bias, a_row):
    q = SSD_CHUNK
    rows = pl.ds(pl.multiple_of(c * q, q), q)
    t = dict(rows=rows, d=d)
    t['xs'] = xc_ref[0, rows, 0:SSD_W]
    t['bm'] = xc_ref[0, rows, SSD_W:SSD_W + LANES]
    t['cm'] = xc_ref[0, rows, SSD_W + LANES:SSD_W + 2 * LANES].astype(BF16)
    t['dt'] = _softplus(dt_ref[0, rows, :] + bias)
    ri = lax.broadcasted_iota(jnp.int32, (q, q), 0)
    ci = lax.broadcasted_iota(jnp.int32, (q, q), 1)
    t['mask'] = (ri >= ci) if d == 0 else (ri <= ci)
    t['acum'] = _cumsum_rows(t['mask'].astype(F32).astype(BF16), t['dt'] * a_row)
    return t


def _ssd_stage_cb(t):
    q = SSD_CHUNK
    lane = lax.broadcasted_iota(jnp.int32, (q, LANES), 1)
    t['tr'] = jnp.where(lane < SSD_NK, t['dt'], t['acum']).T
    bt = t['bm'].T
    t['bt'] = bt
    grp_row = lax.broadcasted_iota(jnp.int32, (LANES, q), 0) < SSD_N
    t['cb'] = [_dot(t['cm'], jnp.where(grp_row if g == 0 else jnp.logical_not(grp_row), bt, 0.0).astype(BF16))
               for g in range(SSD_GROUPS)]
    t['last'] = q - 1 if t['d'] == 0 else 0
    t['etot'] = jnp.exp(t['acum'][t['last']:t['last'] + 1, :])
    t['eac'] = jnp.exp(t['acum'])


def _ssd_stage_diag(t, dskip):
    q = SSD_CHUNK
    d = t['d']
    lo_half = lax.broadcasted_iota(jnp.int32, (q, LANES), 1) < SSD_P
    tr, acum, last = t['tr'], t['acum'], t['last']
    t['yd'], t['cs'] = [], []
    for g in range(SSD_GROUPS):
        k0 = d * SSD_HEADS + 2 * g
        xg = t['xs'][:, g * LANES:(g + 1) * LANES]
        xgb = xg.astype(BF16)
        btg = t['bt'][g * SSD_N:(g + 1) * SSD_N, :]
        yg = None
        cs = []
        for hh in range(2):
            k = k0 + hh
            dt_row = tr[k:k + 1, :]
            ac_row = tr[SSD_NK + k:SSD_NK + k + 1, :]
            dec = jnp.where(t['mask'], jnp.exp(acum[:, k:k + 1] - ac_row), 0.0)
            xm = jnp.where(lo_half if hh == 0 else jnp.logical_not(lo_half), xg, 0.0).astype(BF16)
            part = _dot((t['cb'][g] * dec * dt_row).astype(BF16), xm)
            yg = part if yg is None else yg + part
            w_row = dt_row * jnp.exp(tr[SSD_NK + k:SSD_NK + k + 1, last:last + 1] - ac_row)
            cs.append(_dot((btg * w_row).astype(BF16), xgb))
        if d == 0:
            yg = yg + dskip[:, g * LANES:(g + 1) * LANES] * xg
        t['yd'].append(yg)
        t['cs'].append(cs)


def _ssd_stage_state(t, states):
    q = SSD_CHUNK
    d = t['d']
    lo_half = lax.broadcasted_iota(jnp.int32, (q, LANES), 1) < SSD_P
    lane_n = lax.broadcasted_iota(jnp.int32, (SSD_N, LANES), 1) < SSD_P
    zero = jnp.zeros((SSD_N, LANES), F32)
    ys, new_states = [], []
    for g in range(SSD_GROUPS):
        k0 = d * SSD_HEADS + 2 * g
        s_prev = states[g]
        sz = jnp.concatenate([s_prev, zero] if g == 0 else [zero, s_prev], axis=0).astype(BF16)
        eacs = jnp.where(lo_half, t['eac'][:, k0:k0 + 1], t['eac'][:, k0 + 1:k0 + 2])
        ys.append(t['yd'][g] + _dot(t['cm'], sz) * eacs)
        cdec = jnp.where(lane_n, t['etot'][:, k0:k0 + 1], t['etot'][:, k0 + 1:k0 + 2])
        new_states.append(cdec * s_prev + jnp.where(lane_n, t['cs'][g][0], t['cs'][g][1]))
    return ys, new_states


def _ssd_kernel(nc, unroll, has_init, *refs):
    if has_init:
        xc_ref, dt_ref, s0_ref, bias_ref, alog_ref, d_ref, yf_ref, yb_ref, so_ref, st_ref = refs
        for d in range(2):
            for g in range(SSD_GROUPS):
                st_ref[0, d, g] = _transpose_exact(s0_ref[0, 0, d, g])
    else:
        xc_ref, dt_ref, bias_ref, alog_ref, d_ref, yf_ref, yb_ref, so_ref, st_ref = refs
        st_ref[...] = jnp.zeros(st_ref.shape, F32)
    lane = lax.broadcasted_iota(jnp.int32, (1, LANES), 1)
    a_row = jnp.where(lane < 2 * SSD_NK, -jnp.exp(alog_ref[0]), 0.0)
    bias = bias_ref[0]
    dskip = d_ref[0]

    def body(i, carry):
        states = [[st_ref[0, d, g] for g in range(SSD_GROUPS)] for d in range(2)]
        work = []
        for j in range(unroll):
            c = i * unroll + j
            work.append((_ssd_stage_local(c, 0, xc_ref, dt_ref, bias, a_row), yf_ref))
            work.append((_ssd_stage_local(nc - 1 - c, 1, xc_ref, dt_ref, bias, a_row), yb_ref))
        for t, _ in work:
            _ssd_stage_cb(t)
        for t, _ in work:
            _ssd_stage_diag(t, dskip)
        writes = []
        for t, y_ref in work:
            ys, states[t['d']] = _ssd_stage_state(t, states[t['d']])
            writes.append((y_ref, t['rows'], ys))
        for y_ref, rows, ys in writes:
            for g in range(SSD_GROUPS):
                y_ref[0, rows, g * LANES:(g + 1) * LANES] = ys[g]
        for d in range(2):
            for g in range(SSD_GROUPS):
                st_ref[0, d, g] = states[d][g]
        return carry

    lax.fori_loop(0, nc // unroll, body, 0)
    for d in range(2):
        for g in range(SSD_GROUPS):
            so_ref[0, d, g] = _transpose_exact(st_ref[0, d, g])


def _ssd(xc, dt, s0, l, w):
    b, s, _ = xc.shape
    nc = s // SSD_CHUNK
    unroll = min(SSD_UNROLL, nc)
    assert nc % unroll == 0
    st_shape = (b, 2, SSD_GROUPS, 2 * SSD_P, SSD_N)
    st_spec = pl.BlockSpec((1,) + st_shape[1:], lambda bi: (bi, 0, 0, 0, 0))
    y_spec = pl.BlockSpec((1, s, SSD_W), lambda bi: (bi, 0, 0))
    in_specs = [pl.BlockSpec((1, s, SSD_CONV_CH), lambda bi: (bi, 0, 0)),
                pl.BlockSpec((1, s, LANES), lambda bi: (bi, 0, 0))]
    args = [xc, dt]
    if s0 is not None:
        in_specs.append(pl.BlockSpec((1, 1) + st_shape[1:], lambda bi: (bi, l, 0, 0, 0, 0)))
        args.append(s0)
    weights = [w['dt_bias'], w['a_log'], w['dskip']]
    vmem = 2 * s * (SSD_CONV_CH + LANES + 2 * SSD_W) * 4 + 8 * 1024 * 1024
    return pl.pallas_call(
        functools.partial(_ssd_kernel, nc, unroll, s0 is not None),
        out_shape=[jax.ShapeDtypeStruct((b, s, SSD_W), F32), jax.ShapeDtypeStruct((b, s, SSD_W), F32),
                   jax.ShapeDtypeStruct(st_shape, F32)],
        grid=(b,),
        in_specs=in_specs + [_layer_spec(a, l, 1) for a in weights],
        out_specs=[y_spec, y_spec, st_spec],
        scratch_shapes=[pltpu.VMEM((1, 2, SSD_GROUPS, SSD_N, 2 * SSD_P), F32)],
        compiler_params=_params(("parallel",), vmem),
        name="ssd",
    )(*args, *weights)


def _outproj_kernel(oa_ref, yf_ref, yb_ref, z_ref, oc_ref, od_ref, x_ref, mod_ref, ng_ref, w_ref, o_ref):
    ob = _rms((yf_ref[0] + yb_ref[0]) * _silu(z_ref[0]), SSD_W) * ng_ref[0]
    acc = _dot(oa_ref[0], w_ref[0, 0:BRANCH_W, :])
    acc = acc + _dot(ob.astype(BF16), w_ref[0, BRANCH_W:2 * BRANCH_W, :])
    acc = acc + _dot(oc_ref[0], w_ref[0, 2 * BRANCH_W:3 * BRANCH_W, :])
    acc = acc + _dot(od_ref[0], w_ref[0, 3 * BRANCH_W:4 * BRANCH_W, :])
    o_ref[0] = x_ref[0] + mod_ref[0, 0, 2:3, :] * acc


def _outproj(oa, yf, yb, z, oc, od, x, mod, mod_row, l, w, tm):
    b, s, d = x.shape
    br = lambda: pl.BlockSpec((1, tm, BRANCH_W), lambda bi, i: (bi, i, 0))
    xs = pl.BlockSpec((1, tm, d), lambda bi, i: (bi, i, 0))
    vmem = 2 * (6 * tm * BRANCH_W * 4 + 2 * tm * d * 4 + d * d * 2) + 2 * tm * d * 4
    return pl.pallas_call(
        _outproj_kernel,
        out_shape=jax.ShapeDtypeStruct((b, s, d), F32),
        grid=(b, s // tm),
        in_specs=[br(), br(), br(), br(), br(), br(), xs, _mod_spec(mod, l, mod_row),
                  _layer_spec(w['ssd_ng'], l, 2), _layer_spec(w['wout'], l, 2)],
        out_specs=xs,
        compiler_params=_params(("parallel", "parallel"), vmem),
        name="outproj",
    )(oa, yf, yb, z, oc, od, x, mod, w['ssd_ng'], w['wout'])


def _rope_table(n_tokens, dim, lane0):
    n_rows = n_tokens // GRID_W
    quarter = dim // 4
    inv = ROPE_THETA ** (-jnp.arange(quarter, dtype=F32) / quarter)
    ar = jnp.arange(n_rows, dtype=F32)[:, None] * inv
    ac = jnp.arange(GRID_W, dtype=F32)[:, None] * inv
    zr, zc = jnp.zeros_like(ar), jnp.zeros_like(ac)
    hi = LANES - lane0 - dim

    def lanes(row_part, col_part):
        rp = _pad_last(jnp.concatenate(row_part, axis=-1), lane0, hi)
        cp = _pad_last(jnp.concatenate(col_part, axis=-1), lane0, hi)
        return (rp[:, None, :] + cp[None, :, :]).reshape(n_tokens, LANES)

    off_rotary = jnp.pad(jnp.zeros((1, dim), F32), ((0, 0), (lane0, hi)), constant_values=1.0)
    cos = lanes([jnp.cos(ar), jnp.cos(ar), zr, zr], [zc, zc, jnp.cos(ac), jnp.cos(ac)]) + off_rotary
    sin_a = lanes([-jnp.sin(ar), zr, zr, zr], [zc, zc, -jnp.sin(ac), zc])
    sin_b = lanes([zr, jnp.sin(ar), zr, zr], [zc, zc, zc, jnp.sin(ac)])
    return jnp.stack([cos, sin_a, sin_b])


def _prepare_weights(w_in, w_out, norm_g, mla_q_norm_g, mla_w_uq, mla_kv_norm_g, mla_w_ukv, mla_q_head_g,
                     mla_k_head_g, gqa_q_g, gqa_k_g, ssd_conv_w, ssd_conv_b, ssd_dt_bias, ssd_a_log, ssd_d,
                     ssd_norm_g, cf_conv_w, cf_conv_b, cf_ln_g, cf_ln_b):
    depth = w_in.shape[0]
    pieces = []
    o = 0
    for sz in SPLITS:
        pieces.append(w_in[:, :, o:o + sz])
        o += sz
    (a_q, a_kv, a_r, a_g, b_xbc, b_dt, b_z, c_v, c_gl, c_g, d_q, d_k, d_v, d_g) = pieces
    win = jnp.concatenate([
        _pad_last(a_q, 0, 256 - MLA_Q_RANK), a_kv, _pad_last(a_r, MLA_NOPE, LANES - MLA_QK), a_g,
        b_xbc, _pad_last(jnp.concatenate([b_dt, b_dt], axis=-1), 0, LANES - 2 * SSD_NK), b_z, c_v, c_gl, c_g,
        _pad_heads(d_q, GQA_HEADS, GQA_HD, LANES), _pad_heads(d_k, GQA_KV_HEADS, GQA_HD, LANES),
        _pad_heads(d_v, GQA_KV_HEADS, GQA_HD, LANES), d_g], axis=-1).astype(BF16)
    wuq = jnp.pad(_pad_heads(mla_w_uq, MLA_HEADS, MLA_QK, LANES),
                  ((0, 0), (0, 256 - MLA_Q_RANK), (0, 0))).astype(BF16)
    kv = mla_w_ukv.reshape(depth, MLA_KV_RANK, MLA_HEADS, MLA_NOPE + MLA_V)
    wk = _pad_last(kv[..., :MLA_NOPE], 0, LANES - MLA_NOPE).reshape(depth, MLA_KV_RANK, -1)
    wv = _pad_last(kv[..., MLA_NOPE:], 0, LANES - MLA_V).reshape(depth, MLA_KV_RANK, -1)
    wukv = jnp.concatenate([wk, wv], axis=-1).astype(BF16)
    rows_t = lambda wt: jnp.pad(wt, ((0, 0), (0, 0), (0, ATT_VROWS - wt.shape[2]), (0, 0)))
    wvta = rows_t(kv[..., MLA_NOPE:].transpose(0, 2, 3, 1)).reshape(depth, MLA_HEADS * ATT_VROWS, MLA_KV_RANK)
    wvtd = rows_t(d_v.reshape(depth, -1, GQA_KV_HEADS, GQA_HD).transpose(0, 2, 3, 1))
    wvtd = wvtd.reshape(depth, GQA_KV_HEADS * ATT_VROWS, -1)
    gw = MLA_HEADS * LANES
    row = lambda v: _pad_last(v, 0, gw - v.shape[-1])
    slots = lambda v, n: row(jnp.tile(_pad_last(v, 0, LANES - v.shape[-1]), (1, n)))
    zero = jnp.zeros((depth, gw), F32)
    gains = jnp.stack([row(mla_q_norm_g),
                       slots(mla_q_head_g * (MLA_QK ** -0.5 * LOG2E), MLA_HEADS),
                       row(mla_kv_norm_g),
                       slots(mla_k_head_g, MLA_HEADS),
                       slots(gqa_q_g * (GQA_HD ** -0.5 * LOG2E), GQA_HEADS),
                       slots(gqa_k_g, GQA_KV_HEADS), zero, zero], axis=1)
    vec = lambda v: v[:, None, :]
    lanes8 = lambda v: _pad_last(jnp.tile(v.reshape(depth, 1, -1), (1, 1, 2)), 0, LANES - 2 * SSD_NK)
    return dict(win=win, wuq=wuq, wukv=wukv, wvta=wvta.astype(BF16), wvtd=wvtd.astype(BF16), gains=gains,
                wout=w_out.astype(BF16), ng=vec(norm_g),
                w3=ssd_conv_w, b3=vec(ssd_conv_b), w31=cf_conv_w, b31=vec(cf_conv_b),
                lng=vec(cf_ln_g), lnb=vec(cf_ln_b), dt_bias=lanes8(ssd_dt_bias), a_log=lanes8(ssd_a_log),
                dskip=vec(jnp.repeat(ssd_d, SSD_P, axis=-1)), ssd_ng=vec(ssd_norm_g))


def _mixer_layer(x, mod, mod_row, l, w, caches, ropes, tm):
    latent = caches is not None
    b, s, d = x.shape
    flat = (lambda a: a) if latent else (lambda a: a.reshape(1, b * s, a.shape[-1]))
    unflat = (lambda a: a) if latent else (lambda a: a.reshape(b, s, a.shape[-1]))
    outs = [unflat(o) for o in _inproj(flat(x), mod, mod_row, l, w, ropes, tm)]
    qa, ka, va, ag, qd, kd, vd, dg, xbc, dt, z, glu, cg = outs[:13]
    ctx_a = ctx_d = s0 = None
    if latent:
        kc, vc, kdc, vdc = _ctx_kv(caches['ckv'], caches['kr_pad'], caches['gk'], caches['gv'], l, w)
        ctx_a, ctx_d, s0 = (kc, vc), (kdc, vdc), caches['state']
    tq = min(ATT_TQ, s)
    sub = min(ATT_SUB, tq)
    rep = GQA_HEADS // GQA_KV_HEADS
    if latent:
        oa = _attention_t(qa, ka, va, ctx_a, ag, MLA_HEADS, 1, tq, sub, "attn_mla")
        od = _attention_t(qd, kd, vd, ctx_d, dg, GQA_HEADS, rep, tq, sub, "attn_gqa")
    else:
        oa = _attention(qa, ka, va, None, ag, MLA_HEADS, 1, 1, tq, sub, "attn_mla_ctx")
        od = _attention(qd, kd, vd, None, dg, GQA_HEADS, rep, 1, tq, sub, "attn_gqa_ctx")
    xconv, oc = _convs(xbc, glu, cg, l, w, min(CONV_TL, s))
    yf, yb, s_fin = _ssd(xconv, dt, s0, l, w)
    y = unflat(_outproj(flat(oa), flat(yf), flat(yb), flat(z), flat(oc), flat(od), flat(x), mod, mod_row, l, w, tm))
    new_ctx = None
    if not latent:
        ckv, kr, gk, gv = outs[13:]
        new_ctx = (ckv, kr, gk.reshape(b, -1, GQA_KV_HEADS, GQA_HD), gv.reshape(b, -1, GQA_KV_HEADS, GQA_HD),
                   s_fin.reshape(b, 2, SSD_HEADS, SSD_P, SSD_N))
    return y, new_ctx


def kernel(x_prompt, x_sample, cache_mla_ckv, cache_mla_krope, cache_gqa_k, cache_gqa_v, state_ssd, c, c_ctx, w_mod, b_mod, norm_g, w_in, w_out, mla_q_norm_g, mla_w_uq, mla_kv_norm_g, mla_w_ukv, mla_q_head_g, mla_k_head_g, ssd_conv_w, ssd_conv_b, ssd_dt_bias, ssd_a_log, ssd_d, ssd_norm_g, cf_conv_w, cf_conv_b, cf_ln_g, cf_ln_b, gqa_q_g, gqa_k_g):
    depth = w_in.shape[0]
    dec_b, n_lat = x_sample.shape[0], x_sample.shape[1]
    seq = x_prompt.shape[1]
    past = cache_mla_ckv.shape[2]
    assert dec_b < 8 and x_prompt.shape[-1] == D_MODEL and w_in.shape[-1] == sum(SPLITS)

    cvec = jnp.concatenate([c, c_ctx[None, :], jnp.zeros((8 - dec_b - 1, D_MODEL), F32)], axis=0)
    mod = _modulation(cvec, w_mod, b_mod).reshape(depth, 8, 3, D_MODEL)
    w = _prepare_weights(w_in, w_out, norm_g, mla_q_norm_g, mla_w_uq, mla_kv_norm_g, mla_w_ukv, mla_q_head_g,
                         mla_k_head_g, gqa_q_g, gqa_k_g, ssd_conv_w, ssd_conv_b, ssd_dt_bias, ssd_a_log, ssd_d,
                         ssd_norm_g, cf_conv_w, cf_conv_b, cf_ln_g, cf_ln_b)

    y = x_prompt
    ctx_out = []
    for l in range(depth):
        y, new_ctx = _mixer_layer(y, mod, dec_b, l, w, None, None, 512)
        ctx_out.append(new_ctx)
    new = [jnp.stack([t[i] for t in ctx_out], axis=1) for i in range(5)]

    caches = dict(ckv=cache_mla_ckv,
                  kr_pad=_pad_last(cache_mla_krope, MLA_NOPE, LANES - MLA_QK),
                  gk=cache_gqa_k.reshape(dec_b, depth, past, GQA_KV_HEADS * GQA_HD),
                  gv=cache_gqa_v.reshape(dec_b, depth, past, GQA_KV_HEADS * GQA_HD),
                  state=state_ssd.reshape(dec_b, depth, 2, SSD_GROUPS, 2 * SSD_P, SSD_N))
    ropes = (_rope_table(n_lat, MLA_ROPE, MLA_NOPE), _rope_table(n_lat, GQA_HD, 0))
    z = x_sample
    for l in range(depth):
        z, _ = _mixer_layer(z, mod, None, l, w, caches, ropes, 512)

    return (y, z, new[0], new[1], new[2], new[3], new[4])
```

```python
import functools
import math

import jax
import jax.numpy as jnp
from jax import lax
from jax.experimental import pallas as pl
from jax.experimental.pallas import tpu as pltpu

F32 = jnp.float32
BF16 = jnp.bfloat16

LANES = 128
VMEM_CAP = 56 * 1024 * 1024

EPS = 1e-6
ROPE_THETA = 10000.0
GRID_W = 64
LOG2E = math.log2(math.e)

D_MODEL = 1024
BRANCH_W = 256
MLA_HEADS, MLA_NOPE, MLA_ROPE, MLA_V = 4, 64, 32, 64
MLA_QK = MLA_NOPE + MLA_ROPE
MLA_Q_RANK, MLA_KV_RANK = 192, 128
SSD_HEADS, SSD_P, SSD_N, SSD_GROUPS, SSD_CONV, SSD_CHUNK = 4, 64, 64, 2, 3, 128
SSD_W = SSD_HEADS * SSD_P
SSD_CONV_CH = SSD_W + 2 * SSD_GROUPS * SSD_N
CF_W, CF_K = 256, 31
GQA_HEADS, GQA_KV_HEADS, GQA_HD = 4, 2, 64
SPLITS = (MLA_Q_RANK, MLA_KV_RANK, MLA_ROPE, MLA_HEADS * MLA_V,
          SSD_CONV_CH, 2 * SSD_HEADS, SSD_W,
          CF_W, CF_W, CF_W,
          GQA_HEADS * GQA_HD, GQA_KV_HEADS * GQA_HD, GQA_KV_HEADS * GQA_HD, GQA_HEADS * GQA_HD)

O_QL, O_KV, O_R, O_AG = 0, 256, 384, 512
O_XBC, O_DT, O_Z = 768, 1280, 1408
O_CV, O_CGL, O_CG = 1664, 1920, 2176
O_DQ, O_DK, O_DV, O_DG = 2432, 2688, 2816, 2944
IN_WP = 3200

CONV_HALO_SSD = 8
CONV_HALO_CF = 16
CONV_TL = 512

SSD_UNROLL = 4
INPROJ_SUB = 256
ATT_TQ = 1024
ATT_VROWS = 80
ATT_SUB = 512
ATT_QK_CHUNK = 512
ATT_PV_CHUNK = 256

NT_DIMS = (((1,), (1,)), ((), ()))


def _silu(x):
    return x / (1.0 + jnp.exp(-x))


def _sigmoid(x):
    return 1.0 / (1.0 + jnp.exp(-x))


def _softplus(x):
    return jnp.maximum(x, 0.0) + jnp.log1p(jnp.exp(-jnp.abs(x)))


def _rms(x, n):
    return x * lax.rsqrt(jnp.sum(x * x, axis=-1, keepdims=True) * (1.0 / n) + EPS)


def _dot(a, b):
    return jnp.dot(a, b, preferred_element_type=F32)


def _dot_nt(a, b):
    return lax.dot_general(a, b, NT_DIMS, preferred_element_type=F32)


def _params(sem, vmem_bytes):
    assert vmem_bytes <= VMEM_CAP
    return pltpu.CompilerParams(dimension_semantics=sem, vmem_limit_bytes=VMEM_CAP)


def _pad_heads(w, nh, d, dp):
    s = w.shape[:-1]
    w = w.reshape(s + (nh, d))
    w = jnp.pad(w, [(0, 0)] * len(s) + [(0, 0), (0, dp - d)])
    return w.reshape(s + (nh * dp,))


def _pad_last(w, lo, hi):
    return jnp.pad(w, [(0, 0)] * (w.ndim - 1) + [(lo, hi)])


def _layer_spec(a, l, grid_rank):
    zeros = (0,) * (a.ndim - 1)
    if grid_rank == 1:
        return pl.BlockSpec((1,) + a.shape[1:], lambda bi: (l,) + zeros)
    return pl.BlockSpec((1,) + a.shape[1:], lambda bi, i: (l,) + zeros)


def _with_ones_lane(v, hd):
    lane = lax.broadcasted_iota(jnp.int32, v.shape, v.ndim - 1)
    return jnp.where(lane % LANES == hd, 1.0, v)


def _mod_kernel(c_ref, w_ref, b_ref, o_ref):
    c = c_ref[...]
    o_ref[0] = _dot(_silu(c).astype(BF16), w_ref[0].astype(BF16)) + b_ref[0]


def _modulation(cvec, w_mod, b_mod):
    depth, d, d3 = w_mod.shape
    tn = d
    return pl.pallas_call(
        _mod_kernel,
        out_shape=jax.ShapeDtypeStruct((depth, 8, d3), F32),
        grid=(depth, d3 // tn),
        in_specs=[pl.BlockSpec((8, d), lambda l, j: (0, 0)),
                  pl.BlockSpec((1, d, tn), lambda l, j: (l, 0, j)),
                  pl.BlockSpec((1, 1, tn), lambda l, j: (l, 0, j))],
        out_specs=pl.BlockSpec((1, 8, tn), lambda l, j: (l, 0, j)),
        compiler_params=_params(("arbitrary", "arbitrary"), 4 * d * tn * 4),
        name="modulation",
    )(cvec, w_mod, b_mod.reshape(depth, 1, d3))


def _rope(x, tab_ref, rows, shift):
    cos, sin_a, sin_b = tab_ref[0, rows, :], tab_ref[1, rows, :], tab_ref[2, rows, :]
    outs = []
    for h in range(x.shape[-1] // LANES):
        xh = x[:, h * LANES:(h + 1) * LANES]
        up = pltpu.roll(xh, LANES - shift, 1)
        dn = pltpu.roll(xh, shift, 1)
        outs.append(xh * cos + up * sin_a + dn * sin_b)
    return jnp.concatenate(outs, axis=-1)


def _block_ones(width, block):
    ri = lax.broadcasted_iota(jnp.int32, (width, width), 0) // block
    ci = lax.broadcasted_iota(jnp.int32, (width, width), 1) // block
    return (ri == ci).astype(F32).astype(BF16)


def _head_norm(x, gain, n, ones_bd):
    xx = (x * x).astype(BF16)
    w2 = ones_bd.shape[0]
    ss = jnp.concatenate([_dot(xx[:, j:j + w2], ones_bd) for j in range(0, x.shape[-1], w2)], axis=-1)
    return x * lax.rsqrt(ss * (1.0 / n) + EPS) * gain


def _spread_heads(x, hd):
    lo = lax.broadcasted_iota(jnp.int32, (x.shape[0], LANES), 1) < hd
    outs = []
    for j in range(0, x.shape[-1], LANES):
        pair = x[:, j:j + LANES]
        outs += [jnp.where(lo, pair, 0.0), jnp.where(lo, pltpu.roll(pair, LANES - hd, 1), 0.0)]
    return jnp.concatenate(outs, axis=-1)


def _inproj_refs(latent, refs):
    names = ['x', 'mod', 'ng', 'win', 'wuq', 'wukv', 'gains']
    if latent:
        names += ['wvta', 'wvtd', 'ra', 'rd']
    names += ['qa', 'ka', 'va', 'ag', 'qd', 'kd', 'vd', 'dg', 'xbc', 'dt', 'z', 'glu', 'cg']
    if not latent:
        names += ['ckv', 'kr', 'gk', 'gv']
    return dict(zip(names, refs))


def _inproj_main(r, rows):
    x = r['x'][0, rows, :]
    shift = r['mod'][0, 0, 0:1, :]
    scale = r['mod'][0, 0, 1:2, :]
    h = (_rms(x, D_MODEL) * r['ng'][0] * (1.0 + scale) + shift).astype(BF16)
    return h, _dot(h, r['win'][0])


def _values_t(w_t, x):
    vt = _dot_nt(w_t, x)
    row = lax.broadcasted_iota(jnp.int32, vt.shape, 0)
    return jnp.where(row % ATT_VROWS == MLA_V, 1.0, vt)


def _inproj_latents(r, u):
    g_ql = r['gains'][0, 0:1, 0:256]
    g_kv = r['gains'][0, 2:3, 0:LANES]
    ql = _rms(u[:, O_QL:O_QL + 256], MLA_Q_RANK) * g_ql
    ckv = _rms(u[:, O_KV:O_KV + LANES], MLA_KV_RANK) * g_kv
    return _dot(ql.astype(BF16), r['wuq'][0]), ckv, _dot(ckv.astype(BF16), r['wukv'][0])


def _inproj_tail(latent, r, rows, h, u, q, ckv, kv):
    wa = MLA_HEADS * LANES
    wq, wk = GQA_HEADS * GQA_HD, GQA_KV_HEADS * GQA_HD
    g_qh = r['gains'][0, 1:2, :]
    g_kh = r['gains'][0, 3:4, :]
    g_dq = r['gains'][0, 4:5, 0:wq]
    g_dk = r['gains'][0, 5:6, 0:wk]
    ones_slot = _block_ones(2 * LANES, LANES)
    ones_pair = _block_ones(2 * LANES, GQA_HD)

    q = _head_norm(q, g_qh, MLA_QK, ones_slot)
    kr = u[:, O_R:O_R + LANES]
    k = _head_norm(kv[:, 0:wa] + jnp.concatenate([kr] * MLA_HEADS, axis=-1), g_kh, MLA_QK, ones_slot)
    qd = _head_norm(u[:, O_DQ:O_DQ + wq], g_dq, GQA_HD, ones_pair)
    kd = _head_norm(u[:, O_DK:O_DK + wk], g_dk, GQA_HD, ones_pair[0:wk, 0:wk])
    vd = u[:, O_DV:O_DV + wk]
    if latent:
        q = _rope(q, r['ra'], rows, MLA_ROPE // 4)
        k = _rope(k, r['ra'], rows, MLA_ROPE // 4)
        qd = _rope(qd, r['rd'], rows, GQA_HD // 4)
        kd_att = _rope(kd, r['rd'], rows, GQA_HD // 4)
    else:
        kd_att = kd
        r['ckv'][0, rows, :] = ckv
        r['kr'][0, rows, :] = kr[:, MLA_NOPE:MLA_NOPE + MLA_ROPE]
        r['gk'][0, rows, :] = kd
        r['gv'][0, rows, :] = vd
    qd = _spread_heads(qd, GQA_HD)
    kd_att = _spread_heads(kd_att, GQA_HD)
    r['qa'][0, rows, :] = q.astype(BF16)
    r['ka'][0, rows, :] = k.astype(BF16)
    if latent:
        r['va'][0, :, rows] = _values_t(r['wvta'][0], ckv.astype(BF16)).astype(BF16)
        r['vd'][0, :, rows] = _values_t(r['wvtd'][0], h).astype(BF16)
    else:
        r['va'][0, rows, :] = _with_ones_lane(kv[:, wa:], MLA_V).astype(BF16)
        r['vd'][0, rows, :] = _with_ones_lane(_spread_heads(vd, GQA_HD), GQA_HD).astype(BF16)
    r['ag'][0, rows, :] = u[:, O_AG:O_AG + BRANCH_W]
    r['qd'][0, rows, :] = qd.astype(BF16)
    r['kd'][0, rows, :] = kd_att.astype(BF16)
    r['dg'][0, rows, :] = u[:, O_DG:O_DG + BRANCH_W]

    r['xbc'][0, rows, :] = u[:, O_XBC:O_XBC + SSD_CONV_CH]
    r['dt'][0, rows, :] = u[:, O_DT:O_DT + LANES]
    r['z'][0, rows, :] = u[:, O_Z:O_Z + SSD_W]
    r['glu'][0, rows, :] = u[:, O_CV:O_CV + CF_W] * _sigmoid(u[:, O_CGL:O_CGL + CF_W])
    r['cg'][0, rows, :] = u[:, O_CG:O_CG + CF_W]


def _inproj_kernel(latent, sub, *refs):
    r = _inproj_refs(latent, refs)
    tm = r['x'].shape[1]
    for r0 in range(0, tm, sub):
        rows = slice(r0, r0 + sub)
        h, u = _inproj_main(r, rows)
        q, ckv, kv = _inproj_latents(r, u)
        _inproj_tail(latent, r, rows, h, u, q, ckv, kv)


def _mod_spec(mod, l, row):
    d = mod.shape[-1]
    if row is None:
        return pl.BlockSpec((1, 1, 3, d), lambda bi, i: (l, bi, 0, 0))
    return pl.BlockSpec((1, 1, 3, d), lambda bi, i: (l, row, 0, 0))


def _inproj(x, mod, mod_row, l, w, ropes, tm):
    b, s, d = x.shape
    latent = ropes is not None
    tok = lambda wd: pl.BlockSpec((1, tm, wd), lambda bi, i: (bi, i, 0))
    weights = [w['ng'], w['win'], w['wuq'], w['wukv'], w['gains']]
    if latent:
        weights += [w['wvta'], w['wvtd']]
    in_specs = [tok(d), _mod_spec(mod, l, mod_row)] + [_layer_spec(a, l, 2) for a in weights]
    args = [x, mod] + weights
    if latent:
        in_specs += [pl.BlockSpec((3, tm, LANES), lambda bi, i: (0, i, 0))] * 2
        args += list(ropes)
    widths = [(4 * LANES, BF16), (4 * LANES, BF16), (4 * LANES, BF16), (BRANCH_W, F32),
              (4 * LANES, BF16), (2 * LANES, BF16), (2 * LANES, BF16), (BRANCH_W, F32),
              (SSD_CONV_CH, F32), (LANES, F32), (SSD_W, F32), (CF_W, F32), (CF_W, F32)]
    if not latent:
        widths += [(MLA_KV_RANK, F32), (MLA_ROPE, F32), (LANES, F32), (LANES, F32)]
    out_shape = [jax.ShapeDtypeStruct((b, s, wd), dt) for wd, dt in widths]
    out_specs = [tok(wd) for wd, _ in widths]
    if latent:
        for pos, heads in ((2, MLA_HEADS), (6, GQA_KV_HEADS)):
            out_shape[pos] = jax.ShapeDtypeStruct((b, heads * ATT_VROWS, s), BF16)
            out_specs[pos] = pl.BlockSpec((1, heads * ATT_VROWS, tm), lambda bi, i: (bi, 0, i))
    out_bytes = sum(wd * jnp.dtype(dt).itemsize for wd, dt in widths) * tm
    w_bytes = sum(a[0].size * a.dtype.itemsize for a in weights)
    sub = min(INPROJ_SUB, tm)
    vmem = 2 * (tm * d * 4 + w_bytes + out_bytes) + 3 * sub * IN_WP * 4
    return pl.pallas_call(
        functools.partial(_inproj_kernel, latent, sub),
        out_shape=out_shape,
        grid=(b, s // tm),
        in_specs=in_specs,
        out_specs=out_specs,
        compiler_params=_params(("parallel", "parallel"), vmem),
        name="inproj_lat" if latent else "inproj_ctx",
    )(*args)


def _ctxkv_kernel(ckv_ref, kr_ref, gk_ref, gv_ref, wukv_ref, wvta_ref, gains_ref, kc_ref, vc_ref, kdc_ref, vdc_ref):
    ckv = ckv_ref[0, 0].astype(BF16)
    kv = _dot(ckv, wukv_ref[0])
    kr = kr_ref[0, 0]
    g = gains_ref[0, 3:4, 0:LANES]
    for i in range(MLA_HEADS):
        k = _rms(kv[:, i * LANES:(i + 1) * LANES] + kr, MLA_QK) * g
        kc_ref[0, :, i * LANES:(i + 1) * LANES] = k.astype(BF16)
    vc_ref[0] = _values_t(wvta_ref[0], ckv).astype(BF16)
    gk = gk_ref[0, 0]
    p = gk.shape[0]
    pad = jnp.zeros((p, LANES - GQA_HD), BF16)
    for i in range(GQA_KV_HEADS):
        lo = i * LANES
        kdc_ref[0, :, lo:lo + GQA_HD] = gk[:, i * GQA_HD:(i + 1) * GQA_HD].astype(BF16)
        kdc_ref[0, :, lo + GQA_HD:lo + LANES] = pad
    gvt = gv_ref[0, 0].T
    row = lax.broadcasted_iota(jnp.int32, (ATT_VROWS - GQA_HD, p), 0)
    tail = jnp.where(row == 0, 1.0, 0.0)
    pieces = []
    for i in range(GQA_KV_HEADS):
        pieces += [gvt[i * GQA_HD:(i + 1) * GQA_HD, :], tail]
    vdc_ref[0] = jnp.concatenate(pieces, axis=0).astype(BF16)


def _ctx_kv(ckv, kr_pad, gk, gv, l, w):
    b, _, p, _ = ckv.shape
    cache = lambda a: pl.BlockSpec((1, 1, p, a.shape[-1]), lambda bi: (bi, l, 0, 0))
    out = lambda wd: pl.BlockSpec((1, p, wd), lambda bi: (bi, 0, 0))
    out_t = lambda rows: pl.BlockSpec((1, rows, p), lambda bi: (bi, 0, 0))
    wa, wd = MLA_HEADS * LANES, GQA_KV_HEADS * LANES
    ra, rd = MLA_HEADS * ATT_VROWS, GQA_KV_HEADS * ATT_VROWS
    return pl.pallas_call(
        _ctxkv_kernel,
        out_shape=[jax.ShapeDtypeStruct((b, p, wa), BF16), jax.ShapeDtypeStruct((b, ra, p), BF16),
                   jax.ShapeDtypeStruct((b, p, wd), BF16), jax.ShapeDtypeStruct((b, rd, p), BF16)],
        grid=(b,),
        in_specs=[cache(ckv), cache(kr_pad), cache(gk), cache(gv),
                  _layer_spec(w['wukv'], l, 1), _layer_spec(w['wvta'], l, 1), _layer_spec(w['gains'], l, 1)],
        out_specs=[out(wa), out_t(ra), out(wd), out_t(rd)],
        compiler_params=_params(("parallel",), 16 * p * wa * 4),
        name="ctx_kv",
    )(ckv, kr_pad, gk, gv, w['wukv'], w['wvta'], w['gains'])


def _lane_tile_max(t):
    r = t[:, 0:LANES]
    for j in range(1, t.shape[-1] // LANES):
        r = jnp.maximum(r, t[:, j * LANES:(j + 1) * LANES])
    return r


def _attn_unit(n_heads, rep, has_ctx, refs, bi, rows, h, s_ref):
    if has_ctx:
        q_ref, k_ref, v_ref, kc_ref, vc_ref, g_ref, o_ref = refs
        p_len = kc_ref.shape[1]
    else:
        q_ref, k_ref, v_ref, g_ref, o_ref = refs
        p_len = 0
    s_len = k_ref.shape[1]
    hd = o_ref.shape[-1] // n_heads
    kb = min(ATT_QK_CHUNK, s_len)
    kb2 = ATT_PV_CHUNK
    g = h // rep
    hs = slice(h * LANES, (h + 1) * LANES)
    gs = slice(g * LANES, (g + 1) * LANES)
    qh = q_ref[bi, rows, hs]
    mx = None
    for c in range(s_len // kb):
        sc = _dot_nt(qh, k_ref[bi, c * kb:(c + 1) * kb, gs])
        s_ref[:, c * kb:(c + 1) * kb] = sc
        t = _lane_tile_max(sc)
        mx = t if mx is None else jnp.maximum(mx, t)
    if has_ctx:
        sc = _dot_nt(qh, kc_ref[bi, :, gs])
        s_ref[:, s_len:s_len + p_len] = sc
        mx = jnp.maximum(mx, _lane_tile_max(sc))
    m = jnp.max(mx, axis=-1, keepdims=True)
    acc = None
    for c in range((s_len + p_len) // kb2):
        r0 = c * kb2
        p = jnp.exp2(s_ref[:, r0:r0 + kb2] - m).astype(BF16)
        vv = v_ref[bi, r0:r0 + kb2, gs] if r0 < s_len else vc_ref[bi, r0 - s_len:r0 - s_len + kb2, gs]
        part = _dot(p, vv)
        acc = part if acc is None else acc + part
    o = acc[:, 0:hd] / acc[:, hd:hd + 1]
    os_ = slice(h * hd, (h + 1) * hd)
    o_ref[bi, rows, os_] = (o * _silu(g_ref[bi, rows, os_])).astype(o_ref.dtype)


def _attn_kernel(n_heads, rep, has_ctx, sub, *refs):
    s0_ref, s1_ref = refs[-2:]
    bb, tq = refs[0].shape[0], refs[0].shape[1]
    unit = 0
    for bi in range(bb):
        for r0 in range(0, tq, sub):
            for h in range(n_heads):
                _attn_unit(n_heads, rep, has_ctx, refs[:-2], bi, slice(r0, r0 + sub), h,
                           s0_ref if unit % 2 == 0 else s1_ref)
                unit += 1


def _attention(q, k, v, ctx, gate, n_heads, rep, bb, tq, sub, name):
    b, s, qw = q.shape
    kw = k.shape[-1]
    ow = gate.shape[-1]
    has_ctx = ctx is not None
    in_specs = [pl.BlockSpec((bb, tq, qw), lambda bi, i: (bi, i, 0)),
                pl.BlockSpec((bb, s, kw), lambda bi, i: (bi, 0, 0)),
                pl.BlockSpec((bb, s, kw), lambda bi, i: (bi, 0, 0))]
    args = [q, k, v]
    p = 0
    if has_ctx:
        p = ctx[0].shape[1]
        in_specs += [pl.BlockSpec((bb, p, kw), lambda bi, i: (bi, 0, 0))] * 2
        args += list(ctx)
    in_specs.append(pl.BlockSpec((bb, tq, ow), lambda bi, i: (bi, i, 0)))
    args.append(gate)
    assert s % min(ATT_QK_CHUNK, s) == 0 and s % ATT_PV_CHUNK == 0 and p % ATT_PV_CHUNK == 0
    assert b % bb == 0 and s % tq == 0 and tq % sub == 0
    vmem = (2 * bb * (tq * qw * 2 + 2 * (s + p) * kw * 2 + 2 * tq * ow * 4)
            + 2 * sub * (s + p) * 4 + 2 * sub * 1024 * 4)
    return pl.pallas_call(
        functools.partial(_attn_kernel, n_heads, rep, has_ctx, sub),
        out_shape=jax.ShapeDtypeStruct((b, s, ow), BF16),
        grid=(b // bb, s // tq),
        in_specs=in_specs,
        out_specs=pl.BlockSpec((bb, tq, ow), lambda bi, i: (bi, i, 0)),
        scratch_shapes=[pltpu.VMEM((sub, s + p), F32), pltpu.VMEM((sub, s + p), F32)],
        compiler_params=_params(("parallel", "arbitrary"), vmem),
        name=name,
    )(*args)


def _attn_t_scores(io, rep, sub, unit, s_ref):
    q_ref, k_ref, _, kc_ref = io[:4]
    r0, h = unit
    s_len, p_len = k_ref.shape[1], kc_ref.shape[1]
    kb = min(ATT_QK_CHUNK, s_len)
    gs = slice((h // rep) * LANES, (h // rep + 1) * LANES)
    qh = q_ref[0, r0:r0 + sub, h * LANES:(h + 1) * LANES]
    mx = None
    for c in range(s_len // kb):
        st = _dot_nt(k_ref[0, c * kb:(c + 1) * kb, gs], qh)
        s_ref[c * kb:(c + 1) * kb, :] = st
        t = jnp.max(st, axis=0, keepdims=True)
        mx = t if mx is None else jnp.maximum(mx, t)
    st = _dot_nt(kc_ref[0, :, gs], qh)
    s_ref[s_len:s_len + p_len, :] = st
    return jnp.maximum(mx, jnp.max(st, axis=0, keepdims=True))


def _attn_t_values(io, n_heads, rep, sub, unit, s_ref, m, ot_ref):
    _, k_ref, vt_ref, kc_ref, vct_ref, _, o_ref = io
    r0, h = unit
    s_len, p_len = k_ref.shape[1], kc_ref.shape[1]
    hd = o_ref.shape[-1] // n_heads
    vs = slice((h // rep) * ATT_VROWS, (h // rep + 1) * ATT_VROWS)
    kb2 = ATT_PV_CHUNK
    acc = None
    for c in range((s_len + p_len) // kb2):
        k0 = c * kb2
        pt = jnp.exp2(s_ref[k0:k0 + kb2, :] - m).astype(BF16)
        vv = vt_ref[0, vs, k0:k0 + kb2] if k0 < s_len else vct_ref[0, vs, k0 - s_len:k0 - s_len + kb2]
        part = _dot(vv, pt)
        acc = part if acc is None else acc + part
    ot_ref[h * hd:(h + 1) * hd, r0:r0 + sub] = acc[0:hd, :] / acc[hd:hd + 1, :]


def _attn_t_kernel(n_heads, rep, sub, *refs):
    io, (s0_ref, s1_ref, ot_ref) = refs[:-3], refs[-3:]
    tq = io[0].shape[1]
    units = [(r0, h) for r0 in range(0, tq, sub) for h in range(n_heads)]
    bufs = (s0_ref, s1_ref)
    m = _attn_t_scores(io, rep, sub, units[0], bufs[0])
    for u in range(len(units)):
        m_next = None
        if u + 1 < len(units):
            m_next = _attn_t_scores(io, rep, sub, units[u + 1], bufs[(u + 1) % 2])
        _attn_t_values(io, n_heads, rep, sub, units[u], bufs[u % 2], m, ot_ref)
        m = m_next
    g_ref, o_ref = io[5], io[6]
    o_ref[0] = (ot_ref[...].T * _silu(g_ref[0])).astype(o_ref.dtype)


def _attention_t(q, k, vt, ctx, gate, n_heads, rep, tq, sub, name):
    b, s, qw = q.shape
    kw = k.shape[-1]
    vr = vt.shape[1]
    ow = gate.shape[-1]
    kc, vct = ctx
    p = kc.shape[1]
    assert s % min(ATT_QK_CHUNK, s) == 0 and s % ATT_PV_CHUNK == 0 and p % ATT_PV_CHUNK == 0
    assert s % tq == 0 and tq % sub == 0
    tile = lambda wd: pl.BlockSpec((1, tq, wd), lambda bi, i: (bi, i, 0))
    whole = lambda a: pl.BlockSpec((1,) + a.shape[1:], lambda bi, i: (bi, 0, 0))
    vmem = (2 * (tq * qw * 2 + (s + p) * (kw + vr) * 2 + tq * ow * 6)
            + 2 * sub * (s + p) * 4 + ow * tq * 4 + 2 * sub * 1024 * 4)
    return pl.pallas_call(
        functools.partial(_attn_t_kernel, n_heads, rep, sub),
        out_shape=jax.ShapeDtypeStruct((b, s, ow), BF16),
        grid=(b, s // tq),
        in_specs=[tile(qw), whole(k), whole(vt), whole(kc), whole(vct), tile(ow)],
        out_specs=tile(ow),
        scratch_shapes=[pltpu.VMEM((s + p, sub), F32), pltpu.VMEM((s + p, sub), F32), pltpu.VMEM((ow, tq), F32)],
        compiler_params=_params(("parallel", "arbitrary"), vmem),
        name=name,
    )(q, k, vt, kc, vct, gate)


def _conv_kernel(tl, xc_ref, xp_ref, xn_ref, gc_ref, gp_ref, gn_ref, cg_ref,
                 w3_ref, b3_ref, w31_ref, b31_ref, lng_ref, lnb_ref,
                 xo_ref, co_ref, ext3, ext31):
    i = pl.program_id(1)
    keep_prev = (i > 0).astype(F32)
    keep_next = (i < pl.num_programs(1) - 1).astype(F32)

    h3 = CONV_HALO_SSD
    ext3[0:h3, :] = xp_ref[0] * keep_prev
    ext3[h3:h3 + tl, :] = xc_ref[0]
    ext3[h3 + tl:2 * h3 + tl, :] = xn_ref[0] * keep_next
    acc = jnp.zeros((tl, SSD_CONV_CH), F32) + b3_ref[0]
    for k in range(SSD_CONV):
        acc = acc + w3_ref[0, k:k + 1, :] * ext3[h3 - SSD_CONV // 2 + k:h3 - SSD_CONV // 2 + k + tl, :]
    xo_ref[0] = _silu(acc)

    h31 = CONV_HALO_CF
    ext31[0:h31, :] = gp_ref[0] * keep_prev
    ext31[h31:h31 + tl, :] = gc_ref[0]
    ext31[h31 + tl:2 * h31 + tl, :] = gn_ref[0] * keep_next
    acc = jnp.zeros((tl, CF_W), F32) + b31_ref[0]
    o_lo = h31 - CF_K // 2
    ext = ext31[...]
    n_ext = ext.shape[0]
    for r in range(8):
        taps = [k for k in range(CF_K) if (o_lo + k) % 8 == r]
        if not taps:
            continue
        shifted = ext if r == 0 else pltpu.roll(ext, n_ext - r, 0)
        for k in taps:
            a = (o_lo + k) // 8 * 8
            acc = acc + w31_ref[0, k:k + 1, :] * shifted[a:a + tl, :]
    mu = jnp.mean(acc, axis=-1, keepdims=True)
    cen = acc - mu
    var = jnp.mean(cen * cen, axis=-1, keepdims=True)
    y = cen * lax.rsqrt(var + EPS) * lng_ref[0] + lnb_ref[0]
    co_ref[0] = (_silu(y) * _silu(cg_ref[0])).astype(co_ref.dtype)


def _convs(xbc, glu, cg, l, w, tl):
    b, s, _ = xbc.shape
    nt = s // tl

    def cur(wd):
        return pl.BlockSpec((1, tl, wd), lambda bi, i: (bi, i, 0))

    def prev(wd, hrows):
        r = tl // hrows
        return pl.BlockSpec((1, hrows, wd), lambda bi, i: (bi, jnp.maximum(i * r - 1, 0), 0))

    def nxt(wd, hrows):
        r = tl // hrows
        return pl.BlockSpec((1, hrows, wd), lambda bi, i: (bi, jnp.minimum((i + 1) * r, nt * r - 1), 0))

    weights = [w['w3'], w['b3'], w['w31'], w['b31'], w['lng'], w['lnb']]
    vmem = 4 * tl * (SSD_CONV_CH * 2 + CF_W * 3) * 4 + 6 * tl * SSD_CONV_CH * 4
    return pl.pallas_call(
        functools.partial(_conv_kernel, tl),
        out_shape=[jax.ShapeDtypeStruct((b, s, SSD_CONV_CH), F32),
                   jax.ShapeDtypeStruct((b, s, CF_W), BF16)],
        grid=(b, nt),
        in_specs=[cur(SSD_CONV_CH), prev(SSD_CONV_CH, CONV_HALO_SSD), nxt(SSD_CONV_CH, CONV_HALO_SSD),
                  cur(CF_W), prev(CF_W, CONV_HALO_CF), nxt(CF_W, CONV_HALO_CF), cur(CF_W)]
                 + [_layer_spec(a, l, 2) for a in weights],
        out_specs=[cur(SSD_CONV_CH), cur(CF_W)],
        scratch_shapes=[pltpu.VMEM((tl + 2 * CONV_HALO_SSD, SSD_CONV_CH), F32),
                        pltpu.VMEM((tl + 2 * CONV_HALO_CF, CF_W), F32)],
        compiler_params=_params(("parallel", "parallel"), vmem),
        name="convs",
    )(xbc, xbc, xbc, glu, glu, glu, cg, *weights)


def _split3(x):
    hi = x.astype(BF16)
    r1 = x - hi.astype(F32)
    mid = r1.astype(BF16)
    lo = (r1 - mid.astype(F32)).astype(BF16)
    return hi, mid, lo


def _cumsum_rows(tri, x):
    return sum(_dot(tri, part) for part in _split3(x))


def _transpose_exact(x):
    c = x.shape[1]
    eye = (lax.broadcasted_iota(jnp.int32, (c, c), 0) == lax.broadcasted_iota(jnp.int32, (c, c), 1))
    eye = eye.astype(F32).astype(BF16)
    return sum(_dot_nt(eye, part) for part in _split3(x))


SSD_NK = 2 * SSD_HEADS


def _ssd_stage_local(c, d, xc_ref, dt_ref, bias, a_row):
    q = SSD_CHUNK
    rows = pl.ds(pl.multiple_of(c * q, q), q)
    t = dict(rows=rows, d=d)
    t['xs'] = xc_ref[0, rows, 0:SSD_W]
    t['bm'] = xc_ref[0, rows, SSD_W:SSD_W + LANES]
    t['cm'] = xc_ref[0, rows, SSD_W + LANES:SSD_W + 2 * LANES].astype(BF16)
    t['dt'] = _softplus(dt_ref[0, rows, :] + bias)
    ri = lax.broadcasted_iota(jnp.int32, (q, q), 0)
    ci = lax.broadcasted_iota(jnp.int32, (q, q), 1)
    t['mask'] = (ri >= ci) if d == 0 else (ri <= ci)
    t['acum'] = _cumsum_rows(t['mask'].astype(F32).astype(BF16), t['dt'] * a_row)
    return t


def _ssd_stage_cb(t):
    q = SSD_CHUNK
    lane = lax.broadcasted_iota(jnp.int32, (q, LANES), 1)
    t['tr'] = jnp.where(lane < SSD_NK, t['dt'], t['acum']).T
    bt = t['bm'].T
    t['bt'] = bt
    grp_row = lax.broadcasted_iota(jnp.int32, (LANES, q), 0) < SSD_N
    t['cb'] = [_dot(t['cm'], jnp.where(grp_row if g == 0 else jnp.logical_not(grp_row), bt, 0.0).astype(BF16))
               for g in range(SSD_GROUPS)]
    t['last'] = q - 1 if t['d'] == 0 else 0
    t['etot'] = jnp.exp(t['acum'][t['last']:t['last'] + 1, :])
    t['eac'] = jnp.exp(t['acum'])


def _ssd_stage_diag(t, dskip):
    q = SSD_CHUNK
    d = t['d']
    lo_half = lax.broadcasted_iota(jnp.int32, (q, LANES), 1) < SSD_P
    tr, acum, last = t['tr'], t['acum'], t['last']
    t['yd'], t['cs'] = [], []
    for g in range(SSD_GROUPS):
        k0 = d * SSD_HEADS + 2 * g
        xg = t['xs'][:, g * LANES:(g + 1) * LANES]
        xgb = xg.astype(BF16)
        btg = t['bt'][g * SSD_N:(g + 1) * SSD_N, :]
        yg = None
        cs = []
        for hh in range(2):
            k = k0 + hh
            dt_row = tr[k:k + 1, :]
            ac_row = tr[SSD_NK + k:SSD_NK + k + 1, :]
            dec = jnp.where(t['mask'], jnp.exp(acum[:, k:k + 1] - ac_row), 0.0)
            xm = jnp.where(lo_half if hh == 0 else jnp.logical_not(lo_half), xg, 0.0).astype(BF16)
            part = _dot((t['cb'][g] * dec * dt_row).astype(BF16), xm)
            yg = part if yg is None else yg + part
            w_row = dt_row * jnp.exp(tr[SSD_NK + k:SSD_NK + k + 1, last:last + 1] - ac_row)
            cs.append(_dot((btg * w_row).astype(BF16), xgb))
        if d == 0:
            yg = yg + dskip[:, g * LANES:(g + 1) * LANES] * xg
        t['yd'].append(yg)
        t['cs'].append(cs)


def _ssd_stage_state(t, states):
    q = SSD_CHUNK
    d = t['d']
    lo_half = lax.broadcasted_iota(jnp.int32, (q, LANES), 1) < SSD_P
    lane_n = lax.broadcasted_iota(jnp.int32, (SSD_N, LANES), 1) < SSD_P
    zero = jnp.zeros((SSD_N, LANES), F32)
    ys, new_states = [], []
    for g in range(SSD_GROUPS):
        k0 = d * SSD_HEADS + 2 * g
        s_prev = states[g]
        sz = jnp.concatenate([s_prev, zero] if g == 0 else [zero, s_prev], axis=0).astype(BF16)
        eacs = jnp.where(lo_half, t['eac'][:, k0:k0 + 1], t['eac'][:, k0 + 1:k0 + 2])
        ys.append(t['yd'][g] + _dot(t['cm'], sz) * eacs)
        cdec = jnp.where(lane_n, t['etot'][:, k0:k0 + 1], t['etot'][:, k0 + 1:k0 + 2])
        new_states.append(cdec * s_prev + jnp.where(lane_n, t['cs'][g][0], t['cs'][g][1]))
    return ys, new_states


def _ssd_kernel(nc, unroll, has_init, *refs):
    if has_init:
        xc_ref, dt_ref, s0_ref, bias_ref, alog_ref, d_ref, yf_ref, yb_ref, so_ref, st_ref = refs
        for d in range(2):
            for g in range(SSD_GROUPS):
                st_ref[0, d, g] = _transpose_exact(s0_ref[0, 0, d, g])
    else:
        xc_ref, dt_ref, bias_ref, alog_ref, d_ref, yf_ref, yb_ref, so_ref, st_ref = refs
        st_ref[...] = jnp.zeros(st_ref.shape, F32)
    lane = lax.broadcasted_iota(jnp.int32, (1, LANES), 1)
    a_row = jnp.where(lane < 2 * SSD_NK, -jnp.exp(alog_ref[0]), 0.0)
    bias = bias_ref[0]
    dskip = d_ref[0]

    def body(i, carry):
        states = [[st_ref[0, d, g] for g in range(SSD_GROUPS)] for d in range(2)]
        work = []
        for j in range(unroll):
            c = i * unroll + j
            work.append((_ssd_stage_local(c, 0, xc_ref, dt_ref, bias, a_row), yf_ref))
            work.append((_ssd_stage_local(nc - 1 - c, 1, xc_ref, dt_ref, bias, a_row), yb_ref))
        for t, _ in work:
            _ssd_stage_cb(t)
        for t, _ in work:
            _ssd_stage_diag(t, dskip)
        writes = []
        for t, y_ref in work:
            ys, states[t['d']] = _ssd_stage_state(t, states[t['d']])
            writes.append((y_ref, t['rows'], ys))
        for y_ref, rows, ys in writes:
            for g in range(SSD_GROUPS):
                y_ref[0, rows, g * LANES:(g + 1) * LANES] = ys[g]
        for d in range(2):
            for g in range(SSD_GROUPS):
                st_ref[0, d, g] = states[d][g]
        return carry

    lax.fori_loop(0, nc // unroll, body, 0)
    for d in range(2):
        for g in range(SSD_GROUPS):
            so_ref[0, d, g] = _transpose_exact(st_ref[0, d, g])


def _ssd(xc, dt, s0, l, w):
    b, s, _ = xc.shape
    nc = s // SSD_CHUNK
    unroll = min(SSD_UNROLL, nc)
    assert nc % unroll == 0
    st_shape = (b, 2, SSD_GROUPS, 2 * SSD_P, SSD_N)
    st_spec = pl.BlockSpec((1,) + st_shape[1:], lambda bi: (bi, 0, 0, 0, 0))
    y_spec = pl.BlockSpec((1, s, SSD_W), lambda bi: (bi, 0, 0))
    in_specs = [pl.BlockSpec((1, s, SSD_CONV_CH), lambda bi: (bi, 0, 0)),
                pl.BlockSpec((1, s, LANES), lambda bi: (bi, 0, 0))]
    args = [xc, dt]
    if s0 is not None:
        in_specs.append(pl.BlockSpec((1, 1) + st_shape[1:], lambda bi: (bi, l, 0, 0, 0, 0)))
        args.append(s0)
    weights = [w['dt_bias'], w['a_log'], w['dskip']]
    vmem = 2 * s * (SSD_CONV_CH + LANES + 2 * SSD_W) * 4 + 8 * 1024 * 1024
    return pl.pallas_call(
        functools.partial(_ssd_kernel, nc, unroll, s0 is not None),
        out_shape=[jax.ShapeDtypeStruct((b, s, SSD_W), F32), jax.ShapeDtypeStruct((b, s, SSD_W), F32),
                   jax.ShapeDtypeStruct(st_shape, F32)],
        grid=(b,),
        in_specs=in_specs + [_layer_spec(a, l, 1) for a in weights],
        out_specs=[y_spec, y_spec, st_spec],
        scratch_shapes=[pltpu.VMEM((1, 2, SSD_GROUPS, SSD_N, 2 * SSD_P), F32)],
        compiler_params=_params(("parallel",), vmem),
        name="ssd",
    )(*args, *weights)


def _outproj_kernel(oa_ref, yf_ref, yb_ref, z_ref, oc_ref, od_ref, x_ref, mod_ref, ng_ref, w_ref, o_ref):
    ob = _rms((yf_ref[0] + yb_ref[0]) * _silu(z_ref[0]), SSD_W) * ng_ref[0]
    acc = _dot(oa_ref[0], w_ref[0, 0:BRANCH_W, :])
    acc = acc + _dot(ob.astype(BF16), w_ref[0, BRANCH_W:2 * BRANCH_W, :])
    acc = acc + _dot(oc_ref[0], w_ref[0, 2 * BRANCH_W:3 * BRANCH_W, :])
    acc = acc + _dot(od_ref[0], w_ref[0, 3 * BRANCH_W:4 * BRANCH_W, :])
    o_ref[0] = x_ref[0] + mod_ref[0, 0, 2:3, :] * acc


def _outproj(oa, yf, yb, z, oc, od, x, mod, mod_row, l, w, tm):
    b, s, d = x.shape
    br = lambda: pl.BlockSpec((1, tm, BRANCH_W), lambda bi, i: (bi, i, 0))
    xs = pl.BlockSpec((1, tm, d), lambda bi, i: (bi, i, 0))
    vmem = 2 * (6 * tm * BRANCH_W * 4 + 2 * tm * d * 4 + d * d * 2) + 2 * tm * d * 4
    return pl.pallas_call(
        _outproj_kernel,
        out_shape=jax.ShapeDtypeStruct((b, s, d), F32),
        grid=(b, s // tm),
        in_specs=[br(), br(), br(), br(), br(), br(), xs, _mod_spec(mod, l, mod_row),
                  _layer_spec(w['ssd_ng'], l, 2), _layer_spec(w['wout'], l, 2)],
        out_specs=xs,
        compiler_params=_params(("parallel", "parallel"), vmem),
        name="outproj",
    )(oa, yf, yb, z, oc, od, x, mod, w['ssd_ng'], w['wout'])


def _rope_table(n_tokens, dim, lane0):
    n_rows = n_tokens // GRID_W
    quarter = dim // 4
    inv = ROPE_THETA ** (-jnp.arange(quarter, dtype=F32) / quarter)
    ar = jnp.arange(n_rows, dtype=F32)[:, None] * inv
    ac = jnp.arange(GRID_W, dtype=F32)[:, None] * inv
    zr, zc = jnp.zeros_like(ar), jnp.zeros_like(ac)
    hi = LANES - lane0 - dim

    def lanes(row_part, col_part):
        rp = _pad_last(jnp.concatenate(row_part, axis=-1), lane0, hi)
        cp = _pad_last(jnp.concatenate(col_part, axis=-1), lane0, hi)
        return (rp[:, None, :] + cp[None, :, :]).reshape(n_tokens, LANES)

    off_rotary = jnp.pad(jnp.zeros((1, dim), F32), ((0, 0), (lane0, hi)), constant_values=1.0)
    cos = lanes([jnp.cos(ar), jnp.cos(ar), zr, zr], [zc, zc, jnp.cos(ac), jnp.cos(ac)]) + off_rotary
    sin_a = lanes([-jnp.sin(ar), zr, zr, zr], [zc, zc, -jnp.sin(ac), zc])
    sin_b = lanes([zr, jnp.sin(ar), zr, zr], [zc, zc, zc, jnp.sin(ac)])
    return jnp.stack([cos, sin_a, sin_b])


def _prepare_weights(w_in, w_out, norm_g, mla_q_norm_g, mla_w_uq, mla_kv_norm_g, mla_w_ukv, mla_q_head_g,
                     mla_k_head_g, gqa_q_g, gqa_k_g, ssd_conv_w, ssd_conv_b, ssd_dt_bias, ssd_a_log, ssd_d,
                     ssd_norm_g, cf_conv_w, cf_conv_b, cf_ln_g, cf_ln_b):
    depth = w_in.shape[0]
    pieces = []
    o = 0
    for sz in SPLITS:
        pieces.append(w_in[:, :, o:o + sz])
        o += sz
    (a_q, a_kv, a_r, a_g, b_xbc, b_dt, b_z, c_v, c_gl, c_g, d_q, d_k, d_v, d_g) = pieces
    win = jnp.concatenate([
        _pad_last(a_q, 0, 256 - MLA_Q_RANK), a_kv, _pad_last(a_r, MLA_NOPE, LANES - MLA_QK), a_g,
        b_xbc, _pad_last(jnp.concatenate([b_dt, b_dt], axis=-1), 0, LANES - 2 * SSD_NK), b_z, c_v, c_gl, c_g,
        d_q, d_k, d_v, d_g], axis=-1).astype(BF16)
    wuq = jnp.pad(_pad_heads(mla_w_uq, MLA_HEADS, MLA_QK, LANES),
                  ((0, 0), (0, 256 - MLA_Q_RANK), (0, 0))).astype(BF16)
    kv = mla_w_ukv.reshape(depth, MLA_KV_RANK, MLA_HEADS, MLA_NOPE + MLA_V)
    wk = _pad_last(kv[..., :MLA_NOPE], 0, LANES - MLA_NOPE).reshape(depth, MLA_KV_RANK, -1)
    wv = _pad_last(kv[..., MLA_NOPE:], 0, LANES - MLA_V).reshape(depth, MLA_KV_RANK, -1)
    wukv = jnp.concatenate([wk, wv], axis=-1).astype(BF16)
    rows_t = lambda wt: jnp.pad(wt, ((0, 0), (0, 0), (0, ATT_VROWS - wt.shape[2]), (0, 0)))
    wvta = rows_t(kv[..., MLA_NOPE:].transpose(0, 2, 3, 1)).reshape(depth, MLA_HEADS * ATT_VROWS, MLA_KV_RANK)
    wvtd = rows_t(d_v.reshape(depth, -1, GQA_KV_HEADS, GQA_HD).transpose(0, 2, 3, 1))
    wvtd = wvtd.reshape(depth, GQA_KV_HEADS * ATT_VROWS, -1)
    gw = MLA_HEADS * LANES
    row = lambda v: _pad_last(v, 0, gw - v.shape[-1])
    slots = lambda v, n: row(jnp.tile(_pad_last(v, 0, LANES - v.shape[-1]), (1, n)))
    zero = jnp.zeros((depth, gw), F32)
    gains = jnp.stack([row(mla_q_norm_g),
                       slots(mla_q_head_g * (MLA_QK ** -0.5 * LOG2E), MLA_HEADS),
                       row(mla_kv_norm_g),
                       slots(mla_k_head_g, MLA_HEADS),
                       row(jnp.tile(gqa_q_g * (GQA_HD ** -0.5 * LOG2E), (1, GQA_HEADS))),
                       row(jnp.tile(gqa_k_g, (1, GQA_KV_HEADS))), zero, zero], axis=1)
    vec = lambda v: v[:, None, :]
    lanes8 = lambda v: _pad_last(jnp.tile(v.reshape(depth, 1, -1), (1, 1, 2)), 0, LANES - 2 * SSD_NK)
    return dict(win=win, wuq=wuq, wukv=wukv, wvta=wvta.astype(BF16), wvtd=wvtd.astype(BF16), gains=gains,
                wout=w_out.astype(BF16), ng=vec(norm_g),
                w3=ssd_conv_w, b3=vec(ssd_conv_b), w31=cf_conv_w, b31=vec(cf_conv_b),
                lng=vec(cf_ln_g), lnb=vec(cf_ln_b), dt_bias=lanes8(ssd_dt_bias), a_log=lanes8(ssd_a_log),
                dskip=vec(jnp.repeat(ssd_d, SSD_P, axis=-1)), ssd_ng=vec(ssd_norm_g))


def _mixer_layer(x, mod, mod_row, l, w, caches, ropes, tm):
    latent = caches is not None
    b, s, d = x.shape
    flat = (lambda a: a) if latent else (lambda a: a.reshape(1, b * s, a.shape[-1]))
    unflat = (lambda a: a) if latent else (lambda a: a.reshape(b, s, a.shape[-1]))
    outs = [unflat(o) for o in _inproj(flat(x), mod, mod_row, l, w, ropes, tm)]
    qa, ka, va, ag, qd, kd, vd, dg, xbc, dt, z, glu, cg = outs[:13]
    ctx_a = ctx_d = s0 = None
    if latent:
        kc, vc, kdc, vdc = _ctx_kv(caches['ckv'], caches['kr_pad'], caches['gk'], caches['gv'], l, w)
        ctx_a, ctx_d, s0 = (kc, vc), (kdc, vdc), caches['state']
    tq = min(ATT_TQ, s)
    sub = min(ATT_SUB, tq)
    rep = GQA_HEADS // GQA_KV_HEADS
    if latent:
        oa = _attention_t(qa, ka, va, ctx_a, ag, MLA_HEADS, 1, tq, sub, "attn_mla")
        od = _attention_t(qd, kd, vd, ctx_d, dg, GQA_HEADS, rep, tq, sub, "attn_gqa")
    else:
        oa = _attention(qa, ka, va, None, ag, MLA_HEADS, 1, 1, tq, sub, "attn_mla_ctx")
        od = _attention(qd, kd, vd, None, dg, GQA_HEADS, rep, 1, tq, sub, "attn_gqa_ctx")
    xconv, oc = _convs(xbc, glu, cg, l, w, min(CONV_TL, s))
    yf, yb, s_fin = _ssd(xconv, dt, s0, l, w)
    y = unflat(_outproj(flat(oa), flat(yf), flat(yb), flat(z), flat(oc), flat(od), flat(x), mod, mod_row, l, w, tm))
    new_ctx = None
    if not latent:
        ckv, kr, gk, gv = outs[13:]
        new_ctx = (ckv, kr, gk.reshape(b, -1, GQA_KV_HEADS, GQA_HD), gv.reshape(b, -1, GQA_KV_HEADS, GQA_HD),
                   s_fin.reshape(b, 2, SSD_HEADS, SSD_P, SSD_N))
    return y, new_ctx


def kernel(x_prompt, x_sample, cache_mla_ckv, cache_mla_krope, cache_gqa_k, cache_gqa_v, state_ssd, c, c_ctx, w_mod, b_mod, norm_g, w_in, w_out, mla_q_norm_g, mla_w_uq, mla_kv_norm_g, mla_w_ukv, mla_q_head_g, mla_k_head_g, ssd_conv_w, ssd_conv_b, ssd_dt_bias, ssd_a_log, ssd_d, ssd_norm_g, cf_conv_w, cf_conv_b, cf_ln_g, cf_ln_b, gqa_q_g, gqa_k_g):
    depth = w_in.shape[0]
    dec_b, n_lat = x_sample.shape[0], x_sample.shape[1]
    seq = x_prompt.shape[1]
    past = cache_mla_ckv.shape[2]
    assert dec_b < 8 and x_prompt.shape[-1] == D_MODEL and w_in.shape[-1] == sum(SPLITS)

    cvec = jnp.concatenate([c, c_ctx[None, :], jnp.zeros((8 - dec_b - 1, D_MODEL), F32)], axis=0)
    mod = _modulation(cvec, w_mod, b_mod).reshape(depth, 8, 3, D_MODEL)
    w = _prepare_weights(w_in, w_out, norm_g, mla_q_norm_g, mla_w_uq, mla_kv_norm_g, mla_w_ukv, mla_q_head_g,
                         mla_k_head_g, gqa_q_g, gqa_k_g, ssd_conv_w, ssd_conv_b, ssd_dt_bias, ssd_a_log, ssd_d,
                         ssd_norm_g, cf_conv_w, cf_conv_b, cf_ln_g, cf_ln_b)

    y = x_prompt
    ctx_out = []
    for l in range(depth):
        y, new_ctx = _mixer_layer(y, mod, dec_b, l, w, None, None, 512)
        ctx_out.append(new_ctx)
    new = [jnp.stack([t[i] for t in ctx_out], axis=1) for i in range(5)]

    caches = dict(ckv=cache_mla_ckv,
                  kr_pad=_pad_last(cache_mla_krope, MLA_NOPE, LANES - MLA_QK),
                  gk=cache_gqa_k.reshape(dec_b, depth, past, GQA_KV_HEADS * GQA_HD),
                  gv=cache_gqa_v.reshape(dec_b, depth, past, GQA_KV_HEADS * GQA_HD),
                  state=state_ssd.reshape(dec_b, depth, 2, SSD_GROUPS, 2 * SSD_P, SSD_N))
    rope_d = _rope_table(n_lat, GQA_HD, 0)[:, :, 0:GQA_HD]
    ropes = (_rope_table(n_lat, MLA_ROPE, MLA_NOPE), jnp.concatenate([rope_d, rope_d], axis=-1))
    z = x_sample
    for l in range(depth):
        z, _ = _mixer_layer(z, mod, None, l, w, caches, ropes, 512)

    return (y, z, new[0], new[1], new[2], new[3], new[4])
```

```python
import functools
import math

import jax
import jax.numpy as jnp
from jax import lax
from jax.experimental import pallas as pl
from jax.experimental.pallas import tpu as pltpu

F32 = jnp.float32
BF16 = jnp.bfloat16

LANES = 128
VMEM_CAP = 56 * 1024 * 1024

EPS = 1e-6
ROPE_THETA = 10000.0
GRID_W = 64
LOG2E = math.log2(math.e)

D_MODEL = 1024
BRANCH_W = 256
MLA_HEADS, MLA_NOPE, MLA_ROPE, MLA_V = 4, 64, 32, 64
MLA_QK = MLA_NOPE + MLA_ROPE
MLA_Q_RANK, MLA_KV_RANK = 192, 128
SSD_HEADS, SSD_P, SSD_N, SSD_GROUPS, SSD_CONV, SSD_CHUNK = 4, 64, 64, 2, 3, 128
SSD_W = SSD_HEADS * SSD_P
SSD_CONV_CH = SSD_W + 2 * SSD_GROUPS * SSD_N
CF_W, CF_K = 256, 31
GQA_HEADS, GQA_KV_HEADS, GQA_HD = 4, 2, 64
SPLITS = (MLA_Q_RANK, MLA_KV_RANK, MLA_ROPE, MLA_HEADS * MLA_V,
          SSD_CONV_CH, 2 * SSD_HEADS, SSD_W,
          CF_W, CF_W, CF_W,
          GQA_HEADS * GQA_HD, GQA_KV_HEADS * GQA_HD, GQA_KV_HEADS * GQA_HD, GQA_HEADS * GQA_HD)

O_QL, O_KV, O_R, O_AG = 0, 256, 384, 512
O_XBC, O_DT, O_Z = 768, 1280, 1408
O_CV, O_CGL, O_CG = 1664, 1920, 2176
O_DQ, O_DK, O_DV, O_DG = 2432, 2688, 2816, 2944
IN_WP = 3200

CONV_HALO_SSD = 8
CONV_HALO_CF = 16
CONV_TL = 512

SSD_UNROLL = 4
INPROJ_SUB = 256
ATT_TQ = 1024
ATT_VROWS = 80
ATT_SUB = 512
ATT_QK_CHUNK = 512
ATT_PV_CHUNK = 256

NT_DIMS = (((1,), (1,)), ((), ()))


def _silu(x):
    return x / (1.0 + jnp.exp(-x))


def _sigmoid(x):
    return 1.0 / (1.0 + jnp.exp(-x))


def _softplus(x):
    return jnp.maximum(x, 0.0) + jnp.log1p(jnp.exp(-jnp.abs(x)))


def _rms(x, n):
    return x * lax.rsqrt(jnp.sum(x * x, axis=-1, keepdims=True) * (1.0 / n) + EPS)


def _dot(a, b):
    return jnp.dot(a, b, preferred_element_type=F32)


def _dot_nt(a, b):
    return lax.dot_general(a, b, NT_DIMS, preferred_element_type=F32)


def _params(sem, vmem_bytes):
    assert vmem_bytes <= VMEM_CAP
    return pltpu.CompilerParams(dimension_semantics=sem, vmem_limit_bytes=VMEM_CAP)


def _pad_heads(w, nh, d, dp):
    s = w.shape[:-1]
    w = w.reshape(s + (nh, d))
    w = jnp.pad(w, [(0, 0)] * len(s) + [(0, 0), (0, dp - d)])
    return w.reshape(s + (nh * dp,))


def _pad_last(w, lo, hi):
    return jnp.pad(w, [(0, 0)] * (w.ndim - 1) + [(lo, hi)])


def _layer_spec(a, l, grid_rank):
    zeros = (0,) * (a.ndim - 1)
    if grid_rank == 1:
        return pl.BlockSpec((1,) + a.shape[1:], lambda bi: (l,) + zeros)
    return pl.BlockSpec((1,) + a.shape[1:], lambda bi, i: (l,) + zeros)


def _with_ones_lane(v, hd):
    lane = lax.broadcasted_iota(jnp.int32, v.shape, v.ndim - 1)
    return jnp.where(lane % LANES == hd, 1.0, v)


def _mod_kernel(c_ref, w_ref, b_ref, o_ref):
    c = c_ref[...]
    o_ref[0] = _dot(_silu(c).astype(BF16), w_ref[0].astype(BF16)) + b_ref[0]


def _modulation(cvec, w_mod, b_mod):
    depth, d, d3 = w_mod.shape
    tn = d
    return pl.pallas_call(
        _mod_kernel,
        out_shape=jax.ShapeDtypeStruct((depth, 8, d3), F32),
        grid=(depth, d3 // tn),
        in_specs=[pl.BlockSpec((8, d), lambda l, j: (0, 0)),
                  pl.BlockSpec((1, d, tn), lambda l, j: (l, 0, j)),
                  pl.BlockSpec((1, 1, tn), lambda l, j: (l, 0, j))],
        out_specs=pl.BlockSpec((1, 8, tn), lambda l, j: (l, 0, j)),
        compiler_params=_params(("arbitrary", "arbitrary"), 4 * d * tn * 4),
        name="modulation",
    )(cvec, w_mod, b_mod.reshape(depth, 1, d3))


def _rope(x, tab_ref, rows, shift):
    cos, sin_a, sin_b = tab_ref[0, rows, :], tab_ref[1, rows, :], tab_ref[2, rows, :]
    outs = []
    for h in range(x.shape[-1] // LANES):
        xh = x[:, h * LANES:(h + 1) * LANES]
        up = pltpu.roll(xh, LANES - shift, 1)
        dn = pltpu.roll(xh, shift, 1)
        outs.append(xh * cos + up * sin_a + dn * sin_b)
    return jnp.concatenate(outs, axis=-1)


def _block_ones(width, block):
    ri = lax.broadcasted_iota(jnp.int32, (width, width), 0) // block
    ci = lax.broadcasted_iota(jnp.int32, (width, width), 1) // block
    return (ri == ci).astype(F32).astype(BF16)


def _head_norm(x, gain, n, ones_bd):
    xx = (x * x).astype(BF16)
    w2 = ones_bd.shape[0]
    ss = jnp.concatenate([_dot(xx[:, j:j + w2], ones_bd) for j in range(0, x.shape[-1], w2)], axis=-1)
    return x * lax.rsqrt(ss * (1.0 / n) + EPS) * gain


def _spread_heads(x, hd):
    lo = lax.broadcasted_iota(jnp.int32, (x.shape[0], LANES), 1) < hd
    outs = []
    for j in range(0, x.shape[-1], LANES):
        pair = x[:, j:j + LANES]
        outs += [jnp.where(lo, pair, 0.0), jnp.where(lo, pltpu.roll(pair, LANES - hd, 1), 0.0)]
    return jnp.concatenate(outs, axis=-1)


def _inproj_refs(latent, refs):
    names = ['x', 'mod', 'ng', 'win', 'wuq', 'wukv', 'gains']
    if latent:
        names += ['wvta', 'wvtd', 'ra', 'rd']
    names += ['qa', 'ka', 'va', 'ag', 'qd', 'kd', 'vd', 'dg', 'xbc', 'dt', 'z', 'glu', 'cg']
    if not latent:
        names += ['ckv', 'kr', 'gk', 'gv']
    return dict(zip(names, refs))


def _inproj_main(r, rows):
    x = r['x'][0, rows, :]
    shift = r['mod'][0, 0, 0:1, :]
    scale = r['mod'][0, 0, 1:2, :]
    h = (_rms(x, D_MODEL) * r['ng'][0] * (1.0 + scale) + shift).astype(BF16)
    return h, _dot(h, r['win'][0])


def _values_t(w_t, x):
    vt = _dot_nt(w_t, x)
    row = lax.broadcasted_iota(jnp.int32, vt.shape, 0)
    return jnp.where(row % ATT_VROWS == MLA_V, 1.0, vt)


def _inproj_latents(r, u):
    g_ql = r['gains'][0, 0:1, 0:256]
    g_kv = r['gains'][0, 2:3, 0:LANES]
    ql = _rms(u[:, O_QL:O_QL + 256], MLA_Q_RANK) * g_ql
    ckv = _rms(u[:, O_KV:O_KV + LANES], MLA_KV_RANK) * g_kv
    return _dot(ql.astype(BF16), r['wuq'][0]), ckv, _dot(ckv.astype(BF16), r['wukv'][0])


def _inproj_tail(latent, r, rows, h, u, q, ckv, kv):
    wa = MLA_HEADS * LANES
    wq, wk = GQA_HEADS * GQA_HD, GQA_KV_HEADS * GQA_HD
    g_qh = r['gains'][0, 1:2, :]
    g_kh = r['gains'][0, 3:4, :]
    g_dq = r['gains'][0, 4:5, 0:wq]
    g_dk = r['gains'][0, 5:6, 0:wk]
    ones_slot = _block_ones(2 * LANES, LANES)
    ones_pair = _block_ones(2 * LANES, GQA_HD)

    q = _head_norm(q, g_qh, MLA_QK, ones_slot)
    kr = u[:, O_R:O_R + LANES]
    k = _head_norm(kv[:, 0:wa] + jnp.concatenate([kr] * MLA_HEADS, axis=-1), g_kh, MLA_QK, ones_slot)
    qd = _head_norm(u[:, O_DQ:O_DQ + wq], g_dq, GQA_HD, ones_pair)
    kd = _head_norm(u[:, O_DK:O_DK + wk], g_dk, GQA_HD, ones_pair[0:wk, 0:wk])
    vd = u[:, O_DV:O_DV + wk]
    if latent:
        q = _rope(q, r['ra'], rows, MLA_ROPE // 4)
        k = _rope(k, r['ra'], rows, MLA_ROPE // 4)
        qd = _rope(qd, r['rd'], rows, GQA_HD // 4)
        kd_att = _rope(kd, r['rd'], rows, GQA_HD // 4)
    else:
        kd_att = kd
        r['ckv'][0, rows, :] = ckv
        r['kr'][0, rows, :] = kr[:, MLA_NOPE:MLA_NOPE + MLA_ROPE]
        r['gk'][0, rows, :] = kd
        r['gv'][0, rows, :] = vd
    qd = _spread_heads(qd, GQA_HD)
    kd_att = _spread_heads(kd_att, GQA_HD)
    r['qa'][0, rows, :] = q.astype(BF16)
    r['ka'][0, rows, :] = k.astype(BF16)
    if latent:
        r['va'][0, :, rows] = _values_t(r['wvta'][0], ckv.astype(BF16)).astype(BF16)
        r['vd'][0, :, rows] = _values_t(r['wvtd'][0], h).astype(BF16)
    else:
        r['va'][0, rows, :] = _with_ones_lane(kv[:, wa:], MLA_V).astype(BF16)
        r['vd'][0, rows, :] = _with_ones_lane(_spread_heads(vd, GQA_HD), GQA_HD).astype(BF16)
    r['ag'][0, rows, :] = u[:, O_AG:O_AG + BRANCH_W]
    r['qd'][0, rows, :] = qd.astype(BF16)
    r['kd'][0, rows, :] = kd_att.astype(BF16)
    r['dg'][0, rows, :] = u[:, O_DG:O_DG + BRANCH_W]

    r['xbc'][0, rows, :] = u[:, O_XBC:O_XBC + SSD_CONV_CH]
    r['dt'][0, rows, :] = u[:, O_DT:O_DT + LANES]
    r['z'][0, rows, :] = u[:, O_Z:O_Z + SSD_W]
    r['glu'][0, rows, :] = u[:, O_CV:O_CV + CF_W] * _sigmoid(u[:, O_CGL:O_CGL + CF_W])
    r['cg'][0, rows, :] = u[:, O_CG:O_CG + CF_W]


def _inproj_kernel(latent, sub, *refs):
    r = _inproj_refs(latent, refs)
    tm = r['x'].shape[1]
    for r0 in range(0, tm, sub):
        rows = slice(r0, r0 + sub)
        h, u = _inproj_main(r, rows)
        q, ckv, kv = _inproj_latents(r, u)
        _inproj_tail(latent, r, rows, h, u, q, ckv, kv)


def _mod_spec(mod, l, row):
    d = mod.shape[-1]
    if row is None:
        return pl.BlockSpec((1, 1, 3, d), lambda bi, i: (l, bi, 0, 0))
    return pl.BlockSpec((1, 1, 3, d), lambda bi, i: (l, row, 0, 0))


def _inproj(x, mod, mod_row, l, w, ropes, tm):
    b, s, d = x.shape
    latent = ropes is not None
    tok = lambda wd: pl.BlockSpec((1, tm, wd), lambda bi, i: (bi, i, 0))
    weights = [w['ng'], w['win'], w['wuq'], w['wukv'], w['gains']]
    if latent:
        weights += [w['wvta'], w['wvtd']]
    in_specs = [tok(d), _mod_spec(mod, l, mod_row)] + [_layer_spec(a, l, 2) for a in weights]
    args = [x, mod] + weights
    if latent:
        in_specs += [pl.BlockSpec((3, tm, LANES), lambda bi, i: (0, i, 0))] * 2
        args += list(ropes)
    widths = [(4 * LANES, BF16), (4 * LANES, BF16), (4 * LANES, BF16), (BRANCH_W, F32),
              (4 * LANES, BF16), (2 * LANES, BF16), (2 * LANES, BF16), (BRANCH_W, F32),
              (SSD_CONV_CH, F32), (LANES, F32), (SSD_W, F32), (CF_W, F32), (CF_W, F32)]
    if not latent:
        widths += [(MLA_KV_RANK, F32), (MLA_ROPE, F32), (LANES, F32), (LANES, F32)]
    out_shape = [jax.ShapeDtypeStruct((b, s, wd), dt) for wd, dt in widths]
    out_specs = [tok(wd) for wd, _ in widths]
    if latent:
        for pos, heads in ((2, MLA_HEADS), (6, GQA_KV_HEADS)):
            out_shape[pos] = jax.ShapeDtypeStruct((b, heads * ATT_VROWS, s), BF16)
            out_specs[pos] = pl.BlockSpec((1, heads * ATT_VROWS, tm), lambda bi, i: (bi, 0, i))
    out_bytes = sum(wd * jnp.dtype(dt).itemsize for wd, dt in widths) * tm
    w_bytes = sum(a[0].size * a.dtype.itemsize for a in weights)
    sub = min(INPROJ_SUB, tm)
    vmem = 2 * (tm * d * 4 + w_bytes + out_bytes) + 3 * sub * IN_WP * 4
    return pl.pallas_call(
        functools.partial(_inproj_kernel, latent, sub),
        out_shape=out_shape,
        grid=(b, s // tm),
        in_specs=in_specs,
        out_specs=out_specs,
        compiler_params=_params(("parallel", "parallel"), vmem),
        name="inproj_lat" if latent else "inproj_ctx",
    )(*args)


def _ctxkv_kernel(ckv_ref, kr_ref, gk_ref, gv_ref, wukv_ref, wvta_ref, gains_ref, kc_ref, vc_ref, kdc_ref, vdc_ref):
    ckv = ckv_ref[0, 0].astype(BF16)
    kv = _dot(ckv, wukv_ref[0])
    kr = kr_ref[0, 0]
    g = gains_ref[0, 3:4, 0:LANES]
    for i in range(MLA_HEADS):
        k = _rms(kv[:, i * LANES:(i + 1) * LANES] + kr, MLA_QK) * g
        kc_ref[0, :, i * LANES:(i + 1) * LANES] = k.astype(BF16)
    vc_ref[0] = _values_t(wvta_ref[0], ckv).astype(BF16)
    gk = gk_ref[0, 0]
    p = gk.shape[0]
    pad = jnp.zeros((p, LANES - GQA_HD), BF16)
    for i in range(GQA_KV_HEADS):
        lo = i * LANES
        kdc_ref[0, :, lo:lo + GQA_HD] = gk[:, i * GQA_HD:(i + 1) * GQA_HD].astype(BF16)
        kdc_ref[0, :, lo + GQA_HD:lo + LANES] = pad
    gvt = gv_ref[0, 0].T
    row = lax.broadcasted_iota(jnp.int32, (ATT_VROWS - GQA_HD, p), 0)
    tail = jnp.where(row == 0, 1.0, 0.0)
    pieces = []
    for i in range(GQA_KV_HEADS):
        pieces += [gvt[i * GQA_HD:(i + 1) * GQA_HD, :], tail]
    vdc_ref[0] = jnp.concatenate(pieces, axis=0).astype(BF16)


def _ctx_kv(ckv, kr_pad, gk, gv, l, w):
    b, _, p, _ = ckv.shape
    cache = lambda a: pl.BlockSpec((1, 1, p, a.shape[-1]), lambda bi: (bi, l, 0, 0))
    out = lambda wd: pl.BlockSpec((1, p, wd), lambda bi: (bi, 0, 0))
    out_t = lambda rows: pl.BlockSpec((1, rows, p), lambda bi: (bi, 0, 0))
    wa, wd = MLA_HEADS * LANES, GQA_KV_HEADS * LANES
    ra, rd = MLA_HEADS * ATT_VROWS, GQA_KV_HEADS * ATT_VROWS
    return pl.pallas_call(
        _ctxkv_kernel,
        out_shape=[jax.ShapeDtypeStruct((b, p, wa), BF16), jax.ShapeDtypeStruct((b, ra, p), BF16),
                   jax.ShapeDtypeStruct((b, p, wd), BF16), jax.ShapeDtypeStruct((b, rd, p), BF16)],
        grid=(b,),
        in_specs=[cache(ckv), cache(kr_pad), cache(gk), cache(gv),
                  _layer_spec(w['wukv'], l, 1), _layer_spec(w['wvta'], l, 1), _layer_spec(w['gains'], l, 1)],
        out_specs=[out(wa), out_t(ra), out(wd), out_t(rd)],
        compiler_params=_params(("parallel",), 16 * p * wa * 4),
        name="ctx_kv",
    )(ckv, kr_pad, gk, gv, w['wukv'], w['wvta'], w['gains'])


def _lane_tile_max(t):
    r = t[:, 0:LANES]
    for j in range(1, t.shape[-1] // LANES):
        r = jnp.maximum(r, t[:, j * LANES:(j + 1) * LANES])
    return r


def _attn_unit(n_heads, rep, refs, bi, rows, h, s_ref):
    q_ref, k_ref, v_ref, g_ref, o_ref = refs
    s_len = k_ref.shape[1]
    hd = o_ref.shape[-1] // n_heads
    kb = min(ATT_QK_CHUNK, s_len)
    kb2 = ATT_PV_CHUNK
    g = h // rep
    hs = slice(h * LANES, (h + 1) * LANES)
    gs = slice(g * LANES, (g + 1) * LANES)
    qh = q_ref[bi, rows, hs]
    mx = None
    for c in range(s_len // kb):
        sc = _dot_nt(qh, k_ref[bi, c * kb:(c + 1) * kb, gs])
        s_ref[:, c * kb:(c + 1) * kb] = sc
        t = _lane_tile_max(sc)
        mx = t if mx is None else jnp.maximum(mx, t)
    m = jnp.max(mx, axis=-1, keepdims=True)
    acc = None
    for c in range(s_len // kb2):
        r0 = c * kb2
        p = jnp.exp2(s_ref[:, r0:r0 + kb2] - m).astype(BF16)
        part = _dot(p, v_ref[bi, r0:r0 + kb2, gs])
        acc = part if acc is None else acc + part
    o = acc[:, 0:hd] / acc[:, hd:hd + 1]
    os_ = slice(h * hd, (h + 1) * hd)
    o_ref[bi, rows, os_] = (o * _silu(g_ref[bi, rows, os_])).astype(o_ref.dtype)


def _attn_kernel(n_heads, rep, sub, *refs):
    s0_ref, s1_ref = refs[-2:]
    tq = refs[0].shape[1]
    unit = 0
    for r0 in range(0, tq, sub):
        for h in range(n_heads):
            _attn_unit(n_heads, rep, refs[:-2], 0, slice(r0, r0 + sub), h, s0_ref if unit % 2 == 0 else s1_ref)
            unit += 1


def _attention(q, k, v, gate, n_heads, rep, tq, sub, name):
    b, s, qw = q.shape
    kw = k.shape[-1]
    ow = gate.shape[-1]
    assert s % min(ATT_QK_CHUNK, s) == 0 and s % ATT_PV_CHUNK == 0 and s % tq == 0 and tq % sub == 0
    tile = lambda wd: pl.BlockSpec((1, tq, wd), lambda bi, i: (bi, i, 0))
    whole = pl.BlockSpec((1, s, kw), lambda bi, i: (bi, 0, 0))
    vmem = 2 * (tq * qw * 2 + 2 * s * kw * 2 + tq * ow * 6) + 2 * sub * s * 4 + 2 * sub * 1024 * 4
    return pl.pallas_call(
        functools.partial(_attn_kernel, n_heads, rep, sub),
        out_shape=jax.ShapeDtypeStruct((b, s, ow), BF16),
        grid=(b, s // tq),
        in_specs=[tile(qw), whole, whole, tile(ow)],
        out_specs=tile(ow),
        scratch_shapes=[pltpu.VMEM((sub, s), F32), pltpu.VMEM((sub, s), F32)],
        compiler_params=_params(("parallel", "arbitrary"), vmem),
        name=name,
    )(q, k, v, gate)


def _attn_t_scores(io, rep, sub, unit, s_ref):
    q_ref, k_ref, _, kc_ref = io[:4]
    r0, h = unit
    s_len, p_len = k_ref.shape[1], kc_ref.shape[1]
    kb = min(ATT_QK_CHUNK, s_len)
    gs = slice((h // rep) * LANES, (h // rep + 1) * LANES)
    qh = q_ref[0, r0:r0 + sub, h * LANES:(h + 1) * LANES]
    mx = None
    for c in range(s_len // kb):
        st = _dot_nt(k_ref[0, c * kb:(c + 1) * kb, gs], qh)
        s_ref[c * kb:(c + 1) * kb, :] = st
        t = jnp.max(st, axis=0, keepdims=True)
        mx = t if mx is None else jnp.maximum(mx, t)
    st = _dot_nt(kc_ref[0, :, gs], qh)
    s_ref[s_len:s_len + p_len, :] = st
    return jnp.maximum(mx, jnp.max(st, axis=0, keepdims=True))


def _attn_t_values(io, n_heads, rep, sub, unit, s_ref, m, ot_ref):
    _, k_ref, vt_ref, kc_ref, vct_ref, _, o_ref = io
    r0, h = unit
    s_len, p_len = k_ref.shape[1], kc_ref.shape[1]
    hd = o_ref.shape[-1] // n_heads
    vs = slice((h // rep) * ATT_VROWS, (h // rep + 1) * ATT_VROWS)
    kb2 = ATT_PV_CHUNK
    acc = None
    for c in range((s_len + p_len) // kb2):
        k0 = c * kb2
        pt = jnp.exp2(s_ref[k0:k0 + kb2, :] - m).astype(BF16)
        vv = vt_ref[0, vs, k0:k0 + kb2] if k0 < s_len else vct_ref[0, vs, k0 - s_len:k0 - s_len + kb2]
        part = _dot(vv, pt)
        acc = part if acc is None else acc + part
    ot_ref[h * hd:(h + 1) * hd, r0:r0 + sub] = acc[0:hd, :] / acc[hd:hd + 1, :]


def _attn_t_kernel(n_heads, rep, sub, *refs):
    io, (s0_ref, s1_ref, ot_ref) = refs[:-3], refs[-3:]
    tq = io[0].shape[1]
    units = [(r0, h) for r0 in range(0, tq, sub) for h in range(n_heads)]
    bufs = (s0_ref, s1_ref)
    m = _attn_t_scores(io, rep, sub, units[0], bufs[0])
    for u in range(len(units)):
        m_next = None
        if u + 1 < len(units):
            m_next = _attn_t_scores(io, rep, sub, units[u + 1], bufs[(u + 1) % 2])
        _attn_t_values(io, n_heads, rep, sub, units[u], bufs[u % 2], m, ot_ref)
        m = m_next
    g_ref, o_ref = io[5], io[6]
    o_ref[0] = (ot_ref[...].T * _silu(g_ref[0])).astype(o_ref.dtype)


def _attention_t(q, k, vt, ctx, gate, n_heads, rep, tq, sub, name):
    b, s, qw = q.shape
    kw = k.shape[-1]
    vr = vt.shape[1]
    ow = gate.shape[-1]
    kc, vct = ctx
    p = kc.shape[1]
    assert s % min(ATT_QK_CHUNK, s) == 0 and s % ATT_PV_CHUNK == 0 and p % ATT_PV_CHUNK == 0
    assert s % tq == 0 and tq % sub == 0
    tile = lambda wd: pl.BlockSpec((1, tq, wd), lambda bi, i: (bi, i, 0))
    whole = lambda a: pl.BlockSpec((1,) + a.shape[1:], lambda bi, i: (bi, 0, 0))
    vmem = (2 * (tq * qw * 2 + (s + p) * (kw + vr) * 2 + tq * ow * 6)
            + 2 * sub * (s + p) * 4 + ow * tq * 4 + 2 * sub * 1024 * 4)
    return pl.pallas_call(
        functools.partial(_attn_t_kernel, n_heads, rep, sub),
        out_shape=jax.ShapeDtypeStruct((b, s, ow), BF16),
        grid=(b, s // tq),
        in_specs=[tile(qw), whole(k), whole(vt), whole(kc), whole(vct), tile(ow)],
        out_specs=tile(ow),
        scratch_shapes=[pltpu.VMEM((s + p, sub), F32), pltpu.VMEM((s + p, sub), F32), pltpu.VMEM((ow, tq), F32)],
        compiler_params=_params(("parallel", "arbitrary"), vmem),
        name=name,
    )(q, k, vt, kc, vct, gate)


def _conv_kernel(tl, xc_ref, xp_ref, xn_ref, gc_ref, gp_ref, gn_ref, cg_ref,
                 w3_ref, b3_ref, w31_ref, b31_ref, lng_ref, lnb_ref,
                 xo_ref, co_ref, ext3, ext31):
    i = pl.program_id(1)
    keep_prev = (i > 0).astype(F32)
    keep_next = (i < pl.num_programs(1) - 1).astype(F32)

    h3 = CONV_HALO_SSD
    ext3[0:h3, :] = xp_ref[0] * keep_prev
    ext3[h3:h3 + tl, :] = xc_ref[0]
    ext3[h3 + tl:2 * h3 + tl, :] = xn_ref[0] * keep_next
    acc = jnp.zeros((tl, SSD_CONV_CH), F32) + b3_ref[0]
    for k in range(SSD_CONV):
        acc = acc + w3_ref[0, k:k + 1, :] * ext3[h3 - SSD_CONV // 2 + k:h3 - SSD_CONV // 2 + k + tl, :]
    xo_ref[0] = _silu(acc)

    h31 = CONV_HALO_CF
    ext31[0:h31, :] = gp_ref[0] * keep_prev
    ext31[h31:h31 + tl, :] = gc_ref[0]
    ext31[h31 + tl:2 * h31 + tl, :] = gn_ref[0] * keep_next
    acc = jnp.zeros((tl, CF_W), F32) + b31_ref[0]
    o_lo = h31 - CF_K // 2
    ext = ext31[...]
    n_ext = ext.shape[0]
    for r in range(8):
        taps = [k for k in range(CF_K) if (o_lo + k) % 8 == r]
        if not taps:
            continue
        shifted = ext if r == 0 else pltpu.roll(ext, n_ext - r, 0)
        for k in taps:
            a = (o_lo + k) // 8 * 8
            acc = acc + w31_ref[0, k:k + 1, :] * shifted[a:a + tl, :]
    mu = jnp.mean(acc, axis=-1, keepdims=True)
    cen = acc - mu
    var = jnp.mean(cen * cen, axis=-1, keepdims=True)
    y = cen * lax.rsqrt(var + EPS) * lng_ref[0] + lnb_ref[0]
    co_ref[0] = (_silu(y) * _silu(cg_ref[0])).astype(co_ref.dtype)


def _convs(xbc, glu, cg, l, w, tl):
    b, s, _ = xbc.shape
    nt = s // tl

    def cur(wd):
        return pl.BlockSpec((1, tl, wd), lambda bi, i: (bi, i, 0))

    def prev(wd, hrows):
        r = tl // hrows
        return pl.BlockSpec((1, hrows, wd), lambda bi, i: (bi, jnp.maximum(i * r - 1, 0), 0))

    def nxt(wd, hrows):
        r = tl // hrows
        return pl.BlockSpec((1, hrows, wd), lambda bi, i: (bi, jnp.minimum((i + 1) * r, nt * r - 1), 0))

    weights = [w['w3'], w['b3'], w['w31'], w['b31'], w['lng'], w['lnb']]
    vmem = 4 * tl * (SSD_CONV_CH * 2 + CF_W * 3) * 4 + 6 * tl * SSD_CONV_CH * 4
    return pl.pallas_call(
        functools.partial(_conv_kernel, tl),
        out_shape=[jax.ShapeDtypeStruct((b, s, SSD_CONV_CH), F32),
                   jax.ShapeDtypeStruct((b, s, CF_W), BF16)],
        grid=(b, nt),
        in_specs=[cur(SSD_CONV_CH), prev(SSD_CONV_CH, CONV_HALO_SSD), nxt(SSD_CONV_CH, CONV_HALO_SSD),
                  cur(CF_W), prev(CF_W, CONV_HALO_CF), nxt(CF_W, CONV_HALO_CF), cur(CF_W)]
                 + [_layer_spec(a, l, 2) for a in weights],
        out_specs=[cur(SSD_CONV_CH), cur(CF_W)],
        scratch_shapes=[pltpu.VMEM((tl + 2 * CONV_HALO_SSD, SSD_CONV_CH), F32),
                        pltpu.VMEM((tl + 2 * CONV_HALO_CF, CF_W), F32)],
        compiler_params=_params(("parallel", "parallel"), vmem),
        name="convs",
    )(xbc, xbc, xbc, glu, glu, glu, cg, *weights)


def _split3(x):
    hi = x.astype(BF16)
    r1 = x - hi.astype(F32)
    mid = r1.astype(BF16)
    lo = (r1 - mid.astype(F32)).astype(BF16)
    return hi, mid, lo


def _cumsum_rows(tri, x):
    return sum(_dot(tri, part) for part in _split3(x))


def _transpose_exact(x):
    c = x.shape[1]
    eye = (lax.broadcasted_iota(jnp.int32, (c, c), 0) == lax.broadcasted_iota(jnp.int32, (c, c), 1))
    eye = eye.astype(F32).astype(BF16)
    return sum(_dot_nt(eye, part) for part in _split3(x))


SSD_NK = 2 * SSD_HEADS


def _ssd_stage_local(c, d, xc_ref, dt_ref, bias, a_row):
    q = SSD_CHUNK
    rows = pl.ds(pl.multiple_of(c * q, q), q)
    t = dict(rows=rows, d=d)
    t['xs'] = xc_ref[0, rows, 0:SSD_W]
    t['bm'] = xc_ref[0, rows, SSD_W:SSD_W + LANES]
    t['cm'] = xc_ref[0, rows, SSD_W + LANES:SSD_W + 2 * LANES].astype(BF16)
    t['dt'] = _softplus(dt_ref[0, rows, :] + bias)
    ri = lax.broadcasted_iota(jnp.int32, (q, q), 0)
    ci = lax.broadcasted_iota(jnp.int32, (q, q), 1)
    t['mask'] = (ri >= ci) if d == 0 else (ri <= ci)
    t['acum'] = _cumsum_rows(t['mask'].astype(F32).astype(BF16), t['dt'] * a_row)
    return t


def _ssd_stage_cb(t):
    q = SSD_CHUNK
    lane = lax.broadcasted_iota(jnp.int32, (q, LANES), 1)
    t['tr'] = jnp.where(lane < SSD_NK, t['dt'], t['acum']).T
    bt = t['bm'].T
    t['bt'] = bt
    grp_row = lax.broadcasted_iota(jnp.int32, (LANES, q), 0) < SSD_N
    t['cb'] = [_dot(t['cm'], jnp.where(grp_row if g == 0 else jnp.logical_not(grp_row), bt, 0.0).astype(BF16))
               for g in range(SSD_GROUPS)]
    t['last'] = q - 1 if t['d'] == 0 else 0
    t['etot'] = jnp.exp(t['acum'][t['last']:t['last'] + 1, :])
    t['eac'] = jnp.exp(t['acum'])


def _ssd_stage_diag(t, dskip):
    q = SSD_CHUNK
    d = t['d']
    lo_half = lax.broadcasted_iota(jnp.int32, (q, LANES), 1) < SSD_P
    tr, acum, last = t['tr'], t['acum'], t['last']
    t['yd'], t['cs'] = [], []
    for g in range(SSD_GROUPS):
        k0 = d * SSD_HEADS + 2 * g
        xg = t['xs'][:, g * LANES:(g + 1) * LANES]
        xgb = xg.astype(BF16)
        btg = t['bt'][g * SSD_N:(g + 1) * SSD_N, :]
        yg = None
        cs = []
        for hh in range(2):
            k = k0 + hh
            dt_row = tr[k:k + 1, :]
            ac_row = tr[SSD_NK + k:SSD_NK + k + 1, :]
            dec = jnp.where(t['mask'], jnp.exp(acum[:, k:k + 1] - ac_row), 0.0)
            xm = jnp.where(lo_half if hh == 0 else jnp.logical_not(lo_half), xg, 0.0).astype(BF16)
            part = _dot((t['cb'][g] * dec * dt_row).astype(BF16), xm)
            yg = part if yg is None else yg + part
            w_row = dt_row * jnp.exp(tr[SSD_NK + k:SSD_NK + k + 1, last:last + 1] - ac_row)
            cs.append(_dot((btg * w_row).astype(BF16), xgb))
        if d == 0:
            yg = yg + dskip[:, g * LANES:(g + 1) * LANES] * xg
        t['yd'].append(yg)
        t['cs'].append(cs)


def _ssd_stage_state(t, states):
    q = SSD_CHUNK
    d = t['d']
    lo_half = lax.broadcasted_iota(jnp.int32, (q, LANES), 1) < SSD_P
    lane_n = lax.broadcasted_iota(jnp.int32, (SSD_N, LANES), 1) < SSD_P
    zero = jnp.zeros((SSD_N, LANES), F32)
    ys, new_states = [], []
    for g in range(SSD_GROUPS):
        k0 = d * SSD_HEADS + 2 * g
        s_prev = states[g]
        sz = jnp.concatenate([s_prev, zero] if g == 0 else [zero, s_prev], axis=0).astype(BF16)
        eacs = jnp.where(lo_half, t['eac'][:, k0:k0 + 1], t['eac'][:, k0 + 1:k0 + 2])
        ys.append(t['yd'][g] + _dot(t['cm'], sz) * eacs)
        cdec = jnp.where(lane_n, t['etot'][:, k0:k0 + 1], t['etot'][:, k0 + 1:k0 + 2])
        new_states.append(cdec * s_prev + jnp.where(lane_n, t['cs'][g][0], t['cs'][g][1]))
    return ys, new_states


def _ssd_kernel(nc, unroll, has_init, *refs):
    if has_init:
        xc_ref, dt_ref, s0_ref, bias_ref, alog_ref, d_ref, yf_ref, yb_ref, so_ref, st_ref = refs
        for d in range(2):
            for g in range(SSD_GROUPS):
                st_ref[0, d, g] = _transpose_exact(s0_ref[0, 0, d, g])
    else:
        xc_ref, dt_ref, bias_ref, alog_ref, d_ref, yf_ref, yb_ref, so_ref, st_ref = refs
        st_ref[...] = jnp.zeros(st_ref.shape, F32)
    lane = lax.broadcasted_iota(jnp.int32, (1, LANES), 1)
    a_row = jnp.where(lane < 2 * SSD_NK, -jnp.exp(alog_ref[0]), 0.0)
    bias = bias_ref[0]
    dskip = d_ref[0]

    def body(i, carry):
        states = [[st_ref[0, d, g] for g in range(SSD_GROUPS)] for d in range(2)]
        work = []
        for j in range(unroll):
            c = i * unroll + j
            work.append((_ssd_stage_local(c, 0, xc_ref, dt_ref, bias, a_row), yf_ref))
            work.append((_ssd_stage_local(nc - 1 - c, 1, xc_ref, dt_ref, bias, a_row), yb_ref))
        for t, _ in work:
            _ssd_stage_cb(t)
        for t, _ in work:
            _ssd_stage_diag(t, dskip)
        writes = []
        for t, y_ref in work:
            ys, states[t['d']] = _ssd_stage_state(t, states[t['d']])
            writes.append((y_ref, t['rows'], ys))
        for y_ref, rows, ys in writes:
            for g in range(SSD_GROUPS):
                y_ref[0, rows, g * LANES:(g + 1) * LANES] = ys[g]
        for d in range(2):
            for g in range(SSD_GROUPS):
                st_ref[0, d, g] = states[d][g]
        return carry

    lax.fori_loop(0, nc // unroll, body, 0)
    for d in range(2):
        for g in range(SSD_GROUPS):
            so_ref[0, d, g] = _transpose_exact(st_ref[0, d, g])


def _ssd(xc, dt, s0, l, w):
    b, s, _ = xc.shape
    nc = s // SSD_CHUNK
    unroll = min(SSD_UNROLL, nc)
    assert nc % unroll == 0
    st_shape = (b, 2, SSD_GROUPS, 2 * SSD_P, SSD_N)
    st_spec = pl.BlockSpec((1,) + st_shape[1:], lambda bi: (bi, 0, 0, 0, 0))
    y_spec = pl.BlockSpec((1, s, SSD_W), lambda bi: (bi, 0, 0))
    in_specs = [pl.BlockSpec((1, s, SSD_CONV_CH), lambda bi: (bi, 0, 0)),
                pl.BlockSpec((1, s, LANES), lambda bi: (bi, 0, 0))]
    args = [xc, dt]
    if s0 is not None:
        in_specs.append(pl.BlockSpec((1, 1) + st_shape[1:], lambda bi: (bi, l, 0, 0, 0, 0)))
        args.append(s0)
    weights = [w['dt_bias'], w['a_log'], w['dskip']]
    vmem = 2 * s * (SSD_CONV_CH + LANES + 2 * SSD_W) * 4 + 8 * 1024 * 1024
    return pl.pallas_call(
        functools.partial(_ssd_kernel, nc, unroll, s0 is not None),
        out_shape=[jax.ShapeDtypeStruct((b, s, SSD_W), F32), jax.ShapeDtypeStruct((b, s, SSD_W), F32),
                   jax.ShapeDtypeStruct(st_shape, F32)],
        grid=(b,),
        in_specs=in_specs + [_layer_spec(a, l, 1) for a in weights],
        out_specs=[y_spec, y_spec, st_spec],
        scratch_shapes=[pltpu.VMEM((1, 2, SSD_GROUPS, SSD_N, 2 * SSD_P), F32)],
        compiler_params=_params(("parallel",), vmem),
        name="ssd",
    )(*args, *weights)


def _outproj_kernel(oa_ref, yf_ref, yb_ref, z_ref, oc_ref, od_ref, x_ref, mod_ref, ng_ref, w_ref, o_ref):
    ob = _rms((yf_ref[0] + yb_ref[0]) * _silu(z_ref[0]), SSD_W) * ng_ref[0]
    acc = _dot(oa_ref[0], w_ref[0, 0:BRANCH_W, :])
    acc = acc + _dot(ob.astype(BF16), w_ref[0, BRANCH_W:2 * BRANCH_W, :])
    acc = acc + _dot(oc_ref[0], w_ref[0, 2 * BRANCH_W:3 * BRANCH_W, :])
    acc = acc + _dot(od_ref[0], w_ref[0, 3 * BRANCH_W:4 * BRANCH_W, :])
    o_ref[0] = x_ref[0] + mod_ref[0, 0, 2:3, :] * acc


def _outproj(oa, yf, yb, z, oc, od, x, mod, mod_row, l, w, tm):
    b, s, d = x.shape
    br = lambda: pl.BlockSpec((1, tm, BRANCH_W), lambda bi, i: (bi, i, 0))
    xs = pl.BlockSpec((1, tm, d), lambda bi, i: (bi, i, 0))
    vmem = 2 * (6 * tm * BRANCH_W * 4 + 2 * tm * d * 4 + d * d * 2) + 2 * tm * d * 4
    return pl.pallas_call(
        _outproj_kernel,
        out_shape=jax.ShapeDtypeStruct((b, s, d), F32),
        grid=(b, s // tm),
        in_specs=[br(), br(), br(), br(), br(), br(), xs, _mod_spec(mod, l, mod_row),
                  _layer_spec(w['ssd_ng'], l, 2), _layer_spec(w['wout'], l, 2)],
        out_specs=xs,
        compiler_params=_params(("parallel", "parallel"), vmem),
        name="outproj",
    )(oa, yf, yb, z, oc, od, x, mod, w['ssd_ng'], w['wout'])


def _rope_table(n_tokens, dim, lane0):
    n_rows = n_tokens // GRID_W
    quarter = dim // 4
    inv = ROPE_THETA ** (-jnp.arange(quarter, dtype=F32) / quarter)
    ar = jnp.arange(n_rows, dtype=F32)[:, None] * inv
    ac = jnp.arange(GRID_W, dtype=F32)[:, None] * inv
    zr, zc = jnp.zeros_like(ar), jnp.zeros_like(ac)
    hi = LANES - lane0 - dim

    def lanes(row_part, col_part):
        rp = _pad_last(jnp.concatenate(row_part, axis=-1), lane0, hi)
        cp = _pad_last(jnp.concatenate(col_part, axis=-1), lane0, hi)
        return (rp[:, None, :] + cp[None, :, :]).reshape(n_tokens, LANES)

    off_rotary = jnp.pad(jnp.zeros((1, dim), F32), ((0, 0), (lane0, hi)), constant_values=1.0)
    cos = lanes([jnp.cos(ar), jnp.cos(ar), zr, zr], [zc, zc, jnp.cos(ac), jnp.cos(ac)]) + off_rotary
    sin_a = lanes([-jnp.sin(ar), zr, zr, zr], [zc, zc, -jnp.sin(ac), zc])
    sin_b = lanes([zr, jnp.sin(ar), zr, zr], [zc, zc, zc, jnp.sin(ac)])
    return jnp.stack([cos, sin_a, sin_b])


def _prepare_weights(w_in, w_out, norm_g, mla_q_norm_g, mla_w_uq, mla_kv_norm_g, mla_w_ukv, mla_q_head_g,
                     mla_k_head_g, gqa_q_g, gqa_k_g, ssd_conv_w, ssd_conv_b, ssd_dt_bias, ssd_a_log, ssd_d,
                     ssd_norm_g, cf_conv_w, cf_conv_b, cf_ln_g, cf_ln_b):
    depth = w_in.shape[0]
    pieces = []
    o = 0
    for sz in SPLITS:
        pieces.append(w_in[:, :, o:o + sz])
        o += sz
    (a_q, a_kv, a_r, a_g, b_xbc, b_dt, b_z, c_v, c_gl, c_g, d_q, d_k, d_v, d_g) = pieces
    win = jnp.concatenate([
        _pad_last(a_q, 0, 256 - MLA_Q_RANK), a_kv, _pad_last(a_r, MLA_NOPE, LANES - MLA_QK), a_g,
        b_xbc, _pad_last(jnp.concatenate([b_dt, b_dt], axis=-1), 0, LANES - 2 * SSD_NK), b_z, c_v, c_gl, c_g,
        d_q, d_k, d_v, d_g], axis=-1).astype(BF16)
    wuq = jnp.pad(_pad_heads(mla_w_uq, MLA_HEADS, MLA_QK, LANES),
                  ((0, 0), (0, 256 - MLA_Q_RANK), (0, 0))).astype(BF16)
    kv = mla_w_ukv.reshape(depth, MLA_KV_RANK, MLA_HEADS, MLA_NOPE + MLA_V)
    wk = _pad_last(kv[..., :MLA_NOPE], 0, LANES - MLA_NOPE).reshape(depth, MLA_KV_RANK, -1)
    wv = _pad_last(kv[..., MLA_NOPE:], 0, LANES - MLA_V).reshape(depth, MLA_KV_RANK, -1)
    wukv = jnp.concatenate([wk, wv], axis=-1).astype(BF16)
    rows_t = lambda wt: jnp.pad(wt, ((0, 0), (0, 0), (0, ATT_VROWS - wt.shape[2]), (0, 0)))
    wvta = rows_t(kv[..., MLA_NOPE:].transpose(0, 2, 3, 1)).reshape(depth, MLA_HEADS * ATT_VROWS, MLA_KV_RANK)
    wvtd = rows_t(d_v.reshape(depth, -1, GQA_KV_HEADS, GQA_HD).transpose(0, 2, 3, 1))
    wvtd = wvtd.reshape(depth, GQA_KV_HEADS * ATT_VROWS, -1)
    gw = MLA_HEADS * LANES
    row = lambda v: _pad_last(v, 0, gw - v.shape[-1])
    slots = lambda v, n: row(jnp.tile(_pad_last(v, 0, LANES - v.shape[-1]), (1, n)))
    zero = jnp.zeros((depth, gw), F32)
    gains = jnp.stack([row(mla_q_norm_g),
                       slots(mla_q_head_g * (MLA_QK ** -0.5 * LOG2E), MLA_HEADS),
                       row(mla_kv_norm_g),
                       slots(mla_k_head_g, MLA_HEADS),
                       row(jnp.tile(gqa_q_g * (GQA_HD ** -0.5 * LOG2E), (1, GQA_HEADS))),
                       row(jnp.tile(gqa_k_g, (1, GQA_KV_HEADS))), zero, zero], axis=1)
    vec = lambda v: v[:, None, :]
    lanes8 = lambda v: _pad_last(jnp.tile(v.reshape(depth, 1, -1), (1, 1, 2)), 0, LANES - 2 * SSD_NK)
    return dict(win=win, wuq=wuq, wukv=wukv, wvta=wvta.astype(BF16), wvtd=wvtd.astype(BF16), gains=gains,
                wout=w_out.astype(BF16), ng=vec(norm_g),
                w3=ssd_conv_w, b3=vec(ssd_conv_b), w31=cf_conv_w, b31=vec(cf_conv_b),
                lng=vec(cf_ln_g), lnb=vec(cf_ln_b), dt_bias=lanes8(ssd_dt_bias), a_log=lanes8(ssd_a_log),
                dskip=vec(jnp.repeat(ssd_d, SSD_P, axis=-1)), ssd_ng=vec(ssd_norm_g))


def _mixer_layer(x, mod, mod_row, l, w, caches, ropes, tm):
    latent = caches is not None
    b, s, d = x.shape
    flat = (lambda a: a) if latent else (lambda a: a.reshape(1, b * s, a.shape[-1]))
    unflat = (lambda a: a) if latent else (lambda a: a.reshape(b, s, a.shape[-1]))
    outs = [unflat(o) for o in _inproj(flat(x), mod, mod_row, l, w, ropes, tm)]
    qa, ka, va, ag, qd, kd, vd, dg, xbc, dt, z, glu, cg = outs[:13]
    ctx_a = ctx_d = s0 = None
    if latent:
        kc, vc, kdc, vdc = _ctx_kv(caches['ckv'], caches['kr_pad'], caches['gk'], caches['gv'], l, w)
        ctx_a, ctx_d, s0 = (kc, vc), (kdc, vdc), caches['state']
    tq = min(ATT_TQ, s)
    sub = min(ATT_SUB, tq)
    rep = GQA_HEADS // GQA_KV_HEADS
    if latent:
        oa = _attention_t(qa, ka, va, ctx_a, ag, MLA_HEADS, 1, tq, sub, "attn_mla")
        od = _attention_t(qd, kd, vd, ctx_d, dg, GQA_HEADS, rep, tq, sub, "attn_gqa")
    else:
        oa = _attention(qa, ka, va, ag, MLA_HEADS, 1, tq, sub, "attn_mla_ctx")
        od = _attention(qd, kd, vd, dg, GQA_HEADS, rep, tq, sub, "attn_gqa_ctx")
    xconv, oc = _convs(xbc, glu, cg, l, w, min(CONV_TL, s))
    yf, yb, s_fin = _ssd(xconv, dt, s0, l, w)
    y = unflat(_outproj(flat(oa), flat(yf), flat(yb), flat(z), flat(oc), flat(od), flat(x), mod, mod_row, l, w, tm))
    new_ctx = None
    if not latent:
        ckv, kr, gk, gv = outs[13:]
        new_ctx = (ckv, kr, gk.reshape(b, -1, GQA_KV_HEADS, GQA_HD), gv.reshape(b, -1, GQA_KV_HEADS, GQA_HD),
                   s_fin.reshape(b, 2, SSD_HEADS, SSD_P, SSD_N))
    return y, new_ctx


def kernel(x_prompt, x_sample, cache_mla_ckv, cache_mla_krope, cache_gqa_k, cache_gqa_v, state_ssd, c, c_ctx, w_mod, b_mod, norm_g, w_in, w_out, mla_q_norm_g, mla_w_uq, mla_kv_norm_g, mla_w_ukv, mla_q_head_g, mla_k_head_g, ssd_conv_w, ssd_conv_b, ssd_dt_bias, ssd_a_log, ssd_d, ssd_norm_g, cf_conv_w, cf_conv_b, cf_ln_g, cf_ln_b, gqa_q_g, gqa_k_g):
    depth = w_in.shape[0]
    dec_b, n_lat = x_sample.shape[0], x_sample.shape[1]
    seq = x_prompt.shape[1]
    past = cache_mla_ckv.shape[2]
    assert dec_b < 8 and x_prompt.shape[-1] == D_MODEL and w_in.shape[-1] == sum(SPLITS)

    cvec = jnp.concatenate([c, c_ctx[None, :], jnp.zeros((8 - dec_b - 1, D_MODEL), F32)], axis=0)
    mod = _modulation(cvec, w_mod, b_mod).reshape(depth, 8, 3, D_MODEL)
    w = _prepare_weights(w_in, w_out, norm_g, mla_q_norm_g, mla_w_uq, mla_kv_norm_g, mla_w_ukv, mla_q_head_g,
                         mla_k_head_g, gqa_q_g, gqa_k_g, ssd_conv_w, ssd_conv_b, ssd_dt_bias, ssd_a_log, ssd_d,
                         ssd_norm_g, cf_conv_w, cf_conv_b, cf_ln_g, cf_ln_b)

    y = x_prompt
    ctx_out = []
    for l in range(depth):
        y, new_ctx = _mixer_layer(y, mod, dec_b, l, w, None, None, 512)
        ctx_out.append(new_ctx)
    new = [jnp.stack([t[i] for t in ctx_out], axis=1) for i in range(5)]

    caches = dict(ckv=cache_mla_ckv,
                  kr_pad=_pad_last(cache_mla_krope, MLA_NOPE, LANES - MLA_QK),
                  gk=cache_gqa_k.reshape(dec_b, depth, past, GQA_KV_HEADS * GQA_HD),
                  gv=cache_gqa_v.reshape(dec_b, depth, past, GQA_KV_HEADS * GQA_HD),
                  state=state_ssd.reshape(dec_b, depth, 2, SSD_GROUPS, 2 * SSD_P, SSD_N))
    rope_d = _rope_table(n_lat, GQA_HD, 0)[:, :, 0:GQA_HD]
    ropes = (_rope_table(n_lat, MLA_ROPE, MLA_NOPE), jnp.concatenate([rope_d, rope_d], axis=-1))
    z = x_sample
    for l in range(depth):
        z, _ = _mixer_layer(z, mod, None, l, w, caches, ropes, 512)

    return (y, z, new[0], new[1], new[2], new[3], new[4])
```

```python
import functools
import math

import jax
import jax.numpy as jnp
from jax import lax
from jax.experimental import pallas as pl
from jax.experimental.pallas import tpu as pltpu

F32 = jnp.float32
BF16 = jnp.bfloat16

LANES = 128
VMEM_CAP = 56 * 1024 * 1024

EPS = 1e-6
ROPE_THETA = 10000.0
GRID_W = 64
LOG2E = math.log2(math.e)

D_MODEL = 1024
BRANCH_W = 256
MLA_HEADS, MLA_NOPE, MLA_ROPE, MLA_V = 4, 64, 32, 64
MLA_QK = MLA_NOPE + MLA_ROPE
MLA_Q_RANK, MLA_KV_RANK = 192, 128
SSD_HEADS, SSD_P, SSD_N, SSD_GROUPS, SSD_CONV, SSD_CHUNK = 4, 64, 64, 2, 3, 128
SSD_W = SSD_HEADS * SSD_P
SSD_CONV_CH = SSD_W + 2 * SSD_GROUPS * SSD_N
CF_W, CF_K = 256, 31
GQA_HEADS, GQA_KV_HEADS, GQA_HD = 4, 2, 64
SPLITS = (MLA_Q_RANK, MLA_KV_RANK, MLA_ROPE, MLA_HEADS * MLA_V,
          SSD_CONV_CH, 2 * SSD_HEADS, SSD_W,
          CF_W, CF_W, CF_W,
          GQA_HEADS * GQA_HD, GQA_KV_HEADS * GQA_HD, GQA_KV_HEADS * GQA_HD, GQA_HEADS * GQA_HD)

O_QL, O_KV, O_R, O_AG = 0, 256, 384, 512
O_XBC, O_DT, O_Z = 768, 1280, 1408
O_CV, O_CGL, O_CG = 1664, 1920, 2176
O_DQ, O_DK, O_DV, O_DG = 2432, 2688, 2816, 2944
IN_WP = 3200

CONV_HALO_SSD = 8
CONV_HALO_CF = 16
CONV_TL = 512

SSD_UNROLL = 4
INPROJ_SUB = 256
ATT_TQ = 512
ATT_VROWS = 80
ATT_SUB = 512
ATT_QK_CHUNK = 512
ATT_PV_CHUNK = 256

NT_DIMS = (((1,), (1,)), ((), ()))


def _silu(x):
    return x / (1.0 + jnp.exp(-x))


def _sigmoid(x):
    return 1.0 / (1.0 + jnp.exp(-x))


def _softplus(x):
    return jnp.maximum(x, 0.0) + jnp.log1p(jnp.exp(-jnp.abs(x)))


def _rms(x, n):
    return x * lax.rsqrt(jnp.sum(x * x, axis=-1, keepdims=True) * (1.0 / n) + EPS)


def _dot(a, b):
    return jnp.dot(a, b, preferred_element_type=F32)


def _dot_nt(a, b):
    return lax.dot_general(a, b, NT_DIMS, preferred_element_type=F32)


def _params(sem, vmem_bytes):
    assert vmem_bytes <= VMEM_CAP
    return pltpu.CompilerParams(dimension_semantics=sem, vmem_limit_bytes=VMEM_CAP)


def _pad_heads(w, nh, d, dp):
    s = w.shape[:-1]
    w = w.reshape(s + (nh, d))
    w = jnp.pad(w, [(0, 0)] * len(s) + [(0, 0), (0, dp - d)])
    return w.reshape(s + (nh * dp,))


def _pad_last(w, lo, hi):
    return jnp.pad(w, [(0, 0)] * (w.ndim - 1) + [(lo, hi)])


def _layer_spec(a, l, grid_rank):
    zeros = (0,) * (a.ndim - 1)
    if grid_rank == 1:
        return pl.BlockSpec((1,) + a.shape[1:], lambda bi: (l,) + zeros)
    return pl.BlockSpec((1,) + a.shape[1:], lambda bi, i: (l,) + zeros)


def _with_ones_lane(v, hd):
    lane = lax.broadcasted_iota(jnp.int32, v.shape, v.ndim - 1)
    return jnp.where(lane % LANES == hd, 1.0, v)


def _mod_kernel(c_ref, w_ref, b_ref, o_ref):
    c = c_ref[...]
    o_ref[0] = _dot(_silu(c).astype(BF16), w_ref[0].astype(BF16)) + b_ref[0]


def _modulation(cvec, w_mod, b_mod):
    depth, d, d3 = w_mod.shape
    tn = d
    return pl.pallas_call(
        _mod_kernel,
        out_shape=jax.ShapeDtypeStruct((depth, 8, d3), F32),
        grid=(depth, d3 // tn),
        in_specs=[pl.BlockSpec((8, d), lambda l, j: (0, 0)),
                  pl.BlockSpec((1, d, tn), lambda l, j: (l, 0, j)),
                  pl.BlockSpec((1, 1, tn), lambda l, j: (l, 0, j))],
        out_specs=pl.BlockSpec((1, 8, tn), lambda l, j: (l, 0, j)),
        compiler_params=_params(("arbitrary", "arbitrary"), 4 * d * tn * 4),
        name="modulation",
    )(cvec, w_mod, b_mod.reshape(depth, 1, d3))


def _rope(x, tab_ref, rows, shift):
    cos, sin_a, sin_b = tab_ref[0, rows, :], tab_ref[1, rows, :], tab_ref[2, rows, :]
    outs = []
    for h in range(x.shape[-1] // LANES):
        xh = x[:, h * LANES:(h + 1) * LANES]
        up = pltpu.roll(xh, LANES - shift, 1)
        dn = pltpu.roll(xh, shift, 1)
        outs.append(xh * cos + up * sin_a + dn * sin_b)
    return jnp.concatenate(outs, axis=-1)


def _block_ones(width, block):
    ri = lax.broadcasted_iota(jnp.int32, (width, width), 0) // block
    ci = lax.broadcasted_iota(jnp.int32, (width, width), 1) // block
    return (ri == ci).astype(F32).astype(BF16)


def _head_norm(x, gain, n, ones_bd):
    xx = (x * x).astype(BF16)
    w2 = ones_bd.shape[0]
    ss = jnp.concatenate([_dot(xx[:, j:j + w2], ones_bd) for j in range(0, x.shape[-1], w2)], axis=-1)
    return x * lax.rsqrt(ss * (1.0 / n) + EPS) * gain


def _spread_heads(x, hd):
    lo = lax.broadcasted_iota(jnp.int32, (x.shape[0], LANES), 1) < hd
    outs = []
    for j in range(0, x.shape[-1], LANES):
        pair = x[:, j:j + LANES]
        outs += [jnp.where(lo, pair, 0.0), jnp.where(lo, pltpu.roll(pair, LANES - hd, 1), 0.0)]
    return jnp.concatenate(outs, axis=-1)


def _inproj_refs(latent, refs):
    names = ['x', 'mod', 'ng', 'win', 'wuq', 'wukv', 'gains']
    if latent:
        names += ['wvta', 'wvtd', 'ra', 'rd']
    names += ['qa', 'ka', 'va', 'ag', 'qd', 'kd', 'vd', 'dg', 'xbc', 'dt', 'z', 'glu', 'cg']
    if not latent:
        names += ['ckv', 'kr', 'gk', 'gv']
    return dict(zip(names, refs))


def _inproj_main(r, rows):
    x = r['x'][0, rows, :]
    shift = r['mod'][0, 0, 0:1, :]
    scale = r['mod'][0, 0, 1:2, :]
    h = (_rms(x, D_MODEL) * r['ng'][0] * (1.0 + scale) + shift).astype(BF16)
    return h, _dot(h, r['win'][0])


def _values_t(w_t, x):
    vt = _dot_nt(w_t, x)
    row = lax.broadcasted_iota(jnp.int32, vt.shape, 0)
    return jnp.where(row % ATT_VROWS == MLA_V, 1.0, vt)


def _inproj_latents(r, u):
    g_ql = r['gains'][0, 0:1, 0:256]
    g_kv = r['gains'][0, 2:3, 0:LANES]
    ql = _rms(u[:, O_QL:O_QL + 256], MLA_Q_RANK) * g_ql
    ckv = _rms(u[:, O_KV:O_KV + LANES], MLA_KV_RANK) * g_kv
    return _dot(ql.astype(BF16), r['wuq'][0]), ckv, _dot(ckv.astype(BF16), r['wukv'][0])


def _inproj_tail(latent, r, rows, h, u, q, ckv, kv):
    wa = MLA_HEADS * LANES
    wq, wk = GQA_HEADS * GQA_HD, GQA_KV_HEADS * GQA_HD
    g_qh = r['gains'][0, 1:2, :]
    g_kh = r['gains'][0, 3:4, :]
    g_dq = r['gains'][0, 4:5, 0:wq]
    g_dk = r['gains'][0, 5:6, 0:wk]
    ones_slot = _block_ones(2 * LANES, LANES)
    ones_pair = _block_ones(2 * LANES, GQA_HD)

    q = _head_norm(q, g_qh, MLA_QK, ones_slot)
    kr = u[:, O_R:O_R + LANES]
    k = _head_norm(kv[:, 0:wa] + jnp.concatenate([kr] * MLA_HEADS, axis=-1), g_kh, MLA_QK, ones_slot)
    qd = _head_norm(u[:, O_DQ:O_DQ + wq], g_dq, GQA_HD, ones_pair)
    kd = _head_norm(u[:, O_DK:O_DK + wk], g_dk, GQA_HD, ones_pair[0:wk, 0:wk])
    vd = u[:, O_DV:O_DV + wk]
    if latent:
        q = _rope(q, r['ra'], rows, MLA_ROPE // 4)
        k = _rope(k, r['ra'], rows, MLA_ROPE // 4)
        qd = _rope(qd, r['rd'], rows, GQA_HD // 4)
        kd_att = _rope(kd, r['rd'], rows, GQA_HD // 4)
    else:
        kd_att = kd
        r['ckv'][0, rows, :] = ckv
        r['kr'][0, rows, :] = kr[:, MLA_NOPE:MLA_NOPE + MLA_ROPE]
        r['gk'][0, rows, :] = kd
        r['gv'][0, rows, :] = vd
    qd = _spread_heads(qd, GQA_HD)
    kd_att = _spread_heads(kd_att, GQA_HD)
    r['qa'][0, rows, :] = q.astype(BF16)
    r['ka'][0, rows, :] = k.astype(BF16)
    if latent:
        r['va'][0, :, rows] = _values_t(r['wvta'][0], ckv.astype(BF16)).astype(BF16)
        r['vd'][0, :, rows] = _values_t(r['wvtd'][0], h).astype(BF16)
    else:
        r['va'][0, rows, :] = _with_ones_lane(kv[:, wa:], MLA_V).astype(BF16)
        r['vd'][0, rows, :] = _with_ones_lane(_spread_heads(vd, GQA_HD), GQA_HD).astype(BF16)
    r['ag'][0, rows, :] = u[:, O_AG:O_AG + BRANCH_W]
    r['qd'][0, rows, :] = qd.astype(BF16)
    r['kd'][0, rows, :] = kd_att.astype(BF16)
    r['dg'][0, rows, :] = u[:, O_DG:O_DG + BRANCH_W]

    r['xbc'][0, rows, :] = u[:, O_XBC:O_XBC + SSD_CONV_CH]
    r['dt'][0, rows, :] = u[:, O_DT:O_DT + LANES]
    r['z'][0, rows, :] = u[:, O_Z:O_Z + SSD_W]
    r['glu'][0, rows, :] = u[:, O_CV:O_CV + CF_W] * _sigmoid(u[:, O_CGL:O_CGL + CF_W])
    r['cg'][0, rows, :] = u[:, O_CG:O_CG + CF_W]


def _inproj_kernel(latent, sub, *refs):
    r = _inproj_refs(latent, refs)
    tm = r['x'].shape[1]
    for r0 in range(0, tm, sub):
        rows = slice(r0, r0 + sub)
        h, u = _inproj_main(r, rows)
        q, ckv, kv = _inproj_latents(r, u)
        _inproj_tail(latent, r, rows, h, u, q, ckv, kv)


def _mod_spec(mod, l, row):
    d = mod.shape[-1]
    if row is None:
        return pl.BlockSpec((1, 1, 3, d), lambda bi, i: (l, bi, 0, 0))
    return pl.BlockSpec((1, 1, 3, d), lambda bi, i: (l, row, 0, 0))


def _inproj(x, mod, mod_row, l, w, ropes, tm):
    b, s, d = x.shape
    latent = ropes is not None
    tok = lambda wd: pl.BlockSpec((1, tm, wd), lambda bi, i: (bi, i, 0))
    weights = [w['ng'], w['win'], w['wuq'], w['wukv'], w['gains']]
    if latent:
        weights += [w['wvta'], w['wvtd']]
    in_specs = [tok(d), _mod_spec(mod, l, mod_row)] + [_layer_spec(a, l, 2) for a in weights]
    args = [x, mod] + weights
    if latent:
        in_specs += [pl.BlockSpec((3, tm, LANES), lambda bi, i: (0, i, 0))] * 2
        args += list(ropes)
    widths = [(4 * LANES, BF16), (4 * LANES, BF16), (4 * LANES, BF16), (BRANCH_W, F32),
              (4 * LANES, BF16), (2 * LANES, BF16), (2 * LANES, BF16), (BRANCH_W, F32),
              (SSD_CONV_CH, F32), (LANES, F32), (SSD_W, F32), (CF_W, F32), (CF_W, F32)]
    if not latent:
        widths += [(MLA_KV_RANK, F32), (MLA_ROPE, F32), (LANES, F32), (LANES, F32)]
    out_shape = [jax.ShapeDtypeStruct((b, s, wd), dt) for wd, dt in widths]
    out_specs = [tok(wd) for wd, _ in widths]
    if latent:
        for pos, heads in ((2, MLA_HEADS), (6, GQA_KV_HEADS)):
            out_shape[pos] = jax.ShapeDtypeStruct((b, heads * ATT_VROWS, s), BF16)
            out_specs[pos] = pl.BlockSpec((1, heads * ATT_VROWS, tm), lambda bi, i: (bi, 0, i))
    out_bytes = sum(wd * jnp.dtype(dt).itemsize for wd, dt in widths) * tm
    w_bytes = sum(a[0].size * a.dtype.itemsize for a in weights)
    sub = min(INPROJ_SUB, tm)
    vmem = 2 * (tm * d * 4 + w_bytes + out_bytes) + 3 * sub * IN_WP * 4
    return pl.pallas_call(
        functools.partial(_inproj_kernel, latent, sub),
        out_shape=out_shape,
        grid=(b, s // tm),
        in_specs=in_specs,
        out_specs=out_specs,
        compiler_params=_params(("parallel", "parallel"), vmem),
        name="inproj_lat" if latent else "inproj_ctx",
    )(*args)


def _ctxkv_kernel(ckv_ref, kr_ref, gk_ref, gv_ref, wukv_ref, wvta_ref, gains_ref, kc_ref, vc_ref, kdc_ref, vdc_ref):
    ckv = ckv_ref[0, 0].astype(BF16)
    kv = _dot(ckv, wukv_ref[0])
    kr = kr_ref[0, 0]
    g = gains_ref[0, 3:4, 0:LANES]
    for i in range(MLA_HEADS):
        k = _rms(kv[:, i * LANES:(i + 1) * LANES] + kr, MLA_QK) * g
        kc_ref[0, :, i * LANES:(i + 1) * LANES] = k.astype(BF16)
    vc_ref[0] = _values_t(wvta_ref[0], ckv).astype(BF16)
    gk = gk_ref[0, 0]
    p = gk.shape[0]
    pad = jnp.zeros((p, LANES - GQA_HD), BF16)
    for i in range(GQA_KV_HEADS):
        lo = i * LANES
        kdc_ref[0, :, lo:lo + GQA_HD] = gk[:, i * GQA_HD:(i + 1) * GQA_HD].astype(BF16)
        kdc_ref[0, :, lo + GQA_HD:lo + LANES] = pad
    gvt = gv_ref[0, 0].T
    row = lax.broadcasted_iota(jnp.int32, (ATT_VROWS - GQA_HD, p), 0)
    tail = jnp.where(row == 0, 1.0, 0.0)
    pieces = []
    for i in range(GQA_KV_HEADS):
        pieces += [gvt[i * GQA_HD:(i + 1) * GQA_HD, :], tail]
    vdc_ref[0] = jnp.concatenate(pieces, axis=0).astype(BF16)


def _ctx_kv(ckv, kr_pad, gk, gv, l, w):
    b, _, p, _ = ckv.shape
    cache = lambda a: pl.BlockSpec((1, 1, p, a.shape[-1]), lambda bi: (bi, l, 0, 0))
    out = lambda wd: pl.BlockSpec((1, p, wd), lambda bi: (bi, 0, 0))
    out_t = lambda rows: pl.BlockSpec((1, rows, p), lambda bi: (bi, 0, 0))
    wa, wd = MLA_HEADS * LANES, GQA_KV_HEADS * LANES
    ra, rd = MLA_HEADS * ATT_VROWS, GQA_KV_HEADS * ATT_VROWS
    return pl.pallas_call(
        _ctxkv_kernel,
        out_shape=[jax.ShapeDtypeStruct((b, p, wa), BF16), jax.ShapeDtypeStruct((b, ra, p), BF16),
                   jax.ShapeDtypeStruct((b, p, wd), BF16), jax.ShapeDtypeStruct((b, rd, p), BF16)],
        grid=(b,),
        in_specs=[cache(ckv), cache(kr_pad), cache(gk), cache(gv),
                  _layer_spec(w['wukv'], l, 1), _layer_spec(w['wvta'], l, 1), _layer_spec(w['gains'], l, 1)],
        out_specs=[out(wa), out_t(ra), out(wd), out_t(rd)],
        compiler_params=_params(("parallel",), 16 * p * wa * 4),
        name="ctx_kv",
    )(ckv, kr_pad, gk, gv, w['wukv'], w['wvta'], w['gains'])


def _lane_tile_max(t):
    r = t[:, 0:LANES]
    for j in range(1, t.shape[-1] // LANES):
        r = jnp.maximum(r, t[:, j * LANES:(j + 1) * LANES])
    return r


def _attn_unit(n_heads, rep, refs, bi, rows, h, s_ref):
    q_ref, k_ref, v_ref, g_ref, o_ref = refs
    s_len = k_ref.shape[1]
    hd = o_ref.shape[-1] // n_heads
    kb = min(ATT_QK_CHUNK, s_len)
    kb2 = ATT_PV_CHUNK
    g = h // rep
    hs = slice(h * LANES, (h + 1) * LANES)
    gs = slice(g * LANES, (g + 1) * LANES)
    qh = q_ref[bi, rows, hs]
    mx = None
    for c in range(s_len // kb):
        sc = _dot_nt(qh, k_ref[bi, c * kb:(c + 1) * kb, gs])
        s_ref[:, c * kb:(c + 1) * kb] = sc
        t = _lane_tile_max(sc)
        mx = t if mx is None else jnp.maximum(mx, t)
    m = jnp.max(mx, axis=-1, keepdims=True)
    acc = None
    for c in range(s_len // kb2):
        r0 = c * kb2
        p = jnp.exp2(s_ref[:, r0:r0 + kb2] - m).astype(BF16)
        part = _dot(p, v_ref[bi, r0:r0 + kb2, gs])
        acc = part if acc is None else acc + part
    o = acc[:, 0:hd] / acc[:, hd:hd + 1]
    os_ = slice(h * hd, (h + 1) * hd)
    o_ref[bi, rows, os_] = (o * _silu(g_ref[bi, rows, os_])).astype(o_ref.dtype)


def _attn_kernel(n_heads, rep, sub, *refs):
    s0_ref, s1_ref = refs[-2:]
    tq = refs[0].shape[1]
    unit = 0
    for r0 in range(0, tq, sub):
        for h in range(n_heads):
            _attn_unit(n_heads, rep, refs[:-2], 0, slice(r0, r0 + sub), h, s0_ref if unit % 2 == 0 else s1_ref)
            unit += 1


def _attention(q, k, v, gate, n_heads, rep, tq, sub, name):
    b, s, qw = q.shape
    kw = k.shape[-1]
    ow = gate.shape[-1]
    assert s % min(ATT_QK_CHUNK, s) == 0 and s % ATT_PV_CHUNK == 0 and s % tq == 0 and tq % sub == 0
    tile = lambda wd: pl.BlockSpec((1, tq, wd), lambda bi, i: (bi, i, 0))
    whole = pl.BlockSpec((1, s, kw), lambda bi, i: (bi, 0, 0))
    vmem = 2 * (tq * qw * 2 + 2 * s * kw * 2 + tq * ow * 6) + 2 * sub * s * 4 + 2 * sub * 1024 * 4
    return pl.pallas_call(
        functools.partial(_attn_kernel, n_heads, rep, sub),
        out_shape=jax.ShapeDtypeStruct((b, s, ow), BF16),
        grid=(b, s // tq),
        in_specs=[tile(qw), whole, whole, tile(ow)],
        out_specs=tile(ow),
        scratch_shapes=[pltpu.VMEM((sub, s), F32), pltpu.VMEM((sub, s), F32)],
        compiler_params=_params(("parallel", "arbitrary"), vmem),
        name=name,
    )(q, k, v, gate)


def _attn_t_scores(io, rep, sub, unit, s_ref):
    q_ref, k_ref, _, kc_ref = io[:4]
    r0, h = unit
    s_len, p_len = k_ref.shape[1], kc_ref.shape[1]
    kb = min(ATT_QK_CHUNK, s_len)
    gs = slice((h // rep) * LANES, (h // rep + 1) * LANES)
    qh = q_ref[0, r0:r0 + sub, h * LANES:(h + 1) * LANES]
    mx = None
    for c in range(s_len // kb):
        st = _dot_nt(k_ref[0, c * kb:(c + 1) * kb, gs], qh)
        s_ref[c * kb:(c + 1) * kb, :] = st
        t = jnp.max(st, axis=0, keepdims=True)
        mx = t if mx is None else jnp.maximum(mx, t)
    st = _dot_nt(kc_ref[0, :, gs], qh)
    s_ref[s_len:s_len + p_len, :] = st
    return jnp.maximum(mx, jnp.max(st, axis=0, keepdims=True))


def _attn_t_values(io, n_heads, rep, sub, unit, s_ref, m, ot_ref):
    _, k_ref, vt_ref, kc_ref, vct_ref, _, o_ref = io
    r0, h = unit
    s_len, p_len = k_ref.shape[1], kc_ref.shape[1]
    hd = o_ref.shape[-1] // n_heads
    vs = slice((h // rep) * ATT_VROWS, (h // rep + 1) * ATT_VROWS)
    kb2 = ATT_PV_CHUNK
    acc = None
    for c in range((s_len + p_len) // kb2):
        k0 = c * kb2
        pt = jnp.exp2(s_ref[k0:k0 + kb2, :] - m).astype(BF16)
        vv = vt_ref[0, vs, k0:k0 + kb2] if k0 < s_len else vct_ref[0, vs, k0 - s_len:k0 - s_len + kb2]
        part = _dot(vv, pt)
        acc = part if acc is None else acc + part
    ot_ref[h * hd:(h + 1) * hd, r0:r0 + sub] = acc[0:hd, :] / acc[hd:hd + 1, :]


def _attn_t_kernel(n_heads, rep, sub, *refs):
    io, (s0_ref, s1_ref, ot_ref) = refs[:-3], refs[-3:]
    tq = io[0].shape[1]
    units = [(r0, h) for r0 in range(0, tq, sub) for h in range(n_heads)]
    bufs = (s0_ref, s1_ref)
    m = _attn_t_scores(io, rep, sub, units[0], bufs[0])
    for u in range(len(units)):
        m_next = None
        if u + 1 < len(units):
            m_next = _attn_t_scores(io, rep, sub, units[u + 1], bufs[(u + 1) % 2])
        _attn_t_values(io, n_heads, rep, sub, units[u], bufs[u % 2], m, ot_ref)
        m = m_next
    g_ref, o_ref = io[5], io[6]
    o_ref[0] = (ot_ref[...].T * _silu(g_ref[0])).astype(o_ref.dtype)


def _attention_t(q, k, vt, ctx, gate, n_heads, rep, tq, sub, name):
    b, s, qw = q.shape
    kw = k.shape[-1]
    vr = vt.shape[1]
    ow = gate.shape[-1]
    kc, vct = ctx
    p = kc.shape[1]
    assert s % min(ATT_QK_CHUNK, s) == 0 and s % ATT_PV_CHUNK == 0 and p % ATT_PV_CHUNK == 0
    assert s % tq == 0 and tq % sub == 0
    tile = lambda wd: pl.BlockSpec((1, tq, wd), lambda bi, i: (bi, i, 0))
    whole = lambda a: pl.BlockSpec((1,) + a.shape[1:], lambda bi, i: (bi, 0, 0))
    vmem = (2 * (tq * qw * 2 + (s + p) * (kw + vr) * 2 + tq * ow * 6)
            + 2 * sub * (s + p) * 4 + ow * tq * 4 + 2 * sub * 1024 * 4)
    return pl.pallas_call(
        functools.partial(_attn_t_kernel, n_heads, rep, sub),
        out_shape=jax.ShapeDtypeStruct((b, s, ow), BF16),
        grid=(b, s // tq),
        in_specs=[tile(qw), whole(k), whole(vt), whole(kc), whole(vct), tile(ow)],
        out_specs=tile(ow),
        scratch_shapes=[pltpu.VMEM((s + p, sub), F32), pltpu.VMEM((s + p, sub), F32), pltpu.VMEM((ow, tq), F32)],
        compiler_params=_params(("parallel", "arbitrary"), vmem),
        name=name,
    )(q, k, vt, kc, vct, gate)


def _conv_kernel(tl, xc_ref, xp_ref, xn_ref, gc_ref, gp_ref, gn_ref, cg_ref,
                 w3_ref, b3_ref, w31_ref, b31_ref, lng_ref, lnb_ref,
                 xo_ref, co_ref, ext3, ext31):
    i = pl.program_id(1)
    keep_prev = (i > 0).astype(F32)
    keep_next = (i < pl.num_programs(1) - 1).astype(F32)

    h3 = CONV_HALO_SSD
    ext3[0:h3, :] = xp_ref[0] * keep_prev
    ext3[h3:h3 + tl, :] = xc_ref[0]
    ext3[h3 + tl:2 * h3 + tl, :] = xn_ref[0] * keep_next
    acc = jnp.zeros((tl, SSD_CONV_CH), F32) + b3_ref[0]
    for k in range(SSD_CONV):
        acc = acc + w3_ref[0, k:k + 1, :] * ext3[h3 - SSD_CONV // 2 + k:h3 - SSD_CONV // 2 + k + tl, :]
    xo_ref[0] = _silu(acc)

    h31 = CONV_HALO_CF
    ext31[0:h31, :] = gp_ref[0] * keep_prev
    ext31[h31:h31 + tl, :] = gc_ref[0]
    ext31[h31 + tl:2 * h31 + tl, :] = gn_ref[0] * keep_next
    acc = jnp.zeros((tl, CF_W), F32) + b31_ref[0]
    o_lo = h31 - CF_K // 2
    ext = ext31[...]
    n_ext = ext.shape[0]
    for r in range(8):
        taps = [k for k in range(CF_K) if (o_lo + k) % 8 == r]
        if not taps:
            continue
        shifted = ext if r == 0 else pltpu.roll(ext, n_ext - r, 0)
        for k in taps:
            a = (o_lo + k) // 8 * 8
            acc = acc + w31_ref[0, k:k + 1, :] * shifted[a:a + tl, :]
    mu = jnp.mean(acc, axis=-1, keepdims=True)
    cen = acc - mu
    var = jnp.mean(cen * cen, axis=-1, keepdims=True)
    y = cen * lax.rsqrt(var + EPS) * lng_ref[0] + lnb_ref[0]
    co_ref[0] = (_silu(y) * _silu(cg_ref[0])).astype(co_ref.dtype)


def _convs(xbc, glu, cg, l, w, tl):
    b, s, _ = xbc.shape
    nt = s // tl

    def cur(wd):
        return pl.BlockSpec((1, tl, wd), lambda bi, i: (bi, i, 0))

    def prev(wd, hrows):
        r = tl // hrows
        return pl.BlockSpec((1, hrows, wd), lambda bi, i: (bi, jnp.maximum(i * r - 1, 0), 0))

    def nxt(wd, hrows):
        r = tl // hrows
        return pl.BlockSpec((1, hrows, wd), lambda bi, i: (bi, jnp.minimum((i + 1) * r, nt * r - 1), 0))

    weights = [w['w3'], w['b3'], w['w31'], w['b31'], w['lng'], w['lnb']]
    vmem = 4 * tl * (SSD_CONV_CH * 2 + CF_W * 3) * 4 + 6 * tl * SSD_CONV_CH * 4
    return pl.pallas_call(
        functools.partial(_conv_kernel, tl),
        out_shape=[jax.ShapeDtypeStruct((b, s, SSD_CONV_CH), F32),
                   jax.ShapeDtypeStruct((b, s, CF_W), BF16)],
        grid=(b, nt),
        in_specs=[cur(SSD_CONV_CH), prev(SSD_CONV_CH, CONV_HALO_SSD), nxt(SSD_CONV_CH, CONV_HALO_SSD),
                  cur(CF_W), prev(CF_W, CONV_HALO_CF), nxt(CF_W, CONV_HALO_CF), cur(CF_W)]
                 + [_layer_spec(a, l, 2) for a in weights],
        out_specs=[cur(SSD_CONV_CH), cur(CF_W)],
        scratch_shapes=[pltpu.VMEM((tl + 2 * CONV_HALO_SSD, SSD_CONV_CH), F32),
                        pltpu.VMEM((tl + 2 * CONV_HALO_CF, CF_W), F32)],
        compiler_params=_params(("parallel", "parallel"), vmem),
        name="convs",
    )(xbc, xbc, xbc, glu, glu, glu, cg, *weights)


def _split3(x):
    hi = x.astype(BF16)
    r1 = x - hi.astype(F32)
    mid = r1.astype(BF16)
    lo = (r1 - mid.astype(F32)).astype(BF16)
    return hi, mid, lo


def _cumsum_rows(tri, x):
    return sum(_dot(tri, part) for part in _split3(x))


def _transpose_exact(x):
    c = x.shape[1]
    eye = (lax.broadcasted_iota(jnp.int32, (c, c), 0) == lax.broadcasted_iota(jnp.int32, (c, c), 1))
    eye = eye.astype(F32).astype(BF16)
    return sum(_dot_nt(eye, part) for part in _split3(x))


SSD_NK = 2 * SSD_HEADS


def _ssd_stage_local(c, d, xc_ref, dt_ref, bias, a_row):
    q = SSD_CHUNK
    rows = pl.ds(pl.multiple_of(c * q, q), q)
    t = dict(rows=rows, d=d)
    t['xs'] = xc_ref[0, rows, 0:SSD_W]
    t['bm'] = xc_ref[0, rows, SSD_W:SSD_W + LANES]
    t['cm'] = xc_ref[0, rows, SSD_W + LANES:SSD_W + 2 * LANES].astype(BF16)
    t['dt'] = _softplus(dt_ref[0, rows, :] + bias)
    ri = lax.broadcasted_iota(jnp.int32, (q, q), 0)
    ci = lax.broadcasted_iota(jnp.int32, (q, q), 1)
    t['mask'] = (ri >= ci) if d == 0 else (ri <= ci)
    t['acum'] = _cumsum_rows(t['mask'].astype(F32).astype(BF16), t['dt'] * a_row)
    return t


def _ssd_stage_cb(t):
    q = SSD_CHUNK
    lane = lax.broadcasted_iota(jnp.int32, (q, LANES), 1)
    t['tr'] = jnp.where(lane < SSD_NK, t['dt'], t['acum']).T
    bt = t['bm'].T
    t['bt'] = bt
    grp_row = lax.broadcasted_iota(jnp.int32, (LANES, q), 0) < SSD_N
    t['cb'] = [_dot(t['cm'], jnp.where(grp_row if g == 0 else jnp.logical_not(grp_row), bt, 0.0).astype(BF16))
               for g in range(SSD_GROUPS)]
    t['last'] = q - 1 if t['d'] == 0 else 0
    t['etot'] = jnp.exp(t['acum'][t['last']:t['last'] + 1, :])
    t['eac'] = jnp.exp(t['acum'])


def _ssd_stage_diag(t, dskip):
    q = SSD_CHUNK
    d = t['d']
    lo_half = lax.broadcasted_iota(jnp.int32, (q, LANES), 1) < SSD_P
    tr, acum, last = t['tr'], t['acum'], t['last']
    t['yd'], t['cs'] = [], []
    for g in range(SSD_GROUPS):
        k0 = d * SSD_HEADS + 2 * g
        xg = t['xs'][:, g * LANES:(g + 1) * LANES]
        xgb = xg.astype(BF16)
        btg = t['bt'][g * SSD_N:(g + 1) * SSD_N, :]
        yg = None
        cs = []
        for hh in range(2):
            k = k0 + hh
            dt_row = tr[k:k + 1, :]
            ac_row = tr[SSD_NK + k:SSD_NK + k + 1, :]
            dec = jnp.where(t['mask'], jnp.exp(acum[:, k:k + 1] - ac_row), 0.0)
            xm = jnp.where(lo_half if hh == 0 else jnp.logical_not(lo_half), xg, 0.0).astype(BF16)
            part = _dot((t['cb'][g] * dec * dt_row).astype(BF16), xm)
            yg = part if yg is None else yg + part
            w_row = dt_row * jnp.exp(tr[SSD_NK + k:SSD_NK + k + 1, last:last + 1] - ac_row)
            cs.append(_dot((btg * w_row).astype(BF16), xgb))
        if d == 0:
            yg = yg + dskip[:, g * LANES:(g + 1) * LANES] * xg
        t['yd'].append(yg)
        t['cs'].append(cs)


def _ssd_stage_state(t, states):
    q = SSD_CHUNK
    d = t['d']
    lo_half = lax.broadcasted_iota(jnp.int32, (q, LANES), 1) < SSD_P
    lane_n = lax.broadcasted_iota(jnp.int32, (SSD_N, LANES), 1) < SSD_P
    zero = jnp.zeros((SSD_N, LANES), F32)
    ys, new_states = [], []
    for g in range(SSD_GROUPS):
        k0 = d * SSD_HEADS + 2 * g
        s_prev = states[g]
        sz = jnp.concatenate([s_prev, zero] if g == 0 else [zero, s_prev], axis=0).astype(BF16)
        eacs = jnp.where(lo_half, t['eac'][:, k0:k0 + 1], t['eac'][:, k0 + 1:k0 + 2])
        ys.append(t['yd'][g] + _dot(t['cm'], sz) * eacs)
        cdec = jnp.where(lane_n, t['etot'][:, k0:k0 + 1], t['etot'][:, k0 + 1:k0 + 2])
        new_states.append(cdec * s_prev + jnp.where(lane_n, t['cs'][g][0], t['cs'][g][1]))
    return ys, new_states


def _ssd_kernel(nc, unroll, has_init, *refs):
    if has_init:
        xc_ref, dt_ref, s0_ref, bias_ref, alog_ref, d_ref, yf_ref, yb_ref, so_ref, st_ref = refs
        for d in range(2):
            for g in range(SSD_GROUPS):
                st_ref[0, d, g] = _transpose_exact(s0_ref[0, 0, d, g])
    else:
        xc_ref, dt_ref, bias_ref, alog_ref, d_ref, yf_ref, yb_ref, so_ref, st_ref = refs
        st_ref[...] = jnp.zeros(st_ref.shape, F32)
    lane = lax.broadcasted_iota(jnp.int32, (1, LANES), 1)
    a_row = jnp.where(lane < 2 * SSD_NK, -jnp.exp(alog_ref[0]), 0.0)
    bias = bias_ref[0]
    dskip = d_ref[0]

    def body(i, carry):
        states = [[st_ref[0, d, g] for g in range(SSD_GROUPS)] for d in range(2)]
        work = []
        for j in range(unroll):
            c = i * unroll + j
            work.append((_ssd_stage_local(c, 0, xc_ref, dt_ref, bias, a_row), yf_ref))
            work.append((_ssd_stage_local(nc - 1 - c, 1, xc_ref, dt_ref, bias, a_row), yb_ref))
        for t, _ in work:
            _ssd_stage_cb(t)
        for t, _ in work:
            _ssd_stage_diag(t, dskip)
        writes = []
        for t, y_ref in work:
            ys, states[t['d']] = _ssd_stage_state(t, states[t['d']])
            writes.append((y_ref, t['rows'], ys))
        for y_ref, rows, ys in writes:
            for g in range(SSD_GROUPS):
                y_ref[0, rows, g * LANES:(g + 1) * LANES] = ys[g]
        for d in range(2):
            for g in range(SSD_GROUPS):
                st_ref[0, d, g] = states[d][g]
        return carry

    lax.fori_loop(0, nc // unroll, body, 0)
    for d in range(2):
        for g in range(SSD_GROUPS):
            so_ref[0, d, g] = _transpose_exact(st_ref[0, d, g])


def _ssd(xc, dt, s0, l, w):
    b, s, _ = xc.shape
    nc = s // SSD_CHUNK
    unroll = min(SSD_UNROLL, nc)
    assert nc % unroll == 0
    st_shape = (b, 2, SSD_GROUPS, 2 * SSD_P, SSD_N)
    st_spec = pl.BlockSpec((1,) + st_shape[1:], lambda bi: (bi, 0, 0, 0, 0))
    y_spec = pl.BlockSpec((1, s, SSD_W), lambda bi: (bi, 0, 0))
    in_specs = [pl.BlockSpec((1, s, SSD_CONV_CH), lambda bi: (bi, 0, 0)),
                pl.BlockSpec((1, s, LANES), lambda bi: (bi, 0, 0))]
    args = [xc, dt]
    if s0 is not None:
        in_specs.append(pl.BlockSpec((1, 1) + st_shape[1:], lambda bi: (bi, l, 0, 0, 0, 0)))
        args.append(s0)
    weights = [w['dt_bias'], w['a_log'], w['dskip']]
    vmem = 2 * s * (SSD_CONV_CH + LANES + 2 * SSD_W) * 4 + 8 * 1024 * 1024
    return pl.pallas_call(
        functools.partial(_ssd_kernel, nc, unroll, s0 is not None),
        out_shape=[jax.ShapeDtypeStruct((b, s, SSD_W), F32), jax.ShapeDtypeStruct((b, s, SSD_W), F32),
                   jax.ShapeDtypeStruct(st_shape, F32)],
        grid=(b,),
        in_specs=in_specs + [_layer_spec(a, l, 1) for a in weights],
        out_specs=[y_spec, y_spec, st_spec],
        scratch_shapes=[pltpu.VMEM((1, 2, SSD_GROUPS, SSD_N, 2 * SSD_P), F32)],
        compiler_params=_params(("parallel",), vmem),
        name="ssd",
    )(*args, *weights)


def _outproj_kernel(oa_ref, yf_ref, yb_ref, z_ref, oc_ref, od_ref, x_ref, mod_ref, ng_ref, w_ref, o_ref):
    ob = _rms((yf_ref[0] + yb_ref[0]) * _silu(z_ref[0]), SSD_W) * ng_ref[0]
    acc = _dot(oa_ref[0], w_ref[0, 0:BRANCH_W, :])
    acc = acc + _dot(ob.astype(BF16), w_ref[0, BRANCH_W:2 * BRANCH_W, :])
    acc = acc + _dot(oc_ref[0], w_ref[0, 2 * BRANCH_W:3 * BRANCH_W, :])
    acc = acc + _dot(od_ref[0], w_ref[0, 3 * BRANCH_W:4 * BRANCH_W, :])
    o_ref[0] = x_ref[0] + mod_ref[0, 0, 2:3, :] * acc


def _outproj(oa, yf, yb, z, oc, od, x, mod, mod_row, l, w, tm):
    b, s, d = x.shape
    br = lambda: pl.BlockSpec((1, tm, BRANCH_W), lambda bi, i: (bi, i, 0))
    xs = pl.BlockSpec((1, tm, d), lambda bi, i: (bi, i, 0))
    vmem = 2 * (6 * tm * BRANCH_W * 4 + 2 * tm * d * 4 + d * d * 2) + 2 * tm * d * 4
    return pl.pallas_call(
        _outproj_kernel,
        out_shape=jax.ShapeDtypeStruct((b, s, d), F32),
        grid=(b, s // tm),
        in_specs=[br(), br(), br(), br(), br(), br(), xs, _mod_spec(mod, l, mod_row),
                  _layer_spec(w['ssd_ng'], l, 2), _layer_spec(w['wout'], l, 2)],
        out_specs=xs,
        compiler_params=_params(("parallel", "parallel"), vmem),
        name="outproj",
    )(oa, yf, yb, z, oc, od, x, mod, w['ssd_ng'], w['wout'])


def _rope_table(n_tokens, dim, lane0):
    n_rows = n_tokens // GRID_W
    quarter = dim // 4
    inv = ROPE_THETA ** (-jnp.arange(quarter, dtype=F32) / quarter)
    ar = jnp.arange(n_rows, dtype=F32)[:, None] * inv
    ac = jnp.arange(GRID_W, dtype=F32)[:, None] * inv
    zr, zc = jnp.zeros_like(ar), jnp.zeros_like(ac)
    hi = LANES - lane0 - dim

    def lanes(row_part, col_part):
        rp = _pad_last(jnp.concatenate(row_part, axis=-1), lane0, hi)
        cp = _pad_last(jnp.concatenate(col_part, axis=-1), lane0, hi)
        return (rp[:, None, :] + cp[None, :, :]).reshape(n_tokens, LANES)

    off_rotary = jnp.pad(jnp.zeros((1, dim), F32), ((0, 0), (lane0, hi)), constant_values=1.0)
    cos = lanes([jnp.cos(ar), jnp.cos(ar), zr, zr], [zc, zc, jnp.cos(ac), jnp.cos(ac)]) + off_rotary
    sin_a = lanes([-jnp.sin(ar), zr, zr, zr], [zc, zc, -jnp.sin(ac), zc])
    sin_b = lanes([zr, jnp.sin(ar), zr, zr], [zc, zc, zc, jnp.sin(ac)])
    return jnp.stack([cos, sin_a, sin_b])


def _prepare_weights(w_in, w_out, norm_g, mla_q_norm_g, mla_w_uq, mla_kv_norm_g, mla_w_ukv, mla_q_head_g,
                     mla_k_head_g, gqa_q_g, gqa_k_g, ssd_conv_w, ssd_conv_b, ssd_dt_bias, ssd_a_log, ssd_d,
                     ssd_norm_g, cf_conv_w, cf_conv_b, cf_ln_g, cf_ln_b):
    depth = w_in.shape[0]
    pieces = []
    o = 0
    for sz in SPLITS:
        pieces.append(w_in[:, :, o:o + sz])
        o += sz
    (a_q, a_kv, a_r, a_g, b_xbc, b_dt, b_z, c_v, c_gl, c_g, d_q, d_k, d_v, d_g) = pieces
    win = jnp.concatenate([
        _pad_last(a_q, 0, 256 - MLA_Q_RANK), a_kv, _pad_last(a_r, MLA_NOPE, LANES - MLA_QK), a_g,
        b_xbc, _pad_last(jnp.concatenate([b_dt, b_dt], axis=-1), 0, LANES - 2 * SSD_NK), b_z, c_v, c_gl, c_g,
        d_q, d_k, d_v, d_g], axis=-1).astype(BF16)
    wuq = jnp.pad(_pad_heads(mla_w_uq, MLA_HEADS, MLA_QK, LANES),
                  ((0, 0), (0, 256 - MLA_Q_RANK), (0, 0))).astype(BF16)
    kv = mla_w_ukv.reshape(depth, MLA_KV_RANK, MLA_HEADS, MLA_NOPE + MLA_V)
    wk = _pad_last(kv[..., :MLA_NOPE], 0, LANES - MLA_NOPE).reshape(depth, MLA_KV_RANK, -1)
    wv = _pad_last(kv[..., MLA_NOPE:], 0, LANES - MLA_V).reshape(depth, MLA_KV_RANK, -1)
    wukv = jnp.concatenate([wk, wv], axis=-1).astype(BF16)
    rows_t = lambda wt: jnp.pad(wt, ((0, 0), (0, 0), (0, ATT_VROWS - wt.shape[2]), (0, 0)))
    wvta = rows_t(kv[..., MLA_NOPE:].transpose(0, 2, 3, 1)).reshape(depth, MLA_HEADS * ATT_VROWS, MLA_KV_RANK)
    wvtd = rows_t(d_v.reshape(depth, -1, GQA_KV_HEADS, GQA_HD).transpose(0, 2, 3, 1))
    wvtd = wvtd.reshape(depth, GQA_KV_HEADS * ATT_VROWS, -1)
    gw = MLA_HEADS * LANES
    row = lambda v: _pad_last(v, 0, gw - v.shape[-1])
    slots = lambda v, n: row(jnp.tile(_pad_last(v, 0, LANES - v.shape[-1]), (1, n)))
    zero = jnp.zeros((depth, gw), F32)
    gains = jnp.stack([row(mla_q_norm_g),
                       slots(mla_q_head_g * (MLA_QK ** -0.5 * LOG2E), MLA_HEADS),
                       row(mla_kv_norm_g),
                       slots(mla_k_head_g, MLA_HEADS),
                       row(jnp.tile(gqa_q_g * (GQA_HD ** -0.5 * LOG2E), (1, GQA_HEADS))),
                       row(jnp.tile(gqa_k_g, (1, GQA_KV_HEADS))), zero, zero], axis=1)
    vec = lambda v: v[:, None, :]
    lanes8 = lambda v: _pad_last(jnp.tile(v.reshape(depth, 1, -1), (1, 1, 2)), 0, LANES - 2 * SSD_NK)
    return dict(win=win, wuq=wuq, wukv=wukv, wvta=wvta.astype(BF16), wvtd=wvtd.astype(BF16), gains=gains,
                wout=w_out.astype(BF16), ng=vec(norm_g),
                w3=ssd_conv_w, b3=vec(ssd_conv_b), w31=cf_conv_w, b31=vec(cf_conv_b),
                lng=vec(cf_ln_g), lnb=vec(cf_ln_b), dt_bias=lanes8(ssd_dt_bias), a_log=lanes8(ssd_a_log),
                dskip=vec(jnp.repeat(ssd_d, SSD_P, axis=-1)), ssd_ng=vec(ssd_norm_g))


def _mixer_layer(x, mod, mod_row, l, w, caches, ropes, tm):
    latent = caches is not None
    b, s, d = x.shape
    flat = (lambda a: a) if latent else (lambda a: a.reshape(1, b * s, a.shape[-1]))
    unflat = (lambda a: a) if latent else (lambda a: a.reshape(b, s, a.shape[-1]))
    outs = [unflat(o) for o in _inproj(flat(x), mod, mod_row, l, w, ropes, tm)]
    qa, ka, va, ag, qd, kd, vd, dg, xbc, dt, z, glu, cg = outs[:13]
    ctx_a = ctx_d = s0 = None
    if latent:
        kc, vc, kdc, vdc = _ctx_kv(caches['ckv'], caches['kr_pad'], caches['gk'], caches['gv'], l, w)
        ctx_a, ctx_d, s0 = (kc, vc), (kdc, vdc), caches['state']
    tq = min(ATT_TQ, s)
    sub = min(ATT_SUB, tq)
    rep = GQA_HEADS // GQA_KV_HEADS
    if latent:
        oa = _attention_t(qa, ka, va, ctx_a, ag, MLA_HEADS, 1, tq, sub, "attn_mla")
        od = _attention_t(qd, kd, vd, ctx_d, dg, GQA_HEADS, rep, tq, sub, "attn_gqa")
    else:
        oa = _attention(qa, ka, va, ag, MLA_HEADS, 1, tq, sub, "attn_mla_ctx")
        od = _attention(qd, kd, vd, dg, GQA_HEADS, rep, tq, sub, "attn_gqa_ctx")
    xconv, oc = _convs(xbc, glu, cg, l, w, min(CONV_TL, s))
    yf, yb, s_fin = _ssd(xconv, dt, s0, l, w)
    y = unflat(_outproj(flat(oa), flat(yf), flat(yb), flat(z), flat(oc), flat(od), flat(x), mod, mod_row, l, w, tm))
    new_ctx = None
    if not latent:
        ckv, kr, gk, gv = outs[13:]
        new_ctx = (ckv, kr, gk.reshape(b, -1, GQA_KV_HEADS, GQA_HD), gv.reshape(b, -1, GQA_KV_HEADS, GQA_HD),
                   s_fin.reshape(b, 2, SSD_HEADS, SSD_P, SSD_N))
    return y, new_ctx


def kernel(x_prompt, x_sample, cache_mla_ckv, cache_mla_krope, cache_gqa_k, cache_gqa_v, state_ssd, c, c_ctx, w_mod, b_mod, norm_g, w_in, w_out, mla_q_norm_g, mla_w_uq, mla_kv_norm_g, mla_w_ukv, mla_q_head_g, mla_k_head_g, ssd_conv_w, ssd_conv_b, ssd_dt_bias, ssd_a_log, ssd_d, ssd_norm_g, cf_conv_w, cf_conv_b, cf_ln_g, cf_ln_b, gqa_q_g, gqa_k_g):
    depth = w_in.shape[0]
    dec_b, n_lat = x_sample.shape[0], x_sample.shape[1]
    seq = x_prompt.shape[1]
    past = cache_mla_ckv.shape[2]
    assert dec_b < 8 and x_prompt.shape[-1] == D_MODEL and w_in.shape[-1] == sum(SPLITS)

    cvec = jnp.concatenate([c, c_ctx[None, :], jnp.zeros((8 - dec_b - 1, D_MODEL), F32)], axis=0)
    mod = _modulation(cvec, w_mod, b_mod).reshape(depth, 8, 3, D_MODEL)
    w = _prepare_weights(w_in, w_out, norm_g, mla_q_norm_g, mla_w_uq, mla_kv_norm_g, mla_w_ukv, mla_q_head_g,
                         mla_k_head_g, gqa_q_g, gqa_k_g, ssd_conv_w, ssd_conv_b, ssd_dt_bias, ssd_a_log, ssd_d,
                         ssd_norm_g, cf_conv_w, cf_conv_b, cf_ln_g, cf_ln_b)

    y = x_prompt
    ctx_out = []
    for l in range(depth):
        y, new_ctx = _mixer_layer(y, mod, dec_b, l, w, None, None, 512)
        ctx_out.append(new_ctx)
    new = [jnp.stack([t[i] for t in ctx_out], axis=1) for i in range(5)]

    caches = dict(ckv=cache_mla_ckv,
                  kr_pad=_pad_last(cache_mla_krope, MLA_NOPE, LANES - MLA_QK),
                  gk=cache_gqa_k.reshape(dec_b, depth, past, GQA_KV_HEADS * GQA_HD),
                  gv=cache_gqa_v.reshape(dec_b, depth, past, GQA_KV_HEADS * GQA_HD),
                  state=state_ssd.reshape(dec_b, depth, 2, SSD_GROUPS, 2 * SSD_P, SSD_N))
    rope_d = _rope_table(n_lat, GQA_HD, 0)[:, :, 0:GQA_HD]
    ropes = (_rope_table(n_lat, MLA_ROPE, MLA_NOPE), jnp.concatenate([rope_d, rope_d], axis=-1))
    z = x_sample
    for l in range(depth):
        z, _ = _mixer_layer(z, mod, None, l, w, caches, ropes, 512)

    return (y, z, new[0], new[1], new[2], new[3], new[4])
```

```python
import functools
import math

import jax
import jax.numpy as jnp
from jax import lax
from jax.experimental import pallas as pl
from jax.experimental.pallas import tpu as pltpu

F32 = jnp.float32
BF16 = jnp.bfloat16

LANES = 128
VMEM_CAP = 56 * 1024 * 1024

EPS = 1e-6
ROPE_THETA = 10000.0
GRID_W = 64
LOG2E = math.log2(math.e)

D_MODEL = 1024
BRANCH_W = 256
MLA_HEADS, MLA_NOPE, MLA_ROPE, MLA_V = 4, 64, 32, 64
MLA_QK = MLA_NOPE + MLA_ROPE
MLA_Q_RANK, MLA_KV_RANK = 192, 128
SSD_HEADS, SSD_P, SSD_N, SSD_GROUPS, SSD_CONV, SSD_CHUNK = 4, 64, 64, 2, 3, 128
SSD_W = SSD_HEADS * SSD_P
SSD_CONV_CH = SSD_W + 2 * SSD_GROUPS * SSD_N
CF_W, CF_K = 256, 31
GQA_HEADS, GQA_KV_HEADS, GQA_HD = 4, 2, 64
SPLITS = (MLA_Q_RANK, MLA_KV_RANK, MLA_ROPE, MLA_HEADS * MLA_V,
          SSD_CONV_CH, 2 * SSD_HEADS, SSD_W,
          CF_W, CF_W, CF_W,
          GQA_HEADS * GQA_HD, GQA_KV_HEADS * GQA_HD, GQA_KV_HEADS * GQA_HD, GQA_HEADS * GQA_HD)

O_QL, O_KV, O_R, O_AG = 0, 256, 384, 512
O_XBC, O_DT, O_Z = 768, 1280, 1408
O_CV, O_CGL, O_CG = 1664, 1920, 2176
O_DQ, O_DK, O_DV, O_DG = 2432, 2688, 2816, 2944
IN_WP = 3200

CONV_HALO_SSD = 8
CONV_HALO_CF = 16
CONV_TL = 512

SSD_UNROLL = 4
INPROJ_SUB = 256
OUTPROJ_ROWS_PER_INPROJ_STEP = 2
ATT_TQ = 1024
ATT_VROWS = 80
ATT_SUB = 512
ATT_QK_CHUNK = 512
ATT_PV_CHUNK = 256

NT_DIMS = (((1,), (1,)), ((), ()))


def _silu(x):
    return x / (1.0 + jnp.exp(-x))


def _sigmoid(x):
    return 1.0 / (1.0 + jnp.exp(-x))


def _softplus(x):
    return jnp.maximum(x, 0.0) + jnp.log1p(jnp.exp(-jnp.abs(x)))


def _rms(x, n):
    return x * lax.rsqrt(jnp.sum(x * x, axis=-1, keepdims=True) * (1.0 / n) + EPS)


def _dot(a, b):
    return jnp.dot(a, b, preferred_element_type=F32)


def _dot_nt(a, b):
    return lax.dot_general(a, b, NT_DIMS, preferred_element_type=F32)


def _params(sem, vmem_bytes):
    assert vmem_bytes <= VMEM_CAP
    return pltpu.CompilerParams(dimension_semantics=sem, vmem_limit_bytes=VMEM_CAP)


def _pad_heads(w, nh, d, dp):
    s = w.shape[:-1]
    w = w.reshape(s + (nh, d))
    w = jnp.pad(w, [(0, 0)] * len(s) + [(0, 0), (0, dp - d)])
    return w.reshape(s + (nh * dp,))


def _pad_last(w, lo, hi):
    return jnp.pad(w, [(0, 0)] * (w.ndim - 1) + [(lo, hi)])


def _layer_spec(a, l, grid_rank):
    zeros = (0,) * (a.ndim - 1)
    once = pl.Buffered(1)
    if grid_rank == 1:
        return pl.BlockSpec((1,) + a.shape[1:], lambda bi: (l,) + zeros, pipeline_mode=once)
    return pl.BlockSpec((1,) + a.shape[1:], lambda bi, i: (l,) + zeros, pipeline_mode=once)


def _with_ones_lane(v, hd):
    lane = lax.broadcasted_iota(jnp.int32, v.shape, v.ndim - 1)
    return jnp.where(lane % LANES == hd, 1.0, v)


def _mod_kernel(c_ref, w_ref, b_ref, o_ref):
    c = c_ref[...]
    o_ref[0] = _dot(_silu(c).astype(BF16), w_ref[0].astype(BF16)) + b_ref[0]


def _modulation(cvec, w_mod, b_mod):
    depth, d, d3 = w_mod.shape
    tn = d
    return pl.pallas_call(
        _mod_kernel,
        out_shape=jax.ShapeDtypeStruct((depth, 8, d3), F32),
        grid=(depth, d3 // tn),
        in_specs=[pl.BlockSpec((8, d), lambda l, j: (0, 0)),
                  pl.BlockSpec((1, d, tn), lambda l, j: (l, 0, j)),
                  pl.BlockSpec((1, 1, tn), lambda l, j: (l, 0, j))],
        out_specs=pl.BlockSpec((1, 8, tn), lambda l, j: (l, 0, j)),
        compiler_params=_params(("arbitrary", "arbitrary"), 4 * d * tn * 4),
        name="modulation",
    )(cvec, w_mod, b_mod.reshape(depth, 1, d3))


def _rope(x, tab_ref, rows, shift):
    cos, sin_a, sin_b = tab_ref[0, rows, :], tab_ref[1, rows, :], tab_ref[2, rows, :]
    outs = []
    for h in range(x.shape[-1] // LANES):
        xh = x[:, h * LANES:(h + 1) * LANES]
        up = pltpu.roll(xh, LANES - shift, 1)
        dn = pltpu.roll(xh, shift, 1)
        outs.append(xh * cos + up * sin_a + dn * sin_b)
    return jnp.concatenate(outs, axis=-1)


def _block_ones(width, block):
    ri = lax.broadcasted_iota(jnp.int32, (width, width), 0) // block
    ci = lax.broadcasted_iota(jnp.int32, (width, width), 1) // block
    return (ri == ci).astype(F32).astype(BF16)


def _head_norm(x, gain, n, ones_bd):
    xx = (x * x).astype(BF16)
    w2 = ones_bd.shape[0]
    ss = jnp.concatenate([_dot(xx[:, j:j + w2], ones_bd) for j in range(0, x.shape[-1], w2)], axis=-1)
    return x * lax.rsqrt(ss * (1.0 / n) + EPS) * gain


def _spread_heads(x, hd):
    lo = lax.broadcasted_iota(jnp.int32, (x.shape[0], LANES), 1) < hd
    outs = []
    for j in range(0, x.shape[-1], LANES):
        pair = x[:, j:j + LANES]
        outs += [jnp.where(lo, pair, 0.0), jnp.where(lo, pltpu.roll(pair, LANES - hd, 1), 0.0)]
    return jnp.concatenate(outs, axis=-1)


def _inproj_refs(latent, refs):
    names = ['x', 'mod', 'ng', 'win', 'wuq', 'wukv', 'gains']
    if latent:
        names += ['wvta', 'wvtd', 'ra', 'rd']
    names += ['qa', 'ka', 'va', 'ag', 'qd', 'kd', 'vd', 'dg', 'xbc', 'dt', 'z', 'glu', 'cg']
    if not latent:
        names += ['ckv', 'kr', 'gk', 'gv']
    return dict(zip(names, refs))


def _inproj_main(r, rows):
    x = r['x'][0, rows, :]
    shift = r['mod'][0, 0, 0:1, :]
    scale = r['mod'][0, 0, 1:2, :]
    h = (_rms(x, D_MODEL) * r['ng'][0] * (1.0 + scale) + shift).astype(BF16)
    return h, _dot(h, r['win'][0])


def _values_t(w_t, x):
    vt = _dot_nt(w_t, x)
    row = lax.broadcasted_iota(jnp.int32, vt.shape, 0)
    return jnp.where(row % ATT_VROWS == MLA_V, 1.0, vt)


def _inproj_latents(r, u):
    g_ql = r['gains'][0, 0:1, 0:256]
    g_kv = r['gains'][0, 2:3, 0:LANES]
    ql = _rms(u[:, O_QL:O_QL + 256], MLA_Q_RANK) * g_ql
    ckv = _rms(u[:, O_KV:O_KV + LANES], MLA_KV_RANK) * g_kv
    return _dot(ql.astype(BF16), r['wuq'][0]), ckv, _dot(ckv.astype(BF16), r['wukv'][0])


def _inproj_tail(latent, r, rows, h, u, q, ckv, kv):
    wa = MLA_HEADS * LANES
    wq, wk = GQA_HEADS * GQA_HD, GQA_KV_HEADS * GQA_HD
    g_qh = r['gains'][0, 1:2, :]
    g_kh = r['gains'][0, 3:4, :]
    g_dq = r['gains'][0, 4:5, 0:wq]
    g_dk = r['gains'][0, 5:6, 0:wk]
    ones_slot = _block_ones(2 * LANES, LANES)
    ones_pair = _block_ones(2 * LANES, GQA_HD)

    q = _head_norm(q, g_qh, MLA_QK, ones_slot)
    kr = u[:, O_R:O_R + LANES]
    k = _head_norm(kv[:, 0:wa] + jnp.concatenate([kr] * MLA_HEADS, axis=-1), g_kh, MLA_QK, ones_slot)
    qd = _head_norm(u[:, O_DQ:O_DQ + wq], g_dq, GQA_HD, ones_pair)
    kd = _head_norm(u[:, O_DK:O_DK + wk], g_dk, GQA_HD, ones_pair[0:wk, 0:wk])
    vd = u[:, O_DV:O_DV + wk]
    if latent:
        q = _rope(q, r['ra'], rows, MLA_ROPE // 4)
        k = _rope(k, r['ra'], rows, MLA_ROPE // 4)
        qd = _rope(qd, r['rd'], rows, GQA_HD // 4)
        kd_att = _rope(kd, r['rd'], rows, GQA_HD // 4)
    else:
        kd_att = kd
        r['ckv'][0, rows, :] = ckv
        r['kr'][0, rows, :] = kr[:, MLA_NOPE:MLA_NOPE + MLA_ROPE]
        r['gk'][0, rows, :] = kd
        r['gv'][0, rows, :] = vd
    qd = _spread_heads(qd, GQA_HD)
    kd_att = _spread_heads(kd_att, GQA_HD)
    r['qa'][0, rows, :] = q.astype(BF16)
    r['ka'][0, rows, :] = k.astype(BF16)
    if latent:
        r['va'][0, :, rows] = _values_t(r['wvta'][0], ckv.astype(BF16)).astype(BF16)
        r['vd'][0, :, rows] = _values_t(r['wvtd'][0], h).astype(BF16)
    else:
        r['va'][0, rows, :] = _with_ones_lane(kv[:, wa:], MLA_V).astype(BF16)
        r['vd'][0, rows, :] = _with_ones_lane(_spread_heads(vd, GQA_HD), GQA_HD).astype(BF16)
    r['ag'][0, rows, :] = u[:, O_AG:O_AG + BRANCH_W]
    r['qd'][0, rows, :] = qd.astype(BF16)
    r['kd'][0, rows, :] = kd_att.astype(BF16)
    r['dg'][0, rows, :] = u[:, O_DG:O_DG + BRANCH_W]

    r['xbc'][0, rows, :] = u[:, O_XBC:O_XBC + SSD_CONV_CH]
    r['dt'][0, rows, :] = u[:, O_DT:O_DT + LANES]
    r['z'][0, rows, :] = u[:, O_Z:O_Z + SSD_W]
    r['glu'][0, rows, :] = u[:, O_CV:O_CV + CF_W] * _sigmoid(u[:, O_CGL:O_CGL + CF_W])
    r['cg'][0, rows, :] = u[:, O_CG:O_CG + CF_W]


def _inproj_kernel(latent, sub, *refs):
    r = _inproj_refs(latent, refs)
    tm = r['x'].shape[1]
    for r0 in range(0, tm, sub):
        rows = slice(r0, r0 + sub)
        h, u = _inproj_main(r, rows)
        q, ckv, kv = _inproj_latents(r, u)
        _inproj_tail(latent, r, rows, h, u, q, ckv, kv)


def _mod_spec(mod, l, row):
    d = mod.shape[-1]
    if row is None:
        return pl.BlockSpec((1, 1, 3, d), lambda bi, i: (l, bi, 0, 0))
    return pl.BlockSpec((1, 1, 3, d), lambda bi, i: (l, row, 0, 0))


def _inproj(x, mod, mod_row, l, w, ropes, tm):
    b, s, d = x.shape
    latent = ropes is not None
    tok = lambda wd: pl.BlockSpec((1, tm, wd), lambda bi, i: (bi, i, 0))
    weights = [w['ng'], w['win'], w['wuq'], w['wukv'], w['gains']]
    if latent:
        weights += [w['wvta'], w['wvtd']]
    in_specs = [tok(d), _mod_spec(mod, l, mod_row)] + [_layer_spec(a, l, 2) for a in weights]
    args = [x, mod] + weights
    if latent:
        in_specs += [pl.BlockSpec((3, tm, LANES), lambda bi, i: (0, i, 0))] * 2
        args += list(ropes)
    widths = [(4 * LANES, BF16), (4 * LANES, BF16), (4 * LANES, BF16), (BRANCH_W, F32),
              (4 * LANES, BF16), (2 * LANES, BF16), (2 * LANES, BF16), (BRANCH_W, F32),
              (SSD_CONV_CH, F32), (LANES, F32), (SSD_W, F32), (CF_W, F32), (CF_W, F32)]
    if not latent:
        widths += [(MLA_KV_RANK, F32), (MLA_ROPE, F32), (LANES, F32), (LANES, F32)]
    out_shape = [jax.ShapeDtypeStruct((b, s, wd), dt) for wd, dt in widths]
    out_specs = [tok(wd) for wd, _ in widths]
    if latent:
        for pos, heads in ((2, MLA_HEADS), (6, GQA_KV_HEADS)):
            out_shape[pos] = jax.ShapeDtypeStruct((b, heads * ATT_VROWS, s), BF16)
            out_specs[pos] = pl.BlockSpec((1, heads * ATT_VROWS, tm), lambda bi, i: (bi, 0, i))
    out_bytes = sum(wd * jnp.dtype(dt).itemsize for wd, dt in widths) * tm
    w_bytes = sum(a[0].size * a.dtype.itemsize for a in weights)
    sub = min(INPROJ_SUB, tm)
    vmem = 2 * (tm * d * 4 + w_bytes + out_bytes) + 3 * sub * IN_WP * 4
    return pl.pallas_call(
        functools.partial(_inproj_kernel, latent, sub),
        out_shape=out_shape,
        grid=(b, s // tm),
        in_specs=in_specs,
        out_specs=out_specs,
        compiler_params=_params(("parallel", "parallel"), vmem),
        name="inproj_lat" if latent else "inproj_ctx",
    )(*args)


def _ctxkv_kernel(ckv_ref, kr_ref, gk_ref, gv_ref, wukv_ref, wvta_ref, gains_ref, kc_ref, vc_ref, kdc_ref, vdc_ref):
    ckv = ckv_ref[0, 0].astype(BF16)
    kv = _dot(ckv, wukv_ref[0])
    kr = kr_ref[0, 0]
    g = gains_ref[0, 3:4, 0:LANES]
    for i in range(MLA_HEADS):
        k = _rms(kv[:, i * LANES:(i + 1) * LANES] + kr, MLA_QK) * g
        kc_ref[0, :, i * LANES:(i + 1) * LANES] = k.astype(BF16)
    vc_ref[0] = _values_t(wvta_ref[0], ckv).astype(BF16)
    gk = gk_ref[0, 0]
    p = gk.shape[0]
    pad = jnp.zeros((p, LANES - GQA_HD), BF16)
    for i in range(GQA_KV_HEADS):
        lo = i * LANES
        kdc_ref[0, :, lo:lo + GQA_HD] = gk[:, i * GQA_HD:(i + 1) * GQA_HD].astype(BF16)
        kdc_ref[0, :, lo + GQA_HD:lo + LANES] = pad
    gvt = gv_ref[0, 0].T
    row = lax.broadcasted_iota(jnp.int32, (ATT_VROWS - GQA_HD, p), 0)
    tail = jnp.where(row == 0, 1.0, 0.0)
    pieces = []
    for i in range(GQA_KV_HEADS):
        pieces += [gvt[i * GQA_HD:(i + 1) * GQA_HD, :], tail]
    vdc_ref[0] = jnp.concatenate(pieces, axis=0).astype(BF16)


def _ctx_kv(ckv, kr_pad, gk, gv, l, w):
    b, _, p, _ = ckv.shape
    cache = lambda a: pl.BlockSpec((1, 1, p, a.shape[-1]), lambda bi: (bi, l, 0, 0))
    out = lambda wd: pl.BlockSpec((1, p, wd), lambda bi: (bi, 0, 0))
    out_t = lambda rows: pl.BlockSpec((1, rows, p), lambda bi: (bi, 0, 0))
    wa, wd = MLA_HEADS * LANES, GQA_KV_HEADS * LANES
    ra, rd = MLA_HEADS * ATT_VROWS, GQA_KV_HEADS * ATT_VROWS
    return pl.pallas_call(
        _ctxkv_kernel,
        out_shape=[jax.ShapeDtypeStruct((b, p, wa), BF16), jax.ShapeDtypeStruct((b, ra, p), BF16),
                   jax.ShapeDtypeStruct((b, p, wd), BF16), jax.ShapeDtypeStruct((b, rd, p), BF16)],
        grid=(b,),
        in_specs=[cache(ckv), cache(kr_pad), cache(gk), cache(gv),
                  _layer_spec(w['wukv'], l, 1), _layer_spec(w['wvta'], l, 1), _layer_spec(w['gains'], l, 1)],
        out_specs=[out(wa), out_t(ra), out(wd), out_t(rd)],
        compiler_params=_params(("parallel",), 16 * p * wa * 4),
        name="ctx_kv",
    )(ckv, kr_pad, gk, gv, w['wukv'], w['wvta'], w['gains'])


def _lane_tile_max(t):
    r = t[:, 0:LANES]
    for j in range(1, t.shape[-1] // LANES):
        r = jnp.maximum(r, t[:, j * LANES:(j + 1) * LANES])
    return r


def _attn_unit(n_heads, rep, refs, bi, rows, h, s_ref):
    q_ref, k_ref, v_ref, g_ref, o_ref = refs
    s_len = k_ref.shape[1]
    hd = o_ref.shape[-1] // n_heads
    kb = min(ATT_QK_CHUNK, s_len)
    kb2 = ATT_PV_CHUNK
    g = h // rep
    hs = slice(h * LANES, (h + 1) * LANES)
    gs = slice(g * LANES, (g + 1) * LANES)
    qh = q_ref[bi, rows, hs]
    mx = None
    for c in range(s_len // kb):
        sc = _dot_nt(qh, k_ref[bi, c * kb:(c + 1) * kb, gs])
        s_ref[:, c * kb:(c + 1) * kb] = sc
        t = _lane_tile_max(sc)
        mx = t if mx is None else jnp.maximum(mx, t)
    m = jnp.max(mx, axis=-1, keepdims=True)
    acc = None
    for c in range(s_len // kb2):
        r0 = c * kb2
        p = jnp.exp2(s_ref[:, r0:r0 + kb2] - m).astype(BF16)
        part = _dot(p, v_ref[bi, r0:r0 + kb2, gs])
        acc = part if acc is None else acc + part
    o = acc[:, 0:hd] / acc[:, hd:hd + 1]
    os_ = slice(h * hd, (h + 1) * hd)
    o_ref[bi, rows, os_] = (o * _silu(g_ref[bi, rows, os_])).astype(o_ref.dtype)


def _attn_kernel(n_heads, rep, sub, *refs):
    s0_ref, s1_ref = refs[-2:]
    tq = refs[0].shape[1]
    unit = 0
    for r0 in range(0, tq, sub):
        for h in range(n_heads):
            _attn_unit(n_heads, rep, refs[:-2], 0, slice(r0, r0 + sub), h, s0_ref if unit % 2 == 0 else s1_ref)
            unit += 1


def _attention(q, k, v, gate, n_heads, rep, tq, sub, name):
    b, s, qw = q.shape
    kw = k.shape[-1]
    ow = gate.shape[-1]
    assert s % min(ATT_QK_CHUNK, s) == 0 and s % ATT_PV_CHUNK == 0 and s % tq == 0 and tq % sub == 0
    tile = lambda wd: pl.BlockSpec((1, tq, wd), lambda bi, i: (bi, i, 0))
    whole = pl.BlockSpec((1, s, kw), lambda bi, i: (bi, 0, 0))
    vmem = 2 * (tq * qw * 2 + 2 * s * kw * 2 + tq * ow * 6) + 2 * sub * s * 4 + 2 * sub * 1024 * 4
    return pl.pallas_call(
        functools.partial(_attn_kernel, n_heads, rep, sub),
        out_shape=jax.ShapeDtypeStruct((b, s, ow), BF16),
        grid=(b, s // tq),
        in_specs=[tile(qw), whole, whole, tile(ow)],
        out_specs=tile(ow),
        scratch_shapes=[pltpu.VMEM((sub, s), F32), pltpu.VMEM((sub, s), F32)],
        compiler_params=_params(("parallel", "arbitrary"), vmem),
        name=name,
    )(q, k, v, gate)


def _attn_t_scores(io, rep, sub, unit, s_ref):
    q_ref, k_ref, _, kc_ref = io[:4]
    r0, h = unit
    s_len, p_len = k_ref.shape[1], kc_ref.shape[1]
    kb = min(ATT_QK_CHUNK, s_len)
    gs = slice((h // rep) * LANES, (h // rep + 1) * LANES)
    qh = q_ref[0, r0:r0 + sub, h * LANES:(h + 1) * LANES]
    mx = None
    for c in range(s_len // kb):
        st = _dot_nt(k_ref[0, c * kb:(c + 1) * kb, gs], qh)
        s_ref[c * kb:(c + 1) * kb, :] = st
        t = jnp.max(st, axis=0, keepdims=True)
        mx = t if mx is None else jnp.maximum(mx, t)
    st = _dot_nt(kc_ref[0, :, gs], qh)
    s_ref[s_len:s_len + p_len, :] = st
    return jnp.maximum(mx, jnp.max(st, axis=0, keepdims=True))


def _attn_t_values(io, n_heads, rep, sub, unit, s_ref, m, ot_ref):
    _, k_ref, vt_ref, kc_ref, vct_ref, _, o_ref = io
    r0, h = unit
    s_len, p_len = k_ref.shape[1], kc_ref.shape[1]
    hd = o_ref.shape[-1] // n_heads
    vs = slice((h // rep) * ATT_VROWS, (h // rep + 1) * ATT_VROWS)
    kb2 = ATT_PV_CHUNK
    acc = None
    for c in range((s_len + p_len) // kb2):
        k0 = c * kb2
        pt = jnp.exp2(s_ref[k0:k0 + kb2, :] - m).astype(BF16)
        vv = vt_ref[0, vs, k0:k0 + kb2] if k0 < s_len else vct_ref[0, vs, k0 - s_len:k0 - s_len + kb2]
        part = _dot(vv, pt)
        acc = part if acc is None else acc + part
    ot_ref[h * hd:(h + 1) * hd, r0:r0 + sub] = acc[0:hd, :] / acc[hd:hd + 1, :]


def _attn_t_kernel(n_heads, rep, sub, *refs):
    io, (s0_ref, s1_ref, ot_ref) = refs[:-3], refs[-3:]
    tq = io[0].shape[1]
    units = [(r0, h) for r0 in range(0, tq, sub) for h in range(n_heads)]
    bufs = (s0_ref, s1_ref)
    m = _attn_t_scores(io, rep, sub, units[0], bufs[0])
    for u in range(len(units)):
        m_next = None
        if u + 1 < len(units):
            m_next = _attn_t_scores(io, rep, sub, units[u + 1], bufs[(u + 1) % 2])
        _attn_t_values(io, n_heads, rep, sub, units[u], bufs[u % 2], m, ot_ref)
        m = m_next
    g_ref, o_ref = io[5], io[6]
    o_ref[0] = (ot_ref[...].T * _silu(g_ref[0])).astype(o_ref.dtype)


def _attention_t(q, k, vt, ctx, gate, n_heads, rep, tq, sub, name):
    b, s, qw = q.shape
    kw = k.shape[-1]
    vr = vt.shape[1]
    ow = gate.shape[-1]
    kc, vct = ctx
    p = kc.shape[1]
    assert s % min(ATT_QK_CHUNK, s) == 0 and s % ATT_PV_CHUNK == 0 and p % ATT_PV_CHUNK == 0
    assert s % tq == 0 and tq % sub == 0
    tile = lambda wd: pl.BlockSpec((1, tq, wd), lambda bi, i: (bi, i, 0))
    whole = lambda a: pl.BlockSpec((1,) + a.shape[1:], lambda bi, i: (bi, 0, 0))
    vmem = (2 * (tq * qw * 2 + (s + p) * (kw + vr) * 2 + tq * ow * 6)
            + 2 * sub * (s + p) * 4 + ow * tq * 4 + 2 * sub * 1024 * 4)
    return pl.pallas_call(
        functools.partial(_attn_t_kernel, n_heads, rep, sub),
        out_shape=jax.ShapeDtypeStruct((b, s, ow), BF16),
        grid=(b, s // tq),
        in_specs=[tile(qw), whole(k), whole(vt), whole(kc), whole(vct), tile(ow)],
        out_specs=tile(ow),
        scratch_shapes=[pltpu.VMEM((s + p, sub), F32), pltpu.VMEM((s + p, sub), F32), pltpu.VMEM((ow, tq), F32)],
        compiler_params=_params(("parallel", "arbitrary"), vmem),
        name=name,
    )(q, k, vt, kc, vct, gate)


def _conv_kernel(tl, xc_ref, xp_ref, xn_ref, gc_ref, gp_ref, gn_ref, cg_ref,
                 w3_ref, b3_ref, w31_ref, b31_ref, lng_ref, lnb_ref,
                 xo_ref, co_ref, ext3, ext31):
    i = pl.program_id(1)
    keep_prev = (i > 0).astype(F32)
    keep_next = (i < pl.num_programs(1) - 1).astype(F32)

    h3 = CONV_HALO_SSD
    ext3[0:h3, :] = xp_ref[0] * keep_prev
    ext3[h3:h3 + tl, :] = xc_ref[0]
    ext3[h3 + tl:2 * h3 + tl, :] = xn_ref[0] * keep_next
    acc = jnp.zeros((tl, SSD_CONV_CH), F32) + b3_ref[0]
    for k in range(SSD_CONV):
        acc = acc + w3_ref[0, k:k + 1, :] * ext3[h3 - SSD_CONV // 2 + k:h3 - SSD_CONV // 2 + k + tl, :]
    xo_ref[0] = _silu(acc)

    h31 = CONV_HALO_CF
    ext31[0:h31, :] = gp_ref[0] * keep_prev
    ext31[h31:h31 + tl, :] = gc_ref[0]
    ext31[h31 + tl:2 * h31 + tl, :] = gn_ref[0] * keep_next
    acc = jnp.zeros((tl, CF_W), F32) + b31_ref[0]
    o_lo = h31 - CF_K // 2
    ext = ext31[...]
    n_ext = ext.shape[0]
    for r in range(8):
        taps = [k for k in range(CF_K) if (o_lo + k) % 8 == r]
        if not taps:
            continue
        shifted = ext if r == 0 else pltpu.roll(ext, n_ext - r, 0)
        for k in taps:
            a = (o_lo + k) // 8 * 8
            acc = acc + w31_ref[0, k:k + 1, :] * shifted[a:a + tl, :]
    mu = jnp.mean(acc, axis=-1, keepdims=True)
    cen = acc - mu
    var = jnp.mean(cen * cen, axis=-1, keepdims=True)
    y = cen * lax.rsqrt(var + EPS) * lng_ref[0] + lnb_ref[0]
    co_ref[0] = (_silu(y) * _silu(cg_ref[0])).astype(co_ref.dtype)


def _convs(xbc, glu, cg, l, w, tl):
    b, s, _ = xbc.shape
    nt = s // tl

    def cur(wd):
        return pl.BlockSpec((1, tl, wd), lambda bi, i: (bi, i, 0))

    def prev(wd, hrows):
        r = tl // hrows
        return pl.BlockSpec((1, hrows, wd), lambda bi, i: (bi, jnp.maximum(i * r - 1, 0), 0))

    def nxt(wd, hrows):
        r = tl // hrows
        return pl.BlockSpec((1, hrows, wd), lambda bi, i: (bi, jnp.minimum((i + 1) * r, nt * r - 1), 0))

    weights = [w['w3'], w['b3'], w['w31'], w['b31'], w['lng'], w['lnb']]
    vmem = 4 * tl * (SSD_CONV_CH * 2 + CF_W * 3) * 4 + 6 * tl * SSD_CONV_CH * 4
    return pl.pallas_call(
        functools.partial(_conv_kernel, tl),
        out_shape=[jax.ShapeDtypeStruct((b, s, SSD_CONV_CH), F32),
                   jax.ShapeDtypeStruct((b, s, CF_W), BF16)],
        grid=(b, nt),
        in_specs=[cur(SSD_CONV_CH), prev(SSD_CONV_CH, CONV_HALO_SSD), nxt(SSD_CONV_CH, CONV_HALO_SSD),
                  cur(CF_W), prev(CF_W, CONV_HALO_CF), nxt(CF_W, CONV_HALO_CF), cur(CF_W)]
                 + [_layer_spec(a, l, 2) for a in weights],
        out_specs=[cur(SSD_CONV_CH), cur(CF_W)],
        scratch_shapes=[pltpu.VMEM((tl + 2 * CONV_HALO_SSD, SSD_CONV_CH), F32),
                        pltpu.VMEM((tl + 2 * CONV_HALO_CF, CF_W), F32)],
        compiler_params=_params(("parallel", "parallel"), vmem),
        name="convs",
    )(xbc, xbc, xbc, glu, glu, glu, cg, *weights)


def _split3(x):
    hi = x.astype(BF16)
    r1 = x - hi.astype(F32)
    mid = r1.astype(BF16)
    lo = (r1 - mid.astype(F32)).astype(BF16)
    return hi, mid, lo


def _cumsum_rows(tri, x):
    return sum(_dot(tri, part) for part in _split3(x))


def _transpose_exact(x):
    c = x.shape[1]
    eye = (lax.broadcasted_iota(jnp.int32, (c, c), 0) == lax.broadcasted_iota(jnp.int32, (c, c), 1))
    eye = eye.astype(F32).astype(BF16)
    return sum(_dot_nt(eye, part) for part in _split3(x))


SSD_NK = 2 * SSD_HEADS


def _ssd_stage_local(c, d, xc_ref, dt_ref, bias, a_row):
    q = SSD_CHUNK
    rows = pl.ds(pl.multiple_of(c * q, q), q)
    t = dict(rows=rows, d=d)
    t['xs'] = xc_ref[0, rows, 0:SSD_W]
    t['bm'] = xc_ref[0, rows, SSD_W:SSD_W + LANES]
    t['cm'] = xc_ref[0, rows, SSD_W + LANES:SSD_W + 2 * LANES].astype(BF16)
    t['dt'] = _softplus(dt_ref[0, rows, :] + bias)
    ri = lax.broadcasted_iota(jnp.int32, (q, q), 0)
    ci = lax.broadcasted_iota(jnp.int32, (q, q), 1)
    t['mask'] = (ri >= ci) if d == 0 else (ri <= ci)
    t['acum'] = _cumsum_rows(t['mask'].astype(F32).astype(BF16), t['dt'] * a_row)
    return t


def _ssd_stage_cb(t):
    q = SSD_CHUNK
    lane = lax.broadcasted_iota(jnp.int32, (q, LANES), 1)
    t['tr'] = jnp.where(lane < SSD_NK, t['dt'], t['acum']).T
    bt = t['bm'].T
    t['bt'] = bt
    grp_row = lax.broadcasted_iota(jnp.int32, (LANES, q), 0) < SSD_N
    t['cb'] = [_dot(t['cm'], jnp.where(grp_row if g == 0 else jnp.logical_not(grp_row), bt, 0.0).astype(BF16))
               for g in range(SSD_GROUPS)]
    t['last'] = q - 1 if t['d'] == 0 else 0
    t['etot'] = jnp.exp(t['acum'][t['last']:t['last'] + 1, :])
    t['eac'] = jnp.exp(t['acum'])


def _ssd_stage_diag(t, dskip):
    q = SSD_CHUNK
    d = t['d']
    lo_half = lax.broadcasted_iota(jnp.int32, (q, LANES), 1) < SSD_P
    tr, acum, last = t['tr'], t['acum'], t['last']
    t['yd'], t['cs'] = [], []
    for g in range(SSD_GROUPS):
        k0 = d * SSD_HEADS + 2 * g
        xg = t['xs'][:, g * LANES:(g + 1) * LANES]
        xgb = xg.astype(BF16)
        btg = t['bt'][g * SSD_N:(g + 1) * SSD_N, :]
        yg = None
        cs = []
        for hh in range(2):
            k = k0 + hh
            dt_row = tr[k:k + 1, :]
            ac_row = tr[SSD_NK + k:SSD_NK + k + 1, :]
            dec = jnp.where(t['mask'], jnp.exp(acum[:, k:k + 1] - ac_row), 0.0)
            xm = jnp.where(lo_half if hh == 0 else jnp.logical_not(lo_half), xg, 0.0).astype(BF16)
            part = _dot((t['cb'][g] * dec * dt_row).astype(BF16), xm)
            yg = part if yg is None else yg + part
            w_row = dt_row * jnp.exp(tr[SSD_NK + k:SSD_NK + k + 1, last:last + 1] - ac_row)
            cs.append(_dot((btg * w_row).astype(BF16), xgb))
        if d == 0:
            yg = yg + dskip[:, g * LANES:(g + 1) * LANES] * xg
        t['yd'].append(yg)
        t['cs'].append(cs)


def _ssd_stage_state(t, states):
    q = SSD_CHUNK
    d = t['d']
    lo_half = lax.broadcasted_iota(jnp.int32, (q, LANES), 1) < SSD_P
    lane_n = lax.broadcasted_iota(jnp.int32, (SSD_N, LANES), 1) < SSD_P
    zero = jnp.zeros((SSD_N, LANES), F32)
    ys, new_states = [], []
    for g in range(SSD_GROUPS):
        k0 = d * SSD_HEADS + 2 * g
        s_prev = states[g]
        sz = jnp.concatenate([s_prev, zero] if g == 0 else [zero, s_prev], axis=0).astype(BF16)
        eacs = jnp.where(lo_half, t['eac'][:, k0:k0 + 1], t['eac'][:, k0 + 1:k0 + 2])
        ys.append(t['yd'][g] + _dot(t['cm'], sz) * eacs)
        cdec = jnp.where(lane_n, t['etot'][:, k0:k0 + 1], t['etot'][:, k0 + 1:k0 + 2])
        new_states.append(cdec * s_prev + jnp.where(lane_n, t['cs'][g][0], t['cs'][g][1]))
    return ys, new_states


def _ssd_kernel(nc, unroll, has_init, *refs):
    if has_init:
        xc_ref, dt_ref, s0_ref, bias_ref, alog_ref, d_ref, yf_ref, yb_ref, so_ref, st_ref = refs
        for d in range(2):
            for g in range(SSD_GROUPS):
                st_ref[0, d, g] = _transpose_exact(s0_ref[0, 0, d, g])
    else:
        xc_ref, dt_ref, bias_ref, alog_ref, d_ref, yf_ref, yb_ref, so_ref, st_ref = refs
        st_ref[...] = jnp.zeros(st_ref.shape, F32)
    lane = lax.broadcasted_iota(jnp.int32, (1, LANES), 1)
    a_row = jnp.where(lane < 2 * SSD_NK, -jnp.exp(alog_ref[0]), 0.0)
    bias = bias_ref[0]
    dskip = d_ref[0]

    def body(i, carry):
        states = [[st_ref[0, d, g] for g in range(SSD_GROUPS)] for d in range(2)]
        work = []
        for j in range(unroll):
            c = i * unroll + j
            work.append((_ssd_stage_local(c, 0, xc_ref, dt_ref, bias, a_row), yf_ref))
            work.append((_ssd_stage_local(nc - 1 - c, 1, xc_ref, dt_ref, bias, a_row), yb_ref))
        for t, _ in work:
            _ssd_stage_cb(t)
        for t, _ in work:
            _ssd_stage_diag(t, dskip)
        writes = []
        for t, y_ref in work:
            ys, states[t['d']] = _ssd_stage_state(t, states[t['d']])
            writes.append((y_ref, t['rows'], ys))
        for y_ref, rows, ys in writes:
            for g in range(SSD_GROUPS):
                y_ref[0, rows, g * LANES:(g + 1) * LANES] = ys[g]
        for d in range(2):
            for g in range(SSD_GROUPS):
                st_ref[0, d, g] = states[d][g]
        return carry

    lax.fori_loop(0, nc // unroll, body, 0)
    for d in range(2):
        for g in range(SSD_GROUPS):
            so_ref[0, d, g] = _transpose_exact(st_ref[0, d, g])


def _ssd(xc, dt, s0, l, w):
    b, s, _ = xc.shape
    nc = s // SSD_CHUNK
    unroll = min(SSD_UNROLL, nc)
    assert nc % unroll == 0
    st_shape = (b, 2, SSD_GROUPS, 2 * SSD_P, SSD_N)
    st_spec = pl.BlockSpec((1,) + st_shape[1:], lambda bi: (bi, 0, 0, 0, 0))
    y_spec = pl.BlockSpec((1, s, SSD_W), lambda bi: (bi, 0, 0))
    in_specs = [pl.BlockSpec((1, s, SSD_CONV_CH), lambda bi: (bi, 0, 0)),
                pl.BlockSpec((1, s, LANES), lambda bi: (bi, 0, 0))]
    args = [xc, dt]
    if s0 is not None:
        in_specs.append(pl.BlockSpec((1, 1) + st_shape[1:], lambda bi: (bi, l, 0, 0, 0, 0)))
        args.append(s0)
    weights = [w['dt_bias'], w['a_log'], w['dskip']]
    vmem = 2 * s * (SSD_CONV_CH + LANES + 2 * SSD_W) * 4 + 8 * 1024 * 1024
    return pl.pallas_call(
        functools.partial(_ssd_kernel, nc, unroll, s0 is not None),
        out_shape=[jax.ShapeDtypeStruct((b, s, SSD_W), F32), jax.ShapeDtypeStruct((b, s, SSD_W), F32),
                   jax.ShapeDtypeStruct(st_shape, F32)],
        grid=(b,),
        in_specs=in_specs + [_layer_spec(a, l, 1) for a in weights],
        out_specs=[y_spec, y_spec, st_spec],
        scratch_shapes=[pltpu.VMEM((1, 2, SSD_GROUPS, SSD_N, 2 * SSD_P), F32)],
        compiler_params=_params(("parallel",), vmem),
        name="ssd",
    )(*args, *weights)


def _outproj_kernel(oa_ref, yf_ref, yb_ref, z_ref, oc_ref, od_ref, x_ref, mod_ref, ng_ref, w_ref, o_ref):
    ob = _rms((yf_ref[0] + yb_ref[0]) * _silu(z_ref[0]), SSD_W) * ng_ref[0]
    acc = _dot(oa_ref[0], w_ref[0, 0:BRANCH_W, :])
    acc = acc + _dot(ob.astype(BF16), w_ref[0, BRANCH_W:2 * BRANCH_W, :])
    acc = acc + _dot(oc_ref[0], w_ref[0, 2 * BRANCH_W:3 * BRANCH_W, :])
    acc = acc + _dot(od_ref[0], w_ref[0, 3 * BRANCH_W:4 * BRANCH_W, :])
    o_ref[0] = x_ref[0] + mod_ref[0, 0, 2:3, :] * acc


def _outproj(oa, yf, yb, z, oc, od, x, mod, mod_row, l, w, tm):
    b, s, d = x.shape
    tm = OUTPROJ_ROWS_PER_INPROJ_STEP * tm if s % (OUTPROJ_ROWS_PER_INPROJ_STEP * tm) == 0 else tm
    br = lambda: pl.BlockSpec((1, tm, BRANCH_W), lambda bi, i: (bi, i, 0))
    xs = pl.BlockSpec((1, tm, d), lambda bi, i: (bi, i, 0))
    vmem = 2 * (6 * tm * BRANCH_W * 4 + 2 * tm * d * 4) + d * d * 2 + 2 * tm * d * 4
    return pl.pallas_call(
        _outproj_kernel,
        out_shape=jax.ShapeDtypeStruct((b, s, d), F32),
        grid=(b, s // tm),
        in_specs=[br(), br(), br(), br(), br(), br(), xs, _mod_spec(mod, l, mod_row),
                  _layer_spec(w['ssd_ng'], l, 2), _layer_spec(w['wout'], l, 2)],
        out_specs=xs,
        compiler_params=_params(("parallel", "parallel"), vmem),
        name="outproj",
    )(oa, yf, yb, z, oc, od, x, mod, w['ssd_ng'], w['wout'])


def _rope_table(n_tokens, dim, lane0):
    n_rows = n_tokens // GRID_W
    quarter = dim // 4
    inv = ROPE_THETA ** (-jnp.arange(quarter, dtype=F32) / quarter)
    ar = jnp.arange(n_rows, dtype=F32)[:, None] * inv
    ac = jnp.arange(GRID_W, dtype=F32)[:, None] * inv
    zr, zc = jnp.zeros_like(ar), jnp.zeros_like(ac)
    hi = LANES - lane0 - dim

    def lanes(row_part, col_part):
        rp = _pad_last(jnp.concatenate(row_part, axis=-1), lane0, hi)
        cp = _pad_last(jnp.concatenate(col_part, axis=-1), lane0, hi)
        return (rp[:, None, :] + cp[None, :, :]).reshape(n_tokens, LANES)

    off_rotary = jnp.pad(jnp.zeros((1, dim), F32), ((0, 0), (lane0, hi)), constant_values=1.0)
    cos = lanes([jnp.cos(ar), jnp.cos(ar), zr, zr], [zc, zc, jnp.cos(ac), jnp.cos(ac)]) + off_rotary
    sin_a = lanes([-jnp.sin(ar), zr, zr, zr], [zc, zc, -jnp.sin(ac), zc])
    sin_b = lanes([zr, jnp.sin(ar), zr, zr], [zc, zc, zc, jnp.sin(ac)])
    return jnp.stack([cos, sin_a, sin_b])


def _prepare_weights(w_in, w_out, norm_g, mla_q_norm_g, mla_w_uq, mla_kv_norm_g, mla_w_ukv, mla_q_head_g,
                     mla_k_head_g, gqa_q_g, gqa_k_g, ssd_conv_w, ssd_conv_b, ssd_dt_bias, ssd_a_log, ssd_d,
                     ssd_norm_g, cf_conv_w, cf_conv_b, cf_ln_g, cf_ln_b):
    depth = w_in.shape[0]
    pieces = []
    o = 0
    for sz in SPLITS:
        pieces.append(w_in[:, :, o:o + sz])
        o += sz
    (a_q, a_kv, a_r, a_g, b_xbc, b_dt, b_z, c_v, c_gl, c_g, d_q, d_k, d_v, d_g) = pieces
    win = jnp.concatenate([
        _pad_last(a_q, 0, 256 - MLA_Q_RANK), a_kv, _pad_last(a_r, MLA_NOPE, LANES - MLA_QK), a_g,
        b_xbc, _pad_last(jnp.concatenate([b_dt, b_dt], axis=-1), 0, LANES - 2 * SSD_NK), b_z, c_v, c_gl, c_g,
        d_q, d_k, d_v, d_g], axis=-1).astype(BF16)
    wuq = jnp.pad(_pad_heads(mla_w_uq, MLA_HEADS, MLA_QK, LANES),
                  ((0, 0), (0, 256 - MLA_Q_RANK), (0, 0))).astype(BF16)
    kv = mla_w_ukv.reshape(depth, MLA_KV_RANK, MLA_HEADS, MLA_NOPE + MLA_V)
    wk = _pad_last(kv[..., :MLA_NOPE], 0, LANES - MLA_NOPE).reshape(depth, MLA_KV_RANK, -1)
    wv = _pad_last(kv[..., MLA_NOPE:], 0, LANES - MLA_V).reshape(depth, MLA_KV_RANK, -1)
    wukv = jnp.concatenate([wk, wv], axis=-1).astype(BF16)
    rows_t = lambda wt: jnp.pad(wt, ((0, 0), (0, 0), (0, ATT_VROWS - wt.shape[2]), (0, 0)))
    wvta = rows_t(kv[..., MLA_NOPE:].transpose(0, 2, 3, 1)).reshape(depth, MLA_HEADS * ATT_VROWS, MLA_KV_RANK)
    wvtd = rows_t(d_v.reshape(depth, -1, GQA_KV_HEADS, GQA_HD).transpose(0, 2, 3, 1))
    wvtd = wvtd.reshape(depth, GQA_KV_HEADS * ATT_VROWS, -1)
    gw = MLA_HEADS * LANES
    row = lambda v: _pad_last(v, 0, gw - v.shape[-1])
    slots = lambda v, n: row(jnp.tile(_pad_last(v, 0, LANES - v.shape[-1]), (1, n)))
    zero = jnp.zeros((depth, gw), F32)
    gains = jnp.stack([row(mla_q_norm_g),
                       slots(mla_q_head_g * (MLA_QK ** -0.5 * LOG2E), MLA_HEADS),
                       row(mla_kv_norm_g),
                       slots(mla_k_head_g, MLA_HEADS),
                       row(jnp.tile(gqa_q_g * (GQA_HD ** -0.5 * LOG2E), (1, GQA_HEADS))),
                       row(jnp.tile(gqa_k_g, (1, GQA_KV_HEADS))), zero, zero], axis=1)
    vec = lambda v: v[:, None, :]
    lanes8 = lambda v: _pad_last(jnp.tile(v.reshape(depth, 1, -1), (1, 1, 2)), 0, LANES - 2 * SSD_NK)
    return dict(win=win, wuq=wuq, wukv=wukv, wvta=wvta.astype(BF16), wvtd=wvtd.astype(BF16), gains=gains,
                wout=w_out.astype(BF16), ng=vec(norm_g),
                w3=ssd_conv_w, b3=vec(ssd_conv_b), w31=cf_conv_w, b31=vec(cf_conv_b),
                lng=vec(cf_ln_g), lnb=vec(cf_ln_b), dt_bias=lanes8(ssd_dt_bias), a_log=lanes8(ssd_a_log),
                dskip=vec(jnp.repeat(ssd_d, SSD_P, axis=-1)), ssd_ng=vec(ssd_norm_g))


def _mixer_layer(x, mod, mod_row, l, w, caches, ropes, tm):
    latent = caches is not None
    b, s, d = x.shape
    flat = (lambda a: a) if latent else (lambda a: a.reshape(1, b * s, a.shape[-1]))
    unflat = (lambda a: a) if latent else (lambda a: a.reshape(b, s, a.shape[-1]))
    outs = [unflat(o) for o in _inproj(flat(x), mod, mod_row, l, w, ropes, tm)]
    qa, ka, va, ag, qd, kd, vd, dg, xbc, dt, z, glu, cg = outs[:13]
    ctx_a = ctx_d = s0 = None
    if latent:
        kc, vc, kdc, vdc = _ctx_kv(caches['ckv'], caches['kr_pad'], caches['gk'], caches['gv'], l, w)
        ctx_a, ctx_d, s0 = (kc, vc), (kdc, vdc), caches['state']
    tq = min(ATT_TQ, s)
    sub = min(ATT_SUB, tq)
    rep = GQA_HEADS // GQA_KV_HEADS
    if latent:
        oa = _attention_t(qa, ka, va, ctx_a, ag, MLA_HEADS, 1, tq, sub, "attn_mla")
        od = _attention_t(qd, kd, vd, ctx_d, dg, GQA_HEADS, rep, tq, sub, "attn_gqa")
    else:
        oa = _attention(qa, ka, va, ag, MLA_HEADS, 1, tq, sub, "attn_mla_ctx")
        od = _attention(qd, kd, vd, dg, GQA_HEADS, rep, tq, sub, "attn_gqa_ctx")
    xconv, oc = _convs(xbc, glu, cg, l, w, min(CONV_TL, s))
    yf, yb, s_fin = _ssd(xconv, dt, s0, l, w)
    y = unflat(_outproj(flat(oa), flat(yf), flat(yb), flat(z), flat(oc), flat(od), flat(x), mod, mod_row, l, w, tm))
    new_ctx = None
    if not latent:
        ckv, kr, gk, gv = outs[13:]
        new_ctx = (ckv, kr, gk.reshape(b, -1, GQA_KV_HEADS, GQA_HD), gv.reshape(b, -1, GQA_KV_HEADS, GQA_HD),
                   s_fin.reshape(b, 2, SSD_HEADS, SSD_P, SSD_N))
    return y, new_ctx


def kernel(x_prompt, x_sample, cache_mla_ckv, cache_mla_krope, cache_gqa_k, cache_gqa_v, state_ssd, c, c_ctx, w_mod, b_mod, norm_g, w_in, w_out, mla_q_norm_g, mla_w_uq, mla_kv_norm_g, mla_w_ukv, mla_q_head_g, mla_k_head_g, ssd_conv_w, ssd_conv_b, ssd_dt_bias, ssd_a_log, ssd_d, ssd_norm_g, cf_conv_w, cf_conv_b, cf_ln_g, cf_ln_b, gqa_q_g, gqa_k_g):
    depth = w_in.shape[0]
    dec_b, n_lat = x_sample.shape[0], x_sample.shape[1]
    seq = x_prompt.shape[1]
    past = cache_mla_ckv.shape[2]
    assert dec_b < 8 and x_prompt.shape[-1] == D_MODEL and w_in.shape[-1] == sum(SPLITS)

    cvec = jnp.concatenate([c, c_ctx[None, :], jnp.zeros((8 - dec_b - 1, D_MODEL), F32)], axis=0)
    mod = _modulation(cvec, w_mod, b_mod).reshape(depth, 8, 3, D_MODEL)
    w = _prepare_weights(w_in, w_out, norm_g, mla_q_norm_g, mla_w_uq, mla_kv_norm_g, mla_w_ukv, mla_q_head_g,
                         mla_k_head_g, gqa_q_g, gqa_k_g, ssd_conv_w, ssd_conv_b, ssd_dt_bias, ssd_a_log, ssd_d,
                         ssd_norm_g, cf_conv_w, cf_conv_b, cf_ln_g, cf_ln_b)

    y = x_prompt
    ctx_out = []
    for l in range(depth):
        y, new_ctx = _mixer_layer(y, mod, dec_b, l, w, None, None, 512)
        ctx_out.append(new_ctx)
    new = [jnp.stack([t[i] for t in ctx_out], axis=1) for i in range(5)]

    caches = dict(ckv=cache_mla_ckv,
                  kr_pad=_pad_last(cache_mla_krope, MLA_NOPE, LANES - MLA_QK),
                  gk=cache_gqa_k.reshape(dec_b, depth, past, GQA_KV_HEADS * GQA_HD),
                  gv=cache_gqa_v.reshape(dec_b, depth, past, GQA_KV_HEADS * GQA_HD),
                  state=state_ssd.reshape(dec_b, depth, 2, SSD_GROUPS, 2 * SSD_P, SSD_N))
    rope_d = _rope_table(n_lat, GQA_HD, 0)[:, :, 0:GQA_HD]
    ropes = (_rope_table(n_lat, MLA_ROPE, MLA_NOPE), jnp.concatenate([rope_d, rope_d], axis=-1))
    z = x_sample
    for l in range(depth):
        z, _ = _mixer_layer(z, mod, None, l, w, caches, ropes, 512)

    return (y, z, new[0], new[1], new[2], new[3], new[4])
```
